```python
import math
import jax, jax.numpy as jnp
from jax import lax
import numpy as np

D_MODEL = 1024
BATCH = 16
SEQ = 4096
DEPTH = 4

GRID_W = 64
CTX_LEN = 256

NA_HEADS = 8
NA_HEAD_DIM = 64
NA_WIDTH = NA_HEADS * NA_HEAD_DIM
NA_WIN_ROWS = 8
NA_WIN_COLS = 16

GLA_HEADS = 4
GLA_DK = 32
GLA_DV = 64
GLA_KEY_WIDTH = GLA_HEADS * GLA_DK
GLA_WIDTH = GLA_HEADS * GLA_DV
GLA_RANK = 16
GLA_GATE_NORM = 16.0
GLA_CHUNK = 64

HY_CH = 256
HY_ORDER = 2
HY_SHORT = 3
HY_BANDS = 16
HY_EMB = 1 + 2 * HY_BANDS
HY_FILTER_HIDDEN = 64
HY_TARGET = 1e-2
HY_FAST_DECAY = 0.3
HY_SLOW_DECAY = 1.5

MIX_WIDTH = NA_WIDTH + GLA_WIDTH + HY_CH
IN_SPLITS = (NA_WIDTH, NA_WIDTH, NA_WIDTH, GLA_KEY_WIDTH, GLA_KEY_WIDTH, GLA_WIDTH, GLA_WIDTH, 2 * GLA_RANK, (HY_ORDER + 1) * HY_CH)
D_IN = sum(IN_SPLITS)

ROPE_BASE = 10000.0

N_EXPERTS = 128
TOP_K = 8
N_GROUPS = 8
TOPK_GROUPS = 4
EXPERT_HIDDEN = 256
SHARED_HIDDEN = 256
ROUTED_SCALE = 2.5
MOE_BLOCK = 64

DEEPNORM_ALPHA = (2 * DEPTH) ** 0.25
DEEPNORM_BETA = (8 * DEPTH) ** -0.25
LN_EPS = 1e-6

kernel_name = 'hybrid_na_gla_hyena_moe_diffusion_block'


def layer_norm(x, g, b):
    xf = x.astype(jnp.float32)
    mu = xf.mean(-1, keepdims=True)
    var = jnp.square(xf - mu).mean(-1, keepdims=True)
    return ((xf - mu) * lax.rsqrt(var + LN_EPS) * g + b).astype(x.dtype)


def split_cols(u):
    idx = [int(i) for i in np.cumsum(IN_SPLITS)[:-1]]
    return jnp.split(u, idx, axis=-1)


def rope_axial_2d(t, row, col):
    half = t.shape[-1] // 2
    quarter = half // 2
    inv = ROPE_BASE ** (-jnp.arange(quarter, dtype=jnp.float32) / quarter)

    def rot(x, pos):
        ang = pos.astype(jnp.float32)[:, None] * inv
        cos = jnp.cos(ang)[None, :, None, :]
        sin = jnp.sin(ang)[None, :, None, :]
        x1, x2 = x[..., :quarter], x[..., quarter:]
        return jnp.concatenate([x1 * cos - x2 * sin, x1 * sin + x2 * cos], -1)

    return jnp.concatenate([rot(t[..., :half], row), rot(t[..., half:], col)], -1).astype(t.dtype)


def na_tables(rows, rpb):
    kh = min(NA_WIN_ROWS, rows)
    kw = NA_WIN_COLS
    r = jnp.arange(rows)
    c = jnp.arange(GRID_W)
    kr = jnp.clip(r - kh // 2, 0, rows - kh)[:, None] + jnp.arange(kh)
    kc = jnp.clip(c - kw // 2, 0, GRID_W - kw)[:, None] + jnp.arange(kw)
    idx = (kr[:, None, :, None] * GRID_W + kc[None, :, None, :]).reshape(rows, GRID_W, kh * kw)
    dr = kr - r[:, None] + NA_WIN_ROWS - 1
    dc = kc - c[:, None] + NA_WIN_COLS - 1
    bias = rpb[:, dr[:, None, :, None], dc[None, :, None, :]]
    bias = bias.reshape(rpb.shape[0], rows, GRID_W, kh * kw).transpose(1, 0, 2, 3)
    return idx, bias


def na_attention(q, k, v, k_ctx, v_ctx, rows, rpb):
    B, L, H, d = q.shape
    idx, bias = na_tables(rows, rpb)
    scale = d ** -0.5
    q_rows = q.reshape(B, rows, GRID_W, H, d).transpose(1, 0, 2, 3, 4)

    def row_block(args):
        q_r, idx_r, bias_r = args
        k_g = k[:, idx_r]
        v_g = v[:, idx_r]
        s_loc = jnp.einsum('bwhd,bwkhd->bhwk', q_r, k_g).astype(jnp.float32) * scale + bias_r.astype(jnp.float32)[None]
        s_ctx = jnp.einsum('bwhd,bchd->bhwc', q_r, k_ctx).astype(jnp.float32) * scale
        p = jax.nn.softmax(jnp.concatenate([s_loc, s_ctx], -1), axis=-1).astype(q.dtype)
        n_loc = idx_r.shape[-1]
        return (jnp.einsum('bhwk,bwkhd->bwhd', p[..., :n_loc], v_g)
                + jnp.einsum('bhwc,bchd->bwhd', p[..., n_loc:], v_ctx))

    o = lax.map(row_block, (q_rows, idx, bias))
    return o.transpose(1, 0, 2, 3, 4).reshape(B, L, H * d)


def ctx_attention(q, k, v):
    B, C, H, d = q.shape
    s = jnp.einsum('bqhd,bkhd->bhqk', q, k).astype(jnp.float32) * d ** -0.5
    p = jax.nn.softmax(s, axis=-1).astype(q.dtype)
    return jnp.einsum('bhqk,bkhd->bqhd', p, v).reshape(B, C, H * d)


def gla_chunked(q, k, v, log_a, s0):
    B, L, H, dk = q.shape
    dv = v.shape[-1]
    n = L // GLA_CHUNK

    def chunks(t):
        return t.astype(jnp.float32).reshape(B, n, GLA_CHUNK, H, t.shape[-1]).transpose(1, 0, 2, 3, 4)

    tri = jnp.tril(jnp.ones((GLA_CHUNK, GLA_CHUNK), bool))

    def step(s, inp):
        qc, kc, vc, ac = inp
        b = jnp.cumsum(ac, axis=1)
        qt = qc * jnp.exp(b)
        kt = kc * jnp.exp(-b)
        att = jnp.where(tri[None, None], jnp.einsum('bihd,bjhd->bhij', qt, kt), 0.0)
        o = jnp.einsum('bhij,bjhv->bihv', att, vc) + jnp.einsum('bihd,bhdv->bihv', qt, s)
        b_last = b[:, -1]
        s_new = jnp.exp(b_last)[..., None] * s + jnp.einsum('bjhd,bjhv->bhdv', kc * jnp.exp(b_last[:, None] - b), vc)
        return s_new, o

    s_fin, o = lax.scan(step, s0, (chunks(q), chunks(k), chunks(v), chunks(log_a)))
    return o.transpose(1, 0, 2, 3, 4).reshape(B, L, H, dv), s_fin


def gla_bidir(q, k, v, la_f, la_b, s0_f, s0_b):
    o_f, s_f = gla_chunked(q, k, v, la_f, s0_f)
    flip = lambda t: jnp.flip(t, axis=1)
    o_b, s_b = gla_chunked(flip(q), flip(k), flip(v), flip(la_b), s0_b)
    return o_f + flip(o_b), s_f, s_b


def gla_inputs(gq, gk, gv, glr, w_a2, b_a2):
    B, L, _ = gq.shape
    q = gq.reshape(B, L, GLA_HEADS, GLA_DK) * GLA_DK ** -0.5
    k = gk.reshape(B, L, GLA_HEADS, GLA_DK)
    v = gv.reshape(B, L, GLA_HEADS, GLA_DV)
    lr = glr.reshape(B, L, 2, GLA_RANK)
    logit = jnp.einsum('bldr,drk->bldk', lr, w_a2) + b_a2
    la = (jax.nn.log_sigmoid(logit.astype(jnp.float32)) / GLA_GATE_NORM).reshape(B, L, 2, GLA_HEADS, GLA_DK)
    return q, k, v, la[:, :, 0], la[:, :, 1]


def gla_output(o, r, g):
    B, L = o.shape[:2]
    of = o.astype(jnp.float32)
    of = of * lax.rsqrt(jnp.mean(of * of, -1, keepdims=True) + LN_EPS) * g
    return (of.reshape(B, L, GLA_WIDTH) * jax.nn.silu(r.astype(jnp.float32))).astype(r.dtype)


def short_conv_centred(u, w, b):
    L = u.shape[1]
    pad = HY_SHORT // 2
    up = jnp.pad(u, ((0, 0), (pad, HY_SHORT - 1 - pad), (0, 0)))
    y = b
    for j in range(HY_SHORT):
        y = y + up[:, j:j + L] * w[j]
    return y


def hyena_filters(L, w1, b1, f1, w2, b2, f2, w3, b3, decay):
    t = jnp.linspace(0.0, 1.0, L, dtype=jnp.float32)
    w = 2.0 * math.pi * jnp.arange(L, dtype=jnp.float32) / L
    bands = jnp.linspace(1e-4, HY_BANDS - 1, HY_BANDS, dtype=jnp.float32)
    ang = w[:, None] * bands[None]
    z = jnp.concatenate([t[:, None], jnp.cos(ang), -jnp.sin(ang)], -1)
    h = jnp.sin(f1 * (z @ w1 + b1))
    h = jnp.sin(f2 * (h @ w2 + b2))
    h = (h @ w3 + b3).reshape(L, HY_ORDER, 2, HY_CH)
    window = jnp.exp(-t[:, None, None, None] * jnp.abs(decay)[None])
    return (h * window).astype(jnp.float32)


def bidir_long_conv(u, h_fwd, h_bwd, d_skip):
    B, L, C = u.shape
    kern = jnp.concatenate([h_fwd, jnp.zeros((1, C), h_fwd.dtype), h_bwd[:0:-1]], axis=0)
    kf = jnp.fft.rfft(kern, n=2 * L, axis=0)
    uf32 = u.astype(jnp.float32)
    uf = jnp.fft.rfft(uf32, n=2 * L, axis=1)
    y = jnp.fft.irfft(uf * kf[None], n=2 * L, axis=1)[:, :L]
    return (y + uf32 * d_skip).astype(u.dtype)


def hyena_mixer(hu, short_w, short_b, w1, b1, f1, w2, b2, f2, w3, b3, decay, skip):
    L = hu.shape[1]
    hu = short_conv_centred(hu, short_w, short_b)
    parts = jnp.split(hu, HY_ORDER + 1, axis=-1)
    filt = hyena_filters(L, w1, b1, f1, w2, b2, f2, w3, b3, decay)
    z = parts[0]
    for o in range(HY_ORDER):
        z = parts[o + 1] * bidir_long_conv(z, filt[:, o, 0], filt[:, o, 1], skip[o])
    return z


def swiglu(x, wg, wu, wd):
    return (jax.nn.silu(x @ wg) * (x @ wu)) @ wd


def route(h, w_router, b_corr):
    logits = jnp.einsum('btd,de->bte', h, w_router).astype(jnp.float32)
    s = jax.nn.sigmoid(logits)
    sel = s + b_corr.astype(jnp.float32)
    B, T, E = s.shape
    per = E // N_GROUPS
    grp_score = lax.top_k(sel.reshape(B, T, N_GROUPS, per), 2)[0].sum(-1)
    _, gidx = lax.top_k(grp_score, TOPK_GROUPS)
    gmask = jax.nn.one_hot(gidx, N_GROUPS, dtype=jnp.float32).sum(-2)
    emask = jnp.repeat(gmask, per, axis=-1) > 0
    _, eidx = lax.top_k(jnp.where(emask, sel, -jnp.inf), TOP_K)
    w = jnp.take_along_axis(s, eidx, axis=-1)
    w = w / w.sum(-1, keepdims=True) * ROUTED_SCALE
    return eidx, w


def expert_dispatch(hs, ei, wi, we_g, we_u, we_d):
    T, D = hs.shape
    E = we_g.shape[0]
    M = T * TOP_K
    e_flat = ei.reshape(M)
    tok = jnp.arange(M) // TOP_K
    w_flat = wi.reshape(M)
    order = jnp.argsort(e_flat)
    e_s, tok_s, w_s = e_flat[order], tok[order], w_flat[order]
    counts = jnp.bincount(e_flat, length=E)
    starts = jnp.cumsum(counts) - counts
    padded = (counts + MOE_BLOCK - 1) // MOE_BLOCK * MOE_BLOCK
    pends = jnp.cumsum(padded)
    pstarts = pends - padded
    dest = pstarts[e_s] + jnp.arange(M) - starts[e_s]
    n_blk = -(-M // MOE_BLOCK) + E
    buf = jnp.zeros((n_blk * MOE_BLOCK, D), hs.dtype).at[dest].set(hs[tok_s])
    blk_e = jnp.minimum(jnp.searchsorted(pends, jnp.arange(n_blk) * MOE_BLOCK, side='right'), E - 1)

    def block(args):
        xb, e = args
        return swiglu(xb, we_g[e], we_u[e], we_d[e])

    yb = lax.map(block, (buf.reshape(n_blk, MOE_BLOCK, D), blk_e)).reshape(n_blk * MOE_BLOCK, D)
    return jnp.zeros((T, D), jnp.float32).at[tok_s].add(yb[dest].astype(jnp.float32) * w_s[:, None])


def moe_ffn(h, w_router, b_corr, we_g, we_u, we_d, ws_g, ws_u, ws_d):
    eidx, w = route(h, w_router, b_corr)
    routed = lax.map(lambda a: expert_dispatch(a[0], a[1], a[2], we_g, we_u, we_d), (h, eidx, w))
    return swiglu(h, ws_g, ws_u, ws_d) + routed.astype(h.dtype)


def setup_inputs(seed: int = 0) -> dict:
    key = jax.random.key(seed)
    keys = iter(jax.random.split(key, 48))

    def nrm(shape, std):
        return jax.random.normal(next(keys), shape, jnp.float32) * std

    D = D_MODEL
    hid = HY_FILTER_HIDDEN
    decay_base = jnp.linspace(math.log(HY_TARGET) / HY_SLOW_DECAY, math.log(HY_TARGET) / HY_FAST_DECAY, HY_CH, dtype=jnp.float32)
    return {
        'x': nrm((BATCH, SEQ, D), 1.0),
        'c': nrm((BATCH, D), 1.0),
        'ctx': nrm((BATCH, CTX_LEN, D), 1.0),
        'c_ctx': nrm((D,), 1.0),
        'w_mod': nrm((DEPTH, D, 6 * D), 0.5 * D ** -0.5),
        'b_mod': nrm((DEPTH, 6 * D), 0.02),
        'w_in': nrm((DEPTH, D, D_IN), D ** -0.5),
        'na_rpb': nrm((DEPTH, NA_HEADS, 2 * NA_WIN_ROWS - 1, 2 * NA_WIN_COLS - 1), 0.1),
        'gla_w_a2': nrm((DEPTH, 2, GLA_RANK, GLA_KEY_WIDTH), GLA_RANK ** -0.5),
        'gla_b_a2': nrm((DEPTH, 2, GLA_KEY_WIDTH), 0.1),
        'gla_norm_g': 1.0 + nrm((DEPTH, GLA_DV), 0.02),
        'hy_short_w': nrm((DEPTH, HY_SHORT, (HY_ORDER + 1) * HY_CH), HY_SHORT ** -0.5),
        'hy_short_b': nrm((DEPTH, (HY_ORDER + 1) * HY_CH), 0.02),
        'hy_w1': nrm((DEPTH, HY_EMB, hid), HY_EMB ** -0.5),
        'hy_b1': nrm((DEPTH, hid), 0.1),
        'hy_f1': 1.0 + nrm((DEPTH, hid), 0.1),
        'hy_w2': nrm((DEPTH, hid, hid), hid ** -0.5),
        'hy_b2': nrm((DEPTH, hid), 0.1),
        'hy_f2': 1.0 + nrm((DEPTH, hid), 0.1),
        'hy_w3': nrm((DEPTH, hid, HY_ORDER * 2 * HY_CH), 0.05 * hid ** -0.5),
        'hy_b3': nrm((DEPTH, HY_ORDER * 2 * HY_CH), 0.005),
        'hy_decay': decay_base + nrm((DEPTH, HY_ORDER, 2, HY_CH), 0.1),
        'hy_skip': nrm((DEPTH, HY_ORDER, HY_CH), 0.5),
        'w_out': nrm((DEPTH, MIX_WIDTH, D), MIX_WIDTH ** -0.5 * DEEPNORM_BETA),
        'ln1_g': 1.0 + nrm((DEPTH, D), 0.02),
        'ln1_b': nrm((DEPTH, D), 0.02),
        'router_w': nrm((DEPTH, D, N_EXPERTS), D ** -0.5),
        'router_b': nrm((DEPTH, N_EXPERTS), 0.01),
        'we_gate': nrm((DEPTH, N_EXPERTS, D, EXPERT_HIDDEN), D ** -0.5),
        'we_up': nrm((DEPTH, N_EXPERTS, D, EXPERT_HIDDEN), D ** -0.5),
        'we_down': nrm((DEPTH, N_EXPERTS, EXPERT_HIDDEN, D), EXPERT_HIDDEN ** -0.5 * DEEPNORM_BETA),
        'ws_gate': nrm((DEPTH, D, SHARED_HIDDEN), D ** -0.5),
        'ws_up': nrm((DEPTH, D, SHARED_HIDDEN), D ** -0.5),
        'ws_down': nrm((DEPTH, SHARED_HIDDEN, D), SHARED_HIDDEN ** -0.5 * DEEPNORM_BETA),
        'ln2_g': 1.0 + nrm((DEPTH, D), 0.02),
        'ln2_b': nrm((DEPTH, D), 0.02),
    }


def reference(x, c, ctx, c_ctx, w_mod, b_mod, w_in, na_rpb, gla_w_a2, gla_b_a2, gla_norm_g,
              hy_short_w, hy_short_b, hy_w1, hy_b1, hy_f1, hy_w2, hy_b2, hy_f2, hy_w3, hy_b3,
              hy_decay, hy_skip, w_out, ln1_g, ln1_b, router_w, router_b, we_gate, we_up, we_down,
              ws_gate, ws_up, ws_down, ln2_g, ln2_b):
    B, L, D = x.shape
    rows = L // GRID_W
    t = jnp.arange(L)
    row_pos, col_pos = t // GRID_W, t % GRID_W
    silu_c = jax.nn.silu(c)
    silu_cc = jax.nn.silu(c_ctx)
    xc = ctx

    def na_heads(a):
        return a.reshape(a.shape[0], a.shape[1], NA_HEADS, NA_HEAD_DIM)

    for l in range(DEPTH):
        last = l == DEPTH - 1
        hy_params = (hy_short_w[l], hy_short_b[l], hy_w1[l], hy_b1[l], hy_f1[l], hy_w2[l], hy_b2[l],
                     hy_f2[l], hy_w3[l], hy_b3[l], hy_decay[l], hy_skip[l])
        moe_params = (router_w[l], router_b[l], we_gate[l], we_up[l], we_down[l], ws_gate[l], ws_up[l], ws_down[l])
        mod = (silu_c @ w_mod[l] + b_mod[l])[:, None, :]
        mod_c = silu_cc @ w_mod[l] + b_mod[l]
        sh1, sc1, g1, sh2, sc2, g2 = jnp.split(mod, 6, axis=-1)
        sh1c, sc1c, g1c, sh2c, sc2c, g2c = jnp.split(mod_c, 6, axis=-1)

        u = (x * (1.0 + sc1) + sh1) @ w_in[l]
        uc = (xc * (1.0 + sc1c) + sh1c) @ w_in[l]
        nq, nk, nv, gq, gk, gv, gr, glr, hy = split_cols(u)
        nqc, nkc, nvc, gqc, gkc, gvc, grc, glrc, hyc = split_cols(uc)

        na_lat = na_attention(na_heads(nq), na_heads(nk), na_heads(nv), na_heads(nkc), na_heads(nvc), rows, na_rpb[l])

        q, k, v, la_f, la_b = gla_inputs(gq, gk, gv, glr, gla_w_a2[l], gla_b_a2[l])
        qc_, kc_, vc_, lac_f, lac_b = gla_inputs(gqc, gkc, gvc, glrc, gla_w_a2[l], gla_b_a2[l])
        zero_state = jnp.zeros((B, GLA_HEADS, GLA_DK, GLA_DV), jnp.float32)
        o_ctx, s_ctx_f, s_ctx_b = gla_bidir(qc_, kc_, vc_, lac_f, lac_b, zero_state, zero_state)
        q = rope_axial_2d(q, row_pos, col_pos)
        k = rope_axial_2d(k, row_pos, col_pos)
        o_lat, _, _ = gla_bidir(q, k, v, la_f, la_b, s_ctx_f, s_ctx_b)
        gla_lat = gla_output(o_lat, gr, gla_norm_g[l])

        hy_lat = hyena_mixer(hy, *hy_params)

        mix = jnp.concatenate([na_lat, gla_lat, hy_lat], axis=-1) @ w_out[l]
        x_new = layer_norm(DEEPNORM_ALPHA * x + g1 * mix, ln1_g[l], ln1_b[l])
        if not last:
            na_c = ctx_attention(na_heads(nqc), na_heads(nkc), na_heads(nvc))
            gla_c = gla_output(o_ctx, grc, gla_norm_g[l])
            hy_c = hyena_mixer(hyc, *hy_params)
            mix_c = jnp.concatenate([na_c, gla_c, hy_c], axis=-1) @ w_out[l]
            xc = layer_norm(DEEPNORM_ALPHA * xc + g1c * mix_c, ln1_g[l], ln1_b[l])
        x = x_new

        ff = moe_ffn(x * (1.0 + sc2) + sh2, *moe_params)
        x = layer_norm(DEEPNORM_ALPHA * x + g2 * ff, ln2_g[l], ln2_b[l])
        if not last:
            ffc = moe_ffn(xc * (1.0 + sc2c) + sh2c, *moe_params)
            xc = layer_norm(DEEPNORM_ALPHA * xc + g2c * ffc, ln2_g[l], ln2_b[l])
    return x
```

```python
import functools
import math

import numpy as np
import jax
import jax.numpy as jnp
from jax import lax
from jax.experimental import pallas as pl
from jax.experimental.pallas import tpu as pltpu

F32 = jnp.float32
BF16 = jnp.bfloat16

D_MODEL = 1024
DEPTH = 4
GRID_W = 64
CTX_LEN = 256

NA_HEADS = 8
NA_HEAD_DIM = 64
NA_WIDTH = NA_HEADS * NA_HEAD_DIM
NA_WIN_ROWS = 8
NA_WIN_COLS = 16

GLA_HEADS = 4
GLA_DK = 32
GLA_DV = 64
GLA_KEY_WIDTH = GLA_HEADS * GLA_DK
GLA_WIDTH = GLA_HEADS * GLA_DV
GLA_RANK = 16
GLA_GATE_NORM = 16.0
GLA_CHUNK = 64

HY_CH = 256
HY_ORDER = 2
HY_SHORT = 3
HY_BANDS = 16
HY_EMB = 1 + 2 * HY_BANDS

MIX_WIDTH = NA_WIDTH + GLA_WIDTH + HY_CH
IN_SPLITS = (NA_WIDTH, NA_WIDTH, NA_WIDTH, GLA_KEY_WIDTH, GLA_KEY_WIDTH, GLA_WIDTH, GLA_WIDTH,
             2 * GLA_RANK, (HY_ORDER + 1) * HY_CH)
D_IN = sum(IN_SPLITS)
ROPE_BASE = 10000.0

N_EXPERTS = 128
TOP_K = 8
N_GROUPS = 8
TOPK_GROUPS = 4
EXPERT_HIDDEN = 256
ROUTED_SCALE = 2.5

DEEPNORM_ALPHA = (2 * DEPTH) ** 0.25
LN_EPS = 1e-6

LANE = 128
MXU_DIM = 256
VMEM_LIMIT = 48 * 1024 * 1024

NA_COLS = 3 * NA_WIDTH
D_IN_PAD = -(-D_IN // LANE) * LANE
REST_COLS = D_IN_PAD - NA_COLS
NA_QUAD = MXU_DIM // NA_HEAD_DIM
NEG_BIG = -1e30

TOK_TILE = 512
MOE_TILE = 256
NA_ROW_TILE = 8


def _cparams(*sem):
    return pltpu.CompilerParams(dimension_semantics=sem, vmem_limit_bytes=VMEM_LIMIT)


def _mod_kernel(c_ref, w_ref, b_ref, o_ref):
    c = c_ref[...]
    s = c * jax.nn.sigmoid(c)
    o_ref[0] = jnp.dot(s, w_ref[0], preferred_element_type=F32, precision=lax.Precision.HIGHEST) + b_ref[0]


def _modulation(cs, w_mod, b_mod):
    R = cs.shape[0]
    tn = 1536
    return pl.pallas_call(
        _mod_kernel,
        grid=(DEPTH, 6 * D_MODEL // tn),
        in_specs=[pl.BlockSpec((R, D_MODEL), lambda l, j: (0, 0)),
                  pl.BlockSpec((1, D_MODEL, tn), lambda l, j: (l, 0, j)),
                  pl.BlockSpec((1, 1, tn), lambda l, j: (l, 0, j))],
        out_specs=pl.BlockSpec((1, R, tn), lambda l, j: (l, 0, j)),
        out_shape=jax.ShapeDtypeStruct((DEPTH, R, 6 * D_MODEL), F32),
        compiler_params=_cparams("arbitrary", "arbitrary"),
        name="modulation",
    )(cs, w_mod, b_mod.reshape(DEPTH, 1, 6 * D_MODEL))


def _inproj_kernel(x_ref, sc_ref, sh_ref, w_ref, ona_ref, orest_ref):
    xm = (x_ref[0] * (1.0 + sc_ref[0]) + sh_ref[0]).astype(BF16)
    step = 512
    for c0 in range(0, NA_COLS, step):
        ona_ref[0, :, c0:c0 + step] = jnp.dot(xm, w_ref[:, c0:c0 + step], preferred_element_type=F32).astype(BF16)
    for c0 in range(0, REST_COLS, step):
        c1 = min(c0 + step, REST_COLS)
        orest_ref[0, :, c0:c1] = jnp.dot(xm, w_ref[:, NA_COLS + c0:NA_COLS + c1], preferred_element_type=F32)


def _inproj(x, sc, sh, w_pad, per_batch_mod):
    B, T, D = x.shape
    tm = min(TOK_TILE, T)
    mod_idx = (lambda b, i: (b, 0, 0)) if per_batch_mod else (lambda b, i: (0, 0, 0))
    return pl.pallas_call(
        _inproj_kernel,
        grid=(B, T // tm),
        in_specs=[pl.BlockSpec((1, tm, D), lambda b, i: (b, i, 0)),
                  pl.BlockSpec((1, 1, D), mod_idx),
                  pl.BlockSpec((1, 1, D), mod_idx),
                  pl.BlockSpec((D, D_IN_PAD), lambda b, i: (0, 0))],
        out_specs=[pl.BlockSpec((1, tm, NA_COLS), lambda b, i: (b, i, 0)),
                   pl.BlockSpec((1, tm, REST_COLS), lambda b, i: (b, i, 0))],
        out_shape=[jax.ShapeDtypeStruct((B, T, NA_COLS), BF16),
                   jax.ShapeDtypeStruct((B, T, REST_COLS), F32)],
        compiler_params=_cparams("arbitrary", "arbitrary"),
        name="inproj",
    )(x, sc, sh, w_pad)


def _stack_heads(q, n_rows):
    head = lax.broadcasted_iota(jnp.int32, (n_rows, MXU_DIM), 1) // NA_HEAD_DIM
    return jnp.concatenate([jnp.where(head == h, q, jnp.zeros_like(q)) for h in range(NA_QUAD)], axis=0)


def _unstack_heads(o, n_rows):
    head = lax.broadcasted_iota(jnp.int32, (n_rows, MXU_DIM), 1) // NA_HEAD_DIM
    out = jnp.zeros((n_rows, MXU_DIM), F32)
    for h in range(NA_QUAD):
        out = jnp.where(head == h, o[h * n_rows:(h + 1) * n_rows], out)
    return out


_NT = (((1,), (1,)), ((), ()))


def _na_kernel(q_ref, k_ref, v_ref, kc_ref, vc_ref, bias_ref, o_ref):
    rt = pl.program_id(2)
    scale = NA_HEAD_DIM ** -0.5
    kc = kc_ref[0]
    vc = vc_ref[0]
    n_loc = NA_WIN_ROWS * GRID_W

    def row(rl, carry):
        r = rt * NA_ROW_TILE + rl
        kr0 = jnp.clip(r - NA_WIN_ROWS // 2, 0, GRID_W - NA_WIN_ROWS)
        dr0 = kr0 - r + NA_WIN_ROWS - 1
        q = q_ref[0, pl.ds(pl.multiple_of(rl * GRID_W, GRID_W), GRID_W), :]
        qs = _stack_heads(q, GRID_W)
        k0 = pl.multiple_of(kr0 * GRID_W, GRID_W)
        ks = k_ref[0, pl.ds(k0, n_loc), :]
        vs = v_ref[0, pl.ds(k0, n_loc), :]
        s_loc = lax.dot_general(qs, ks, _NT, preferred_element_type=F32) * scale + bias_ref[0, dr0]
        s_ctx = lax.dot_general(qs, kc, _NT, preferred_element_type=F32) * scale
        m = jnp.maximum(jnp.max(s_loc, axis=-1, keepdims=True), jnp.max(s_ctx, axis=-1, keepdims=True))
        p_loc = jnp.exp(s_loc - m)
        p_ctx = jnp.exp(s_ctx - m)
        den = jnp.sum(p_loc, axis=-1, keepdims=True) + jnp.sum(p_ctx, axis=-1, keepdims=True)
        o = (jnp.dot(p_loc.astype(BF16), vs, preferred_element_type=F32)
             + jnp.dot(p_ctx.astype(BF16), vc, preferred_element_type=F32)) / den
        o_ref[0, pl.ds(pl.multiple_of(rl * GRID_W, GRID_W), GRID_W), :] = _unstack_heads(o, GRID_W)
        return carry

    lax.fori_loop(0, NA_ROW_TILE, row, 0)


def _na_bias_table(rpb):
    c = np.arange(GRID_W)
    kc0 = np.clip(c - NA_WIN_COLS // 2, 0, GRID_W - NA_WIN_COLS)
    kc = np.arange(GRID_W)
    valid = (kc[None, :] >= kc0[:, None]) & (kc[None, :] < kc0[:, None] + NA_WIN_COLS)
    dc = np.clip(kc[None, :] - c[:, None] + NA_WIN_COLS - 1, 0, 2 * NA_WIN_COLS - 2)
    dr = np.arange(NA_WIN_ROWS)[:, None] + np.arange(NA_WIN_ROWS)[None, :]
    tab = rpb[:, dr[:, :, None, None], dc[None, None, :, :]]
    tab = jnp.where(valid[None, None, None], tab, NEG_BIG)
    tab = tab.transpose(0, 1, 3, 2, 4).reshape(NA_HEADS, NA_WIN_ROWS, GRID_W, NA_WIN_ROWS * GRID_W)
    tab = tab.reshape(NA_HEADS // NA_QUAD, NA_QUAD, NA_WIN_ROWS, GRID_W, NA_WIN_ROWS * GRID_W)
    return tab.transpose(0, 2, 1, 3, 4).reshape(NA_HEADS // NA_QUAD, NA_WIN_ROWS, NA_QUAD * GRID_W,
                                                NA_WIN_ROWS * GRID_W).astype(F32)


def _na_attention(u_na, uc_na, bias_tab):
    B, L, _ = u_na.shape
    C = uc_na.shape[1]
    nq = NA_WIDTH // MXU_DIM
    tq = NA_ROW_TILE * GRID_W
    return pl.pallas_call(
        _na_kernel,
        grid=(B, nq, L // tq),
        in_specs=[pl.BlockSpec((1, tq, MXU_DIM), lambda b, j, i: (b, i, j)),
                  pl.BlockSpec((1, L, MXU_DIM), lambda b, j, i: (b, 0, nq + j)),
                  pl.BlockSpec((1, L, MXU_DIM), lambda b, j, i: (b, 0, 2 * nq + j)),
                  pl.BlockSpec((1, C, MXU_DIM), lambda b, j, i: (b, 0, nq + j)),
                  pl.BlockSpec((1, C, MXU_DIM), lambda b, j, i: (b, 0, 2 * nq + j)),
                  pl.BlockSpec((1, NA_WIN_ROWS, NA_QUAD * GRID_W, NA_WIN_ROWS * GRID_W), lambda b, j, i: (j, 0, 0, 0))],
        out_specs=pl.BlockSpec((1, tq, MXU_DIM), lambda b, j, i: (b, i, j)),
        out_shape=jax.ShapeDtypeStruct((B, L, NA_WIDTH), F32),
        compiler_params=_cparams("arbitrary", "arbitrary", "arbitrary"),
        name="na_attention",
    )(u_na, u_na, u_na, uc_na, uc_na, bias_tab)


def _ctx_attn_kernel(q_ref, k_ref, v_ref, o_ref):
    C = q_ref.shape[1]
    qs = _stack_heads(q_ref[0], C)
    s = lax.dot_general(qs, k_ref[0], _NT, preferred_element_type=F32) * NA_HEAD_DIM ** -0.5
    p = jnp.exp(s - jnp.max(s, axis=-1, keepdims=True))
    den = jnp.sum(p, axis=-1, keepdims=True)
    o = jnp.dot(p.astype(BF16), v_ref[0], preferred_element_type=F32) / den
    o_ref[0] = _unstack_heads(o, C)


def _ctx_attention(uc_na):
    B, C, _ = uc_na.shape
    nq = NA_WIDTH // MXU_DIM
    return pl.pallas_call(
        _ctx_attn_kernel,
        grid=(B, nq),
        in_specs=[pl.BlockSpec((1, C, MXU_DIM), lambda b, j: (b, 0, j)),
                  pl.BlockSpec((1, C, MXU_DIM), lambda b, j: (b, 0, nq + j)),
                  pl.BlockSpec((1, C, MXU_DIM), lambda b, j: (b, 0, 2 * nq + j))],
        out_specs=pl.BlockSpec((1, C, MXU_DIM), lambda b, j: (b, 0, j)),
        out_shape=jax.ShapeDtypeStruct((B, C, NA_WIDTH), F32),
        compiler_params=_cparams("arbitrary", "arbitrary"),
        name="ctx_attention",
    )(uc_na, uc_na, uc_na)


def _layer_norm_rows(y, g, b):
    mu = jnp.mean(y, axis=-1, keepdims=True)
    d = y - mu
    var = jnp.mean(d * d, axis=-1, keepdims=True)
    return d * lax.rsqrt(var + LN_EPS) * g + b


def _outproj_kernel(na_ref, gla_ref, hy_ref, x_ref, g1_ref, w_ref, lg_ref, lb_ref, sc_ref, sh_ref, wr_ref,
                    xo_ref, h_ref, logit_ref):
    mix = (jnp.dot(na_ref[0].astype(BF16), w_ref[0:NA_WIDTH, :], preferred_element_type=F32)
           + jnp.dot(gla_ref[0].astype(BF16), w_ref[NA_WIDTH:NA_WIDTH + GLA_WIDTH, :], preferred_element_type=F32)
           + jnp.dot(hy_ref[0].astype(BF16), w_ref[NA_WIDTH + GLA_WIDTH:, :], preferred_element_type=F32))
    xn = _layer_norm_rows(DEEPNORM_ALPHA * x_ref[0] + g1_ref[0] * mix, lg_ref[...], lb_ref[...])
    xo_ref[0] = xn
    h = xn * (1.0 + sc_ref[0]) + sh_ref[0]
    h_ref[0] = h.astype(BF16)
    logit_ref[0] = jnp.dot(h, wr_ref[...], preferred_element_type=F32, precision=lax.Precision.HIGHEST)


def _outproj(na, gla, hy, x, g1, w_out, ln_g, ln_b, sc2, sh2, w_router, per_batch_mod):
    B, T, D = x.shape
    tm = min(TOK_TILE, T)
    mod_idx = (lambda b, i: (b, 0, 0)) if per_batch_mod else (lambda b, i: (0, 0, 0))
    tok = lambda w: pl.BlockSpec((1, tm, w), lambda b, i: (b, i, 0))
    full = lambda s: pl.BlockSpec(s, lambda b, i: (0,) * len(s))
    mod = pl.BlockSpec((1, 1, D), mod_idx)
    return pl.pallas_call(
        _outproj_kernel,
        grid=(B, T // tm),
        in_specs=[tok(NA_WIDTH), tok(GLA_WIDTH), tok(HY_CH), tok(D), mod, full((MIX_WIDTH, D)),
                  full((1, D)), full((1, D)), mod, mod, full((D, N_EXPERTS))],
        out_specs=[tok(D), tok(D), tok(N_EXPERTS)],
        out_shape=[jax.ShapeDtypeStruct((B, T, D), F32), jax.ShapeDtypeStruct((B, T, D), BF16),
                   jax.ShapeDtypeStruct((B, T, N_EXPERTS), F32)],
        compiler_params=_cparams("arbitrary", "arbitrary"),
        name="outproj_ln1",
    )(na, gla, hy, x, g1, w_out, ln_g, ln_b, sc2, sh2, w_router)


def _moe_kernel(te_ref, nt_ref, xs_ref, wrow_ref, wg_ref, wu_ref, wd_ref, ys_ref):
    i = pl.program_id(0)

    @pl.when(i < nt_ref[0])
    def _():
        xs = xs_ref[...]
        g = jnp.dot(xs, wg_ref[0].astype(BF16), preferred_element_type=F32)
        u = jnp.dot(xs, wu_ref[0].astype(BF16), preferred_element_type=F32)
        a = (g * jax.nn.sigmoid(g) * u).astype(BF16)
        y = jnp.dot(a, wd_ref[0].astype(BF16), preferred_element_type=F32)
        ys_ref[...] = y * wrow_ref[...]

    @pl.when(i >= nt_ref[0])
    def _():
        ys_ref[...] = jnp.zeros_like(ys_ref)


def _moe_grouped(tile_expert, n_used, xs, wrow, we_g, we_u, we_d, layer):
    Mp, D = xs.shape
    n_tiles = Mp // MOE_TILE
    H = EXPERT_HIDDEN
    grid_spec = pltpu.PrefetchScalarGridSpec(
        num_scalar_prefetch=2,
        grid=(n_tiles,),
        in_specs=[pl.BlockSpec((MOE_TILE, D), lambda i, te, nt: (i, 0)),
                  pl.BlockSpec((MOE_TILE, 1), lambda i, te, nt: (i, 0)),
                  pl.BlockSpec((None, 1, D, H), lambda i, te, nt: (layer, te[i], 0, 0)),
                  pl.BlockSpec((None, 1, D, H), lambda i, te, nt: (layer, te[i], 0, 0)),
                  pl.BlockSpec((None, 1, H, D), lambda i, te, nt: (layer, te[i], 0, 0))],
        out_specs=pl.BlockSpec((MOE_TILE, D), lambda i, te, nt: (i, 0)),
    )
    return pl.pallas_call(
        _moe_kernel,
        grid_spec=grid_spec,
        out_shape=jax.ShapeDtypeStruct((Mp, D), F32),
        compiler_params=_cparams("arbitrary"),
        name="moe_experts",
    )(tile_expert, n_used, xs, wrow, we_g, we_u, we_d)


def _shared_kernel(x_ref, h_ref, r_ref, g2_ref, wg_ref, wu_ref, wd_ref, lg_ref, lb_ref, o_ref):
    h = h_ref[0]
    g = jnp.dot(h, wg_ref[...], preferred_element_type=F32)
    u = jnp.dot(h, wu_ref[...], preferred_element_type=F32)
    a = (g * jax.nn.sigmoid(g) * u).astype(BF16)
    ff = jnp.dot(a, wd_ref[...], preferred_element_type=F32) + r_ref[0]
    o_ref[0] = _layer_norm_rows(DEEPNORM_ALPHA * x_ref[0] + g2_ref[0] * ff, lg_ref[...], lb_ref[...])


def _shared_ln2(x, h, routed, g2, ws_g, ws_u, ws_d, ln_g, ln_b, per_batch_mod):
    B, T, D = x.shape
    tm = min(TOK_TILE, T)
    mod_idx = (lambda b, i: (b, 0, 0)) if per_batch_mod else (lambda b, i: (0, 0, 0))
    tok = pl.BlockSpec((1, tm, D), lambda b, i: (b, i, 0))
    full = lambda s: pl.BlockSpec(s, lambda b, i: (0,) * len(s))
    return pl.pallas_call(
        _shared_kernel,
        grid=(B, T // tm),
        in_specs=[tok, tok, tok, pl.BlockSpec((1, 1, D), mod_idx), full((D, EXPERT_HIDDEN)), full((D, EXPERT_HIDDEN)),
                  full((EXPERT_HIDDEN, D)), full((1, D)), full((1, D))],
        out_specs=tok,
        out_shape=jax.ShapeDtypeStruct((B, T, D), F32),
        compiler_params=_cparams("arbitrary", "arbitrary"),
        name="shared_ln2",
    )(x, h, routed, g2, ws_g, ws_u, ws_d, ln_g, ln_b)


def _route(logits, b_corr):
    s = jax.nn.sigmoid(logits)
    sel = s + b_corr
    T, E = s.shape
    per = E // N_GROUPS
    grp_score = lax.top_k(sel.reshape(T, N_GROUPS, per), 2)[0].sum(-1)
    _, gidx = lax.top_k(grp_score, TOPK_GROUPS)
    gmask = jax.nn.one_hot(gidx, N_GROUPS, dtype=F32).sum(-2)
    emask = jnp.repeat(gmask, per, axis=-1) > 0
    _, eidx = lax.top_k(jnp.where(emask, sel, -jnp.inf), TOP_K)
    w = jnp.take_along_axis(s, eidx, axis=-1)
    w = w / w.sum(-1, keepdims=True) * ROUTED_SCALE
    return eidx, w


def _dispatch_plan(eidx, w):
    T = eidx.shape[0]
    M = T * TOP_K
    n_tiles = M // MOE_TILE + N_EXPERTS
    e_flat = eidx.reshape(M).astype(jnp.int32)
    order = jnp.argsort(e_flat)
    e_s = e_flat[order]
    counts = jnp.bincount(e_flat, length=N_EXPERTS).astype(jnp.int32)
    starts = jnp.cumsum(counts) - counts
    padded = (counts + MOE_TILE - 1) // MOE_TILE * MOE_TILE
    pends = jnp.cumsum(padded)
    pstarts = pends - padded
    dest = pstarts[e_s] + jnp.arange(M, dtype=jnp.int32) - starts[e_s]
    src_tok = jnp.zeros((n_tiles * MOE_TILE,), jnp.int32).at[dest].set(order // TOP_K)
    wrow = jnp.zeros((n_tiles * MOE_TILE,), F32).at[dest].set(w.reshape(M)[order])
    pos = jnp.zeros((M,), jnp.int32).at[order].set(dest).reshape(T, TOP_K)
    tile_expert = jnp.minimum(jnp.searchsorted(pends, jnp.arange(n_tiles, dtype=jnp.int32) * MOE_TILE, side='right'),
                              N_EXPERTS - 1).astype(jnp.int32)
    n_used = (pends[-1] // MOE_TILE).astype(jnp.int32).reshape(1)
    return src_tok, wrow, pos, tile_expert, n_used


def _moe_routed(h_flat, logits_flat, b_corr, we_g, we_u, we_d, layer):
    eidx, w = _route(logits_flat, b_corr)
    src_tok, wrow, pos, tile_expert, n_used = _dispatch_plan(eidx, w)
    xs = jnp.take(h_flat, src_tok, axis=0)
    ys = _moe_grouped(tile_expert, n_used, xs, wrow.reshape(-1, 1), we_g, we_u, we_d, layer)
    return jnp.take(ys, pos, axis=0).sum(axis=1)


def _rope_axial_2d(t, row, col):
    half = t.shape[-1] // 2
    quarter = half // 2
    inv = ROPE_BASE ** (-jnp.arange(quarter, dtype=F32) / quarter)

    def rot(x, pos):
        ang = pos.astype(F32)[:, None] * inv
        cos = jnp.cos(ang)[None, :, None, :]
        sin = jnp.sin(ang)[None, :, None, :]
        x1, x2 = x[..., :quarter], x[..., quarter:]
        return jnp.concatenate([x1 * cos - x2 * sin, x1 * sin + x2 * cos], -1)

    return jnp.concatenate([rot(t[..., :half], row), rot(t[..., half:], col)], -1)


def _gla_chunked(q, k, v, log_a, s0):
    B, L, H, dk = q.shape
    n = L // GLA_CHUNK

    def chunks(t):
        return t.reshape(B, n, GLA_CHUNK, H, t.shape[-1]).transpose(1, 0, 2, 3, 4)

    tri = jnp.tril(jnp.ones((GLA_CHUNK, GLA_CHUNK), bool))

    def step(s, inp):
        qc, kc, vc, ac = inp
        b = jnp.cumsum(ac, axis=1)
        qt = qc * jnp.exp(b)
        kt = kc * jnp.exp(-b)
        att = jnp.where(tri[None, None], jnp.einsum('bihd,bjhd->bhij', qt, kt), 0.0)
        o = jnp.einsum('bhij,bjhv->bihv', att, vc) + jnp.einsum('bihd,bhdv->bihv', qt, s)
        b_last = b[:, -1]
        s_new = jnp.exp(b_last)[..., None] * s + jnp.einsum('bjhd,bjhv->bhdv', kc * jnp.exp(b_last[:, None] - b), vc)
        return s_new, o

    s_fin, o = lax.scan(step, s0, (chunks(q), chunks(k), chunks(v), chunks(log_a)))
    return o.transpose(1, 0, 2, 3, 4).reshape(B, L, H, v.shape[-1]), s_fin


def _gla_bidir(q, k, v, la_f, la_b, s0_f, s0_b):
    o_f, s_f = _gla_chunked(q, k, v, la_f, s0_f)
    flip = lambda t: jnp.flip(t, axis=1)
    o_b, s_b = _gla_chunked(flip(q), flip(k), flip(v), flip(la_b), s0_b)
    return o_f + flip(o_b), s_f, s_b


def _gla_inputs(gq, gk, gv, glr, w_a2, b_a2):
    B, L, _ = gq.shape
    q = gq.reshape(B, L, GLA_HEADS, GLA_DK) * GLA_DK ** -0.5
    k = gk.reshape(B, L, GLA_HEADS, GLA_DK)
    v = gv.reshape(B, L, GLA_HEADS, GLA_DV)
    lr = glr.reshape(B, L, 2, GLA_RANK)
    logit = jnp.einsum('bldr,drk->bldk', lr, w_a2) + b_a2
    la = (jax.nn.log_sigmoid(logit) / GLA_GATE_NORM).reshape(B, L, 2, GLA_HEADS, GLA_DK)
    return q, k, v, la[:, :, 0], la[:, :, 1]


def _gla_output(o, r, g):
    B, L = o.shape[:2]
    of = o * lax.rsqrt(jnp.mean(o * o, -1, keepdims=True) + LN_EPS) * g
    return of.reshape(B, L, GLA_WIDTH) * jax.nn.silu(r)


def _short_conv(u, w, b):
    L = u.shape[1]
    pad = HY_SHORT // 2
    up = jnp.pad(u, ((0, 0), (pad, HY_SHORT - 1 - pad), (0, 0)))
    y = b
    for j in range(HY_SHORT):
        y = y + up[:, j:j + L] * w[j]
    return y


def _hyena_filters(L, w1, b1, f1, w2, b2, f2, w3, b3, decay):
    t = jnp.linspace(0.0, 1.0, L, dtype=F32)
    w = 2.0 * math.pi * jnp.arange(L, dtype=F32) / L
    bands = jnp.linspace(1e-4, HY_BANDS - 1, HY_BANDS, dtype=F32)
    ang = w[:, None] * bands[None]
    z = jnp.concatenate([t[:, None], jnp.cos(ang), -jnp.sin(ang)], -1)
    h = jnp.sin(f1 * (z @ w1 + b1))
    h = jnp.sin(f2 * (h @ w2 + b2))
    h = (h @ w3 + b3).reshape(L, HY_ORDER, 2, HY_CH)
    window = jnp.exp(-t[:, None, None, None] * jnp.abs(decay)[None])
    return h * window


def _bidir_long_conv(u, h_fwd, h_bwd, d_skip):
    B, L, C = u.shape
    kern = jnp.concatenate([h_fwd, jnp.zeros((1, C), h_fwd.dtype), h_bwd[:0:-1]], axis=0)
    kf = jnp.fft.rfft(kern, n=2 * L, axis=0)
    uf = jnp.fft.rfft(u, n=2 * L, axis=1)
    y = jnp.fft.irfft(uf * kf[None], n=2 * L, axis=1)[:, :L]
    return y + u * d_skip


def _hyena_mixer(hu, short_w, short_b, filt, skip):
    hu = _short_conv(hu, short_w, short_b)
    parts = jnp.split(hu, HY_ORDER + 1, axis=-1)
    z = parts[0]
    for o in range(HY_ORDER):
        z = parts[o + 1] * _bidir_long_conv(z, filt[:, o, 0], filt[:, o, 1], skip[o])
    return z


def kernel(x, c, ctx, c_ctx, w_mod, b_mod, w_in, na_rpb, gla_w_a2, gla_b_a2, gla_norm_g, hy_short_w, hy_short_b, hy_w1, hy_b1, hy_f1, hy_w2, hy_b2, hy_f2, hy_w3, hy_b3, hy_decay, hy_skip, w_out, ln1_g, ln1_b, router_w, router_b, we_gate, we_up, we_down, ws_gate, ws_up, ws_down, ln2_g, ln2_b):
    B, L, D = x.shape
    C = ctx.shape[1]
    t = jnp.arange(L)
    row_pos, col_pos = t // GRID_W, t % GRID_W

    n_mod = -(-(B + 1) // 8) * 8
    cs = jnp.zeros((n_mod, D), F32).at[:B].set(c).at[B].set(c_ctx)
    mod_all = _modulation(cs, w_mod, b_mod)

    xc = ctx
    for l in range(DEPTH):
        last = l == DEPTH - 1
        mods = mod_all[l].reshape(n_mod, 6, 1, D)
        lat = lambda j: mods[:B, j]
        cm = lambda j: mods[B:B + 1, j]
        w_pad = jnp.pad(w_in[l], ((0, 0), (0, D_IN_PAD - D_IN))).astype(BF16)
        w_out_b = w_out[l].astype(BF16)
        lg1, lb1 = ln1_g[l].reshape(1, D), ln1_b[l].reshape(1, D)
        lg2, lb2 = ln2_g[l].reshape(1, D), ln2_b[l].reshape(1, D)

        u_na, u_rest = _inproj(x, lat(1), lat(0), w_pad, True)
        uc_na, uc_rest = _inproj(xc, cm(1), cm(0), w_pad, False)

        def rest_split(u):
            o = 0
            outs = []
            for wdt in IN_SPLITS[3:]:
                outs.append(u[..., o:o + wdt])
                o += wdt
            return outs

        gq, gk, gv, gr, glr, hy = rest_split(u_rest)
        gqc, gkc, gvc, grc, glrc, hyc = rest_split(uc_rest)

        na_lat = _na_attention(u_na, uc_na, _na_bias_table(na_rpb[l]))

        q, k, v, la_f, la_b = _gla_inputs(gq, gk, gv, glr, gla_w_a2[l], gla_b_a2[l])
        qc_, kc_, vc_, lac_f, lac_b = _gla_inputs(gqc, gkc, gvc, glrc, gla_w_a2[l], gla_b_a2[l])
        zero_state = jnp.zeros((B, GLA_HEADS, GLA_DK, GLA_DV), F32)
        o_ctx, s_ctx_f, s_ctx_b = _gla_bidir(qc_, kc_, vc_, lac_f, lac_b, zero_state, zero_state)
        q = _rope_axial_2d(q, row_pos, col_pos)
        k = _rope_axial_2d(k, row_pos, col_pos)
        o_lat, _, _ = _gla_bidir(q, k, v, la_f, la_b, s_ctx_f, s_ctx_b)
        gla_lat = _gla_output(o_lat, gr, gla_norm_g[l])

        filt_args = (hy_w1[l], hy_b1[l], hy_f1[l], hy_w2[l], hy_b2[l], hy_f2[l], hy_w3[l], hy_b3[l], hy_decay[l])
        hy_lat = _hyena_mixer(hy, hy_short_w[l], hy_short_b[l], _hyena_filters(L, *filt_args), hy_skip[l])

        x, h_lat, logit_lat = _outproj(na_lat, gla_lat, hy_lat, x, lat(2), w_out_b, lg1, lb1, lat(4), lat(3),
                                       router_w[l], True)
        if not last:
            na_c = _ctx_attention(uc_na)
            gla_c = _gla_output(o_ctx, grc, gla_norm_g[l])
            hy_c = _hyena_mixer(hyc, hy_short_w[l], hy_short_b[l], _hyena_filters(C, *filt_args), hy_skip[l])
            xc, h_c, logit_c = _outproj(na_c, gla_c, hy_c, xc, cm(2), w_out_b, lg1, lb1, cm(4), cm(3),
                                        router_w[l], False)
            h_flat = jnp.concatenate([h_lat.reshape(B * L, D), h_c.reshape(B * C, D)], axis=0)
            logit_flat = jnp.concatenate([logit_lat.reshape(B * L, N_EXPERTS), logit_c.reshape(B * C, N_EXPERTS)], axis=0)
        else:
            h_flat = h_lat.reshape(B * L, D)
            logit_flat = logit_lat.reshape(B * L, N_EXPERTS)

        routed = _moe_routed(h_flat, logit_flat, router_b[l], we_gate, we_up, we_down, l)
        wsg, wsu, wsd = ws_gate[l].astype(BF16), ws_up[l].astype(BF16), ws_down[l].astype(BF16)
        x = _shared_ln2(x, h_lat, routed[:B * L].reshape(B, L, D), lat(5), wsg, wsu, wsd, lg2, lb2, True)
        if not last:
            xc = _shared_ln2(xc, h_c, routed[B * L:].reshape(B, C, D), cm(5), wsg, wsu, wsd, lg2, lb2, False)
    return x
```

```python
import functools
import math

import numpy as np
import jax
import jax.numpy as jnp
from jax import lax
from jax.experimental import pallas as pl
from jax.experimental.pallas import tpu as pltpu

F32 = jnp.float32
BF16 = jnp.bfloat16

D_MODEL = 1024
DEPTH = 4
GRID_W = 64
CTX_LEN = 256

NA_HEADS = 8
NA_HEAD_DIM = 64
NA_WIDTH = NA_HEADS * NA_HEAD_DIM
NA_WIN_ROWS = 8
NA_WIN_COLS = 16

GLA_HEADS = 4
GLA_DK = 32
GLA_DV = 64
GLA_KEY_WIDTH = GLA_HEADS * GLA_DK
GLA_WIDTH = GLA_HEADS * GLA_DV
GLA_RANK = 16
GLA_GATE_NORM = 16.0
GLA_CHUNK = 64

HY_CH = 256
HY_ORDER = 2
HY_SHORT = 3
HY_BANDS = 16
HY_EMB = 1 + 2 * HY_BANDS

MIX_WIDTH = NA_WIDTH + GLA_WIDTH + HY_CH
IN_SPLITS = (NA_WIDTH, NA_WIDTH, NA_WIDTH, GLA_KEY_WIDTH, GLA_KEY_WIDTH, GLA_WIDTH, GLA_WIDTH,
             2 * GLA_RANK, (HY_ORDER + 1) * HY_CH)
D_IN = sum(IN_SPLITS)
ROPE_BASE = 10000.0

N_EXPERTS = 128
TOP_K = 8
N_GROUPS = 8
TOPK_GROUPS = 4
EXPERT_HIDDEN = 256
ROUTED_SCALE = 2.5

DEEPNORM_ALPHA = (2 * DEPTH) ** 0.25
LN_EPS = 1e-6

LANE = 128
MXU_DIM = 256
VMEM_LIMIT = 48 * 1024 * 1024

NA_COLS = 3 * NA_WIDTH
D_IN_PAD = -(-D_IN // LANE) * LANE
REST_COLS = D_IN_PAD - NA_COLS
NA_QUAD = MXU_DIM // NA_HEAD_DIM
NEG_BIG = -1e30

TOK_TILE = 512
MOE_TILE = 256
NA_ROW_TILE = 8
ROUTE_TILE = 256
GLA_TILE = 512


def _cparams(*sem):
    return pltpu.CompilerParams(dimension_semantics=sem, vmem_limit_bytes=VMEM_LIMIT)


def _mod_kernel(c_ref, w_ref, b_ref, o_ref):
    c = c_ref[...]
    s = c * jax.nn.sigmoid(c)
    o_ref[0] = jnp.dot(s, w_ref[0], preferred_element_type=F32, precision=lax.Precision.HIGHEST) + b_ref[0]


def _modulation(cs, w_mod, b_mod):
    R = cs.shape[0]
    tn = 1536
    return pl.pallas_call(
        _mod_kernel,
        grid=(DEPTH, 6 * D_MODEL // tn),
        in_specs=[pl.BlockSpec((R, D_MODEL), lambda l, j: (0, 0)),
                  pl.BlockSpec((1, D_MODEL, tn), lambda l, j: (l, 0, j)),
                  pl.BlockSpec((1, 1, tn), lambda l, j: (l, 0, j))],
        out_specs=pl.BlockSpec((1, R, tn), lambda l, j: (l, 0, j)),
        out_shape=jax.ShapeDtypeStruct((DEPTH, R, 6 * D_MODEL), F32),
        compiler_params=_cparams("arbitrary", "arbitrary"),
        name="modulation",
    )(cs, w_mod, b_mod.reshape(DEPTH, 1, 6 * D_MODEL))


def _inproj_kernel(x_ref, sc_ref, sh_ref, w_ref, ona_ref, orest_ref):
    xm = (x_ref[0] * (1.0 + sc_ref[0]) + sh_ref[0]).astype(BF16)
    step = 512
    for c0 in range(0, NA_COLS, step):
        ona_ref[0, :, c0:c0 + step] = jnp.dot(xm, w_ref[:, c0:c0 + step], preferred_element_type=F32).astype(BF16)
    for c0 in range(0, REST_COLS, step):
        c1 = min(c0 + step, REST_COLS)
        orest_ref[0, :, c0:c1] = jnp.dot(xm, w_ref[:, NA_COLS + c0:NA_COLS + c1], preferred_element_type=F32)


def _inproj(x, sc, sh, w_pad, per_batch_mod):
    B, T, D = x.shape
    tm = min(TOK_TILE, T)
    mod_idx = (lambda b, i: (b, 0, 0)) if per_batch_mod else (lambda b, i: (0, 0, 0))
    return pl.pallas_call(
        _inproj_kernel,
        grid=(B, T // tm),
        in_specs=[pl.BlockSpec((1, tm, D), lambda b, i: (b, i, 0)),
                  pl.BlockSpec((1, 1, D), mod_idx),
                  pl.BlockSpec((1, 1, D), mod_idx),
                  pl.BlockSpec((D, D_IN_PAD), lambda b, i: (0, 0))],
        out_specs=[pl.BlockSpec((1, tm, NA_COLS), lambda b, i: (b, i, 0)),
                   pl.BlockSpec((1, tm, REST_COLS), lambda b, i: (b, i, 0))],
        out_shape=[jax.ShapeDtypeStruct((B, T, NA_COLS), BF16),
                   jax.ShapeDtypeStruct((B, T, REST_COLS), F32)],
        compiler_params=_cparams("arbitrary", "arbitrary"),
        name="inproj",
    )(x, sc, sh, w_pad)


def _stack_heads(q, n_rows):
    head = lax.broadcasted_iota(jnp.int32, (n_rows, MXU_DIM), 1) // NA_HEAD_DIM
    return jnp.concatenate([jnp.where(head == h, q, jnp.zeros_like(q)) for h in range(NA_QUAD)], axis=0)


def _unstack_heads(o, n_rows):
    head = lax.broadcasted_iota(jnp.int32, (n_rows, MXU_DIM), 1) // NA_HEAD_DIM
    out = jnp.zeros((n_rows, MXU_DIM), F32)
    for h in range(NA_QUAD):
        out = jnp.where(head == h, o[h * n_rows:(h + 1) * n_rows], out)
    return out


_NT = (((1,), (1,)), ((), ()))


def _na_kernel(q_ref, k_ref, v_ref, kc_ref, vc_ref, bias_ref, o_ref):
    rt = pl.program_id(2)
    scale = NA_HEAD_DIM ** -0.5
    kc = kc_ref[0]
    vc = vc_ref[0]
    n_loc = NA_WIN_ROWS * GRID_W

    def row(rl, carry):
        r = rt * NA_ROW_TILE + rl
        kr0 = jnp.clip(r - NA_WIN_ROWS // 2, 0, GRID_W - NA_WIN_ROWS)
        dr0 = kr0 - r + NA_WIN_ROWS - 1
        q = q_ref[0, pl.ds(pl.multiple_of(rl * GRID_W, GRID_W), GRID_W), :]
        qs = _stack_heads(q, GRID_W)
        k0 = pl.multiple_of(kr0 * GRID_W, GRID_W)
        ks = k_ref[0, pl.ds(k0, n_loc), :]
        vs = v_ref[0, pl.ds(k0, n_loc), :]
        s_loc = lax.dot_general(qs, ks, _NT, preferred_element_type=F32) * scale + bias_ref[0, dr0]
        s_ctx = lax.dot_general(qs, kc, _NT, preferred_element_type=F32) * scale
        m = jnp.maximum(jnp.max(s_loc, axis=-1, keepdims=True), jnp.max(s_ctx, axis=-1, keepdims=True))
        p_loc = jnp.exp(s_loc - m)
        p_ctx = jnp.exp(s_ctx - m)
        den = jnp.sum(p_loc, axis=-1, keepdims=True) + jnp.sum(p_ctx, axis=-1, keepdims=True)
        o = (jnp.dot(p_loc.astype(BF16), vs, preferred_element_type=F32)
             + jnp.dot(p_ctx.astype(BF16), vc, preferred_element_type=F32)) / den
        o_ref[0, pl.ds(pl.multiple_of(rl * GRID_W, GRID_W), GRID_W), :] = _unstack_heads(o, GRID_W)
        return carry

    lax.fori_loop(0, NA_ROW_TILE, row, 0)


def _na_bias_table(rpb):
    c = np.arange(GRID_W)
    kc0 = np.clip(c - NA_WIN_COLS // 2, 0, GRID_W - NA_WIN_COLS)
    kc = np.arange(GRID_W)
    valid = (kc[None, :] >= kc0[:, None]) & (kc[None, :] < kc0[:, None] + NA_WIN_COLS)
    dc = np.clip(kc[None, :] - c[:, None] + NA_WIN_COLS - 1, 0, 2 * NA_WIN_COLS - 2)
    dr = np.arange(NA_WIN_ROWS)[:, None] + np.arange(NA_WIN_ROWS)[None, :]
    tab = rpb[:, dr[:, :, None, None], dc[None, None, :, :]]
    tab = jnp.where(valid[None, None, None], tab, NEG_BIG)
    tab = tab.transpose(0, 1, 3, 2, 4).reshape(NA_HEADS, NA_WIN_ROWS, GRID_W, NA_WIN_ROWS * GRID_W)
    tab = tab.reshape(NA_HEADS // NA_QUAD, NA_QUAD, NA_WIN_ROWS, GRID_W, NA_WIN_ROWS * GRID_W)
    return tab.transpose(0, 2, 1, 3, 4).reshape(NA_HEADS // NA_QUAD, NA_WIN_ROWS, NA_QUAD * GRID_W,
                                                NA_WIN_ROWS * GRID_W).astype(F32)


def _na_attention(u_na, uc_na, bias_tab):
    B, L, _ = u_na.shape
    C = uc_na.shape[1]
    nq = NA_WIDTH // MXU_DIM
    tq = NA_ROW_TILE * GRID_W
    return pl.pallas_call(
        _na_kernel,
        grid=(B, nq, L // tq),
        in_specs=[pl.BlockSpec((1, tq, MXU_DIM), lambda b, j, i: (b, i, j)),
                  pl.BlockSpec((1, L, MXU_DIM), lambda b, j, i: (b, 0, nq + j)),
                  pl.BlockSpec((1, L, MXU_DIM), lambda b, j, i: (b, 0, 2 * nq + j)),
                  pl.BlockSpec((1, C, MXU_DIM), lambda b, j, i: (b, 0, nq + j)),
                  pl.BlockSpec((1, C, MXU_DIM), lambda b, j, i: (b, 0, 2 * nq + j)),
                  pl.BlockSpec((1, NA_WIN_ROWS, NA_QUAD * GRID_W, NA_WIN_ROWS * GRID_W), lambda b, j, i: (j, 0, 0, 0))],
        out_specs=pl.BlockSpec((1, tq, MXU_DIM), lambda b, j, i: (b, i, j)),
        out_shape=jax.ShapeDtypeStruct((B, L, NA_WIDTH), F32),
        compiler_params=_cparams("arbitrary", "arbitrary", "arbitrary"),
        name="na_attention",
    )(u_na, u_na, u_na, uc_na, uc_na, bias_tab)


def _ctx_attn_kernel(q_ref, k_ref, v_ref, o_ref):
    C = q_ref.shape[1]
    qs = _stack_heads(q_ref[0], C)
    s = lax.dot_general(qs, k_ref[0], _NT, preferred_element_type=F32) * NA_HEAD_DIM ** -0.5
    p = jnp.exp(s - jnp.max(s, axis=-1, keepdims=True))
    den = jnp.sum(p, axis=-1, keepdims=True)
    o = jnp.dot(p.astype(BF16), v_ref[0], preferred_element_type=F32) / den
    o_ref[0] = _unstack_heads(o, C)


def _ctx_attention(uc_na):
    B, C, _ = uc_na.shape
    nq = NA_WIDTH // MXU_DIM
    return pl.pallas_call(
        _ctx_attn_kernel,
        grid=(B, nq),
        in_specs=[pl.BlockSpec((1, C, MXU_DIM), lambda b, j: (b, 0, j)),
                  pl.BlockSpec((1, C, MXU_DIM), lambda b, j: (b, 0, nq + j)),
                  pl.BlockSpec((1, C, MXU_DIM), lambda b, j: (b, 0, 2 * nq + j))],
        out_specs=pl.BlockSpec((1, C, MXU_DIM), lambda b, j: (b, 0, j)),
        out_shape=jax.ShapeDtypeStruct((B, C, NA_WIDTH), F32),
        compiler_params=_cparams("arbitrary", "arbitrary"),
        name="ctx_attention",
    )(uc_na, uc_na, uc_na)


def _layer_norm_rows(y, g, b):
    mu = jnp.mean(y, axis=-1, keepdims=True)
    d = y - mu
    var = jnp.mean(d * d, axis=-1, keepdims=True)
    return d * lax.rsqrt(var + LN_EPS) * g + b


def _outproj_kernel(na_ref, gla_ref, hy_ref, x_ref, g1_ref, w_ref, lg_ref, lb_ref, sc_ref, sh_ref, wr_ref,
                    xo_ref, h_ref, logit_ref):
    mix = (jnp.dot(na_ref[0].astype(BF16), w_ref[0:NA_WIDTH, :], preferred_element_type=F32)
           + jnp.dot(gla_ref[0].astype(BF16), w_ref[NA_WIDTH:NA_WIDTH + GLA_WIDTH, :], preferred_element_type=F32)
           + jnp.dot(hy_ref[0].astype(BF16), w_ref[NA_WIDTH + GLA_WIDTH:, :], preferred_element_type=F32))
    xn = _layer_norm_rows(DEEPNORM_ALPHA * x_ref[0] + g1_ref[0] * mix, lg_ref[...], lb_ref[...])
    xo_ref[0] = xn
    h = xn * (1.0 + sc_ref[0]) + sh_ref[0]
    h_ref[0] = h.astype(BF16)
    logit_ref[...] = lax.dot_general(wr_ref[...], h, _NT, preferred_element_type=F32, precision=lax.Precision.HIGHEST)


def _outproj(na, gla, hy, x, g1, w_out, ln_g, ln_b, sc2, sh2, w_router, per_batch_mod):
    B, T, D = x.shape
    tm = min(TOK_TILE, T)
    mod_idx = (lambda b, i: (b, 0, 0)) if per_batch_mod else (lambda b, i: (0, 0, 0))
    tok = lambda w: pl.BlockSpec((1, tm, w), lambda b, i: (b, i, 0))
    full = lambda s: pl.BlockSpec(s, lambda b, i: (0,) * len(s))
    mod = pl.BlockSpec((1, 1, D), mod_idx)
    return pl.pallas_call(
        _outproj_kernel,
        grid=(B, T // tm),
        in_specs=[tok(NA_WIDTH), tok(GLA_WIDTH), tok(HY_CH), tok(D), mod, full((MIX_WIDTH, D)),
                  full((1, D)), full((1, D)), mod, mod, full((N_EXPERTS, D))],
        out_specs=[tok(D), tok(D), pl.BlockSpec((N_EXPERTS, tm), lambda b, i: (0, b * (T // tm) + i))],
        out_shape=[jax.ShapeDtypeStruct((B, T, D), F32), jax.ShapeDtypeStruct((B, T, D), BF16),
                   jax.ShapeDtypeStruct((N_EXPERTS, B * T), F32)],
        compiler_params=_cparams("arbitrary", "arbitrary"),
        name="outproj_ln1",
    )(na, gla, hy, x, g1, w_out, ln_g, ln_b, sc2, sh2, w_router)


def _moe_kernel(te_ref, nt_ref, xs_ref, wg_ref, wu_ref, wd_ref, ys_ref):
    i = pl.program_id(0)

    @pl.when(i < nt_ref[0])
    def _():
        xs = xs_ref[...]
        g = jnp.dot(xs, wg_ref[0].astype(BF16), preferred_element_type=F32)
        u = jnp.dot(xs, wu_ref[0].astype(BF16), preferred_element_type=F32)
        a = (g * jax.nn.sigmoid(g) * u).astype(BF16)
        ys_ref[...] = jnp.dot(a, wd_ref[0].astype(BF16), preferred_element_type=F32)

    @pl.when(i >= nt_ref[0])
    def _():
        ys_ref[...] = jnp.zeros_like(ys_ref)


def _moe_grouped(tile_expert, n_used, xs, we_g, we_u, we_d, layer):
    Mp, D = xs.shape
    n_tiles = Mp // MOE_TILE
    H = EXPERT_HIDDEN
    grid_spec = pltpu.PrefetchScalarGridSpec(
        num_scalar_prefetch=2,
        grid=(n_tiles,),
        in_specs=[pl.BlockSpec((MOE_TILE, D), lambda i, te, nt: (i, 0)),
                  pl.BlockSpec((None, 1, D, H), lambda i, te, nt: (layer, te[i], 0, 0)),
                  pl.BlockSpec((None, 1, D, H), lambda i, te, nt: (layer, te[i], 0, 0)),
                  pl.BlockSpec((None, 1, H, D), lambda i, te, nt: (layer, te[i], 0, 0))],
        out_specs=pl.BlockSpec((MOE_TILE, D), lambda i, te, nt: (i, 0)),
    )
    return pl.pallas_call(
        _moe_kernel,
        grid_spec=grid_spec,
        out_shape=jax.ShapeDtypeStruct((Mp, D), F32),
        compiler_params=_cparams("arbitrary"),
        name="moe_experts",
    )(tile_expert, n_used, xs, we_g, we_u, we_d)


def _shared_kernel(x_ref, h_ref, r_ref, g2_ref, wg_ref, wu_ref, wd_ref, lg_ref, lb_ref, o_ref):
    h = h_ref[0]
    g = jnp.dot(h, wg_ref[...], preferred_element_type=F32)
    u = jnp.dot(h, wu_ref[...], preferred_element_type=F32)
    a = (g * jax.nn.sigmoid(g) * u).astype(BF16)
    ff = jnp.dot(a, wd_ref[...], preferred_element_type=F32) + r_ref[0]
    o_ref[0] = _layer_norm_rows(DEEPNORM_ALPHA * x_ref[0] + g2_ref[0] * ff, lg_ref[...], lb_ref[...])


def _shared_ln2(x, h, routed, g2, ws_g, ws_u, ws_d, ln_g, ln_b, per_batch_mod):
    B, T, D = x.shape
    tm = min(TOK_TILE, T)
    mod_idx = (lambda b, i: (b, 0, 0)) if per_batch_mod else (lambda b, i: (0, 0, 0))
    tok = pl.BlockSpec((1, tm, D), lambda b, i: (b, i, 0))
    full = lambda s: pl.BlockSpec(s, lambda b, i: (0,) * len(s))
    return pl.pallas_call(
        _shared_kernel,
        grid=(B, T // tm),
        in_specs=[tok, tok, tok, pl.BlockSpec((1, 1, D), mod_idx), full((D, EXPERT_HIDDEN)), full((D, EXPERT_HIDDEN)),
                  full((EXPERT_HIDDEN, D)), full((1, D)), full((1, D))],
        out_specs=tok,
        out_shape=jax.ShapeDtypeStruct((B, T, D), F32),
        compiler_params=_cparams("arbitrary", "arbitrary"),
        name="shared_ln2",
    )(x, h, routed, g2, ws_g, ws_u, ws_d, ln_g, ln_b)


def _first_max(vals, iota, n):
    m = jnp.max(vals, axis=0, keepdims=True)
    idx = jnp.min(jnp.where(vals == m, iota, n), axis=0, keepdims=True)
    return m, idx


def _route_kernel(lt_ref, b_ref, eidx_ref, w_ref, rank_ref, cnt_ref, base_ref):
    @pl.when(pl.program_id(0) == 0)
    def _():
        base_ref[...] = jnp.zeros_like(base_ref)

    tm = lt_ref.shape[1]
    per = N_EXPERTS // N_GROUPS
    s = jax.nn.sigmoid(lt_ref[...])
    sel = s + b_ref[...]
    io_g = lax.broadcasted_iota(jnp.int32, (per, tm), 0)
    scores = []
    for g in range(N_GROUPS):
        blk = sel[g * per:(g + 1) * per]
        m1, i1 = _first_max(blk, io_g, per)
        m2 = jnp.max(jnp.where(io_g == i1, -jnp.inf, blk), axis=0, keepdims=True)
        scores.append(m1 + m2)
    cur = jnp.concatenate(scores, axis=0)
    io_8 = lax.broadcasted_iota(jnp.int32, (N_GROUPS, tm), 0)
    gmask = jnp.zeros((N_GROUPS, tm), F32)
    for _ in range(TOPK_GROUPS):
        _, gi = _first_max(cur, io_8, N_GROUPS)
        hit = io_8 == gi
        gmask = jnp.where(hit, 1.0, gmask)
        cur = jnp.where(hit, -jnp.inf, cur)
    masked = jnp.concatenate(
        [jnp.where(gmask[g:g + 1] > 0.0, sel[g * per:(g + 1) * per], -jnp.inf) for g in range(N_GROUPS)], axis=0)
    io_e = lax.broadcasted_iota(jnp.int32, (N_EXPERTS, tm), 0)
    chosen = jnp.zeros((N_EXPERTS, tm), F32)
    eidx, gates = [], []
    for _ in range(TOP_K):
        _, ei = _first_max(masked, io_e, N_EXPERTS)
        hit = io_e == ei
        eidx.append(ei)
        gates.append(jnp.sum(jnp.where(hit, s, 0.0), axis=0, keepdims=True))
        masked = jnp.where(hit, -jnp.inf, masked)
        chosen = jnp.where(hit, 1.0, chosen)
    wk = jnp.concatenate(gates, axis=0)
    w_ref[...] = wk / jnp.sum(wk, axis=0, keepdims=True) * ROUTED_SCALE
    eidx_ref[...] = jnp.concatenate(eidx, axis=0)
    earlier = (lax.broadcasted_iota(jnp.int32, (tm, tm), 0) < lax.broadcasted_iota(jnp.int32, (tm, tm), 1))
    pos = jnp.dot(chosen.astype(BF16), jnp.where(earlier, 1.0, 0.0).astype(BF16), preferred_element_type=F32)
    pos = pos + base_ref[...]
    ranks = [jnp.sum(jnp.where(io_e == eidx[k], pos, 0.0), axis=0, keepdims=True) for k in range(TOP_K)]
    rank_ref[...] = jnp.concatenate(ranks, axis=0).astype(jnp.int32)
    base_ref[...] = base_ref[...] + jnp.sum(chosen, axis=1, keepdims=True)
    cnt_ref[...] = base_ref[...]


def _route(logits_t, b_corr):
    E, T = logits_t.shape
    tm = ROUTE_TILE
    tokk = pl.BlockSpec((TOP_K, tm), lambda i: (0, i))
    return pl.pallas_call(
        _route_kernel,
        grid=(T // tm,),
        in_specs=[pl.BlockSpec((E, tm), lambda i: (0, i)), pl.BlockSpec((E, 1), lambda i: (0, 0))],
        out_specs=[tokk, tokk, tokk, pl.BlockSpec((E, 1), lambda i: (0, 0))],
        out_shape=[jax.ShapeDtypeStruct((TOP_K, T), jnp.int32), jax.ShapeDtypeStruct((TOP_K, T), F32),
                   jax.ShapeDtypeStruct((TOP_K, T), jnp.int32), jax.ShapeDtypeStruct((E, 1), F32)],
        scratch_shapes=[pltpu.VMEM((E, 1), F32)],
        compiler_params=_cparams("arbitrary"),
        name="route",
    )(logits_t, b_corr.reshape(E, 1))


def _slot_kernel(eidx_ref, rank_ref, pstart_ref, dest_ref):
    tm = eidx_ref.shape[1]
    io_e = lax.broadcasted_iota(jnp.int32, (N_EXPERTS, tm), 0)
    ei = eidx_ref[...]
    starts = [jnp.sum(jnp.where(io_e == ei[k:k + 1], pstart_ref[...], 0.0), axis=0, keepdims=True)
              for k in range(TOP_K)]
    dest_ref[...] = jnp.concatenate(starts, axis=0).astype(jnp.int32) + rank_ref[...]


def _slots(eidx, rank, pstart):
    K, T = eidx.shape
    tm = ROUTE_TILE
    tokk = pl.BlockSpec((K, tm), lambda i: (0, i))
    return pl.pallas_call(
        _slot_kernel,
        grid=(T // tm,),
        in_specs=[tokk, tokk, pl.BlockSpec((N_EXPERTS, 1), lambda i: (0, 0))],
        out_specs=tokk,
        out_shape=jax.ShapeDtypeStruct((K, T), jnp.int32),
        compiler_params=_cparams("arbitrary"),
        name="route_slots",
    )(eidx, rank, pstart)


def _moe_routed(h_flat, logits_t, b_corr, we_g, we_u, we_d, layer):
    T = h_flat.shape[0]
    n_tiles = T * TOP_K // MOE_TILE + N_EXPERTS
    eidx, gates, rank, counts = _route(logits_t, b_corr)
    counts = counts[:, 0].astype(jnp.int32)
    padded = (counts + MOE_TILE - 1) // MOE_TILE * MOE_TILE
    pends = jnp.cumsum(padded)
    tile_expert = jnp.minimum(jnp.searchsorted(pends, jnp.arange(n_tiles, dtype=jnp.int32) * MOE_TILE, side='right'),
                              N_EXPERTS - 1).astype(jnp.int32)
    n_used = (pends[-1] // MOE_TILE).astype(jnp.int32).reshape(1)
    dest = _slots(eidx, rank, (pends - padded).astype(F32).reshape(N_EXPERTS, 1))
    tok = jnp.broadcast_to(jnp.arange(T, dtype=jnp.int32)[None], (TOP_K, T))
    src_tok = jnp.zeros((n_tiles * MOE_TILE,), jnp.int32).at[dest.reshape(-1)].set(
        tok.reshape(-1), unique_indices=True, mode='promise_in_bounds')
    xs = jnp.take(h_flat, src_tok, axis=0)
    ys = _moe_grouped(tile_expert, n_used, xs, we_g, we_u, we_d, layer)
    picked = jnp.take(ys, dest.T, axis=0)
    return jnp.sum(picked * gates.T[:, :, None], axis=1)


_GLA_QK_BLK, _GLA_V_BLK, _GLA_R_BLK = 0, 1, 2
_GLA_LR_BLK = (2 * GLA_KEY_WIDTH + 2 * GLA_WIDTH) // LANE


def _rope_tables(L):
    t = np.arange(L)
    lane = np.arange(GLA_KEY_WIDTH)
    d = lane % GLA_DK
    pos = np.where(d[None, :] < GLA_DK // 2, (t // GRID_W)[:, None], (t % GRID_W)[:, None]).astype(np.float32)
    quarter = GLA_DK // 4
    inv = ROPE_BASE ** (-jnp.arange(quarter, dtype=F32) / quarter)
    ang = jnp.asarray(pos) * inv[jnp.asarray(d % quarter)][None, :]
    sign = np.where(d % (2 * quarter) < quarter, -1.0, 1.0).astype(np.float32)
    return jnp.cos(ang), jnp.sin(ang) * sign[None, :]


def _rope_partner(x):
    lane = lax.broadcasted_iota(jnp.int32, x.shape, 1)
    quarter = GLA_DK // 4
    return jnp.where(lane % (2 * quarter) < quarter, pltpu.roll(x, GLA_KEY_WIDTH - quarter, 1), pltpu.roll(x, quarter, 1))


def _log_sigmoid(x):
    return jnp.minimum(x, 0.0) - jnp.log(1.0 + jnp.exp(-jnp.abs(x)))


def _gla_kernel(reverse, finalize, *refs):
    if finalize:
        (qk_ref, v_ref, lr_ref, cos_ref, sin_ref, wa_ref, ba_ref, s0_ref, of_ref, r_ref, g_ref,
         o_ref, sfin_ref, s_scr) = refs
    else:
        qk_ref, v_ref, lr_ref, cos_ref, sin_ref, wa_ref, ba_ref, s0_ref, o_ref, sfin_ref, s_scr = refs
    hi = lax.Precision.HIGHEST

    @pl.when(pl.program_id(1) == 0)
    def _():
        s_scr[...] = s0_ref[0]

    tg = qk_ref.shape[1]
    C = GLA_CHUNK
    KW, VW = GLA_KEY_WIDTH, GLA_WIDTH
    qk = qk_ref[0]
    cos, sin = cos_ref[...], sin_ref[...]
    q = qk[:, :KW] * GLA_DK ** -0.5
    k = qk[:, KW:]
    q = q * cos + _rope_partner(q) * sin
    k = k * cos + _rope_partner(k) * sin
    v = v_ref[0]
    logit = jnp.dot(lr_ref[0], wa_ref[...], preferred_element_type=F32, precision=hi) + ba_ref[...]
    la = _log_sigmoid(logit) / GLA_GATE_NORM

    ri = lax.broadcasted_iota(jnp.int32, (C, C), 0)
    ci = lax.broadcasted_iota(jnp.int32, (C, C), 1)
    tri = jnp.where((ci >= ri) if reverse else (ci <= ri), 1.0, 0.0)
    tri_h = jnp.concatenate([tri] * GLA_HEADS, axis=0)
    head_k = lax.broadcasted_iota(jnp.int32, (C, KW), 1) // GLA_DK
    head_v = lax.broadcasted_iota(jnp.int32, (C, VW), 1) // GLA_DV
    own_block = (lax.broadcasted_iota(jnp.int32, (KW, VW), 0) // GLA_DK
                 == lax.broadcasted_iota(jnp.int32, (KW, VW), 1) // GLA_DV)
    eye = lax.broadcasted_iota(jnp.int32, (KW, KW), 0) == lax.broadcasted_iota(jnp.int32, (KW, KW), 1)

    S = s_scr[...]
    nc = tg // C
    outs = [None] * nc
    for c in (range(nc - 1, -1, -1) if reverse else range(nc)):
        sl = slice(c * C, (c + 1) * C)
        b = jnp.dot(tri, la[sl], preferred_element_type=F32, precision=hi)
        qt = q[sl] * jnp.exp(b)
        kt = k[sl] * jnp.exp(-b)
        qs = jnp.concatenate([jnp.where(head_k == h, qt, 0.0) for h in range(GLA_HEADS)], axis=0).astype(BF16)
        att = lax.dot_general(qs, kt.astype(BF16), _NT, preferred_element_type=F32)
        att = jnp.where(tri_h > 0.0, att, 0.0)
        vb = v[sl].astype(BF16)
        oi = jnp.dot(att.astype(BF16), vb, preferred_element_type=F32)
        o_intra = jnp.zeros((C, VW), F32)
        for h in range(GLA_HEADS):
            o_intra = jnp.where(head_v == h, oi[h * C:(h + 1) * C], o_intra)
        o_inter = jnp.dot(qt.astype(BF16), S.astype(BF16), preferred_element_type=F32)
        outs[c] = o_intra + o_inter
        b_last = b[0:1] if reverse else b[C - 1:C]
        kdec = (k[sl] * jnp.exp(b_last - b)).astype(BF16)
        kv = lax.dot_general(kdec, vb, (((0,), (0,)), ((), ())), preferred_element_type=F32)
        decay_col = jnp.sum(jnp.where(eye, jnp.exp(b_last), 0.0), axis=1, keepdims=True)
        S = decay_col * S + jnp.where(own_block, kv, 0.0)
    s_scr[...] = S
    sfin_ref[0] = S
    o = jnp.concatenate(outs, axis=0)
    if finalize:
        o = of_ref[0] + o
        same_head = (lax.broadcasted_iota(jnp.int32, (VW, VW), 0) // GLA_DV
                     == lax.broadcasted_iota(jnp.int32, (VW, VW), 1) // GLA_DV)
        ms = jnp.dot(o * o, jnp.where(same_head, 1.0 / GLA_DV, 0.0), preferred_element_type=F32, precision=hi)
        r = r_ref[0]
        o = o * lax.rsqrt(ms + LN_EPS) * g_ref[...] * (r * jax.nn.sigmoid(r))
    o_ref[0] = o


def _gla_pass(u_rest, cos, sin, wa, ba, s0, reverse, fin=None):
    B, T, _ = u_rest.shape
    tg = min(GLA_TILE, T)
    n = T // tg
    ti = (lambda i: n - 1 - i) if reverse else (lambda i: i)
    KW, VW = GLA_KEY_WIDTH, GLA_WIDTH
    ublk = lambda w, j: pl.BlockSpec((1, tg, w), lambda b, i: (b, ti(i), j))
    full = lambda s: pl.BlockSpec(s, lambda b, i: (0,) * len(s))
    state = pl.BlockSpec((1, KW, VW), lambda b, i: (b, 0, 0))
    tab = pl.BlockSpec((tg, KW), lambda b, i: (ti(i), 0))
    in_specs = [ublk(2 * KW, _GLA_QK_BLK), ublk(VW, _GLA_V_BLK), ublk(LANE, _GLA_LR_BLK), tab, tab,
                full((LANE, KW)), full((1, KW)), state]
    args = [u_rest, u_rest, u_rest, cos, sin, wa, ba, s0]
    if fin is not None:
        in_specs += [ublk(VW, 0), ublk(VW, _GLA_R_BLK), full((1, VW))]
        args += [fin[0], u_rest, fin[1]]
    return pl.pallas_call(
        functools.partial(_gla_kernel, reverse, fin is not None),
        grid=(B, n),
        in_specs=in_specs,
        out_specs=[ublk(VW, 0), state],
        out_shape=[jax.ShapeDtypeStruct((B, T, VW), F32), jax.ShapeDtypeStruct((B, KW, VW), F32)],
        scratch_shapes=[pltpu.VMEM((KW, VW), F32)],
        compiler_params=_cparams("arbitrary", "arbitrary"),
        name="gla_bwd" if reverse else "gla_fwd",
    )(*args)


def _gla_bidir(u_rest, cos, sin, w_a2, b_a2, norm_g, s0_f, s0_b):
    def decay_w(d):
        return jnp.zeros((LANE, GLA_KEY_WIDTH), F32).at[d * GLA_RANK:(d + 1) * GLA_RANK].set(w_a2[d])

    o_f, s_f = _gla_pass(u_rest, cos, sin, decay_w(0), b_a2[0:1], s0_f, False)
    g = jnp.tile(norm_g, GLA_HEADS).reshape(1, GLA_WIDTH)
    o, s_b = _gla_pass(u_rest, cos, sin, decay_w(1), b_a2[1:2], s0_b, True, (o_f, g))
    return o, s_f, s_b


def _short_conv(u, w, b):
    L = u.shape[1]
    pad = HY_SHORT // 2
    up = jnp.pad(u, ((0, 0), (pad, HY_SHORT - 1 - pad), (0, 0)))
    y = b
    for j in range(HY_SHORT):
        y = y + up[:, j:j + L] * w[j]
    return y


def _hyena_filters(L, w1, b1, f1, w2, b2, f2, w3, b3, decay):
    t = jnp.linspace(0.0, 1.0, L, dtype=F32)
    w = 2.0 * math.pi * jnp.arange(L, dtype=F32) / L
    bands = jnp.linspace(1e-4, HY_BANDS - 1, HY_BANDS, dtype=F32)
    ang = w[:, None] * bands[None]
    z = jnp.concatenate([t[:, None], jnp.cos(ang), -jnp.sin(ang)], -1)
    h = jnp.sin(f1 * (z @ w1 + b1))
    h = jnp.sin(f2 * (h @ w2 + b2))
    h = (h @ w3 + b3).reshape(L, HY_ORDER, 2, HY_CH)
    window = jnp.exp(-t[:, None, None, None] * jnp.abs(decay)[None])
    return h * window


def _bidir_long_conv(u, h_fwd, h_bwd, d_skip):
    B, L, C = u.shape
    kern = jnp.concatenate([h_fwd, jnp.zeros((1, C), h_fwd.dtype), h_bwd[:0:-1]], axis=0)
    kf = jnp.fft.rfft(kern, n=2 * L, axis=0)
    uf = jnp.fft.rfft(u, n=2 * L, axis=1)
    y = jnp.fft.irfft(uf * kf[None], n=2 * L, axis=1)[:, :L]
    return y + u * d_skip


def _hyena_mixer(hu, short_w, short_b, filt, skip):
    hu = _short_conv(hu, short_w, short_b)
    parts = jnp.split(hu, HY_ORDER + 1, axis=-1)
    z = parts[0]
    for o in range(HY_ORDER):
        z = parts[o + 1] * _bidir_long_conv(z, filt[:, o, 0], filt[:, o, 1], skip[o])
    return z


def kernel(x, c, ctx, c_ctx, w_mod, b_mod, w_in, na_rpb, gla_w_a2, gla_b_a2, gla_norm_g, hy_short_w, hy_short_b, hy_w1, hy_b1, hy_f1, hy_w2, hy_b2, hy_f2, hy_w3, hy_b3, hy_decay, hy_skip, w_out, ln1_g, ln1_b, router_w, router_b, we_gate, we_up, we_down, ws_gate, ws_up, ws_down, ln2_g, ln2_b):
    B, L, D = x.shape
    C = ctx.shape[1]
    rope_cos, rope_sin = _rope_tables(L)
    ctx_cos, ctx_sin = jnp.ones((C, GLA_KEY_WIDTH), F32), jnp.zeros((C, GLA_KEY_WIDTH), F32)
    zero_state = jnp.zeros((B, GLA_KEY_WIDTH, GLA_WIDTH), F32)

    n_mod = -(-(B + 1) // 8) * 8
    cs = jnp.zeros((n_mod, D), F32).at[:B].set(c).at[B].set(c_ctx)
    mod_all = _modulation(cs, w_mod, b_mod)

    xc = ctx
    for l in range(DEPTH):
        last = l == DEPTH - 1
        mods = mod_all[l].reshape(n_mod, 6, 1, D)
        lat = lambda j: mods[:B, j]
        cm = lambda j: mods[B:B + 1, j]
        w_pad = jnp.pad(w_in[l], ((0, 0), (0, D_IN_PAD - D_IN))).astype(BF16)
        w_out_b = w_out[l].astype(BF16)
        lg1, lb1 = ln1_g[l].reshape(1, D), ln1_b[l].reshape(1, D)
        lg2, lb2 = ln2_g[l].reshape(1, D), ln2_b[l].reshape(1, D)

        u_na, u_rest = _inproj(x, lat(1), lat(0), w_pad, True)
        uc_na, uc_rest = _inproj(xc, cm(1), cm(0), w_pad, False)

        hy_off = sum(IN_SPLITS[3:-1])
        hy = u_rest[..., hy_off:hy_off + IN_SPLITS[-1]]
        hyc = uc_rest[..., hy_off:hy_off + IN_SPLITS[-1]]

        na_lat = _na_attention(u_na, uc_na, _na_bias_table(na_rpb[l]))

        gla_c, s_ctx_f, s_ctx_b = _gla_bidir(uc_rest, ctx_cos, ctx_sin, gla_w_a2[l], gla_b_a2[l], gla_norm_g[l],
                                             zero_state, zero_state)
        gla_lat, _, _ = _gla_bidir(u_rest, rope_cos, rope_sin, gla_w_a2[l], gla_b_a2[l], gla_norm_g[l],
                                   s_ctx_f, s_ctx_b)

        filt_args = (hy_w1[l], hy_b1[l], hy_f1[l], hy_w2[l], hy_b2[l], hy_f2[l], hy_w3[l], hy_b3[l], hy_decay[l])
        hy_lat = _hyena_mixer(hy, hy_short_w[l], hy_short_b[l], _hyena_filters(L, *filt_args), hy_skip[l])

        wr_t = router_w[l].T
        x, h_lat, logit_lat = _outproj(na_lat, gla_lat, hy_lat, x, lat(2), w_out_b, lg1, lb1, lat(4), lat(3), wr_t, True)
        if not last:
            na_c = _ctx_attention(uc_na)
            hy_c = _hyena_mixer(hyc, hy_short_w[l], hy_short_b[l], _hyena_filters(C, *filt_args), hy_skip[l])
            xc, h_c, logit_c = _outproj(na_c, gla_c, hy_c, xc, cm(2), w_out_b, lg1, lb1, cm(4), cm(3), wr_t, False)
            h_flat = jnp.concatenate([h_lat.reshape(B * L, D), h_c.reshape(B * C, D)], axis=0)
            logit_t = jnp.concatenate([logit_lat, logit_c], axis=1)
        else:
            h_flat = h_lat.reshape(B * L, D)
            logit_t = logit_lat

        routed = _moe_routed(h_flat, logit_t, router_b[l], we_gate, we_up, we_down, l)
        wsg, wsu, wsd = ws_gate[l].astype(BF16), ws_up[l].astype(BF16), ws_down[l].astype(BF16)
        x = _shared_ln2(x, h_lat, routed[:B * L].reshape(B, L, D), lat(5), wsg, wsu, wsd, lg2, lb2, True)
        if not last:
            xc = _shared_ln2(xc, h_c, routed[B * L:].reshape(B, C, D), cm(5), wsg, wsu, wsd, lg2, lb2, False)
    return x
```

```python
import functools
import math

import numpy as np
import jax
import jax.numpy as jnp
from jax import lax
from jax.experimental import pallas as pl
from jax.experimental.pallas import tpu as pltpu

F32 = jnp.float32
BF16 = jnp.bfloat16

D_MODEL = 1024
DEPTH = 4
GRID_W = 64
CTX_LEN = 256

NA_HEADS = 8
NA_HEAD_DIM = 64
NA_WIDTH = NA_HEADS * NA_HEAD_DIM
NA_WIN_ROWS = 8
NA_WIN_COLS = 16

GLA_HEADS = 4
GLA_DK = 32
GLA_DV = 64
GLA_KEY_WIDTH = GLA_HEADS * GLA_DK
GLA_WIDTH = GLA_HEADS * GLA_DV
GLA_RANK = 16
GLA_GATE_NORM = 16.0
GLA_CHUNK = 64

HY_CH = 256
HY_ORDER = 2
HY_SHORT = 3
HY_BANDS = 16
HY_EMB = 1 + 2 * HY_BANDS

MIX_WIDTH = NA_WIDTH + GLA_WIDTH + HY_CH
IN_SPLITS = (NA_WIDTH, NA_WIDTH, NA_WIDTH, GLA_KEY_WIDTH, GLA_KEY_WIDTH, GLA_WIDTH, GLA_WIDTH,
             2 * GLA_RANK, (HY_ORDER + 1) * HY_CH)
D_IN = sum(IN_SPLITS)
ROPE_BASE = 10000.0

N_EXPERTS = 128
TOP_K = 8
N_GROUPS = 8
TOPK_GROUPS = 4
EXPERT_HIDDEN = 256
ROUTED_SCALE = 2.5

DEEPNORM_ALPHA = (2 * DEPTH) ** 0.25
LN_EPS = 1e-6

LANE = 128
MXU_DIM = 256
VMEM_LIMIT = 48 * 1024 * 1024

NA_COLS = 3 * NA_WIDTH
GLA_COLS = sum(IN_SPLITS[3:8])
HY_COLS = IN_SPLITS[8]
REST_COLS = -(-GLA_COLS // LANE) * LANE
D_IN_PAD = NA_COLS + REST_COLS
NA_QUAD = MXU_DIM // NA_HEAD_DIM
NEG_BIG = -1e30

TOK_TILE = 512
MOE_TILE = 256
NA_ROW_TILE = 8
ROUTE_TILE = 256
GLA_TILE = 512


def _cparams(*sem):
    return pltpu.CompilerParams(dimension_semantics=sem, vmem_limit_bytes=VMEM_LIMIT)


def _mod_kernel(c_ref, w_ref, b_ref, o_ref):
    c = c_ref[...]
    s = c * jax.nn.sigmoid(c)
    o_ref[0] = jnp.dot(s, w_ref[0], preferred_element_type=F32, precision=lax.Precision.HIGHEST) + b_ref[0]


def _modulation(cs, w_mod, b_mod):
    R = cs.shape[0]
    tn = 1536
    return pl.pallas_call(
        _mod_kernel,
        grid=(DEPTH, 6 * D_MODEL // tn),
        in_specs=[pl.BlockSpec((R, D_MODEL), lambda l, j: (0, 0)),
                  pl.BlockSpec((1, D_MODEL, tn), lambda l, j: (l, 0, j)),
                  pl.BlockSpec((1, 1, tn), lambda l, j: (l, 0, j))],
        out_specs=pl.BlockSpec((1, R, tn), lambda l, j: (l, 0, j)),
        out_shape=jax.ShapeDtypeStruct((DEPTH, R, 6 * D_MODEL), F32),
        compiler_params=_cparams("arbitrary", "arbitrary"),
        name="modulation",
    )(cs, w_mod, b_mod.reshape(DEPTH, 1, 6 * D_MODEL))


def _inproj_kernel(x_ref, sc_ref, sh_ref, w_ref, wh_ref, ona_ref, orest_ref, ohy_ref):
    xm = (x_ref[0] * (1.0 + sc_ref[0]) + sh_ref[0]).astype(BF16)
    step = 512
    for c0 in range(0, NA_COLS, step):
        ona_ref[0, :, c0:c0 + step] = jnp.dot(xm, w_ref[:, c0:c0 + step], preferred_element_type=F32).astype(BF16)
    for c0 in range(0, REST_COLS, step):
        c1 = min(c0 + step, REST_COLS)
        orest_ref[0, :, c0:c1] = jnp.dot(xm, w_ref[:, NA_COLS + c0:NA_COLS + c1], preferred_element_type=F32)
    hy = lax.dot_general(wh_ref[...], xm, _NT, preferred_element_type=F32)
    for j in range(ohy_ref.shape[1]):
        ohy_ref[0, j] = hy[:, j * LANE:(j + 1) * LANE]


def _inproj(x, sc, sh, w_pad, w_hy_t, per_batch_mod):
    B, T, D = x.shape
    tm = min(TOK_TILE, T)
    mod_idx = (lambda b, i: (b, 0, 0)) if per_batch_mod else (lambda b, i: (0, 0, 0))
    return pl.pallas_call(
        _inproj_kernel,
        grid=(B, T // tm),
        in_specs=[pl.BlockSpec((1, tm, D), lambda b, i: (b, i, 0)),
                  pl.BlockSpec((1, 1, D), mod_idx),
                  pl.BlockSpec((1, 1, D), mod_idx),
                  pl.BlockSpec((D, D_IN_PAD), lambda b, i: (0, 0)),
                  pl.BlockSpec((HY_COLS, D), lambda b, i: (0, 0))],
        out_specs=[pl.BlockSpec((1, tm, NA_COLS), lambda b, i: (b, i, 0)),
                   pl.BlockSpec((1, tm, REST_COLS), lambda b, i: (b, i, 0)),
                   pl.BlockSpec((1, tm // LANE, HY_COLS, LANE), lambda b, i: (b, i, 0, 0))],
        out_shape=[jax.ShapeDtypeStruct((B, T, NA_COLS), BF16),
                   jax.ShapeDtypeStruct((B, T, REST_COLS), F32),
                   jax.ShapeDtypeStruct((B, T // LANE, HY_COLS, LANE), F32)],
        compiler_params=_cparams("arbitrary", "arbitrary"),
        name="inproj",
    )(x, sc, sh, w_pad, w_hy_t)


def _stack_heads(q, n_rows):
    head = lax.broadcasted_iota(jnp.int32, (n_rows, MXU_DIM), 1) // NA_HEAD_DIM
    return jnp.concatenate([jnp.where(head == h, q, jnp.zeros_like(q)) for h in range(NA_QUAD)], axis=0)


def _unstack_heads(o, n_rows):
    head = lax.broadcasted_iota(jnp.int32, (n_rows, MXU_DIM), 1) // NA_HEAD_DIM
    out = jnp.zeros((n_rows, MXU_DIM), F32)
    for h in range(NA_QUAD):
        out = jnp.where(head == h, o[h * n_rows:(h + 1) * n_rows], out)
    return out


_NT = (((1,), (1,)), ((), ()))


def _na_kernel(q_ref, k_ref, v_ref, kc_ref, vc_ref, bias_ref, o_ref):
    rt = pl.program_id(2)
    scale = NA_HEAD_DIM ** -0.5
    kc = kc_ref[0]
    vc = vc_ref[0]
    n_loc = NA_WIN_ROWS * GRID_W

    def row(rl, carry):
        r = rt * NA_ROW_TILE + rl
        kr0 = jnp.clip(r - NA_WIN_ROWS // 2, 0, GRID_W - NA_WIN_ROWS)
        dr0 = kr0 - r + NA_WIN_ROWS - 1
        q = q_ref[0, pl.ds(pl.multiple_of(rl * GRID_W, GRID_W), GRID_W), :]
        qs = _stack_heads(q, GRID_W)
        k0 = pl.multiple_of(kr0 * GRID_W, GRID_W)
        ks = k_ref[0, pl.ds(k0, n_loc), :]
        vs = v_ref[0, pl.ds(k0, n_loc), :]
        s_loc = lax.dot_general(qs, ks, _NT, preferred_element_type=F32) * scale + bias_ref[0, dr0]
        s_ctx = lax.dot_general(qs, kc, _NT, preferred_element_type=F32) * scale
        m = jnp.maximum(jnp.max(s_loc, axis=-1, keepdims=True), jnp.max(s_ctx, axis=-1, keepdims=True))
        p_loc = jnp.exp(s_loc - m)
        p_ctx = jnp.exp(s_ctx - m)
        den = jnp.sum(p_loc, axis=-1, keepdims=True) + jnp.sum(p_ctx, axis=-1, keepdims=True)
        o = (jnp.dot(p_loc.astype(BF16), vs, preferred_element_type=F32)
             + jnp.dot(p_ctx.astype(BF16), vc, preferred_element_type=F32)) / den
        o_ref[0, pl.ds(pl.multiple_of(rl * GRID_W, GRID_W), GRID_W), :] = _unstack_heads(o, GRID_W)
        return carry

    lax.fori_loop(0, NA_ROW_TILE, row, 0)


def _na_bias_table(rpb):
    c = np.arange(GRID_W)
    kc0 = np.clip(c - NA_WIN_COLS // 2, 0, GRID_W - NA_WIN_COLS)
    kc = np.arange(GRID_W)
    valid = (kc[None, :] >= kc0[:, None]) & (kc[None, :] < kc0[:, None] + NA_WIN_COLS)
    dc = np.clip(kc[None, :] - c[:, None] + NA_WIN_COLS - 1, 0, 2 * NA_WIN_COLS - 2)
    dr = np.arange(NA_WIN_ROWS)[:, None] + np.arange(NA_WIN_ROWS)[None, :]
    tab = rpb[:, dr[:, :, None, None], dc[None, None, :, :]]
    tab = jnp.where(valid[None, None, None], tab, NEG_BIG)
    tab = tab.transpose(0, 1, 3, 2, 4).reshape(NA_HEADS, NA_WIN_ROWS, GRID_W, NA_WIN_ROWS * GRID_W)
    tab = tab.reshape(NA_HEADS // NA_QUAD, NA_QUAD, NA_WIN_ROWS, GRID_W, NA_WIN_ROWS * GRID_W)
    return tab.transpose(0, 2, 1, 3, 4).reshape(NA_HEADS // NA_QUAD, NA_WIN_ROWS, NA_QUAD * GRID_W,
                                                NA_WIN_ROWS * GRID_W).astype(F32)


def _na_attention(u_na, uc_na, bias_tab):
    B, L, _ = u_na.shape
    C = uc_na.shape[1]
    nq = NA_WIDTH // MXU_DIM
    tq = NA_ROW_TILE * GRID_W
    return pl.pallas_call(
        _na_kernel,
        grid=(B, nq, L // tq),
        in_specs=[pl.BlockSpec((1, tq, MXU_DIM), lambda b, j, i: (b, i, j)),
                  pl.BlockSpec((1, L, MXU_DIM), lambda b, j, i: (b, 0, nq + j)),
                  pl.BlockSpec((1, L, MXU_DIM), lambda b, j, i: (b, 0, 2 * nq + j)),
                  pl.BlockSpec((1, C, MXU_DIM), lambda b, j, i: (b, 0, nq + j)),
                  pl.BlockSpec((1, C, MXU_DIM), lambda b, j, i: (b, 0, 2 * nq + j)),
                  pl.BlockSpec((1, NA_WIN_ROWS, NA_QUAD * GRID_W, NA_WIN_ROWS * GRID_W), lambda b, j, i: (j, 0, 0, 0))],
        out_specs=pl.BlockSpec((1, tq, MXU_DIM), lambda b, j, i: (b, i, j)),
        out_shape=jax.ShapeDtypeStruct((B, L, NA_WIDTH), F32),
        compiler_params=_cparams("arbitrary", "arbitrary", "arbitrary"),
        name="na_attention",
    )(u_na, u_na, u_na, uc_na, uc_na, bias_tab)


def _ctx_attn_kernel(q_ref, k_ref, v_ref, o_ref):
    C = q_ref.shape[1]
    qs = _stack_heads(q_ref[0], C)
    s = lax.dot_general(qs, k_ref[0], _NT, preferred_element_type=F32) * NA_HEAD_DIM ** -0.5
    p = jnp.exp(s - jnp.max(s, axis=-1, keepdims=True))
    den = jnp.sum(p, axis=-1, keepdims=True)
    o = jnp.dot(p.astype(BF16), v_ref[0], preferred_element_type=F32) / den
    o_ref[0] = _unstack_heads(o, C)


def _ctx_attention(uc_na):
    B, C, _ = uc_na.shape
    nq = NA_WIDTH // MXU_DIM
    return pl.pallas_call(
        _ctx_attn_kernel,
        grid=(B, nq),
        in_specs=[pl.BlockSpec((1, C, MXU_DIM), lambda b, j: (b, 0, j)),
                  pl.BlockSpec((1, C, MXU_DIM), lambda b, j: (b, 0, nq + j)),
                  pl.BlockSpec((1, C, MXU_DIM), lambda b, j: (b, 0, 2 * nq + j))],
        out_specs=pl.BlockSpec((1, C, MXU_DIM), lambda b, j: (b, 0, j)),
        out_shape=jax.ShapeDtypeStruct((B, C, NA_WIDTH), F32),
        compiler_params=_cparams("arbitrary", "arbitrary"),
        name="ctx_attention",
    )(uc_na, uc_na, uc_na)


def _layer_norm_rows(y, g, b):
    mu = jnp.mean(y, axis=-1, keepdims=True)
    d = y - mu
    var = jnp.mean(d * d, axis=-1, keepdims=True)
    return d * lax.rsqrt(var + LN_EPS) * g + b


def _outproj_kernel(na_ref, gla_ref, hy_ref, x_ref, g1_ref, w_ref, lg_ref, lb_ref, sc_ref, sh_ref, wr_ref,
                    xo_ref, h_ref, logit_ref):
    w_hy = w_ref[NA_WIDTH + GLA_WIDTH:, :]
    hy_mix = jnp.concatenate(
        [lax.dot_general(hy_ref[0, j].astype(BF16), w_hy, (((0,), (0,)), ((), ())), preferred_element_type=F32)
         for j in range(hy_ref.shape[1])], axis=0)
    mix = (jnp.dot(na_ref[0].astype(BF16), w_ref[0:NA_WIDTH, :], preferred_element_type=F32)
           + jnp.dot(gla_ref[0].astype(BF16), w_ref[NA_WIDTH:NA_WIDTH + GLA_WIDTH, :], preferred_element_type=F32)
           + hy_mix)
    xn = _layer_norm_rows(DEEPNORM_ALPHA * x_ref[0] + g1_ref[0] * mix, lg_ref[...], lb_ref[...])
    xo_ref[0] = xn
    h = xn * (1.0 + sc_ref[0]) + sh_ref[0]
    h_ref[0] = h.astype(BF16)
    logit_ref[...] = lax.dot_general(wr_ref[...], h, _NT, preferred_element_type=F32, precision=lax.Precision.HIGHEST)


def _outproj(na, gla, hy, x, g1, w_out, ln_g, ln_b, sc2, sh2, w_router, per_batch_mod):
    B, T, D = x.shape
    tm = min(TOK_TILE, T)
    mod_idx = (lambda b, i: (b, 0, 0)) if per_batch_mod else (lambda b, i: (0, 0, 0))
    tok = lambda w: pl.BlockSpec((1, tm, w), lambda b, i: (b, i, 0))
    full = lambda s: pl.BlockSpec(s, lambda b, i: (0,) * len(s))
    mod = pl.BlockSpec((1, 1, D), mod_idx)
    return pl.pallas_call(
        _outproj_kernel,
        grid=(B, T // tm),
        in_specs=[tok(NA_WIDTH), tok(GLA_WIDTH),
                  pl.BlockSpec((1, tm // LANE, HY_CH, LANE), lambda b, i: (b, i, 0, 0)), tok(D), mod, full((MIX_WIDTH, D)),
                  full((1, D)), full((1, D)), mod, mod, full((N_EXPERTS, D))],
        out_specs=[tok(D), tok(D), pl.BlockSpec((N_EXPERTS, tm), lambda b, i: (0, b * (T // tm) + i))],
        out_shape=[jax.ShapeDtypeStruct((B, T, D), F32), jax.ShapeDtypeStruct((B, T, D), BF16),
                   jax.ShapeDtypeStruct((N_EXPERTS, B * T), F32)],
        compiler_params=_cparams("arbitrary", "arbitrary"),
        name="outproj_ln1",
    )(na, gla, hy, x, g1, w_out, ln_g, ln_b, sc2, sh2, w_router)


def _moe_kernel(te_ref, nt_ref, xs_ref, wg_ref, wu_ref, wd_ref, ys_ref):
    i = pl.program_id(0)

    @pl.when(i < nt_ref[0])
    def _():
        xs = xs_ref[...]
        g = jnp.dot(xs, wg_ref[0].astype(BF16), preferred_element_type=F32)
        u = jnp.dot(xs, wu_ref[0].astype(BF16), preferred_element_type=F32)
        a = (g * jax.nn.sigmoid(g) * u).astype(BF16)
        ys_ref[...] = jnp.dot(a, wd_ref[0].astype(BF16), preferred_element_type=F32).astype(ys_ref.dtype)

    @pl.when(i >= nt_ref[0])
    def _():
        ys_ref[...] = jnp.zeros_like(ys_ref)


def _moe_grouped(tile_expert, n_used, xs, we_g, we_u, we_d, layer):
    Mp, D = xs.shape
    n_tiles = Mp // MOE_TILE
    H = EXPERT_HIDDEN
    grid_spec = pltpu.PrefetchScalarGridSpec(
        num_scalar_prefetch=2,
        grid=(n_tiles,),
        in_specs=[pl.BlockSpec((MOE_TILE, D), lambda i, te, nt: (i, 0)),
                  pl.BlockSpec((None, 1, D, H), lambda i, te, nt: (layer, te[i], 0, 0)),
                  pl.BlockSpec((None, 1, D, H), lambda i, te, nt: (layer, te[i], 0, 0)),
                  pl.BlockSpec((None, 1, H, D), lambda i, te, nt: (layer, te[i], 0, 0))],
        out_specs=pl.BlockSpec((MOE_TILE, D), lambda i, te, nt: (i, 0)),
    )
    return pl.pallas_call(
        _moe_kernel,
        grid_spec=grid_spec,
        out_shape=jax.ShapeDtypeStruct((Mp, D), BF16),
        compiler_params=_cparams("arbitrary"),
        name="moe_experts",
    )(tile_expert, n_used, xs, we_g, we_u, we_d)


def _shared_kernel(x_ref, h_ref, r_ref, g2_ref, wg_ref, wu_ref, wd_ref, lg_ref, lb_ref, o_ref):
    h = h_ref[0]
    g = jnp.dot(h, wg_ref[...], preferred_element_type=F32)
    u = jnp.dot(h, wu_ref[...], preferred_element_type=F32)
    a = (g * jax.nn.sigmoid(g) * u).astype(BF16)
    ff = jnp.dot(a, wd_ref[...], preferred_element_type=F32) + r_ref[0]
    o_ref[0] = _layer_norm_rows(DEEPNORM_ALPHA * x_ref[0] + g2_ref[0] * ff, lg_ref[...], lb_ref[...])


def _shared_ln2(x, h, routed, g2, ws_g, ws_u, ws_d, ln_g, ln_b, per_batch_mod):
    B, T, D = x.shape
    tm = min(TOK_TILE, T)
    mod_idx = (lambda b, i: (b, 0, 0)) if per_batch_mod else (lambda b, i: (0, 0, 0))
    tok = pl.BlockSpec((1, tm, D), lambda b, i: (b, i, 0))
    full = lambda s: pl.BlockSpec(s, lambda b, i: (0,) * len(s))
    return pl.pallas_call(
        _shared_kernel,
        grid=(B, T // tm),
        in_specs=[tok, tok, tok, pl.BlockSpec((1, 1, D), mod_idx), full((D, EXPERT_HIDDEN)), full((D, EXPERT_HIDDEN)),
                  full((EXPERT_HIDDEN, D)), full((1, D)), full((1, D))],
        out_specs=tok,
        out_shape=jax.ShapeDtypeStruct((B, T, D), F32),
        compiler_params=_cparams("arbitrary", "arbitrary"),
        name="shared_ln2",
    )(x, h, routed, g2, ws_g, ws_u, ws_d, ln_g, ln_b)


def _first_max(vals, iota, n):
    m = jnp.max(vals, axis=0, keepdims=True)
    idx = jnp.min(jnp.where(vals == m, iota, n), axis=0, keepdims=True)
    return m, idx


def _route_kernel(lt_ref, b_ref, eidx_ref, w_ref, rank_ref, cnt_ref, base_ref):
    @pl.when(pl.program_id(0) == 0)
    def _():
        base_ref[...] = jnp.zeros_like(base_ref)

    tm = lt_ref.shape[1]
    per = N_EXPERTS // N_GROUPS
    s = jax.nn.sigmoid(lt_ref[...])
    sel = s + b_ref[...]
    io_g = lax.broadcasted_iota(jnp.int32, (per, tm), 0)
    scores = []
    for g in range(N_GROUPS):
        blk = sel[g * per:(g + 1) * per]
        m1, i1 = _first_max(blk, io_g, per)
        m2 = jnp.max(jnp.where(io_g == i1, -jnp.inf, blk), axis=0, keepdims=True)
        scores.append(m1 + m2)
    cur = jnp.concatenate(scores, axis=0)
    io_8 = lax.broadcasted_iota(jnp.int32, (N_GROUPS, tm), 0)
    gmask = jnp.zeros((N_GROUPS, tm), F32)
    for _ in range(TOPK_GROUPS):
        _, gi = _first_max(cur, io_8, N_GROUPS)
        hit = io_8 == gi
        gmask = jnp.where(hit, 1.0, gmask)
        cur = jnp.where(hit, -jnp.inf, cur)
    masked = jnp.concatenate(
        [jnp.where(gmask[g:g + 1] > 0.0, sel[g * per:(g + 1) * per], -jnp.inf) for g in range(N_GROUPS)], axis=0)
    io_e = lax.broadcasted_iota(jnp.int32, (N_EXPERTS, tm), 0)
    chosen = jnp.zeros((N_EXPERTS, tm), F32)
    eidx, gates = [], []
    for _ in range(TOP_K):
        _, ei = _first_max(masked, io_e, N_EXPERTS)
        hit = io_e == ei
        eidx.append(ei)
        gates.append(jnp.sum(jnp.where(hit, s, 0.0), axis=0, keepdims=True))
        masked = jnp.where(hit, -jnp.inf, masked)
        chosen = jnp.where(hit, 1.0, chosen)
    wk = jnp.concatenate(gates, axis=0)
    w_ref[...] = wk / jnp.sum(wk, axis=0, keepdims=True) * ROUTED_SCALE
    eidx_ref[...] = jnp.concatenate(eidx, axis=0)
    earlier = (lax.broadcasted_iota(jnp.int32, (tm, tm), 0) < lax.broadcasted_iota(jnp.int32, (tm, tm), 1))
    pos = jnp.dot(chosen.astype(BF16), jnp.where(earlier, 1.0, 0.0).astype(BF16), preferred_element_type=F32)
    pos = pos + base_ref[...]
    ranks = [jnp.sum(jnp.where(io_e == eidx[k], pos, 0.0), axis=0, keepdims=True) for k in range(TOP_K)]
    rank_ref[...] = jnp.concatenate(ranks, axis=0).astype(jnp.int32)
    base_ref[...] = base_ref[...] + jnp.sum(chosen, axis=1, keepdims=True)
    cnt_ref[...] = base_ref[...]


def _route(logits_t, b_corr):
    E, T = logits_t.shape
    tm = ROUTE_TILE
    tokk = pl.BlockSpec((TOP_K, tm), lambda i: (0, i))
    return pl.pallas_call(
        _route_kernel,
        grid=(T // tm,),
        in_specs=[pl.BlockSpec((E, tm), lambda i: (0, i)), pl.BlockSpec((E, 1), lambda i: (0, 0))],
        out_specs=[tokk, tokk, tokk, pl.BlockSpec((E, 1), lambda i: (0, 0))],
        out_shape=[jax.ShapeDtypeStruct((TOP_K, T), jnp.int32), jax.ShapeDtypeStruct((TOP_K, T), F32),
                   jax.ShapeDtypeStruct((TOP_K, T), jnp.int32), jax.ShapeDtypeStruct((E, 1), F32)],
        scratch_shapes=[pltpu.VMEM((E, 1), F32)],
        compiler_params=_cparams("arbitrary"),
        name="route",
    )(logits_t, b_corr.reshape(E, 1))


def _slot_kernel(eidx_ref, rank_ref, pstart_ref, dest_ref):
    tm = eidx_ref.shape[1]
    io_e = lax.broadcasted_iota(jnp.int32, (N_EXPERTS, tm), 0)
    ei = eidx_ref[...]
    starts = [jnp.sum(jnp.where(io_e == ei[k:k + 1], pstart_ref[...], 0.0), axis=0, keepdims=True)
              for k in range(TOP_K)]
    dest_ref[...] = jnp.concatenate(starts, axis=0).astype(jnp.int32) + rank_ref[...]


def _slots(eidx, rank, pstart):
    K, T = eidx.shape
    tm = ROUTE_TILE
    tokk = pl.BlockSpec((K, tm), lambda i: (0, i))
    return pl.pallas_call(
        _slot_kernel,
        grid=(T // tm,),
        in_specs=[tokk, tokk, pl.BlockSpec((N_EXPERTS, 1), lambda i: (0, 0))],
        out_specs=tokk,
        out_shape=jax.ShapeDtypeStruct((K, T), jnp.int32),
        compiler_params=_cparams("arbitrary"),
        name="route_slots",
    )(eidx, rank, pstart)


def _moe_routed(h_flat, logits_t, b_corr, we_g, we_u, we_d, layer):
    T = h_flat.shape[0]
    n_tiles = T * TOP_K // MOE_TILE + N_EXPERTS
    eidx, gates, rank, counts = _route(logits_t, b_corr)
    counts = counts[:, 0].astype(jnp.int32)
    padded = (counts + MOE_TILE - 1) // MOE_TILE * MOE_TILE
    pends = jnp.cumsum(padded)
    tile_expert = jnp.minimum(jnp.searchsorted(pends, jnp.arange(n_tiles, dtype=jnp.int32) * MOE_TILE, side='right'),
                              N_EXPERTS - 1).astype(jnp.int32)
    n_used = (pends[-1] // MOE_TILE).astype(jnp.int32).reshape(1)
    dest = _slots(eidx, rank, (pends - padded).astype(F32).reshape(N_EXPERTS, 1))
    tok = jnp.broadcast_to(jnp.arange(T, dtype=jnp.int32)[None], (TOP_K, T))
    src_tok = jnp.zeros((n_tiles * MOE_TILE,), jnp.int32).at[dest.reshape(-1)].set(
        tok.reshape(-1), unique_indices=True, mode='promise_in_bounds')
    xs = jnp.take(h_flat, src_tok, axis=0)
    ys = _moe_grouped(tile_expert, n_used, xs, we_g, we_u, we_d, layer)
    picked = jnp.take(ys, dest.T, axis=0)
    return jnp.sum(picked.astype(F32) * gates.T[:, :, None], axis=1)


_GLA_QK_BLK, _GLA_V_BLK, _GLA_R_BLK = 0, 1, 2
_GLA_LR_BLK = (2 * GLA_KEY_WIDTH + 2 * GLA_WIDTH) // LANE


def _rope_tables(L):
    t = np.arange(L)
    lane = np.arange(GLA_KEY_WIDTH)
    d = lane % GLA_DK
    pos = np.where(d[None, :] < GLA_DK // 2, (t // GRID_W)[:, None], (t % GRID_W)[:, None]).astype(np.float32)
    quarter = GLA_DK // 4
    inv = ROPE_BASE ** (-jnp.arange(quarter, dtype=F32) / quarter)
    ang = jnp.asarray(pos) * inv[jnp.asarray(d % quarter)][None, :]
    sign = np.where(d % (2 * quarter) < quarter, -1.0, 1.0).astype(np.float32)
    return jnp.cos(ang), jnp.sin(ang) * sign[None, :]


def _rope_partner(x):
    lane = lax.broadcasted_iota(jnp.int32, x.shape, 1)
    quarter = GLA_DK // 4
    return jnp.where(lane % (2 * quarter) < quarter, pltpu.roll(x, GLA_KEY_WIDTH - quarter, 1), pltpu.roll(x, quarter, 1))


def _log_sigmoid(x):
    return jnp.minimum(x, 0.0) - jnp.log(1.0 + jnp.exp(-jnp.abs(x)))


def _gla_kernel(reverse, finalize, *refs):
    if finalize:
        (qk_ref, v_ref, lr_ref, cos_ref, sin_ref, wa_ref, ba_ref, s0_ref, of_ref, r_ref, g_ref,
         o_ref, sfin_ref, s_scr) = refs
    else:
        qk_ref, v_ref, lr_ref, cos_ref, sin_ref, wa_ref, ba_ref, s0_ref, o_ref, sfin_ref, s_scr = refs
    hi = lax.Precision.HIGHEST

    @pl.when(pl.program_id(1) == 0)
    def _():
        s_scr[...] = s0_ref[0]

    tg = qk_ref.shape[1]
    C = GLA_CHUNK
    KW, VW = GLA_KEY_WIDTH, GLA_WIDTH
    qk = qk_ref[0]
    cos, sin = cos_ref[...], sin_ref[...]
    q = qk[:, :KW] * GLA_DK ** -0.5
    k = qk[:, KW:]
    q = q * cos + _rope_partner(q) * sin
    k = k * cos + _rope_partner(k) * sin
    v = v_ref[0]
    logit = jnp.dot(lr_ref[0], wa_ref[...], preferred_element_type=F32, precision=hi) + ba_ref[...]
    la = _log_sigmoid(logit) / GLA_GATE_NORM

    ri = lax.broadcasted_iota(jnp.int32, (C, C), 0)
    ci = lax.broadcasted_iota(jnp.int32, (C, C), 1)
    tri = jnp.where((ci >= ri) if reverse else (ci <= ri), 1.0, 0.0)
    tri_h = jnp.concatenate([tri] * GLA_HEADS, axis=0)
    head_k = lax.broadcasted_iota(jnp.int32, (C, KW), 1) // GLA_DK
    head_v = lax.broadcasted_iota(jnp.int32, (C, VW), 1) // GLA_DV
    own_block = (lax.broadcasted_iota(jnp.int32, (KW, VW), 0) // GLA_DK
                 == lax.broadcasted_iota(jnp.int32, (KW, VW), 1) // GLA_DV)
    eye = lax.broadcasted_iota(jnp.int32, (KW, KW), 0) == lax.broadcasted_iota(jnp.int32, (KW, KW), 1)

    S = s_scr[...]
    nc = tg // C
    outs = [None] * nc
    for c in (range(nc - 1, -1, -1) if reverse else range(nc)):
        sl = slice(c * C, (c + 1) * C)
        b = jnp.dot(tri, la[sl], preferred_element_type=F32, precision=hi)
        qt = q[sl] * jnp.exp(b)
        kt = k[sl] * jnp.exp(-b)
        qs = jnp.concatenate([jnp.where(head_k == h, qt, 0.0) for h in range(GLA_HEADS)], axis=0).astype(BF16)
        att = lax.dot_general(qs, kt.astype(BF16), _NT, preferred_element_type=F32)
        att = jnp.where(tri_h > 0.0, att, 0.0)
        vb = v[sl].astype(BF16)
        oi = jnp.dot(att.astype(BF16), vb, preferred_element_type=F32)
        o_intra = jnp.zeros((C, VW), F32)
        for h in range(GLA_HEADS):
            o_intra = jnp.where(head_v == h, oi[h * C:(h + 1) * C], o_intra)
        o_inter = jnp.dot(qt.astype(BF16), S.astype(BF16), preferred_element_type=F32)
        outs[c] = o_intra + o_inter
        b_last = b[0:1] if reverse else b[C - 1:C]
        kdec = (k[sl] * jnp.exp(b_last - b)).astype(BF16)
        kv = lax.dot_general(kdec, vb, (((0,), (0,)), ((), ())), preferred_element_type=F32)
        decay_col = jnp.sum(jnp.where(eye, jnp.exp(b_last), 0.0), axis=1, keepdims=True)
        S = decay_col * S + jnp.where(own_block, kv, 0.0)
    s_scr[...] = S
    sfin_ref[0] = S
    o = jnp.concatenate(outs, axis=0)
    if finalize:
        o = of_ref[0] + o
        same_head = (lax.broadcasted_iota(jnp.int32, (VW, VW), 0) // GLA_DV
                     == lax.broadcasted_iota(jnp.int32, (VW, VW), 1) // GLA_DV)
        ms = jnp.dot(o * o, jnp.where(same_head, 1.0 / GLA_DV, 0.0), preferred_element_type=F32, precision=hi)
        r = r_ref[0]
        o = o * lax.rsqrt(ms + LN_EPS) * g_ref[...] * (r * jax.nn.sigmoid(r))
    o_ref[0] = o


def _gla_pass(u_rest, cos, sin, wa, ba, s0, reverse, fin=None):
    B, T, _ = u_rest.shape
    tg = min(GLA_TILE, T)
    n = T // tg
    ti = (lambda i: n - 1 - i) if reverse else (lambda i: i)
    KW, VW = GLA_KEY_WIDTH, GLA_WIDTH
    ublk = lambda w, j: pl.BlockSpec((1, tg, w), lambda b, i: (b, ti(i), j))
    full = lambda s: pl.BlockSpec(s, lambda b, i: (0,) * len(s))
    state = pl.BlockSpec((1, KW, VW), lambda b, i: (b, 0, 0))
    tab = pl.BlockSpec((tg, KW), lambda b, i: (ti(i), 0))
    in_specs = [ublk(2 * KW, _GLA_QK_BLK), ublk(VW, _GLA_V_BLK), ublk(LANE, _GLA_LR_BLK), tab, tab,
                full((LANE, KW)), full((1, KW)), state]
    args = [u_rest, u_rest, u_rest, cos, sin, wa, ba, s0]
    if fin is not None:
        in_specs += [ublk(VW, 0), ublk(VW, _GLA_R_BLK), full((1, VW))]
        args += [fin[0], u_rest, fin[1]]
    return pl.pallas_call(
        functools.partial(_gla_kernel, reverse, fin is not None),
        grid=(B, n),
        in_specs=in_specs,
        out_specs=[ublk(VW, 0), state],
        out_shape=[jax.ShapeDtypeStruct((B, T, VW), F32), jax.ShapeDtypeStruct((B, KW, VW), F32)],
        scratch_shapes=[pltpu.VMEM((KW, VW), F32)],
        compiler_params=_cparams("arbitrary", "arbitrary"),
        name="gla_bwd" if reverse else "gla_fwd",
    )(*args)


def _gla_bidir(u_rest, cos, sin, w_a2, b_a2, norm_g, s0_f, s0_b):
    def decay_w(d):
        return jnp.zeros((LANE, GLA_KEY_WIDTH), F32).at[d * GLA_RANK:(d + 1) * GLA_RANK].set(w_a2[d])

    o_f, s_f = _gla_pass(u_rest, cos, sin, decay_w(0), b_a2[0:1], s0_f, False)
    g = jnp.tile(norm_g, GLA_HEADS).reshape(1, GLA_WIDTH)
    o, s_b = _gla_pass(u_rest, cos, sin, decay_w(1), b_a2[1:2], s0_b, True, (o_f, g))
    return o, s_f, s_b


HY_LANES = LANE
HY_CH_TILE = 16
HY_ROW_TILE = 2048
HY_CTX_N1 = 16


def _hy_consts(n1, N1):
    W = HY_LANES
    n1p = max(n1, 16)
    n1o = max(n1, 8)
    a = np.arange(N1)
    th1 = 2.0 * np.pi * ((a[:, None] * a[None, :]) % N1) / N1
    f1 = np.zeros((2 * N1, n1p)); f1[:N1, :n1] = np.cos(th1[:, :n1]); f1[N1:, :n1] = -np.sin(th1[:, :n1])
    f1_full = np.concatenate([np.cos(th1), -np.sin(th1)], axis=0)
    ginv = np.zeros((n1o, 2 * N1)); ginv[:n1, :N1] = np.cos(th1.T[:n1]); ginv[:n1, N1:] = -np.sin(th1.T[:n1])
    r = np.arange(W)
    tht = 2.0 * np.pi * ((a[:, None] * r[None, :]) % (N1 * W)) / (N1 * W)
    tr = np.tile(np.cos(tht), (1, HY_CH_TILE)); ti = np.tile(-np.sin(tht), (1, HY_CH_TILE))
    th2 = 2.0 * np.pi * ((r[:, None] * r[None, :]) % W) / W
    c2, s2 = np.cos(th2), -np.sin(th2)
    m2f = np.block([[c2, s2], [-s2, c2]])
    m2i = np.block([[c2, -s2], [s2, c2]])
    f = lambda m: jnp.asarray(m, F32)
    return dict(n1p=n1p, n1o=n1o, f1=f(f1), f1_full=f(f1_full), ginv=f(ginv), tr=f(tr), ti=f(ti), m2f=f(m2f), m2i=f(m2i))


def _short_conv_rows(u, w_ref, b_ref, n_rows):
    R, Wd = u.shape
    lane = lax.broadcasted_iota(jnp.int32, (R, Wd), 1) % HY_LANES
    row = lax.broadcasted_iota(jnp.int32, (R, Wd), 0)
    up = jnp.where(row == 0, 0.0, pltpu.roll(u, 1, 0))
    dn = jnp.where(row == n_rows - 1, 0.0, pltpu.roll(u, R - 1, 0))
    prev = jnp.where(lane == 0, pltpu.roll(up, Wd - (HY_LANES - 1), 1), pltpu.roll(u, 1, 1))
    nxt = jnp.where(lane == HY_LANES - 1, pltpu.roll(dn, HY_LANES - 1, 1), pltpu.roll(u, Wd - 1, 1))
    return b_ref[...] + prev * w_ref[0:1] + u * w_ref[1:2] + nxt * w_ref[2:3]


def _pad_rows(u, rows):
    return u if u.shape[0] == rows else jnp.concatenate([u, jnp.zeros((rows - u.shape[0], u.shape[1]), u.dtype)], axis=0)


def _hy_forward_stage1(z, f1_ref, tr_ref, ti_ref, a_ref, N1, prec):
    if prec is None:
        a = jnp.dot(f1_ref[...].astype(BF16), z.astype(BF16), preferred_element_type=F32)
    else:
        a = jnp.dot(f1_ref[...], z, preferred_element_type=F32, precision=prec)
    ar, ai = a[:N1], a[N1:]
    tr, ti = tr_ref[...], ti_ref[...]
    a_ref[0, 0] = ar * tr - ai * ti
    a_ref[0, 1] = ar * ti + ai * tr


def _hy_inverse_stage1(p_ref, ginv_ref, tr_ref, ti_ref, n1):
    pr, pi = p_ref[0, 0], p_ref[0, 1]
    tr, ti = tr_ref[...], ti_ref[...]
    q = jnp.concatenate([pr * tr + pi * ti, pi * tr - pr * ti], axis=0).astype(BF16)
    y = jnp.dot(ginv_ref[...].astype(BF16), q, preferred_element_type=F32)
    return y


def _hy_first_kernel(n1, N1, n1p, v_ref, sw_ref, sb_ref, f1_ref, tr_ref, ti_ref, a_ref):
    rows = max(n1, 8)
    z = _short_conv_rows(_pad_rows(v_ref[0], rows), sw_ref, sb_ref, n1)
    _hy_forward_stage1(_pad_rows(z, n1p), f1_ref, tr_ref, ti_ref, a_ref, N1, None)


def _hy_mid_kernel(n1, N1, n1p, p_ref, v_ref, x_ref, swv_ref, sbv_ref, swx_ref, sbx_ref, skip_ref, ginv_ref, f1_ref,
                   tr_ref, ti_ref, a_ref, z_ref):
    rows = max(n1, 8)
    y = _hy_inverse_stage1(p_ref, ginv_ref, tr_ref, ti_ref, n1)
    z0 = _short_conv_rows(_pad_rows(v_ref[0], rows), swv_ref, sbv_ref, n1)
    gate = _short_conv_rows(_pad_rows(x_ref[0], rows), swx_ref, sbx_ref, n1)
    z1 = gate * (y + skip_ref[0] * z0)
    z_ref[0] = z1[:n1]
    _hy_forward_stage1(_pad_rows(z1, n1p), f1_ref, tr_ref, ti_ref, a_ref, N1, None)


def _hy_last_kernel(n1, N1, p_ref, z_ref, x_ref, swx_ref, sbx_ref, skip_ref, ginv_ref, tr_ref, ti_ref, o_ref):
    rows = max(n1, 8)
    y = _hy_inverse_stage1(p_ref, ginv_ref, tr_ref, ti_ref, n1)
    gate = _short_conv_rows(_pad_rows(x_ref[0], rows), swx_ref, sbx_ref, n1)
    o_ref[0] = (gate * (y + skip_ref[0] * _pad_rows(z_ref[0], rows)))[:n1]


def _hy_kern_kernel(N1, k_ref, f1_ref, tr_ref, ti_ref, a_ref):
    _hy_forward_stage1(k_ref[0], f1_ref, tr_ref, ti_ref, a_ref, N1, lax.Precision.HIGHEST)


def _hy_stage2_kernel(a_ref, kf_ref, m2f_ref, m2i_ref, p_ref):
    W = HY_LANES
    a = jnp.concatenate([a_ref[0, 0], a_ref[0, 1]], axis=1).astype(BF16)
    x = jnp.dot(a, m2f_ref[...].astype(BF16), preferred_element_type=F32)
    xr, xi = x[:, :W], x[:, W:]
    kf = kf_ref[0]
    kr, ki = kf[:, :W], kf[:, W:]
    y = jnp.concatenate([xr * kr - xi * ki, xr * ki + xi * kr], axis=1).astype(BF16)
    p = jnp.dot(y, m2i_ref[...].astype(BF16), preferred_element_type=F32)
    p_ref[0, 0] = p[:, :W]
    p_ref[0, 1] = p[:, W:]


def _hy_kspec_kernel(scale, a_ref, m2f_ref, kf_ref):
    a = jnp.concatenate([a_ref[0, 0], a_ref[0, 1]], axis=1)
    kf_ref[0] = jnp.dot(a, m2f_ref[...], preferred_element_type=F32, precision=lax.Precision.HIGHEST) * scale


def _hy_stage2(a, kf, cst, order):
    B, _, N1, CW = a.shape
    W = HY_LANES
    rows = N1 * CW // W
    rt = min(HY_ROW_TILE, rows)
    a4 = a.reshape(B, 2, rows, W)
    blk = pl.BlockSpec((1, 2, rt, W), lambda i, b: (b, 0, i, 0))
    full = lambda s: pl.BlockSpec(s, lambda i, b: (0,) * len(s))
    p = pl.pallas_call(
        _hy_stage2_kernel,
        grid=(rows // rt, B),
        in_specs=[blk, pl.BlockSpec((1, rt, 2 * W), lambda i, b: (order, i, 0)), full((2 * W, 2 * W)), full((2 * W, 2 * W))],
        out_specs=blk,
        out_shape=jax.ShapeDtypeStruct((B, 2, rows, W), F32),
        compiler_params=_cparams("arbitrary", "arbitrary"),
        name="hyena_stage2",
    )(a4, kf, cst["m2f"], cst["m2i"])
    return p.reshape(B, 2, N1, CW)


def _hy_filter_spectrum(filt, n1, N1, cst):
    Ls, n_ord, _, C = filt.shape
    W = HY_LANES
    N = N1 * W
    kern = jnp.concatenate([filt[:, :, 0], jnp.zeros((N - 2 * Ls + 1, n_ord, C), F32), filt[:0:-1, :, 1]], axis=0)
    kern = kern.reshape(N1, W, n_ord, C).transpose(2, 0, 3, 1).reshape(n_ord, N1, C * W)
    Wd = HY_CH_TILE * W
    full = lambda s: pl.BlockSpec(s, lambda o, j: (0,) * len(s))
    a = pl.pallas_call(
        functools.partial(_hy_kern_kernel, N1),
        grid=(n_ord, C // HY_CH_TILE),
        in_specs=[pl.BlockSpec((1, N1, Wd), lambda o, j: (o, 0, j)), full((2 * N1, N1)), full((N1, Wd)), full((N1, Wd))],
        out_specs=pl.BlockSpec((1, 2, N1, Wd), lambda o, j: (o, 0, 0, j)),
        out_shape=jax.ShapeDtypeStruct((n_ord, 2, N1, C * W), F32),
        compiler_params=_cparams("arbitrary", "arbitrary"),
        name="hyena_filter_stage1",
    )(kern, cst["f1_full"], cst["tr"], cst["ti"])
    rows = N1 * C
    rt = min(HY_ROW_TILE, rows)
    return pl.pallas_call(
        functools.partial(_hy_kspec_kernel, 1.0 / N),
        grid=(n_ord, rows // rt),
        in_specs=[pl.BlockSpec((1, 2, rt, W), lambda o, i: (o, 0, i, 0)), pl.BlockSpec((2 * W, 2 * W), lambda o, i: (0, 0))],
        out_specs=pl.BlockSpec((1, rt, 2 * W), lambda o, i: (o, i, 0)),
        out_shape=jax.ShapeDtypeStruct((n_ord, rows, 2 * W), F32),
        compiler_params=_cparams("arbitrary", "arbitrary"),
        name="hyena_filter_stage2",
    )(a.reshape(n_ord, 2, rows, W), cst["m2f"])


def _hyena_mixer(hy_t, short_w, short_b, filt, skip, N1):
    B, n1, C3, W = hy_t.shape
    C = C3 // (HY_ORDER + 1)
    cst = _hy_consts(n1, N1)
    n1p = cst["n1p"]
    kf = _hy_filter_spectrum(filt, n1, N1, cst)
    Wd = HY_CH_TILE * W
    nj = C // HY_CH_TILE
    hy2 = hy_t.reshape(B, n1, C3 * W)
    rep = lambda v: jnp.repeat(v, W, axis=-1)
    sw, sb = rep(short_w), rep(short_b).reshape(1, C3 * W)
    sk = rep(skip).reshape(HY_ORDER, 1, C * W)
    chan = lambda part: pl.BlockSpec((1, n1, Wd), lambda b, j: (b, 0, part * nj + j))
    wsp = lambda part: pl.BlockSpec((HY_SHORT, Wd), lambda b, j: (0, part * nj + j))
    bsp = lambda part: pl.BlockSpec((1, Wd), lambda b, j: (0, part * nj + j))
    ksp = lambda o: pl.BlockSpec((1, 1, Wd), lambda b, j: (o, 0, j))
    full = lambda s: pl.BlockSpec(s, lambda b, j: (0,) * len(s))
    spec = pl.BlockSpec((1, 2, N1, Wd), lambda b, j: (b, 0, 0, j))
    zsp = pl.BlockSpec((1, n1, Wd), lambda b, j: (b, 0, j))
    a_shape = jax.ShapeDtypeStruct((B, 2, N1, C * W), F32)
    z_shape = jax.ShapeDtypeStruct((B, n1, C * W), F32)
    cp = _cparams("arbitrary", "arbitrary")
    f1s, gis, tws = full(cst["f1"].shape), full(cst["ginv"].shape), full((N1, Wd))

    a0 = pl.pallas_call(
        functools.partial(_hy_first_kernel, n1, N1, n1p), grid=(B, nj),
        in_specs=[chan(0), wsp(0), bsp(0), f1s, tws, tws], out_specs=spec, out_shape=a_shape,
        compiler_params=cp, name="hyena_first",
    )(hy2, sw, sb, cst["f1"], cst["tr"], cst["ti"])
    p0 = _hy_stage2(a0, kf, cst, 0)
    a1, z1 = pl.pallas_call(
        functools.partial(_hy_mid_kernel, n1, N1, n1p), grid=(B, nj),
        in_specs=[spec, chan(0), chan(1), wsp(0), bsp(0), wsp(1), bsp(1), ksp(0), gis, f1s, tws, tws],
        out_specs=[spec, zsp], out_shape=[a_shape, z_shape], compiler_params=cp, name="hyena_mid",
    )(p0, hy2, hy2, sw, sb, sw, sb, sk, cst["ginv"], cst["f1"], cst["tr"], cst["ti"])
    p1 = _hy_stage2(a1, kf, cst, 1)
    out = pl.pallas_call(
        functools.partial(_hy_last_kernel, n1, N1), grid=(B, nj),
        in_specs=[spec, zsp, chan(2), wsp(2), bsp(2), ksp(1), gis, tws, tws],
        out_specs=zsp, out_shape=z_shape, compiler_params=cp, name="hyena_last",
    )(p1, z1, hy2, sw, sb, sk, cst["ginv"], cst["tr"], cst["ti"])
    return out.reshape(B, n1, C, W)


def _hyena_filters(L, w1, b1, f1, w2, b2, f2, w3, b3, decay):
    hi = lax.Precision.HIGHEST
    t = jnp.linspace(0.0, 1.0, L, dtype=F32)
    w = 2.0 * math.pi * jnp.arange(L, dtype=F32) / L
    bands = jnp.linspace(1e-4, HY_BANDS - 1, HY_BANDS, dtype=F32)
    ang = w[:, None] * bands[None]
    z = jnp.concatenate([t[:, None], jnp.cos(ang), -jnp.sin(ang)], -1)
    h = jnp.sin(f1 * (jnp.dot(z, w1, precision=hi) + b1))
    h = jnp.sin(f2 * (jnp.dot(h, w2, precision=hi) + b2))
    h = (jnp.dot(h, w3, precision=hi) + b3).reshape(L, HY_ORDER, 2, HY_CH)
    window = jnp.exp(-t[:, None, None, None] * jnp.abs(decay)[None])
    return h * window


def kernel(x, c, ctx, c_ctx, w_mod, b_mod, w_in, na_rpb, gla_w_a2, gla_b_a2, gla_norm_g, hy_short_w, hy_short_b, hy_w1, hy_b1, hy_f1, hy_w2, hy_b2, hy_f2, hy_w3, hy_b3, hy_decay, hy_skip, w_out, ln1_g, ln1_b, router_w, router_b, we_gate, we_up, we_down, ws_gate, ws_up, ws_down, ln2_g, ln2_b):
    B, L, D = x.shape
    C = ctx.shape[1]
    rope_cos, rope_sin = _rope_tables(L)
    ctx_cos, ctx_sin = jnp.ones((C, GLA_KEY_WIDTH), F32), jnp.zeros((C, GLA_KEY_WIDTH), F32)
    zero_state = jnp.zeros((B, GLA_KEY_WIDTH, GLA_WIDTH), F32)

    n_mod = -(-(B + 1) // 8) * 8
    cs = jnp.zeros((n_mod, D), F32).at[:B].set(c).at[B].set(c_ctx)
    mod_all = _modulation(cs, w_mod, b_mod)

    xc = ctx
    for l in range(DEPTH):
        last = l == DEPTH - 1
        mods = mod_all[l].reshape(n_mod, 6, 1, D)
        lat = lambda j: mods[:B, j]
        cm = lambda j: mods[B:B + 1, j]
        n_tok = NA_COLS + GLA_COLS
        w_pad = jnp.pad(w_in[l][:, :n_tok], ((0, 0), (0, D_IN_PAD - n_tok))).astype(BF16)
        w_hy_t = w_in[l][:, n_tok:].T.astype(BF16)
        w_out_b = w_out[l].astype(BF16)
        lg1, lb1 = ln1_g[l].reshape(1, D), ln1_b[l].reshape(1, D)
        lg2, lb2 = ln2_g[l].reshape(1, D), ln2_b[l].reshape(1, D)

        u_na, u_rest, hy = _inproj(x, lat(1), lat(0), w_pad, w_hy_t, True)
        uc_na, uc_rest, hyc = _inproj(xc, cm(1), cm(0), w_pad, w_hy_t, False)

        na_lat = _na_attention(u_na, uc_na, _na_bias_table(na_rpb[l]))

        gla_c, s_ctx_f, s_ctx_b = _gla_bidir(uc_rest, ctx_cos, ctx_sin, gla_w_a2[l], gla_b_a2[l], gla_norm_g[l],
                                             zero_state, zero_state)
        gla_lat, _, _ = _gla_bidir(u_rest, rope_cos, rope_sin, gla_w_a2[l], gla_b_a2[l], gla_norm_g[l],
                                   s_ctx_f, s_ctx_b)

        filt_args = (hy_w1[l], hy_b1[l], hy_f1[l], hy_w2[l], hy_b2[l], hy_f2[l], hy_w3[l], hy_b3[l], hy_decay[l])
        hy_lat = _hyena_mixer(hy, hy_short_w[l], hy_short_b[l], _hyena_filters(L, *filt_args), hy_skip[l],
                              2 * L // HY_LANES)

        wr_t = router_w[l].T
        x, h_lat, logit_lat = _outproj(na_lat, gla_lat, hy_lat, x, lat(2), w_out_b, lg1, lb1, lat(4), lat(3), wr_t, True)
        if not last:
            na_c = _ctx_attention(uc_na)
            hy_c = _hyena_mixer(hyc, hy_short_w[l], hy_short_b[l], _hyena_filters(C, *filt_args), hy_skip[l],
                                HY_CTX_N1)
            xc, h_c, logit_c = _outproj(na_c, gla_c, hy_c, xc, cm(2), w_out_b, lg1, lb1, cm(4), cm(3), wr_t, False)
            h_flat = jnp.concatenate([h_lat.reshape(B * L, D), h_c.reshape(B * C, D)], axis=0)
            logit_t = jnp.concatenate([logit_lat, logit_c], axis=1)
        else:
            h_flat = h_lat.reshape(B * L, D)
            logit_t = logit_lat

        routed = _moe_routed(h_flat, logit_t, router_b[l], we_gate, we_up, we_down, l)
        wsg, wsu, wsd = ws_gate[l].astype(BF16), ws_up[l].astype(BF16), ws_down[l].astype(BF16)
        x = _shared_ln2(x, h_lat, routed[:B * L].reshape(B, L, D), lat(5), wsg, wsu, wsd, lg2, lb2, True)
        if not last:
            xc = _shared_ln2(xc, h_c, routed[B * L:].reshape(B, C, D), cm(5), wsg, wsu, wsd, lg2, lb2, False)
    return x
```

```python
import functools
import math

import numpy as np
import jax
import jax.numpy as jnp
from jax import lax
from jax.experimental import pallas as pl
from jax.experimental.pallas import tpu as pltpu

F32 = jnp.float32
BF16 = jnp.bfloat16

D_MODEL = 1024
DEPTH = 4
GRID_W = 64
CTX_LEN = 256

NA_HEADS = 8
NA_HEAD_DIM = 64
NA_WIDTH = NA_HEADS * NA_HEAD_DIM
NA_WIN_ROWS = 8
NA_WIN_COLS = 16

GLA_HEADS = 4
GLA_DK = 32
GLA_DV = 64
GLA_KEY_WIDTH = GLA_HEADS * GLA_DK
GLA_WIDTH = GLA_HEADS * GLA_DV
GLA_RANK = 16
GLA_GATE_NORM = 16.0
GLA_CHUNK = 64

HY_CH = 256
HY_ORDER = 2
HY_SHORT = 3
HY_BANDS = 16
HY_EMB = 1 + 2 * HY_BANDS

MIX_WIDTH = NA_WIDTH + GLA_WIDTH + HY_CH
IN_SPLITS = (NA_WIDTH, NA_WIDTH, NA_WIDTH, GLA_KEY_WIDTH, GLA_KEY_WIDTH, GLA_WIDTH, GLA_WIDTH,
             2 * GLA_RANK, (HY_ORDER + 1) * HY_CH)
D_IN = sum(IN_SPLITS)
ROPE_BASE = 10000.0

N_EXPERTS = 128
TOP_K = 8
N_GROUPS = 8
TOPK_GROUPS = 4
EXPERT_HIDDEN = 256
ROUTED_SCALE = 2.5

DEEPNORM_ALPHA = (2 * DEPTH) ** 0.25
LN_EPS = 1e-6

LANE = 128
MXU_DIM = 256
VMEM_LIMIT = 48 * 1024 * 1024

NA_COLS = 3 * NA_WIDTH
GLA_COLS = sum(IN_SPLITS[3:8])
HY_COLS = IN_SPLITS[8]
REST_COLS = -(-GLA_COLS // LANE) * LANE
D_IN_PAD = NA_COLS + REST_COLS
NA_QUAD = MXU_DIM // NA_HEAD_DIM
NEG_BIG = -1e30

TOK_TILE = 512
MOE_TILE = 256
NA_ROW_TILE = 8
ROUTE_TILE = 256
GLA_TILE = 512
MOE_GROUPS = 4


def _cparams(*sem):
    return pltpu.CompilerParams(dimension_semantics=sem, vmem_limit_bytes=VMEM_LIMIT)


def _mod_kernel(c_ref, w_ref, b_ref, o_ref):
    c = c_ref[...]
    s = c * jax.nn.sigmoid(c)
    o_ref[0] = jnp.dot(s, w_ref[0], preferred_element_type=F32, precision=lax.Precision.HIGHEST) + b_ref[0]


def _modulation(cs, w_mod, b_mod):
    R = cs.shape[0]
    tn = 1536
    return pl.pallas_call(
        _mod_kernel,
        grid=(DEPTH, 6 * D_MODEL // tn),
        in_specs=[pl.BlockSpec((R, D_MODEL), lambda l, j: (0, 0)),
                  pl.BlockSpec((1, D_MODEL, tn), lambda l, j: (l, 0, j)),
                  pl.BlockSpec((1, 1, tn), lambda l, j: (l, 0, j))],
        out_specs=pl.BlockSpec((1, R, tn), lambda l, j: (l, 0, j)),
        out_shape=jax.ShapeDtypeStruct((DEPTH, R, 6 * D_MODEL), F32),
        compiler_params=_cparams("arbitrary", "arbitrary"),
        name="modulation",
    )(cs, w_mod, b_mod.reshape(DEPTH, 1, 6 * D_MODEL))


def _inproj_kernel(x_ref, sc_ref, sh_ref, w_ref, wh_ref, ona_ref, orest_ref, ohy_ref):
    xm = (x_ref[0] * (1.0 + sc_ref[0]) + sh_ref[0]).astype(BF16)
    step = 512
    for c0 in range(0, NA_COLS, step):
        ona_ref[0, :, c0:c0 + step] = jnp.dot(xm, w_ref[:, c0:c0 + step], preferred_element_type=F32).astype(BF16)
    for c0 in range(0, REST_COLS, step):
        c1 = min(c0 + step, REST_COLS)
        orest_ref[0, :, c0:c1] = jnp.dot(xm, w_ref[:, NA_COLS + c0:NA_COLS + c1], preferred_element_type=F32)
    hy = lax.dot_general(wh_ref[...], xm, _NT, preferred_element_type=F32)
    for j in range(ohy_ref.shape[1]):
        ohy_ref[0, j] = hy[:, j * LANE:(j + 1) * LANE]


def _inproj(x, sc, sh, w_pad, w_hy_t, per_batch_mod):
    B, T, D = x.shape
    tm = min(TOK_TILE, T)
    mod_idx = (lambda b, i: (b, 0, 0)) if per_batch_mod else (lambda b, i: (0, 0, 0))
    return pl.pallas_call(
        _inproj_kernel,
        grid=(B, T // tm),
        in_specs=[pl.BlockSpec((1, tm, D), lambda b, i: (b, i, 0)),
                  pl.BlockSpec((1, 1, D), mod_idx),
                  pl.BlockSpec((1, 1, D), mod_idx),
                  pl.BlockSpec((D, D_IN_PAD), lambda b, i: (0, 0)),
                  pl.BlockSpec((HY_COLS, D), lambda b, i: (0, 0))],
        out_specs=[pl.BlockSpec((1, tm, NA_COLS), lambda b, i: (b, i, 0)),
                   pl.BlockSpec((1, tm, REST_COLS), lambda b, i: (b, i, 0)),
                   pl.BlockSpec((1, tm // LANE, HY_COLS, LANE), lambda b, i: (b, i, 0, 0))],
        out_shape=[jax.ShapeDtypeStruct((B, T, NA_COLS), BF16),
                   jax.ShapeDtypeStruct((B, T, REST_COLS), F32),
                   jax.ShapeDtypeStruct((B, T // LANE, HY_COLS, LANE), F32)],
        compiler_params=_cparams("arbitrary", "arbitrary"),
        name="inproj",
    )(x, sc, sh, w_pad, w_hy_t)


def _stack_heads(q, n_rows):
    head = lax.broadcasted_iota(jnp.int32, (n_rows, MXU_DIM), 1) // NA_HEAD_DIM
    return jnp.concatenate([jnp.where(head == h, q, jnp.zeros_like(q)) for h in range(NA_QUAD)], axis=0)


def _unstack_heads(o, n_rows):
    head = lax.broadcasted_iota(jnp.int32, (n_rows, MXU_DIM), 1) // NA_HEAD_DIM
    out = jnp.zeros((n_rows, MXU_DIM), F32)
    for h in range(NA_QUAD):
        out = jnp.where(head == h, o[h * n_rows:(h + 1) * n_rows], out)
    return out


_NT = (((1,), (1,)), ((), ()))


def _na_kernel(q_ref, k_ref, v_ref, kc_ref, vc_ref, bias_ref, o_ref):
    rt = pl.program_id(2)
    scale = NA_HEAD_DIM ** -0.5
    kc = kc_ref[0]
    vc = vc_ref[0]
    n_loc = NA_WIN_ROWS * GRID_W

    def row(rl, carry):
        r = rt * NA_ROW_TILE + rl
        kr0 = jnp.clip(r - NA_WIN_ROWS // 2, 0, GRID_W - NA_WIN_ROWS)
        dr0 = kr0 - r + NA_WIN_ROWS - 1
        q = q_ref[0, pl.ds(pl.multiple_of(rl * GRID_W, GRID_W), GRID_W), :]
        qs = _stack_heads(q, GRID_W)
        k0 = pl.multiple_of(kr0 * GRID_W, GRID_W)
        ks = k_ref[0, pl.ds(k0, n_loc), :]
        vs = v_ref[0, pl.ds(k0, n_loc), :]
        s_loc = lax.dot_general(qs, ks, _NT, preferred_element_type=F32) * scale + bias_ref[0, dr0]
        s_ctx = lax.dot_general(qs, kc, _NT, preferred_element_type=F32) * scale
        m = jnp.maximum(jnp.max(s_loc, axis=-1, keepdims=True), jnp.max(s_ctx, axis=-1, keepdims=True))
        p_loc = jnp.exp(s_loc - m)
        p_ctx = jnp.exp(s_ctx - m)
        den = jnp.sum(p_loc, axis=-1, keepdims=True) + jnp.sum(p_ctx, axis=-1, keepdims=True)
        o = (jnp.dot(p_loc.astype(BF16), vs, preferred_element_type=F32)
             + jnp.dot(p_ctx.astype(BF16), vc, preferred_element_type=F32)) / den
        o_ref[0, pl.ds(pl.multiple_of(rl * GRID_W, GRID_W), GRID_W), :] = _unstack_heads(o, GRID_W)
        return carry

    lax.fori_loop(0, NA_ROW_TILE, row, 0)


def _na_bias_table(rpb):
    n_lyr = rpb.shape[0]
    c = np.arange(GRID_W)
    kc0 = np.clip(c - NA_WIN_COLS // 2, 0, GRID_W - NA_WIN_COLS)
    kc = np.arange(GRID_W)
    valid = (kc[None, :] >= kc0[:, None]) & (kc[None, :] < kc0[:, None] + NA_WIN_COLS)
    dc = kc[None, :] - c[:, None] + NA_WIN_COLS - 1
    onehot = (np.arange(2 * NA_WIN_COLS - 1)[:, None, None] == dc[None]) & valid[None]
    toep = jnp.einsum('lhrd,dck->lhrck', rpb, jnp.asarray(onehot, F32), precision=lax.Precision.HIGHEST)
    toep = jnp.where(jnp.asarray(valid)[None, None, None], toep, NEG_BIG)
    tab = jnp.stack([toep[:, :, d:d + NA_WIN_ROWS] for d in range(NA_WIN_ROWS)], axis=2)
    tab = tab.transpose(0, 1, 2, 4, 3, 5).reshape(n_lyr, NA_HEADS // NA_QUAD, NA_QUAD, NA_WIN_ROWS, GRID_W,
                                                  NA_WIN_ROWS * GRID_W)
    return tab.transpose(0, 1, 3, 2, 4, 5).reshape(n_lyr, NA_HEADS // NA_QUAD, NA_WIN_ROWS, NA_QUAD * GRID_W,
                                                   NA_WIN_ROWS * GRID_W)


def _na_attention(u_na, uc_na, bias_tab, layer):
    B, L, _ = u_na.shape
    C = uc_na.shape[1]
    nq = NA_WIDTH // MXU_DIM
    tq = NA_ROW_TILE * GRID_W
    return pl.pallas_call(
        _na_kernel,
        grid=(B, nq, L // tq),
        in_specs=[pl.BlockSpec((1, tq, MXU_DIM), lambda b, j, i: (b, i, j)),
                  pl.BlockSpec((1, L, MXU_DIM), lambda b, j, i: (b, 0, nq + j)),
                  pl.BlockSpec((1, L, MXU_DIM), lambda b, j, i: (b, 0, 2 * nq + j)),
                  pl.BlockSpec((1, C, MXU_DIM), lambda b, j, i: (b, 0, nq + j)),
                  pl.BlockSpec((1, C, MXU_DIM), lambda b, j, i: (b, 0, 2 * nq + j)),
                  pl.BlockSpec((None, 1, NA_WIN_ROWS, NA_QUAD * GRID_W, NA_WIN_ROWS * GRID_W),
                               lambda b, j, i: (layer, j, 0, 0, 0))],
        out_specs=pl.BlockSpec((1, tq, MXU_DIM), lambda b, j, i: (b, i, j)),
        out_shape=jax.ShapeDtypeStruct((B, L, NA_WIDTH), F32),
        compiler_params=_cparams("arbitrary", "arbitrary", "arbitrary"),
        name="na_attention",
    )(u_na, u_na, u_na, uc_na, uc_na, bias_tab)


def _ctx_attn_kernel(q_ref, k_ref, v_ref, o_ref):
    C = q_ref.shape[1]
    qs = _stack_heads(q_ref[0], C)
    s = lax.dot_general(qs, k_ref[0], _NT, preferred_element_type=F32) * NA_HEAD_DIM ** -0.5
    p = jnp.exp(s - jnp.max(s, axis=-1, keepdims=True))
    den = jnp.sum(p, axis=-1, keepdims=True)
    o = jnp.dot(p.astype(BF16), v_ref[0], preferred_element_type=F32) / den
    o_ref[0] = _unstack_heads(o, C)


def _ctx_attention(uc_na):
    B, C, _ = uc_na.shape
    nq = NA_WIDTH // MXU_DIM
    return pl.pallas_call(
        _ctx_attn_kernel,
        grid=(B, nq),
        in_specs=[pl.BlockSpec((1, C, MXU_DIM), lambda b, j: (b, 0, j)),
                  pl.BlockSpec((1, C, MXU_DIM), lambda b, j: (b, 0, nq + j)),
                  pl.BlockSpec((1, C, MXU_DIM), lambda b, j: (b, 0, 2 * nq + j))],
        out_specs=pl.BlockSpec((1, C, MXU_DIM), lambda b, j: (b, 0, j)),
        out_shape=jax.ShapeDtypeStruct((B, C, NA_WIDTH), F32),
        compiler_params=_cparams("arbitrary", "arbitrary"),
        name="ctx_attention",
    )(uc_na, uc_na, uc_na)


def _layer_norm_rows(y, g, b):
    mu = jnp.mean(y, axis=-1, keepdims=True)
    d = y - mu
    var = jnp.mean(d * d, axis=-1, keepdims=True)
    return d * lax.rsqrt(var + LN_EPS) * g + b


def _outproj_kernel(na_ref, gla_ref, hy_ref, x_ref, g1_ref, w_ref, lg_ref, lb_ref, sc_ref, sh_ref, wr_ref,
                    xo_ref, h_ref, logit_ref):
    w_hy = w_ref[NA_WIDTH + GLA_WIDTH:, :]
    hy_mix = jnp.concatenate(
        [lax.dot_general(hy_ref[0, j].astype(BF16), w_hy, (((0,), (0,)), ((), ())), preferred_element_type=F32)
         for j in range(hy_ref.shape[1])], axis=0)
    mix = (jnp.dot(na_ref[0].astype(BF16), w_ref[0:NA_WIDTH, :], preferred_element_type=F32)
           + jnp.dot(gla_ref[0].astype(BF16), w_ref[NA_WIDTH:NA_WIDTH + GLA_WIDTH, :], preferred_element_type=F32)
           + hy_mix)
    xn = _layer_norm_rows(DEEPNORM_ALPHA * x_ref[0] + g1_ref[0] * mix, lg_ref[...], lb_ref[...])
    xo_ref[0] = xn
    h = xn * (1.0 + sc_ref[0]) + sh_ref[0]
    h_ref[0] = h.astype(BF16)
    logit_ref[...] = lax.dot_general(wr_ref[...], h, _NT, preferred_element_type=F32, precision=lax.Precision.HIGHEST)


def _outproj(na, gla, hy, x, g1, w_out, ln_g, ln_b, sc2, sh2, w_router, per_batch_mod):
    B, T, D = x.shape
    tm = min(TOK_TILE, T)
    mod_idx = (lambda b, i: (b, 0, 0)) if per_batch_mod else (lambda b, i: (0, 0, 0))
    tok = lambda w: pl.BlockSpec((1, tm, w), lambda b, i: (b, i, 0))
    full = lambda s: pl.BlockSpec(s, lambda b, i: (0,) * len(s))
    mod = pl.BlockSpec((1, 1, D), mod_idx)
    return pl.pallas_call(
        _outproj_kernel,
        grid=(B, T // tm),
        in_specs=[tok(NA_WIDTH), tok(GLA_WIDTH),
                  pl.BlockSpec((1, tm // LANE, HY_CH, LANE), lambda b, i: (b, i, 0, 0)), tok(D), mod, full((MIX_WIDTH, D)),
                  full((1, D)), full((1, D)), mod, mod, full((N_EXPERTS, D))],
        out_specs=[tok(D), tok(D), pl.BlockSpec((N_EXPERTS, tm), lambda b, i: (0, b * (T // tm) + i))],
        out_shape=[jax.ShapeDtypeStruct((B, T, D), F32), jax.ShapeDtypeStruct((B, T, D), BF16),
                   jax.ShapeDtypeStruct((N_EXPERTS, B * T), F32)],
        compiler_params=_cparams("arbitrary", "arbitrary"),
        name="outproj_ln1",
    )(na, gla, hy, x, g1, w_out, ln_g, ln_b, sc2, sh2, w_router)


def _moe_kernel(te_ref, nt_ref, xs_ref, wg_ref, wu_ref, wd_ref, ys_ref):
    i = pl.program_id(0)

    @pl.when(i < nt_ref[0])
    def _():
        xs = xs_ref[...]
        g = jnp.dot(xs, wg_ref[0].astype(BF16), preferred_element_type=F32)
        u = jnp.dot(xs, wu_ref[0].astype(BF16), preferred_element_type=F32)
        a = (g * jax.nn.sigmoid(g) * u).astype(BF16)
        ys_ref[...] = jnp.dot(a, wd_ref[0].astype(BF16), preferred_element_type=F32).astype(ys_ref.dtype)

    @pl.when(i >= nt_ref[0])
    def _():
        ys_ref[...] = jnp.zeros_like(ys_ref)


def _moe_grouped(tile_expert, n_used, xs, we_g, we_u, we_d, layer):
    Mp, D = xs.shape
    n_tiles = Mp // MOE_TILE
    H = EXPERT_HIDDEN
    grid_spec = pltpu.PrefetchScalarGridSpec(
        num_scalar_prefetch=2,
        grid=(n_tiles,),
        in_specs=[pl.BlockSpec((MOE_TILE, D), lambda i, te, nt: (i, 0)),
                  pl.BlockSpec((None, 1, D, H), lambda i, te, nt: (layer, te[i], 0, 0)),
                  pl.BlockSpec((None, 1, D, H), lambda i, te, nt: (layer, te[i], 0, 0)),
                  pl.BlockSpec((None, 1, H, D), lambda i, te, nt: (layer, te[i], 0, 0))],
        out_specs=pl.BlockSpec((MOE_TILE, D), lambda i, te, nt: (i, 0)),
    )
    return pl.pallas_call(
        _moe_kernel,
        grid_spec=grid_spec,
        out_shape=jax.ShapeDtypeStruct((Mp, D), BF16),
        compiler_params=_cparams("arbitrary"),
        name="moe_experts",
    )(tile_expert, n_used, xs, we_g, we_u, we_d)


def _shared_kernel(x_ref, h_ref, r_ref, g2_ref, wg_ref, wu_ref, wd_ref, lg_ref, lb_ref, o_ref):
    h = h_ref[0]
    g = jnp.dot(h, wg_ref[...], preferred_element_type=F32)
    u = jnp.dot(h, wu_ref[...], preferred_element_type=F32)
    a = (g * jax.nn.sigmoid(g) * u).astype(BF16)
    ff = jnp.dot(a, wd_ref[...], preferred_element_type=F32) + r_ref[0]
    o_ref[0] = _layer_norm_rows(DEEPNORM_ALPHA * x_ref[0] + g2_ref[0] * ff, lg_ref[...], lb_ref[...])


def _shared_ln2(x, h, routed, g2, ws_g, ws_u, ws_d, ln_g, ln_b, per_batch_mod):
    B, T, D = x.shape
    tm = min(TOK_TILE, T)
    mod_idx = (lambda b, i: (b, 0, 0)) if per_batch_mod else (lambda b, i: (0, 0, 0))
    tok = pl.BlockSpec((1, tm, D), lambda b, i: (b, i, 0))
    full = lambda s: pl.BlockSpec(s, lambda b, i: (0,) * len(s))
    return pl.pallas_call(
        _shared_kernel,
        grid=(B, T // tm),
        in_specs=[tok, tok, tok, pl.BlockSpec((1, 1, D), mod_idx), full((D, EXPERT_HIDDEN)), full((D, EXPERT_HIDDEN)),
                  full((EXPERT_HIDDEN, D)), full((1, D)), full((1, D))],
        out_specs=tok,
        out_shape=jax.ShapeDtypeStruct((B, T, D), F32),
        compiler_params=_cparams("arbitrary", "arbitrary"),
        name="shared_ln2",
    )(x, h, routed, g2, ws_g, ws_u, ws_d, ln_g, ln_b)


def _first_max(vals, iota, n):
    m = jnp.max(vals, axis=0, keepdims=True)
    idx = jnp.min(jnp.where(vals == m, iota, n), axis=0, keepdims=True)
    return m, idx


def _route_kernel(lt_ref, b_ref, eidx_ref, w_ref, rank_ref, cnt_ref, base_ref):
    @pl.when(pl.program_id(0) == 0)
    def _():
        base_ref[...] = jnp.zeros_like(base_ref)

    tm = lt_ref.shape[1]
    per = N_EXPERTS // N_GROUPS
    s = jax.nn.sigmoid(lt_ref[...])
    sel = s + b_ref[...]
    io_g = lax.broadcasted_iota(jnp.int32, (per, tm), 0)
    scores = []
    for g in range(N_GROUPS):
        blk = sel[g * per:(g + 1) * per]
        m1, i1 = _first_max(blk, io_g, per)
        m2 = jnp.max(jnp.where(io_g == i1, -jnp.inf, blk), axis=0, keepdims=True)
        scores.append(m1 + m2)
    cur = jnp.concatenate(scores, axis=0)
    io_8 = lax.broadcasted_iota(jnp.int32, (N_GROUPS, tm), 0)
    gmask = jnp.zeros((N_GROUPS, tm), F32)
    for _ in range(TOPK_GROUPS):
        _, gi = _first_max(cur, io_8, N_GROUPS)
        hit = io_8 == gi
        gmask = jnp.where(hit, 1.0, gmask)
        cur = jnp.where(hit, -jnp.inf, cur)
    masked = jnp.concatenate(
        [jnp.where(gmask[g:g + 1] > 0.0, sel[g * per:(g + 1) * per], -jnp.inf) for g in range(N_GROUPS)], axis=0)
    io_e = lax.broadcasted_iota(jnp.int32, (N_EXPERTS, tm), 0)
    chosen = jnp.zeros((N_EXPERTS, tm), F32)
    eidx, gates = [], []
    for _ in range(TOP_K):
        _, ei = _first_max(masked, io_e, N_EXPERTS)
        hit = io_e == ei
        eidx.append(ei)
        gates.append(jnp.sum(jnp.where(hit, s, 0.0), axis=0, keepdims=True))
        masked = jnp.where(hit, -jnp.inf, masked)
        chosen = jnp.where(hit, 1.0, chosen)
    wk = jnp.concatenate(gates, axis=0)
    w_ref[...] = wk / jnp.sum(wk, axis=0, keepdims=True) * ROUTED_SCALE
    eidx_ref[...] = jnp.concatenate(eidx, axis=0)
    earlier = (lax.broadcasted_iota(jnp.int32, (tm, tm), 0) < lax.broadcasted_iota(jnp.int32, (tm, tm), 1))
    pos = jnp.dot(chosen.astype(BF16), jnp.where(earlier, 1.0, 0.0).astype(BF16), preferred_element_type=F32)
    pos = pos + base_ref[...]
    ranks = [jnp.sum(jnp.where(io_e == eidx[k], pos, 0.0), axis=0, keepdims=True) for k in range(TOP_K)]
    rank_ref[...] = jnp.concatenate(ranks, axis=0).astype(jnp.int32)
    base_ref[...] = base_ref[...] + jnp.sum(chosen, axis=1, keepdims=True)
    cnt_ref[...] = base_ref[...]


def _route(logits_t, b_corr):
    E, T = logits_t.shape
    tm = ROUTE_TILE
    tokk = pl.BlockSpec((TOP_K, tm), lambda i: (0, i))
    return pl.pallas_call(
        _route_kernel,
        grid=(T // tm,),
        in_specs=[pl.BlockSpec((E, tm), lambda i: (0, i)), pl.BlockSpec((E, 1), lambda i: (0, 0))],
        out_specs=[tokk, tokk, tokk, pl.BlockSpec((E, 1), lambda i: (0, 0))],
        out_shape=[jax.ShapeDtypeStruct((TOP_K, T), jnp.int32), jax.ShapeDtypeStruct((TOP_K, T), F32),
                   jax.ShapeDtypeStruct((TOP_K, T), jnp.int32), jax.ShapeDtypeStruct((E, 1), F32)],
        scratch_shapes=[pltpu.VMEM((E, 1), F32)],
        compiler_params=_cparams("arbitrary"),
        name="route",
    )(logits_t, b_corr.reshape(E, 1))


def _slot_kernel(eidx_ref, rank_ref, pstart_ref, dest_ref):
    tm = eidx_ref.shape[1]
    io_e = lax.broadcasted_iota(jnp.int32, (N_EXPERTS, tm), 0)
    ei = eidx_ref[...]
    starts = [jnp.sum(jnp.where(io_e == ei[k:k + 1], pstart_ref[...], 0.0), axis=0, keepdims=True)
              for k in range(TOP_K)]
    dest_ref[...] = jnp.concatenate(starts, axis=0).astype(jnp.int32) + rank_ref[...]


def _slots(eidx, rank, pstart):
    K, T = eidx.shape
    tm = ROUTE_TILE
    tokk = pl.BlockSpec((K, tm), lambda i: (0, i))
    return pl.pallas_call(
        _slot_kernel,
        grid=(T // tm,),
        in_specs=[tokk, tokk, pl.BlockSpec((N_EXPERTS, 1), lambda i: (0, 0))],
        out_specs=tokk,
        out_shape=jax.ShapeDtypeStruct((K, T), jnp.int32),
        compiler_params=_cparams("arbitrary"),
        name="route_slots",
    )(eidx, rank, pstart)


def _moe_routed(h_flat, logits_t, b_corr, we_g, we_u, we_d, layer):
    T = h_flat.shape[0]
    n_tiles = T * TOP_K // MOE_TILE + N_EXPERTS
    eidx, gates, rank, counts = _route(logits_t, b_corr)
    counts = counts[:, 0].astype(jnp.int32)
    padded = (counts + MOE_TILE - 1) // MOE_TILE * MOE_TILE
    pends = jnp.cumsum(padded)
    tile_expert = jnp.minimum(jnp.searchsorted(pends, jnp.arange(n_tiles, dtype=jnp.int32) * MOE_TILE, side='right'),
                              N_EXPERTS - 1).astype(jnp.int32)
    n_used = (pends[-1] // MOE_TILE).astype(jnp.int32).reshape(1)
    dest = _slots(eidx, rank, (pends - padded).astype(F32).reshape(N_EXPERTS, 1))
    tok = jnp.broadcast_to(jnp.arange(T, dtype=jnp.int32)[None], (TOP_K, T))
    src_tok = (jnp.arange(n_tiles * MOE_TILE, dtype=jnp.int32) % T).at[dest.reshape(-1)].set(
        tok.reshape(-1), unique_indices=True, mode='promise_in_bounds')
    xs = jnp.take(h_flat, src_tok, axis=0)
    ys = _moe_grouped(tile_expert, n_used, xs, we_g, we_u, we_d, layer)
    picked = jnp.take(ys, dest.T, axis=0)
    return jnp.sum(picked.astype(F32) * gates.T[:, :, None], axis=1)


_GLA_QK_BLK, _GLA_V_BLK, _GLA_R_BLK = 0, 1, 2
_GLA_LR_BLK = (2 * GLA_KEY_WIDTH + 2 * GLA_WIDTH) // LANE


def _rope_tables(L):
    t = np.arange(L)
    lane = np.arange(GLA_KEY_WIDTH)
    d = lane % GLA_DK
    pos = np.where(d[None, :] < GLA_DK // 2, (t // GRID_W)[:, None], (t % GRID_W)[:, None]).astype(np.float32)
    quarter = GLA_DK // 4
    inv = ROPE_BASE ** (-jnp.arange(quarter, dtype=F32) / quarter)
    ang = jnp.asarray(pos) * inv[jnp.asarray(d % quarter)][None, :]
    sign = np.where(d % (2 * quarter) < quarter, -1.0, 1.0).astype(np.float32)
    return jnp.cos(ang), jnp.sin(ang) * sign[None, :]


def _rope_partner(x):
    lane = lax.broadcasted_iota(jnp.int32, x.shape, 1)
    quarter = GLA_DK // 4
    return jnp.where(lane % (2 * quarter) < quarter, pltpu.roll(x, GLA_KEY_WIDTH - quarter, 1), pltpu.roll(x, quarter, 1))


def _log_sigmoid(x):
    return jnp.minimum(x, 0.0) - jnp.log(1.0 + jnp.exp(-jnp.abs(x)))


def _gla_kernel(reverse, finalize, *refs):
    if finalize:
        (qk_ref, v_ref, lr_ref, cos_ref, sin_ref, wa_ref, ba_ref, s0_ref, of_ref, r_ref, g_ref,
         o_ref, sfin_ref, s_scr) = refs
    else:
        qk_ref, v_ref, lr_ref, cos_ref, sin_ref, wa_ref, ba_ref, s0_ref, o_ref, sfin_ref, s_scr = refs
    hi = lax.Precision.HIGHEST

    @pl.when(pl.program_id(1) == 0)
    def _():
        s_scr[...] = s0_ref[0]

    tg = qk_ref.shape[1]
    C = GLA_CHUNK
    KW, VW = GLA_KEY_WIDTH, GLA_WIDTH
    qk = qk_ref[0]
    cos, sin = cos_ref[...], sin_ref[...]
    q = qk[:, :KW] * GLA_DK ** -0.5
    k = qk[:, KW:]
    q = q * cos + _rope_partner(q) * sin
    k = k * cos + _rope_partner(k) * sin
    v = v_ref[0]
    logit = jnp.dot(lr_ref[0], wa_ref[...], preferred_element_type=F32, precision=hi) + ba_ref[...]
    la = _log_sigmoid(logit) / GLA_GATE_NORM

    ri = lax.broadcasted_iota(jnp.int32, (C, C), 0)
    ci = lax.broadcasted_iota(jnp.int32, (C, C), 1)
    tri = jnp.where((ci >= ri) if reverse else (ci <= ri), 1.0, 0.0)
    tri_h = jnp.concatenate([tri] * GLA_HEADS, axis=0)
    head_k = lax.broadcasted_iota(jnp.int32, (C, KW), 1) // GLA_DK
    head_v = lax.broadcasted_iota(jnp.int32, (C, VW), 1) // GLA_DV
    own_block = (lax.broadcasted_iota(jnp.int32, (KW, VW), 0) // GLA_DK
                 == lax.broadcasted_iota(jnp.int32, (KW, VW), 1) // GLA_DV)
    eye = lax.broadcasted_iota(jnp.int32, (KW, KW), 0) == lax.broadcasted_iota(jnp.int32, (KW, KW), 1)

    S = s_scr[...]
    nc = tg // C
    outs = [None] * nc
    for c in (range(nc - 1, -1, -1) if reverse else range(nc)):
        sl = slice(c * C, (c + 1) * C)
        b = jnp.dot(tri, la[sl], preferred_element_type=F32, precision=hi)
        qt = q[sl] * jnp.exp(b)
        kt = k[sl] * jnp.exp(-b)
        qs = jnp.concatenate([jnp.where(head_k == h, qt, 0.0) for h in range(GLA_HEADS)], axis=0).astype(BF16)
        att = lax.dot_general(qs, kt.astype(BF16), _NT, preferred_element_type=F32)
        att = jnp.where(tri_h > 0.0, att, 0.0)
        vb = v[sl].astype(BF16)
        oi = jnp.dot(att.astype(BF16), vb, preferred_element_type=F32)
        o_intra = jnp.zeros((C, VW), F32)
        for h in range(GLA_HEADS):
            o_intra = jnp.where(head_v == h, oi[h * C:(h + 1) * C], o_intra)
        o_inter = jnp.dot(qt.astype(BF16), S.astype(BF16), preferred_element_type=F32)
        outs[c] = o_intra + o_inter
        b_last = b[0:1] if reverse else b[C - 1:C]
        kdec = (k[sl] * jnp.exp(b_last - b)).astype(BF16)
        kv = lax.dot_general(kdec, vb, (((0,), (0,)), ((), ())), preferred_element_type=F32)
        decay_col = jnp.sum(jnp.where(eye, jnp.exp(b_last), 0.0), axis=1, keepdims=True)
        S = decay_col * S + jnp.where(own_block, kv, 0.0)
    s_scr[...] = S
    sfin_ref[0] = S
    o = jnp.concatenate(outs, axis=0)
    if finalize:
        o = of_ref[0] + o
        same_head = (lax.broadcasted_iota(jnp.int32, (VW, VW), 0) // GLA_DV
                     == lax.broadcasted_iota(jnp.int32, (VW, VW), 1) // GLA_DV)
        ms = jnp.dot(o * o, jnp.where(same_head, 1.0 / GLA_DV, 0.0), preferred_element_type=F32, precision=hi)
        r = r_ref[0]
        o = o * lax.rsqrt(ms + LN_EPS) * g_ref[...] * (r * jax.nn.sigmoid(r))
    o_ref[0] = o


def _gla_pass(u_rest, cos, sin, wa, ba, s0, reverse, fin=None):
    B, T, _ = u_rest.shape
    tg = min(GLA_TILE, T)
    n = T // tg
    ti = (lambda i: n - 1 - i) if reverse else (lambda i: i)
    KW, VW = GLA_KEY_WIDTH, GLA_WIDTH
    ublk = lambda w, j: pl.BlockSpec((1, tg, w), lambda b, i: (b, ti(i), j))
    full = lambda s: pl.BlockSpec(s, lambda b, i: (0,) * len(s))
    state = pl.BlockSpec((1, KW, VW), lambda b, i: (b, 0, 0))
    tab = pl.BlockSpec((tg, KW), lambda b, i: (ti(i), 0))
    in_specs = [ublk(2 * KW, _GLA_QK_BLK), ublk(VW, _GLA_V_BLK), ublk(LANE, _GLA_LR_BLK), tab, tab,
                full((LANE, KW)), full((1, KW)), state]
    args = [u_rest, u_rest, u_rest, cos, sin, wa, ba, s0]
    if fin is not None:
        in_specs += [ublk(VW, 0), ublk(VW, _GLA_R_BLK), full((1, VW))]
        args += [fin[0], u_rest, fin[1]]
    return pl.pallas_call(
        functools.partial(_gla_kernel, reverse, fin is not None),
        grid=(B, n),
        in_specs=in_specs,
        out_specs=[ublk(VW, 0), state],
        out_shape=[jax.ShapeDtypeStruct((B, T, VW), F32), jax.ShapeDtypeStruct((B, KW, VW), F32)],
        scratch_shapes=[pltpu.VMEM((KW, VW), F32)],
        compiler_params=_cparams("arbitrary", "arbitrary"),
        name="gla_bwd" if reverse else "gla_fwd",
    )(*args)


def _gla_bidir(u_rest, cos, sin, w_a2, b_a2, norm_g, s0_f, s0_b):
    def decay_w(d):
        return jnp.zeros((LANE, GLA_KEY_WIDTH), F32).at[d * GLA_RANK:(d + 1) * GLA_RANK].set(w_a2[d])

    o_f, s_f = _gla_pass(u_rest, cos, sin, decay_w(0), b_a2[0:1], s0_f, False)
    g = jnp.tile(norm_g, GLA_HEADS).reshape(1, GLA_WIDTH)
    o, s_b = _gla_pass(u_rest, cos, sin, decay_w(1), b_a2[1:2], s0_b, True, (o_f, g))
    return o, s_f, s_b


HY_LANES = LANE
HY_CH_TILE = 16
HY_CTX_N1 = 16


def _hy_consts(n1, N1):
    W = HY_LANES
    n1p = max(n1, 16)
    n1o = max(n1, 8)
    a = np.arange(N1)
    th1 = 2.0 * np.pi * ((a[:, None] * a[None, :]) % N1) / N1
    f1 = np.zeros((2 * N1, n1p)); f1[:N1, :n1] = np.cos(th1[:, :n1]); f1[N1:, :n1] = -np.sin(th1[:, :n1])
    f1_full = np.concatenate([np.cos(th1), -np.sin(th1)], axis=0)
    ginv = np.zeros((n1o, 2 * N1)); ginv[:n1, :N1] = np.cos(th1.T[:n1]); ginv[:n1, N1:] = -np.sin(th1.T[:n1])
    r = np.arange(W)
    tht = 2.0 * np.pi * ((a[:, None] * r[None, :]) % (N1 * W)) / (N1 * W)
    tr = np.tile(np.cos(tht), (1, HY_CH_TILE)); ti = np.tile(-np.sin(tht), (1, HY_CH_TILE))
    th2 = 2.0 * np.pi * ((r[:, None] * r[None, :]) % W) / W
    c2, s2 = np.cos(th2), -np.sin(th2)
    m2f = np.block([[c2, s2], [-s2, c2]])
    m2i = np.block([[c2, -s2], [s2, c2]])
    f = lambda m: jnp.asarray(m, F32)
    return dict(n1p=n1p, n1o=n1o, f1=f(f1), f1_full=f(f1_full), ginv=f(ginv), tr=f(tr), ti=f(ti), m2f=f(m2f), m2i=f(m2i))


def _short_conv_rows(u, w_ref, b_ref, n_rows):
    R, Wd = u.shape
    lane = lax.broadcasted_iota(jnp.int32, (R, Wd), 1) % HY_LANES
    row = lax.broadcasted_iota(jnp.int32, (R, Wd), 0)
    up = jnp.where(row == 0, 0.0, pltpu.roll(u, 1, 0))
    dn = jnp.where(row == n_rows - 1, 0.0, pltpu.roll(u, R - 1, 0))
    prev = jnp.where(lane == 0, pltpu.roll(up, Wd - (HY_LANES - 1), 1), pltpu.roll(u, 1, 1))
    nxt = jnp.where(lane == HY_LANES - 1, pltpu.roll(dn, HY_LANES - 1, 1), pltpu.roll(u, Wd - 1, 1))
    return b_ref[...] + prev * w_ref[0:1] + u * w_ref[1:2] + nxt * w_ref[2:3]


def _pad_rows(u, rows):
    return u if u.shape[0] == rows else jnp.concatenate([u, jnp.zeros((rows - u.shape[0], u.shape[1]), u.dtype)], axis=0)


def _hy_chunk_dft(z, f1, tr, ti, N1, prec):
    if prec is None:
        a = jnp.dot(f1.astype(BF16), z.astype(BF16), preferred_element_type=F32)
    else:
        a = jnp.dot(f1, z, preferred_element_type=F32, precision=prec)
    ar, ai = a[:N1], a[N1:]
    a_re, a_im = ar * tr - ai * ti, ar * ti + ai * tr
    W = HY_LANES
    return jnp.concatenate(
        [jnp.concatenate([a_re[:, c * W:(c + 1) * W], a_im[:, c * W:(c + 1) * W]], axis=1) for c in range(z.shape[1] // W)],
        axis=0)


def _hy_long_conv(z, kf, f1, ginv, tr, ti, m2f, m2i, N1):
    W = HY_LANES
    m = z.shape[1] // W
    a = _hy_chunk_dft(z, f1, tr, ti, N1, None).astype(BF16)
    x = jnp.dot(a, m2f.astype(BF16), preferred_element_type=F32)
    xr, xi = x[:, :W], x[:, W:]
    kr, ki = kf[:, :W], kf[:, W:]
    y = jnp.concatenate([xr * kr - xi * ki, xr * ki + xi * kr], axis=1).astype(BF16)
    p = jnp.dot(y, m2i.astype(BF16), preferred_element_type=F32)
    pr = jnp.concatenate([p[c * N1:(c + 1) * N1, :W] for c in range(m)], axis=1)
    pi = jnp.concatenate([p[c * N1:(c + 1) * N1, W:] for c in range(m)], axis=1)
    q = jnp.concatenate([pr * tr + pi * ti, pi * tr - pr * ti], axis=0).astype(BF16)
    return jnp.dot(ginv.astype(BF16), q, preferred_element_type=F32)


def _hy_mixer_kernel(n1, N1, n1p, v_ref, x1_ref, x2_ref, swv_ref, sbv_ref, swx1_ref, sbx1_ref, swx2_ref, sbx2_ref,
                     skip_ref, kf_ref, f1_ref, ginv_ref, tr_ref, ti_ref, m2f_ref, m2i_ref, o_ref):
    rows = max(n1, 8)
    cst = (f1_ref[...], ginv_ref[...], tr_ref[...], ti_ref[...], m2f_ref[...], m2i_ref[...], N1)
    z = _short_conv_rows(_pad_rows(v_ref[0], rows), swv_ref, sbv_ref, n1)
    gates = (_short_conv_rows(_pad_rows(x1_ref[0], rows), swx1_ref, sbx1_ref, n1),
             _short_conv_rows(_pad_rows(x2_ref[0], rows), swx2_ref, sbx2_ref, n1))
    for o in range(HY_ORDER):
        y = _hy_long_conv(_pad_rows(z, n1p), kf_ref[o, 0], *cst)
        z = gates[o] * (y + skip_ref[o:o + 1] * z)
    o_ref[0] = z[:n1]


def _hy_kspec_kernel(N1, scale, k_ref, f1_ref, tr_ref, ti_ref, m2f_ref, kf_ref):
    a = _hy_chunk_dft(k_ref[0], f1_ref[...], tr_ref[...], ti_ref[...], N1, lax.Precision.HIGHEST)
    kf_ref[0, 0] = jnp.dot(a, m2f_ref[...], preferred_element_type=F32, precision=lax.Precision.HIGHEST) * scale


def _hy_filter_spectrum(filt, N1, cst):
    Ls, n_ord, _, C = filt.shape
    W = HY_LANES
    N = N1 * W
    kern = jnp.concatenate([filt[:, :, 0], jnp.zeros((N - 2 * Ls + 1, n_ord, C), F32), filt[:0:-1, :, 1]], axis=0)
    kern = kern.reshape(N1, W, n_ord, C).transpose(2, 0, 3, 1).reshape(n_ord, N1, C * W)
    Wd = HY_CH_TILE * W
    nj = C // HY_CH_TILE
    full = lambda s: pl.BlockSpec(s, lambda o, j: (0,) * len(s))
    return pl.pallas_call(
        functools.partial(_hy_kspec_kernel, N1, 1.0 / N),
        grid=(n_ord, nj),
        in_specs=[pl.BlockSpec((1, N1, Wd), lambda o, j: (o, 0, j)), full((2 * N1, N1)), full((N1, Wd)), full((N1, Wd)),
                  full((2 * W, 2 * W))],
        out_specs=pl.BlockSpec((1, 1, HY_CH_TILE * N1, 2 * W), lambda o, j: (o, j, 0, 0)),
        out_shape=jax.ShapeDtypeStruct((n_ord, nj, HY_CH_TILE * N1, 2 * W), F32),
        compiler_params=_cparams("arbitrary", "arbitrary"),
        name="hyena_filter_spectrum",
    )(kern, cst["f1_full"], cst["tr"], cst["ti"], cst["m2f"])


def _hyena_mixer(hy_t, short_w, short_b, filt, skip, N1):
    B, n1, C3, W = hy_t.shape
    C = C3 // (HY_ORDER + 1)
    cst = _hy_consts(n1, N1)
    kf = _hy_filter_spectrum(filt, N1, cst)
    Wd = HY_CH_TILE * W
    nj = C // HY_CH_TILE
    hy2 = hy_t.reshape(B, n1, C3 * W)
    rep = lambda v: jnp.repeat(v, W, axis=-1)
    sw, sb = rep(short_w), rep(short_b).reshape(1, C3 * W)
    sk = rep(skip)
    chan = lambda part: pl.BlockSpec((1, n1, Wd), lambda j, b: (b, 0, part * nj + j))
    wsp = lambda part: pl.BlockSpec((HY_SHORT, Wd), lambda j, b: (0, part * nj + j))
    bsp = lambda part: pl.BlockSpec((1, Wd), lambda j, b: (0, part * nj + j))
    full = lambda a: pl.BlockSpec(a.shape, lambda j, b: (0,) * a.ndim)
    consts = [cst[k] for k in ("f1", "ginv", "tr", "ti", "m2f", "m2i")]
    out = pl.pallas_call(
        functools.partial(_hy_mixer_kernel, n1, N1, cst["n1p"]),
        grid=(nj, B),
        in_specs=[chan(0), chan(1), chan(2), wsp(0), bsp(0), wsp(1), bsp(1), wsp(2), bsp(2),
                  pl.BlockSpec((HY_ORDER, Wd), lambda j, b: (0, j)),
                  pl.BlockSpec((HY_ORDER, 1, HY_CH_TILE * N1, 2 * W), lambda j, b: (0, j, 0, 0))]
                 + [full(a) for a in consts],
        out_specs=pl.BlockSpec((1, n1, Wd), lambda j, b: (b, 0, j)),
        out_shape=jax.ShapeDtypeStruct((B, n1, C * W), F32),
        compiler_params=_cparams("arbitrary", "arbitrary"),
        name="hyena_mixer",
    )(hy2, hy2, hy2, sw, sb, sw, sb, sw, sb, sk, kf, *consts)
    return out.reshape(B, n1, C, W)


def _hyena_filters(L, w1, b1, f1, w2, b2, f2, w3, b3, decay):
    hi = lax.Precision.HIGHEST
    t = jnp.linspace(0.0, 1.0, L, dtype=F32)
    w = 2.0 * math.pi * jnp.arange(L, dtype=F32) / L
    bands = jnp.linspace(1e-4, HY_BANDS - 1, HY_BANDS, dtype=F32)
    ang = w[:, None] * bands[None]
    z = jnp.concatenate([t[:, None], jnp.cos(ang), -jnp.sin(ang)], -1)
    h = jnp.sin(f1 * (jnp.dot(z, w1, precision=hi) + b1))
    h = jnp.sin(f2 * (jnp.dot(h, w2, precision=hi) + b2))
    h = (jnp.dot(h, w3, precision=hi) + b3).reshape(L, HY_ORDER, 2, HY_CH)
    window = jnp.exp(-t[:, None, None, None] * jnp.abs(decay)[None])
    return h * window


def kernel(x, c, ctx, c_ctx, w_mod, b_mod, w_in, na_rpb, gla_w_a2, gla_b_a2, gla_norm_g, hy_short_w, hy_short_b, hy_w1, hy_b1, hy_f1, hy_w2, hy_b2, hy_f2, hy_w3, hy_b3, hy_decay, hy_skip, w_out, ln1_g, ln1_b, router_w, router_b, we_gate, we_up, we_down, ws_gate, ws_up, ws_down, ln2_g, ln2_b):
    B, L, D = x.shape
    C = ctx.shape[1]
    rope_cos, rope_sin = _rope_tables(L)
    ctx_cos, ctx_sin = jnp.ones((C, GLA_KEY_WIDTH), F32), jnp.zeros((C, GLA_KEY_WIDTH), F32)
    zero_state = jnp.zeros((B, GLA_KEY_WIDTH, GLA_WIDTH), F32)
    na_bias = _na_bias_table(na_rpb)

    n_mod = -(-(B + 1) // 8) * 8
    cs = jnp.zeros((n_mod, D), F32).at[:B].set(c).at[B].set(c_ctx)
    mod_all = _modulation(cs, w_mod, b_mod)

    xc = ctx
    for l in range(DEPTH):
        last = l == DEPTH - 1
        mods = mod_all[l].reshape(n_mod, 6, 1, D)
        lat = lambda j: mods[:B, j]
        cm = lambda j: mods[B:B + 1, j]
        n_tok = NA_COLS + GLA_COLS
        w_pad = jnp.pad(w_in[l][:, :n_tok], ((0, 0), (0, D_IN_PAD - n_tok))).astype(BF16)
        w_hy_t = w_in[l][:, n_tok:].T.astype(BF16)
        w_out_b = w_out[l].astype(BF16)
        lg1, lb1 = ln1_g[l].reshape(1, D), ln1_b[l].reshape(1, D)
        lg2, lb2 = ln2_g[l].reshape(1, D), ln2_b[l].reshape(1, D)

        u_na, u_rest, hy = _inproj(x, lat(1), lat(0), w_pad, w_hy_t, True)
        uc_na, uc_rest, hyc = _inproj(xc, cm(1), cm(0), w_pad, w_hy_t, False)

        na_lat = _na_attention(u_na, uc_na, na_bias, l)

        gla_c, s_ctx_f, s_ctx_b = _gla_bidir(uc_rest, ctx_cos, ctx_sin, gla_w_a2[l], gla_b_a2[l], gla_norm_g[l],
                                             zero_state, zero_state)
        gla_lat, _, _ = _gla_bidir(u_rest, rope_cos, rope_sin, gla_w_a2[l], gla_b_a2[l], gla_norm_g[l],
                                   s_ctx_f, s_ctx_b)

        filt_args = (hy_w1[l], hy_b1[l], hy_f1[l], hy_w2[l], hy_b2[l], hy_f2[l], hy_w3[l], hy_b3[l], hy_decay[l])
        hy_lat = _hyena_mixer(hy, hy_short_w[l], hy_short_b[l], _hyena_filters(L, *filt_args), hy_skip[l],
                              2 * L // HY_LANES)

        wr_t = router_w[l].T
        x, h_lat, logit_lat = _outproj(na_lat, gla_lat, hy_lat, x, lat(2), w_out_b, lg1, lb1, lat(4), lat(3), wr_t, True)
        if not last:
            na_c = _ctx_attention(uc_na)
            hy_c = _hyena_mixer(hyc, hy_short_w[l], hy_short_b[l], _hyena_filters(C, *filt_args), hy_skip[l],
                                HY_CTX_N1)
            xc, h_c, logit_c = _outproj(na_c, gla_c, hy_c, xc, cm(2), w_out_b, lg1, lb1, cm(4), cm(3), wr_t, False)

        moe = lambda h, lt: _moe_routed(h, lt, router_b[l], we_gate, we_up, we_down, l)
        bg = B // MOE_GROUPS if B % MOE_GROUPS == 0 else B
        routed = jnp.concatenate(
            [moe(h_lat[b0:b0 + bg].reshape(bg * L, D), logit_lat[:, b0 * L:(b0 + bg) * L]) for b0 in range(0, B, bg)],
            axis=0).reshape(B, L, D)
        wsg, wsu, wsd = ws_gate[l].astype(BF16), ws_up[l].astype(BF16), ws_down[l].astype(BF16)
        x = _shared_ln2(x, h_lat, routed, lat(5), wsg, wsu, wsd, lg2, lb2, True)
        if not last:
            routed_c = moe(h_c.reshape(B * C, D), logit_c).reshape(B, C, D)
            xc = _shared_ln2(xc, h_c, routed_c, cm(5), wsg, wsu, wsd, lg2, lb2, False)
    return x
```

```python
import functools
import math

import numpy as np
import jax
import jax.numpy as jnp
from jax import lax
from jax.experimental import pallas as pl
from jax.experimental.pallas import tpu as pltpu

F32 = jnp.float32
BF16 = jnp.bfloat16

D_MODEL = 1024
DEPTH = 4
GRID_W = 64
CTX_LEN = 256

NA_HEADS = 8
NA_HEAD_DIM = 64
NA_WIDTH = NA_HEADS * NA_HEAD_DIM
NA_WIN_ROWS = 8
NA_WIN_COLS = 16

GLA_HEADS = 4
GLA_DK = 32
GLA_DV = 64
GLA_KEY_WIDTH = GLA_HEADS * GLA_DK
GLA_WIDTH = GLA_HEADS * GLA_DV
GLA_RANK = 16
GLA_GATE_NORM = 16.0
GLA_CHUNK = 64

HY_CH = 256
HY_ORDER = 2
HY_SHORT = 3
HY_BANDS = 16
HY_EMB = 1 + 2 * HY_BANDS

MIX_WIDTH = NA_WIDTH + GLA_WIDTH + HY_CH
IN_SPLITS = (NA_WIDTH, NA_WIDTH, NA_WIDTH, GLA_KEY_WIDTH, GLA_KEY_WIDTH, GLA_WIDTH, GLA_WIDTH,
             2 * GLA_RANK, (HY_ORDER + 1) * HY_CH)
D_IN = sum(IN_SPLITS)
ROPE_BASE = 10000.0

N_EXPERTS = 128
TOP_K = 8
N_GROUPS = 8
TOPK_GROUPS = 4
EXPERT_HIDDEN = 256
ROUTED_SCALE = 2.5

DEEPNORM_ALPHA = (2 * DEPTH) ** 0.25
LN_EPS = 1e-6

LANE = 128
MXU_DIM = 256
VMEM_LIMIT = 48 * 1024 * 1024

NA_COLS = 3 * NA_WIDTH
GLA_COLS = sum(IN_SPLITS[3:8])
HY_COLS = IN_SPLITS[8]
REST_COLS = -(-GLA_COLS // LANE) * LANE
D_IN_PAD = NA_COLS + REST_COLS
NA_QUAD = MXU_DIM // NA_HEAD_DIM
NEG_BIG = -1e30

TOK_TILE = 512
MOE_TILE = 512
NA_ROW_TILE = 8
NA_ROW_UNROLL = 4
ROUTE_TILE = 256
GLA_TILE = 512


def _cparams(*sem):
    return pltpu.CompilerParams(dimension_semantics=sem, vmem_limit_bytes=VMEM_LIMIT)


def _mod_kernel(c_ref, w_ref, b_ref, o_ref):
    c = c_ref[...]
    s = c * jax.nn.sigmoid(c)
    o_ref[0] = jnp.dot(s, w_ref[0], preferred_element_type=F32, precision=lax.Precision.HIGHEST) + b_ref[0]


def _modulation(cs, w_mod, b_mod):
    R = cs.shape[0]
    tn = 1536
    return pl.pallas_call(
        _mod_kernel,
        grid=(DEPTH, 6 * D_MODEL // tn),
        in_specs=[pl.BlockSpec((R, D_MODEL), lambda l, j: (0, 0)),
                  pl.BlockSpec((1, D_MODEL, tn), lambda l, j: (l, 0, j)),
                  pl.BlockSpec((1, 1, tn), lambda l, j: (l, 0, j))],
        out_specs=pl.BlockSpec((1, R, tn), lambda l, j: (l, 0, j)),
        out_shape=jax.ShapeDtypeStruct((DEPTH, R, 6 * D_MODEL), F32),
        compiler_params=_cparams("arbitrary", "arbitrary"),
        name="modulation",
    )(cs, w_mod, b_mod.reshape(DEPTH, 1, 6 * D_MODEL))


def _inproj_kernel(x_ref, sc_ref, sh_ref, w_ref, wh_ref, ona_ref, orest_ref, ohy_ref):
    xm = (x_ref[0] * (1.0 + sc_ref[0]) + sh_ref[0]).astype(BF16)
    step = 512
    for c0 in range(0, NA_COLS, step):
        ona_ref[0, :, c0:c0 + step] = jnp.dot(xm, w_ref[:, c0:c0 + step], preferred_element_type=F32).astype(BF16)
    for c0 in range(0, REST_COLS, step):
        c1 = min(c0 + step, REST_COLS)
        orest_ref[0, :, c0:c1] = jnp.dot(xm, w_ref[:, NA_COLS + c0:NA_COLS + c1], preferred_element_type=F32)
    hy = lax.dot_general(wh_ref[...], xm, _NT, preferred_element_type=F32)
    for j in range(ohy_ref.shape[1]):
        ohy_ref[0, j] = hy[:, j * LANE:(j + 1) * LANE]


def _inproj(x, sc, sh, w_pad, w_hy_t, per_batch_mod):
    B, T, D = x.shape
    tm = min(TOK_TILE, T)
    mod_idx = (lambda b, i: (b, 0, 0)) if per_batch_mod else (lambda b, i: (0, 0, 0))
    return pl.pallas_call(
        _inproj_kernel,
        grid=(B, T // tm),
        in_specs=[pl.BlockSpec((1, tm, D), lambda b, i: (b, i, 0)),
                  pl.BlockSpec((1, 1, D), mod_idx),
                  pl.BlockSpec((1, 1, D), mod_idx),
                  pl.BlockSpec((D, D_IN_PAD), lambda b, i: (0, 0)),
                  pl.BlockSpec((HY_COLS, D), lambda b, i: (0, 0))],
        out_specs=[pl.BlockSpec((1, tm, NA_COLS), lambda b, i: (b, i, 0)),
                   pl.BlockSpec((1, tm, REST_COLS), lambda b, i: (b, i, 0)),
                   pl.BlockSpec((1, tm // LANE, HY_COLS, LANE), lambda b, i: (b, i, 0, 0))],
        out_shape=[jax.ShapeDtypeStruct((B, T, NA_COLS), BF16),
                   jax.ShapeDtypeStruct((B, T, REST_COLS), F32),
                   jax.ShapeDtypeStruct((B, T // LANE, HY_COLS, LANE), F32)],
        compiler_params=_cparams("arbitrary", "arbitrary"),
        name="inproj",
    )(x, sc, sh, w_pad, w_hy_t)


def _stack_heads(q, n_rows):
    head = lax.broadcasted_iota(jnp.int32, (n_rows, MXU_DIM), 1) // NA_HEAD_DIM
    return jnp.concatenate([jnp.where(head == h, q, jnp.zeros_like(q)) for h in range(NA_QUAD)], axis=0)


def _unstack_heads(o, n_rows):
    head = lax.broadcasted_iota(jnp.int32, (n_rows, MXU_DIM), 1) // NA_HEAD_DIM
    out = jnp.zeros((n_rows, MXU_DIM), F32)
    for h in range(NA_QUAD):
        out = jnp.where(head == h, o[h * n_rows:(h + 1) * n_rows], out)
    return out


_NT = (((1,), (1,)), ((), ()))


def _na_kernel(q_ref, k_ref, v_ref, kc_ref, vc_ref, bias_ref, o_ref):
    rt = pl.program_id(2)
    scale = NA_HEAD_DIM ** -0.5
    kc = kc_ref[0]
    vc = vc_ref[0]
    n_loc = NA_WIN_ROWS * GRID_W

    def row(rl, carry):
        r = rt * NA_ROW_TILE + rl
        kr0 = jnp.clip(r - NA_WIN_ROWS // 2, 0, GRID_W - NA_WIN_ROWS)
        dr0 = kr0 - r + NA_WIN_ROWS - 1
        q = q_ref[0, pl.ds(pl.multiple_of(rl * GRID_W, GRID_W), GRID_W), :]
        qs = _stack_heads(q, GRID_W)
        k0 = pl.multiple_of(kr0 * GRID_W, GRID_W)
        ks = k_ref[0, pl.ds(k0, n_loc), :]
        vs = v_ref[0, pl.ds(k0, n_loc), :]
        s_loc = lax.dot_general(qs, ks, _NT, preferred_element_type=F32) * scale + bias_ref[0, dr0]
        s_ctx = lax.dot_general(qs, kc, _NT, preferred_element_type=F32) * scale
        m = jnp.maximum(jnp.max(s_loc, axis=-1, keepdims=True), jnp.max(s_ctx, axis=-1, keepdims=True))
        p_loc = jnp.exp(s_loc - m)
        p_ctx = jnp.exp(s_ctx - m)
        den = jnp.sum(p_loc, axis=-1, keepdims=True) + jnp.sum(p_ctx, axis=-1, keepdims=True)
        o = (jnp.dot(p_loc.astype(BF16), vs, preferred_element_type=F32)
             + jnp.dot(p_ctx.astype(BF16), vc, preferred_element_type=F32)) / den
        o_ref[0, pl.ds(pl.multiple_of(rl * GRID_W, GRID_W), GRID_W), :] = _unstack_heads(o, GRID_W)
        return carry

    lax.fori_loop(0, NA_ROW_TILE, row, 0, unroll=NA_ROW_UNROLL)


def _na_bias_table(rpb):
    n_lyr = rpb.shape[0]
    c = np.arange(GRID_W)
    kc0 = np.clip(c - NA_WIN_COLS // 2, 0, GRID_W - NA_WIN_COLS)
    kc = np.arange(GRID_W)
    valid = (kc[None, :] >= kc0[:, None]) & (kc[None, :] < kc0[:, None] + NA_WIN_COLS)
    dc = kc[None, :] - c[:, None] + NA_WIN_COLS - 1
    onehot = (np.arange(2 * NA_WIN_COLS - 1)[:, None, None] == dc[None]) & valid[None]
    toep = jnp.einsum('lhrd,dck->lhrck', rpb, jnp.asarray(onehot, F32), precision=lax.Precision.HIGHEST)
    toep = jnp.where(jnp.asarray(valid)[None, None, None], toep, NEG_BIG)
    tab = jnp.stack([toep[:, :, d:d + NA_WIN_ROWS] for d in range(NA_WIN_ROWS)], axis=2)
    tab = tab.transpose(0, 1, 2, 4, 3, 5).reshape(n_lyr, NA_HEADS // NA_QUAD, NA_QUAD, NA_WIN_ROWS, GRID_W,
                                                  NA_WIN_ROWS * GRID_W)
    return tab.transpose(0, 1, 3, 2, 4, 5).reshape(n_lyr, NA_HEADS // NA_QUAD, NA_WIN_ROWS, NA_QUAD * GRID_W,
                                                   NA_WIN_ROWS * GRID_W)


def _na_attention(u_na, uc_na, bias_tab, layer):
    B, L, _ = u_na.shape
    C = uc_na.shape[1]
    nq = NA_WIDTH // MXU_DIM
    tq = NA_ROW_TILE * GRID_W
    return pl.pallas_call(
        _na_kernel,
        grid=(B, nq, L // tq),
        in_specs=[pl.BlockSpec((1, tq, MXU_DIM), lambda b, j, i: (b, i, j)),
                  pl.BlockSpec((1, L, MXU_DIM), lambda b, j, i: (b, 0, nq + j)),
                  pl.BlockSpec((1, L, MXU_DIM), lambda b, j, i: (b, 0, 2 * nq + j)),
                  pl.BlockSpec((1, C, MXU_DIM), lambda b, j, i: (b, 0, nq + j)),
                  pl.BlockSpec((1, C, MXU_DIM), lambda b, j, i: (b, 0, 2 * nq + j)),
                  pl.BlockSpec((None, 1, NA_WIN_ROWS, NA_QUAD * GRID_W, NA_WIN_ROWS * GRID_W),
                               lambda b, j, i: (layer, j, 0, 0, 0))],
        out_specs=pl.BlockSpec((1, tq, MXU_DIM), lambda b, j, i: (b, i, j)),
        out_shape=jax.ShapeDtypeStruct((B, L, NA_WIDTH), F32),
        compiler_params=_cparams("arbitrary", "arbitrary", "arbitrary"),
        name="na_attention",
    )(u_na, u_na, u_na, uc_na, uc_na, bias_tab)


def _ctx_attn_kernel(q_ref, k_ref, v_ref, o_ref):
    C = q_ref.shape[1]
    qs = _stack_heads(q_ref[0], C)
    s = lax.dot_general(qs, k_ref[0], _NT, preferred_element_type=F32) * NA_HEAD_DIM ** -0.5
    p = jnp.exp(s - jnp.max(s, axis=-1, keepdims=True))
    den = jnp.sum(p, axis=-1, keepdims=True)
    o = jnp.dot(p.astype(BF16), v_ref[0], preferred_element_type=F32) / den
    o_ref[0] = _unstack_heads(o, C)


def _ctx_attention(uc_na):
    B, C, _ = uc_na.shape
    nq = NA_WIDTH // MXU_DIM
    return pl.pallas_call(
        _ctx_attn_kernel,
        grid=(B, nq),
        in_specs=[pl.BlockSpec((1, C, MXU_DIM), lambda b, j: (b, 0, j)),
                  pl.BlockSpec((1, C, MXU_DIM), lambda b, j: (b, 0, nq + j)),
                  pl.BlockSpec((1, C, MXU_DIM), lambda b, j: (b, 0, 2 * nq + j))],
        out_specs=pl.BlockSpec((1, C, MXU_DIM), lambda b, j: (b, 0, j)),
        out_shape=jax.ShapeDtypeStruct((B, C, NA_WIDTH), F32),
        compiler_params=_cparams("arbitrary", "arbitrary"),
        name="ctx_attention",
    )(uc_na, uc_na, uc_na)


def _layer_norm_rows(y, g, b):
    mu = jnp.mean(y, axis=-1, keepdims=True)
    d = y - mu
    var = jnp.mean(d * d, axis=-1, keepdims=True)
    return d * lax.rsqrt(var + LN_EPS) * g + b


def _outproj_kernel(na_ref, gla_ref, hy_ref, x_ref, g1_ref, w_ref, lg_ref, lb_ref, sc_ref, sh_ref, wr_ref,
                    xo_ref, h_ref, logit_ref):
    w_hy = w_ref[NA_WIDTH + GLA_WIDTH:, :]
    hy_mix = jnp.concatenate(
        [lax.dot_general(hy_ref[0, j].astype(BF16), w_hy, (((0,), (0,)), ((), ())), preferred_element_type=F32)
         for j in range(hy_ref.shape[1])], axis=0)
    mix = (jnp.dot(na_ref[0].astype(BF16), w_ref[0:NA_WIDTH, :], preferred_element_type=F32)
           + jnp.dot(gla_ref[0].astype(BF16), w_ref[NA_WIDTH:NA_WIDTH + GLA_WIDTH, :], preferred_element_type=F32)
           + hy_mix)
    xn = _layer_norm_rows(DEEPNORM_ALPHA * x_ref[0] + g1_ref[0] * mix, lg_ref[...], lb_ref[...])
    xo_ref[0] = xn
    h = xn * (1.0 + sc_ref[0]) + sh_ref[0]
    h_ref[0] = h.astype(BF16)
    logit_ref[...] = lax.dot_general(wr_ref[...], h, _NT, preferred_element_type=F32, precision=lax.Precision.HIGHEST)


def _outproj(na, gla, hy, x, g1, w_out, ln_g, ln_b, sc2, sh2, w_router, per_batch_mod):
    B, T, D = x.shape
    tm = min(TOK_TILE, T)
    mod_idx = (lambda b, i: (b, 0, 0)) if per_batch_mod else (lambda b, i: (0, 0, 0))
    tok = lambda w: pl.BlockSpec((1, tm, w), lambda b, i: (b, i, 0))
    full = lambda s: pl.BlockSpec(s, lambda b, i: (0,) * len(s))
    mod = pl.BlockSpec((1, 1, D), mod_idx)
    return pl.pallas_call(
        _outproj_kernel,
        grid=(B, T // tm),
        in_specs=[tok(NA_WIDTH), tok(GLA_WIDTH),
                  pl.BlockSpec((1, tm // LANE, HY_CH, LANE), lambda b, i: (b, i, 0, 0)), tok(D), mod, full((MIX_WIDTH, D)),
                  full((1, D)), full((1, D)), mod, mod, full((N_EXPERTS, D))],
        out_specs=[tok(D), tok(D), pl.BlockSpec((N_EXPERTS, tm), lambda b, i: (0, b * (T // tm) + i))],
        out_shape=[jax.ShapeDtypeStruct((B, T, D), F32), jax.ShapeDtypeStruct((B, T, D), BF16),
                   jax.ShapeDtypeStruct((N_EXPERTS, B * T), F32)],
        compiler_params=_cparams("arbitrary", "arbitrary"),
        name="outproj_ln1",
    )(na, gla, hy, x, g1, w_out, ln_g, ln_b, sc2, sh2, w_router)


def _moe_kernel(te_ref, nt_ref, xs_ref, wg_ref, wu_ref, wd_ref, ys_ref, wg_s, wu_s, wd_s):
    i = pl.program_id(0)

    @pl.when(jnp.logical_or(i == 0, te_ref[i] != te_ref[jnp.maximum(i - 1, 0)]))
    def _():
        wg_s[...] = wg_ref[0].astype(BF16)
        wu_s[...] = wu_ref[0].astype(BF16)
        wd_s[...] = wd_ref[0].astype(BF16)

    @pl.when(i < nt_ref[0])
    def _():
        xs = xs_ref[...]
        g = jnp.dot(xs, wg_s[...], preferred_element_type=F32)
        u = jnp.dot(xs, wu_s[...], preferred_element_type=F32)
        a = (g * jax.nn.sigmoid(g) * u).astype(BF16)
        ys_ref[...] = jnp.dot(a, wd_s[...], preferred_element_type=F32).astype(ys_ref.dtype)

    @pl.when(i >= nt_ref[0])
    def _():
        ys_ref[...] = jnp.zeros_like(ys_ref)


def _moe_grouped(tile_expert, n_used, xs, we_g, we_u, we_d, layer):
    Mp, D = xs.shape
    n_tiles = Mp // MOE_TILE
    H = EXPERT_HIDDEN
    grid_spec = pltpu.PrefetchScalarGridSpec(
        num_scalar_prefetch=2,
        grid=(n_tiles,),
        in_specs=[pl.BlockSpec((MOE_TILE, D), lambda i, te, nt: (i, 0)),
                  pl.BlockSpec((None, 1, D, H), lambda i, te, nt: (layer, te[i], 0, 0)),
                  pl.BlockSpec((None, 1, D, H), lambda i, te, nt: (layer, te[i], 0, 0)),
                  pl.BlockSpec((None, 1, H, D), lambda i, te, nt: (layer, te[i], 0, 0))],
        out_specs=pl.BlockSpec((MOE_TILE, D), lambda i, te, nt: (i, 0)),
        scratch_shapes=[pltpu.VMEM((D, H), BF16), pltpu.VMEM((D, H), BF16), pltpu.VMEM((H, D), BF16)],
    )
    return pl.pallas_call(
        _moe_kernel,
        grid_spec=grid_spec,
        out_shape=jax.ShapeDtypeStruct((Mp, D), BF16),
        compiler_params=_cparams("arbitrary"),
        name="moe_experts",
    )(tile_expert, n_used, xs, we_g, we_u, we_d)


def _shared_kernel(x_ref, h_ref, r_ref, g2_ref, wg_ref, wu_ref, wd_ref, lg_ref, lb_ref, o_ref):
    h = h_ref[0]
    g = jnp.dot(h, wg_ref[...], preferred_element_type=F32)
    u = jnp.dot(h, wu_ref[...], preferred_element_type=F32)
    a = (g * jax.nn.sigmoid(g) * u).astype(BF16)
    ff = jnp.dot(a, wd_ref[...], preferred_element_type=F32) + r_ref[0]
    o_ref[0] = _layer_norm_rows(DEEPNORM_ALPHA * x_ref[0] + g2_ref[0] * ff, lg_ref[...], lb_ref[...])


def _shared_ln2(x, h, routed, g2, ws_g, ws_u, ws_d, ln_g, ln_b, per_batch_mod):
    B, T, D = x.shape
    tm = min(TOK_TILE, T)
    mod_idx = (lambda b, i: (b, 0, 0)) if per_batch_mod else (lambda b, i: (0, 0, 0))
    tok = pl.BlockSpec((1, tm, D), lambda b, i: (b, i, 0))
    full = lambda s: pl.BlockSpec(s, lambda b, i: (0,) * len(s))
    return pl.pallas_call(
        _shared_kernel,
        grid=(B, T // tm),
        in_specs=[tok, tok, tok, pl.BlockSpec((1, 1, D), mod_idx), full((D, EXPERT_HIDDEN)), full((D, EXPERT_HIDDEN)),
                  full((EXPERT_HIDDEN, D)), full((1, D)), full((1, D))],
        out_specs=tok,
        out_shape=jax.ShapeDtypeStruct((B, T, D), F32),
        compiler_params=_cparams("arbitrary", "arbitrary"),
        name="shared_ln2",
    )(x, h, routed, g2, ws_g, ws_u, ws_d, ln_g, ln_b)


def _first_max(vals, iota, n):
    m = jnp.max(vals, axis=0, keepdims=True)
    idx = jnp.min(jnp.where(vals == m, iota, n), axis=0, keepdims=True)
    return m, idx


def _route_kernel(lt_ref, b_ref, eidx_ref, w_ref, rank_ref, cnt_ref, base_ref):
    @pl.when(pl.program_id(0) == 0)
    def _():
        base_ref[...] = jnp.zeros_like(base_ref)

    tm = lt_ref.shape[1]
    per = N_EXPERTS // N_GROUPS
    s = jax.nn.sigmoid(lt_ref[...])
    sel = s + b_ref[...]
    io_g = lax.broadcasted_iota(jnp.int32, (per, tm), 0)
    scores = []
    for g in range(N_GROUPS):
        blk = sel[g * per:(g + 1) * per]
        m1, i1 = _first_max(blk, io_g, per)
        m2 = jnp.max(jnp.where(io_g == i1, -jnp.inf, blk), axis=0, keepdims=True)
        scores.append(m1 + m2)
    cur = jnp.concatenate(scores, axis=0)
    io_8 = lax.broadcasted_iota(jnp.int32, (N_GROUPS, tm), 0)
    gmask = jnp.zeros((N_GROUPS, tm), F32)
    for _ in range(TOPK_GROUPS):
        _, gi = _first_max(cur, io_8, N_GROUPS)
        hit = io_8 == gi
        gmask = jnp.where(hit, 1.0, gmask)
        cur = jnp.where(hit, -jnp.inf, cur)
    masked = jnp.concatenate(
        [jnp.where(gmask[g:g + 1] > 0.0, sel[g * per:(g + 1) * per], -jnp.inf) for g in range(N_GROUPS)], axis=0)
    io_e = lax.broadcasted_iota(jnp.int32, (N_EXPERTS, tm), 0)
    chosen = jnp.zeros((N_EXPERTS, tm), F32)
    eidx, gates = [], []
    for _ in range(TOP_K):
        _, ei = _first_max(masked, io_e, N_EXPERTS)
        hit = io_e == ei
        eidx.append(ei)
        gates.append(jnp.sum(jnp.where(hit, s, 0.0), axis=0, keepdims=True))
        masked = jnp.where(hit, -jnp.inf, masked)
        chosen = jnp.where(hit, 1.0, chosen)
    wk = jnp.concatenate(gates, axis=0)
    w_ref[...] = wk / jnp.sum(wk, axis=0, keepdims=True) * ROUTED_SCALE
    eidx_ref[...] = jnp.concatenate(eidx, axis=0)
    earlier = (lax.broadcasted_iota(jnp.int32, (tm, tm), 0) < lax.broadcasted_iota(jnp.int32, (tm, tm), 1))
    pos = jnp.dot(chosen.astype(BF16), jnp.where(earlier, 1.0, 0.0).astype(BF16), preferred_element_type=F32)
    pos = pos + base_ref[...]
    ranks = [jnp.sum(jnp.where(io_e == eidx[k], pos, 0.0), axis=0, keepdims=True) for k in range(TOP_K)]
    rank_ref[...] = jnp.concatenate(ranks, axis=0).astype(jnp.int32)
    base_ref[...] = base_ref[...] + jnp.sum(chosen, axis=1, keepdims=True)
    cnt_ref[...] = base_ref[...]


def _route(logits_t, b_corr):
    E, T = logits_t.shape
    tm = ROUTE_TILE
    tokk = pl.BlockSpec((TOP_K, tm), lambda i: (0, i))
    return pl.pallas_call(
        _route_kernel,
        grid=(T // tm,),
        in_specs=[pl.BlockSpec((E, tm), lambda i: (0, i)), pl.BlockSpec((E, 1), lambda i: (0, 0))],
        out_specs=[tokk, tokk, tokk, pl.BlockSpec((E, 1), lambda i: (0, 0))],
        out_shape=[jax.ShapeDtypeStruct((TOP_K, T), jnp.int32), jax.ShapeDtypeStruct((TOP_K, T), F32),
                   jax.ShapeDtypeStruct((TOP_K, T), jnp.int32), jax.ShapeDtypeStruct((E, 1), F32)],
        scratch_shapes=[pltpu.VMEM((E, 1), F32)],
        compiler_params=_cparams("arbitrary"),
        name="route",
    )(logits_t, b_corr.reshape(E, 1))


def _slot_kernel(eidx_ref, rank_ref, pstart_ref, dest_ref):
    tm = eidx_ref.shape[1]
    io_e = lax.broadcasted_iota(jnp.int32, (N_EXPERTS, tm), 0)
    ei = eidx_ref[...]
    starts = [jnp.sum(jnp.where(io_e == ei[k:k + 1], pstart_ref[...], 0.0), axis=0, keepdims=True)
              for k in range(TOP_K)]
    dest_ref[...] = jnp.concatenate(starts, axis=0).astype(jnp.int32) + rank_ref[...]


def _slots(eidx, rank, pstart):
    K, T = eidx.shape
    tm = ROUTE_TILE
    tokk = pl.BlockSpec((K, tm), lambda i: (0, i))
    return pl.pallas_call(
        _slot_kernel,
        grid=(T // tm,),
        in_specs=[tokk, tokk, pl.BlockSpec((N_EXPERTS, 1), lambda i: (0, 0))],
        out_specs=tokk,
        out_shape=jax.ShapeDtypeStruct((K, T), jnp.int32),
        compiler_params=_cparams("arbitrary"),
        name="route_slots",
    )(eidx, rank, pstart)


def _moe_routed(h_flat, logits_t, b_corr, we_g, we_u, we_d, layer):
    T = h_flat.shape[0]
    n_tiles = T * TOP_K // MOE_TILE + N_EXPERTS
    eidx, gates, rank, counts = _route(logits_t, b_corr)
    counts = counts[:, 0].astype(jnp.int32)
    padded = (counts + MOE_TILE - 1) // MOE_TILE * MOE_TILE
    pends = jnp.cumsum(padded)
    tile_expert = jnp.minimum(jnp.searchsorted(pends, jnp.arange(n_tiles, dtype=jnp.int32) * MOE_TILE, side='right'),
                              N_EXPERTS - 1).astype(jnp.int32)
    n_used = (pends[-1] // MOE_TILE).astype(jnp.int32).reshape(1)
    dest = _slots(eidx, rank, (pends - padded).astype(F32).reshape(N_EXPERTS, 1))
    tok = jnp.broadcast_to(jnp.arange(T, dtype=jnp.int32)[None], (TOP_K, T))
    src_tok = (jnp.arange(n_tiles * MOE_TILE, dtype=jnp.int32) % T).at[dest.reshape(-1)].set(
        tok.reshape(-1), unique_indices=True, mode='promise_in_bounds')
    xs = jnp.take(h_flat, src_tok, axis=0)
    ys = _moe_grouped(tile_expert, n_used, xs, we_g, we_u, we_d, layer)
    picked = jnp.take(ys, dest.T, axis=0)
    return jnp.sum(picked.astype(F32) * gates.T[:, :, None], axis=1)


_GLA_QK_BLK, _GLA_V_BLK, _GLA_R_BLK = 0, 1, 2
_GLA_LR_BLK = (2 * GLA_KEY_WIDTH + 2 * GLA_WIDTH) // LANE


def _rope_tables(L):
    t = np.arange(L)
    lane = np.arange(GLA_KEY_WIDTH)
    d = lane % GLA_DK
    pos = np.where(d[None, :] < GLA_DK // 2, (t // GRID_W)[:, None], (t % GRID_W)[:, None]).astype(np.float32)
    quarter = GLA_DK // 4
    inv = ROPE_BASE ** (-jnp.arange(quarter, dtype=F32) / quarter)
    ang = jnp.asarray(pos) * inv[jnp.asarray(d % quarter)][None, :]
    sign = np.where(d % (2 * quarter) < quarter, -1.0, 1.0).astype(np.float32)
    return jnp.cos(ang), jnp.sin(ang) * sign[None, :]


def _rope_partner(x):
    lane = lax.broadcasted_iota(jnp.int32, x.shape, 1)
    quarter = GLA_DK // 4
    return jnp.where(lane % (2 * quarter) < quarter, pltpu.roll(x, GLA_KEY_WIDTH - quarter, 1), pltpu.roll(x, quarter, 1))


def _log_sigmoid(x):
    return jnp.minimum(x, 0.0) - jnp.log(1.0 + jnp.exp(-jnp.abs(x)))


def _gla_kernel(reverse, finalize, *refs):
    if finalize:
        (qk_ref, v_ref, lr_ref, cos_ref, sin_ref, wa_ref, ba_ref, s0_ref, of_ref, r_ref, g_ref,
         o_ref, sfin_ref, s_scr) = refs
    else:
        qk_ref, v_ref, lr_ref, cos_ref, sin_ref, wa_ref, ba_ref, s0_ref, o_ref, sfin_ref, s_scr = refs
    hi = lax.Precision.HIGHEST

    @pl.when(pl.program_id(1) == 0)
    def _():
        s_scr[...] = s0_ref[0]

    tg = qk_ref.shape[1]
    C = GLA_CHUNK
    KW, VW = GLA_KEY_WIDTH, GLA_WIDTH
    qk = qk_ref[0]
    cos, sin = cos_ref[...], sin_ref[...]
    q = qk[:, :KW] * GLA_DK ** -0.5
    k = qk[:, KW:]
    q = q * cos + _rope_partner(q) * sin
    k = k * cos + _rope_partner(k) * sin
    v = v_ref[0]
    logit = jnp.dot(lr_ref[0], wa_ref[...], preferred_element_type=F32, precision=hi) + ba_ref[...]
    la = _log_sigmoid(logit) / GLA_GATE_NORM

    ri = lax.broadcasted_iota(jnp.int32, (C, C), 0)
    ci = lax.broadcasted_iota(jnp.int32, (C, C), 1)
    tri = jnp.where((ci >= ri) if reverse else (ci <= ri), 1.0, 0.0)
    tri_h = jnp.concatenate([tri] * GLA_HEADS, axis=0)
    head_k = lax.broadcasted_iota(jnp.int32, (C, KW), 1) // GLA_DK
    head_v = lax.broadcasted_iota(jnp.int32, (C, VW), 1) // GLA_DV
    own_block = (lax.broadcasted_iota(jnp.int32, (KW, VW), 0) // GLA_DK
                 == lax.broadcasted_iota(jnp.int32, (KW, VW), 1) // GLA_DV)
    eye = lax.broadcasted_iota(jnp.int32, (KW, KW), 0) == lax.broadcasted_iota(jnp.int32, (KW, KW), 1)

    S = s_scr[...]
    nc = tg // C
    outs = [None] * nc
    for c in (range(nc - 1, -1, -1) if reverse else range(nc)):
        sl = slice(c * C, (c + 1) * C)
        b = jnp.dot(tri, la[sl], preferred_element_type=F32, precision=hi)
        qt = q[sl] * jnp.exp(b)
        kt = k[sl] * jnp.exp(-b)
        qs = jnp.concatenate([jnp.where(head_k == h, qt, 0.0) for h in range(GLA_HEADS)], axis=0).astype(BF16)
        att = lax.dot_general(qs, kt.astype(BF16), _NT, preferred_element_type=F32)
        att = jnp.where(tri_h > 0.0, att, 0.0)
        vb = v[sl].astype(BF16)
        oi = jnp.dot(att.astype(BF16), vb, preferred_element_type=F32)
        o_intra = jnp.zeros((C, VW), F32)
        for h in range(GLA_HEADS):
            o_intra = jnp.where(head_v == h, oi[h * C:(h + 1) * C], o_intra)
        o_inter = jnp.dot(qt.astype(BF16), S.astype(BF16), preferred_element_type=F32)
        outs[c] = o_intra + o_inter
        b_last = b[0:1] if reverse else b[C - 1:C]
        kdec = (k[sl] * jnp.exp(b_last - b)).astype(BF16)
        kv = lax.dot_general(kdec, vb, (((0,), (0,)), ((), ())), preferred_element_type=F32)
        decay_col = jnp.sum(jnp.where(eye, jnp.exp(b_last), 0.0), axis=1, keepdims=True)
        S = decay_col * S + jnp.where(own_block, kv, 0.0)
    s_scr[...] = S
    sfin_ref[0] = S
    o = jnp.concatenate(outs, axis=0)
    if finalize:
        o = of_ref[0] + o
        same_head = (lax.broadcasted_iota(jnp.int32, (VW, VW), 0) // GLA_DV
                     == lax.broadcasted_iota(jnp.int32, (VW, VW), 1) // GLA_DV)
        ms = jnp.dot(o * o, jnp.where(same_head, 1.0 / GLA_DV, 0.0), preferred_element_type=F32, precision=hi)
        r = r_ref[0]
        o = o * lax.rsqrt(ms + LN_EPS) * g_ref[...] * (r * jax.nn.sigmoid(r))
    o_ref[0] = o


def _gla_pass(u_rest, cos, sin, wa, ba, s0, reverse, fin=None):
    B, T, _ = u_rest.shape
    tg = min(GLA_TILE, T)
    n = T // tg
    ti = (lambda i: n - 1 - i) if reverse else (lambda i: i)
    KW, VW = GLA_KEY_WIDTH, GLA_WIDTH
    ublk = lambda w, j: pl.BlockSpec((1, tg, w), lambda b, i: (b, ti(i), j))
    full = lambda s: pl.BlockSpec(s, lambda b, i: (0,) * len(s))
    state = pl.BlockSpec((1, KW, VW), lambda b, i: (b, 0, 0))
    tab = pl.BlockSpec((tg, KW), lambda b, i: (ti(i), 0))
    in_specs = [ublk(2 * KW, _GLA_QK_BLK), ublk(VW, _GLA_V_BLK), ublk(LANE, _GLA_LR_BLK), tab, tab,
                full((LANE, KW)), full((1, KW)), state]
    args = [u_rest, u_rest, u_rest, cos, sin, wa, ba, s0]
    if fin is not None:
        in_specs += [ublk(VW, 0), ublk(VW, _GLA_R_BLK), full((1, VW))]
        args += [fin[0], u_rest, fin[1]]
    return pl.pallas_call(
        functools.partial(_gla_kernel, reverse, fin is not None),
        grid=(B, n),
        in_specs=in_specs,
        out_specs=[ublk(VW, 0), state],
        out_shape=[jax.ShapeDtypeStruct((B, T, VW), F32), jax.ShapeDtypeStruct((B, KW, VW), F32)],
        scratch_shapes=[pltpu.VMEM((KW, VW), F32)],
        compiler_params=_cparams("arbitrary", "arbitrary"),
        name="gla_bwd" if reverse else "gla_fwd",
    )(*args)


def _gla_bidir(u_rest, cos, sin, w_a2, b_a2, norm_g, s0_f, s0_b):
    def decay_w(d):
        return jnp.zeros((LANE, GLA_KEY_WIDTH), F32).at[d * GLA_RANK:(d + 1) * GLA_RANK].set(w_a2[d])

    o_f, s_f = _gla_pass(u_rest, cos, sin, decay_w(0), b_a2[0:1], s0_f, False)
    g = jnp.tile(norm_g, GLA_HEADS).reshape(1, GLA_WIDTH)
    o, s_b = _gla_pass(u_rest, cos, sin, decay_w(1), b_a2[1:2], s0_b, True, (o_f, g))
    return o, s_f, s_b


HY_LANES = LANE
HY_CH_TILE = 16
HY_CTX_N1 = 16


def _hy_consts(n1, N1):
    W = HY_LANES
    n1p = max(n1, 16)
    n1o = max(n1, 8)
    a = np.arange(N1)
    th1 = 2.0 * np.pi * ((a[:, None] * a[None, :]) % N1) / N1
    f1 = np.zeros((2 * N1, n1p)); f1[:N1, :n1] = np.cos(th1[:, :n1]); f1[N1:, :n1] = -np.sin(th1[:, :n1])
    f1_full = np.concatenate([np.cos(th1), -np.sin(th1)], axis=0)
    ginv = np.zeros((n1o, 2 * N1)); ginv[:n1, :N1] = np.cos(th1.T[:n1]); ginv[:n1, N1:] = -np.sin(th1.T[:n1])
    r = np.arange(W)
    tht = 2.0 * np.pi * ((a[:, None] * r[None, :]) % (N1 * W)) / (N1 * W)
    tr = np.tile(np.cos(tht), (1, HY_CH_TILE)); ti = np.tile(-np.sin(tht), (1, HY_CH_TILE))
    th2 = 2.0 * np.pi * ((r[:, None] * r[None, :]) % W) / W
    c2, s2 = np.cos(th2), -np.sin(th2)
    m2f = np.block([[c2, s2], [-s2, c2]])
    m2i = np.block([[c2, -s2], [s2, c2]])
    f = lambda m: jnp.asarray(m, F32)
    return dict(n1p=n1p, n1o=n1o, f1=f(f1), f1_full=f(f1_full), ginv=f(ginv), tr=f(tr), ti=f(ti), m2f=f(m2f), m2i=f(m2i))


def _short_conv_rows(u, w_ref, b_ref, n_rows):
    R, Wd = u.shape
    lane = lax.broadcasted_iota(jnp.int32, (R, Wd), 1) % HY_LANES
    row = lax.broadcasted_iota(jnp.int32, (R, Wd), 0)
    up = jnp.where(row == 0, 0.0, pltpu.roll(u, 1, 0))
    dn = jnp.where(row == n_rows - 1, 0.0, pltpu.roll(u, R - 1, 0))
    prev = jnp.where(lane == 0, pltpu.roll(up, Wd - (HY_LANES - 1), 1), pltpu.roll(u, 1, 1))
    nxt = jnp.where(lane == HY_LANES - 1, pltpu.roll(dn, HY_LANES - 1, 1), pltpu.roll(u, Wd - 1, 1))
    return b_ref[...] + prev * w_ref[0:1] + u * w_ref[1:2] + nxt * w_ref[2:3]


def _pad_rows(u, rows):
    return u if u.shape[0] == rows else jnp.concatenate([u, jnp.zeros((rows - u.shape[0], u.shape[1]), u.dtype)], axis=0)


def _hy_chunk_dft(z, f1, tr, ti, N1, prec):
    if prec is None:
        a = jnp.dot(f1.astype(BF16), z.astype(BF16), preferred_element_type=F32)
    else:
        a = jnp.dot(f1, z, preferred_element_type=F32, precision=prec)
    ar, ai = a[:N1], a[N1:]
    a_re, a_im = ar * tr - ai * ti, ar * ti + ai * tr
    W = HY_LANES
    return jnp.concatenate(
        [jnp.concatenate([a_re[:, c * W:(c + 1) * W], a_im[:, c * W:(c + 1) * W]], axis=1) for c in range(z.shape[1] // W)],
        axis=0)


def _hy_long_conv(z, kf, f1, ginv, tr, ti, m2f, m2i, N1):
    W = HY_LANES
    m = z.shape[1] // W
    a = _hy_chunk_dft(z, f1, tr, ti, N1, None).astype(BF16)
    x = jnp.dot(a, m2f.astype(BF16), preferred_element_type=F32)
    xr, xi = x[:, :W], x[:, W:]
    kr, ki = kf[:, :W], kf[:, W:]
    y = jnp.concatenate([xr * kr - xi * ki, xr * ki + xi * kr], axis=1).astype(BF16)
    p = jnp.dot(y, m2i.astype(BF16), preferred_element_type=F32)
    pr = jnp.concatenate([p[c * N1:(c + 1) * N1, :W] for c in range(m)], axis=1)
    pi = jnp.concatenate([p[c * N1:(c + 1) * N1, W:] for c in range(m)], axis=1)
    q = jnp.concatenate([pr * tr + pi * ti, pi * tr - pr * ti], axis=0).astype(BF16)
    return jnp.dot(ginv.astype(BF16), q, preferred_element_type=F32)


def _hy_mixer_kernel(n1, N1, n1p, v_ref, x1_ref, x2_ref, swv_ref, sbv_ref, swx1_ref, sbx1_ref, swx2_ref, sbx2_ref,
                     skip_ref, kf_ref, f1_ref, ginv_ref, tr_ref, ti_ref, m2f_ref, m2i_ref, o_ref):
    rows = max(n1, 8)
    cst = (f1_ref[...], ginv_ref[...], tr_ref[...], ti_ref[...], m2f_ref[...], m2i_ref[...], N1)
    z = _short_conv_rows(_pad_rows(v_ref[0], rows), swv_ref, sbv_ref, n1)
    gates = (_short_conv_rows(_pad_rows(x1_ref[0], rows), swx1_ref, sbx1_ref, n1),
             _short_conv_rows(_pad_rows(x2_ref[0], rows), swx2_ref, sbx2_ref, n1))
    for o in range(HY_ORDER):
        y = _hy_long_conv(_pad_rows(z, n1p), kf_ref[o, 0], *cst)
        z = gates[o] * (y + skip_ref[o:o + 1] * z)
    o_ref[0] = z[:n1]


def _hy_kspec_kernel(N1, scale, k_ref, f1_ref, tr_ref, ti_ref, m2f_ref, kf_ref):
    a = _hy_chunk_dft(k_ref[0], f1_ref[...], tr_ref[...], ti_ref[...], N1, lax.Precision.HIGHEST)
    kf_ref[0, 0] = jnp.dot(a, m2f_ref[...], preferred_element_type=F32, precision=lax.Precision.HIGHEST) * scale


def _hy_filter_spectrum(filt, N1, cst):
    Ls, n_ord, _, C = filt.shape
    W = HY_LANES
    N = N1 * W
    kern = jnp.concatenate([filt[:, :, 0], jnp.zeros((N - 2 * Ls + 1, n_ord, C), F32), filt[:0:-1, :, 1]], axis=0)
    kern = kern.reshape(N1, W, n_ord, C).transpose(2, 0, 3, 1).reshape(n_ord, N1, C * W)
    Wd = HY_CH_TILE * W
    nj = C // HY_CH_TILE
    full = lambda s: pl.BlockSpec(s, lambda o, j: (0,) * len(s))
    return pl.pallas_call(
        functools.partial(_hy_kspec_kernel, N1, 1.0 / N),
        grid=(n_ord, nj),
        in_specs=[pl.BlockSpec((1, N1, Wd), lambda o, j: (o, 0, j)), full((2 * N1, N1)), full((N1, Wd)), full((N1, Wd)),
                  full((2 * W, 2 * W))],
        out_specs=pl.BlockSpec((1, 1, HY_CH_TILE * N1, 2 * W), lambda o, j: (o, j, 0, 0)),
        out_shape=jax.ShapeDtypeStruct((n_ord, nj, HY_CH_TILE * N1, 2 * W), F32),
        compiler_params=_cparams("arbitrary", "arbitrary"),
        name="hyena_filter_spectrum",
    )(kern, cst["f1_full"], cst["tr"], cst["ti"], cst["m2f"])


def _hyena_mixer(hy_t, short_w, short_b, filt, skip, N1):
    B, n1, C3, W = hy_t.shape
    C = C3 // (HY_ORDER + 1)
    cst = _hy_consts(n1, N1)
    kf = _hy_filter_spectrum(filt, N1, cst)
    Wd = HY_CH_TILE * W
    nj = C // HY_CH_TILE
    hy2 = hy_t.reshape(B, n1, C3 * W)
    rep = lambda v: jnp.repeat(v, W, axis=-1)
    sw, sb = rep(short_w), rep(short_b).reshape(1, C3 * W)
    sk = rep(skip)
    chan = lambda part: pl.BlockSpec((1, n1, Wd), lambda j, b: (b, 0, part * nj + j))
    wsp = lambda part: pl.BlockSpec((HY_SHORT, Wd), lambda j, b: (0, part * nj + j))
    bsp = lambda part: pl.BlockSpec((1, Wd), lambda j, b: (0, part * nj + j))
    full = lambda a: pl.BlockSpec(a.shape, lambda j, b: (0,) * a.ndim)
    consts = [cst[k] for k in ("f1", "ginv", "tr", "ti", "m2f", "m2i")]
    out = pl.pallas_call(
        functools.partial(_hy_mixer_kernel, n1, N1, cst["n1p"]),
        grid=(nj, B),
        in_specs=[chan(0), chan(1), chan(2), wsp(0), bsp(0), wsp(1), bsp(1), wsp(2), bsp(2),
                  pl.BlockSpec((HY_ORDER, Wd), lambda j, b: (0, j)),
                  pl.BlockSpec((HY_ORDER, 1, HY_CH_TILE * N1, 2 * W), lambda j, b: (0, j, 0, 0))]
                 + [full(a) for a in consts],
        out_specs=pl.BlockSpec((1, n1, Wd), lambda j, b: (b, 0, j)),
        out_shape=jax.ShapeDtypeStruct((B, n1, C * W), F32),
        compiler_params=_cparams("arbitrary", "arbitrary"),
        name="hyena_mixer",
    )(hy2, hy2, hy2, sw, sb, sw, sb, sw, sb, sk, kf, *consts)
    return out.reshape(B, n1, C, W)


def _hyena_filters(L, w1, b1, f1, w2, b2, f2, w3, b3, decay):
    hi = lax.Precision.HIGHEST
    t = jnp.linspace(0.0, 1.0, L, dtype=F32)
    w = 2.0 * math.pi * jnp.arange(L, dtype=F32) / L
    bands = jnp.linspace(1e-4, HY_BANDS - 1, HY_BANDS, dtype=F32)
    ang = w[:, None] * bands[None]
    z = jnp.concatenate([t[:, None], jnp.cos(ang), -jnp.sin(ang)], -1)
    h = jnp.sin(f1 * (jnp.dot(z, w1, precision=hi) + b1))
    h = jnp.sin(f2 * (jnp.dot(h, w2, precision=hi) + b2))
    h = (jnp.dot(h, w3, precision=hi) + b3).reshape(L, HY_ORDER, 2, HY_CH)
    window = jnp.exp(-t[:, None, None, None] * jnp.abs(decay)[None])
    return h * window


def kernel(x, c, ctx, c_ctx, w_mod, b_mod, w_in, na_rpb, gla_w_a2, gla_b_a2, gla_norm_g, hy_short_w, hy_short_b, hy_w1, hy_b1, hy_f1, hy_w2, hy_b2, hy_f2, hy_w3, hy_b3, hy_decay, hy_skip, w_out, ln1_g, ln1_b, router_w, router_b, we_gate, we_up, we_down, ws_gate, ws_up, ws_down, ln2_g, ln2_b):
    B, L, D = x.shape
    C = ctx.shape[1]
    rope_cos, rope_sin = _rope_tables(L)
    ctx_cos, ctx_sin = jnp.ones((C, GLA_KEY_WIDTH), F32), jnp.zeros((C, GLA_KEY_WIDTH), F32)
    zero_state = jnp.zeros((B, GLA_KEY_WIDTH, GLA_WIDTH), F32)
    na_bias = _na_bias_table(na_rpb)

    n_mod = -(-(B + 1) // 8) * 8
    cs = jnp.zeros((n_mod, D), F32).at[:B].set(c).at[B].set(c_ctx)
    mod_all = _modulation(cs, w_mod, b_mod)

    xc = ctx
    for l in range(DEPTH):
        last = l == DEPTH - 1
        mods = mod_all[l].reshape(n_mod, 6, 1, D)
        lat = lambda j: mods[:B, j]
        cm = lambda j: mods[B:B + 1, j]
        n_tok = NA_COLS + GLA_COLS
        w_pad = jnp.pad(w_in[l][:, :n_tok], ((0, 0), (0, D_IN_PAD - n_tok))).astype(BF16)
        w_hy_t = w_in[l][:, n_tok:].T.astype(BF16)
        w_out_b = w_out[l].astype(BF16)
        lg1, lb1 = ln1_g[l].reshape(1, D), ln1_b[l].reshape(1, D)
        lg2, lb2 = ln2_g[l].reshape(1, D), ln2_b[l].reshape(1, D)

        u_na, u_rest, hy = _inproj(x, lat(1), lat(0), w_pad, w_hy_t, True)
        uc_na, uc_rest, hyc = _inproj(xc, cm(1), cm(0), w_pad, w_hy_t, False)

        na_lat = _na_attention(u_na, uc_na, na_bias, l)

        gla_c, s_ctx_f, s_ctx_b = _gla_bidir(uc_rest, ctx_cos, ctx_sin, gla_w_a2[l], gla_b_a2[l], gla_norm_g[l],
                                             zero_state, zero_state)
        gla_lat, _, _ = _gla_bidir(u_rest, rope_cos, rope_sin, gla_w_a2[l], gla_b_a2[l], gla_norm_g[l],
                                   s_ctx_f, s_ctx_b)

        filt_args = (hy_w1[l], hy_b1[l], hy_f1[l], hy_w2[l], hy_b2[l], hy_f2[l], hy_w3[l], hy_b3[l], hy_decay[l])
        hy_lat = _hyena_mixer(hy, hy_short_w[l], hy_short_b[l], _hyena_filters(L, *filt_args), hy_skip[l],
                              2 * L // HY_LANES)

        wr_t = router_w[l].T
        x, h_lat, logit_lat = _outproj(na_lat, gla_lat, hy_lat, x, lat(2), w_out_b, lg1, lb1, lat(4), lat(3), wr_t, True)
        if not last:
            na_c = _ctx_attention(uc_na)
            hy_c = _hyena_mixer(hyc, hy_short_w[l], hy_short_b[l], _hyena_filters(C, *filt_args), hy_skip[l],
                                HY_CTX_N1)
            xc, h_c, logit_c = _outproj(na_c, gla_c, hy_c, xc, cm(2), w_out_b, lg1, lb1, cm(4), cm(3), wr_t, False)

            h_flat = jnp.concatenate([h_lat.reshape(B * L, D), h_c.reshape(B * C, D)], axis=0)
            logit_t = jnp.concatenate([logit_lat, logit_c], axis=1)
        else:
            h_flat = h_lat.reshape(B * L, D)
            logit_t = logit_lat

        routed = _moe_routed(h_flat, logit_t, router_b[l], we_gate, we_up, we_down, l)
        wsg, wsu, wsd = ws_gate[l].astype(BF16), ws_up[l].astype(BF16), ws_down[l].astype(BF16)
        x = _shared_ln2(x, h_lat, routed[:B * L].reshape(B, L, D), lat(5), wsg, wsu, wsd, lg2, lb2, True)
        if not last:
            xc = _shared_ln2(xc, h_c, routed[B * L:].reshape(B, C, D), cm(5), wsg, wsu, wsd, lg2, lb2, False)
    return x
```

```python
import functools
import math

import numpy as np
import jax
import jax.numpy as jnp
from jax import lax
from jax.experimental import pallas as pl
from jax.experimental.pallas import tpu as pltpu

F32 = jnp.float32
BF16 = jnp.bfloat16

D_MODEL = 1024
DEPTH = 4
GRID_W = 64
CTX_LEN = 256

NA_HEADS = 8
NA_HEAD_DIM = 64
NA_WIDTH = NA_HEADS * NA_HEAD_DIM
NA_WIN_ROWS = 8
NA_WIN_COLS = 16

GLA_HEADS = 4
GLA_DK = 32
GLA_DV = 64
GLA_KEY_WIDTH = GLA_HEADS * GLA_DK
GLA_WIDTH = GLA_HEADS * GLA_DV
GLA_RANK = 16
GLA_GATE_NORM = 16.0
GLA_CHUNK = 64

HY_CH = 256
HY_ORDER = 2
HY_SHORT = 3
HY_BANDS = 16
HY_EMB = 1 + 2 * HY_BANDS

MIX_WIDTH = NA_WIDTH + GLA_WIDTH + HY_CH
IN_SPLITS = (NA_WIDTH, NA_WIDTH, NA_WIDTH, GLA_KEY_WIDTH, GLA_KEY_WIDTH, GLA_WIDTH, GLA_WIDTH,
             2 * GLA_RANK, (HY_ORDER + 1) * HY_CH)
D_IN = sum(IN_SPLITS)
ROPE_BASE = 10000.0

N_EXPERTS = 128
TOP_K = 8
N_GROUPS = 8
TOPK_GROUPS = 4
EXPERT_HIDDEN = 256
ROUTED_SCALE = 2.5

DEEPNORM_ALPHA = (2 * DEPTH) ** 0.25
LN_EPS = 1e-6

LANE = 128
MXU_DIM = 256
VMEM_LIMIT = 48 * 1024 * 1024

NA_COLS = 3 * NA_WIDTH
GLA_COLS = sum(IN_SPLITS[3:8])
HY_COLS = IN_SPLITS[8]
REST_COLS = -(-GLA_COLS // LANE) * LANE
D_IN_PAD = NA_COLS + REST_COLS
NA_QUAD = MXU_DIM // NA_HEAD_DIM
NEG_BIG = -1e30

TOK_TILE = 512
MOE_TILE = 512
NA_ROW_TILE = 8
NA_ROW_UNROLL = 4
ROUTE_TILE = 256
GLA_TILE = 512


def _cparams(*sem):
    return pltpu.CompilerParams(dimension_semantics=sem, vmem_limit_bytes=VMEM_LIMIT)


def _mod_kernel(c_ref, w_ref, b_ref, o_ref):
    c = c_ref[...]
    s = c * jax.nn.sigmoid(c)
    o_ref[0] = jnp.dot(s, w_ref[0], preferred_element_type=F32, precision=lax.Precision.HIGHEST) + b_ref[0]


def _modulation(cs, w_mod, b_mod):
    R = cs.shape[0]
    tn = 1536
    return pl.pallas_call(
        _mod_kernel,
        grid=(DEPTH, 6 * D_MODEL // tn),
        in_specs=[pl.BlockSpec((R, D_MODEL), lambda l, j: (0, 0)),
                  pl.BlockSpec((1, D_MODEL, tn), lambda l, j: (l, 0, j)),
                  pl.BlockSpec((1, 1, tn), lambda l, j: (l, 0, j))],
        out_specs=pl.BlockSpec((1, R, tn), lambda l, j: (l, 0, j)),
        out_shape=jax.ShapeDtypeStruct((DEPTH, R, 6 * D_MODEL), F32),
        compiler_params=_cparams("arbitrary", "arbitrary"),
        name="modulation",
    )(cs, w_mod, b_mod.reshape(DEPTH, 1, 6 * D_MODEL))


def _inproj_kernel(x_ref, sc_ref, sh_ref, w_ref, wh_ref, ona_ref, orest_ref, ohy_ref):
    xm = (x_ref[0] * (1.0 + sc_ref[0]) + sh_ref[0]).astype(BF16)
    step = 512
    for c0 in range(0, NA_COLS, step):
        ona_ref[0, :, c0:c0 + step] = jnp.dot(xm, w_ref[:, c0:c0 + step], preferred_element_type=F32).astype(BF16)
    for c0 in range(0, REST_COLS, step):
        c1 = min(c0 + step, REST_COLS)
        orest_ref[0, :, c0:c1] = jnp.dot(xm, w_ref[:, NA_COLS + c0:NA_COLS + c1], preferred_element_type=F32)
    hy = lax.dot_general(wh_ref[...], xm, _NT, preferred_element_type=F32)
    for j in range(ohy_ref.shape[1]):
        ohy_ref[0, j] = hy[:, j * LANE:(j + 1) * LANE]


def _inproj(x, sc, sh, w_pad, w_hy_t, per_batch_mod):
    B, T, D = x.shape
    tm = min(TOK_TILE, T)
    mod_idx = (lambda b, i: (b, 0, 0)) if per_batch_mod else (lambda b, i: (0, 0, 0))
    return pl.pallas_call(
        _inproj_kernel,
        grid=(B, T // tm),
        in_specs=[pl.BlockSpec((1, tm, D), lambda b, i: (b, i, 0)),
                  pl.BlockSpec((1, 1, D), mod_idx),
                  pl.BlockSpec((1, 1, D), mod_idx),
                  pl.BlockSpec((D, D_IN_PAD), lambda b, i: (0, 0)),
                  pl.BlockSpec((HY_COLS, D), lambda b, i: (0, 0))],
        out_specs=[pl.BlockSpec((1, tm, NA_COLS), lambda b, i: (b, i, 0)),
                   pl.BlockSpec((1, tm, REST_COLS), lambda b, i: (b, i, 0)),
                   pl.BlockSpec((1, tm // LANE, HY_COLS, LANE), lambda b, i: (b, i, 0, 0))],
        out_shape=[jax.ShapeDtypeStruct((B, T, NA_COLS), BF16),
                   jax.ShapeDtypeStruct((B, T, REST_COLS), F32),
                   jax.ShapeDtypeStruct((B, T // LANE, HY_COLS, LANE), F32)],
        compiler_params=_cparams("arbitrary", "arbitrary"),
        name="inproj",
    )(x, sc, sh, w_pad, w_hy_t)


def _stack_heads(q, n_rows):
    head = lax.broadcasted_iota(jnp.int32, (n_rows, MXU_DIM), 1) // NA_HEAD_DIM
    return jnp.concatenate([jnp.where(head == h, q, jnp.zeros_like(q)) for h in range(NA_QUAD)], axis=0)


def _unstack_heads(o, n_rows):
    head = lax.broadcasted_iota(jnp.int32, (n_rows, MXU_DIM), 1) // NA_HEAD_DIM
    out = jnp.zeros((n_rows, MXU_DIM), F32)
    for h in range(NA_QUAD):
        out = jnp.where(head == h, o[h * n_rows:(h + 1) * n_rows], out)
    return out


_NT = (((1,), (1,)), ((), ()))


def _na_kernel(q_ref, k_ref, v_ref, kc_ref, vc_ref, bias_ref, o_ref):
    rt = pl.program_id(2)
    scale = NA_HEAD_DIM ** -0.5
    kc = kc_ref[0]
    vc = vc_ref[0]
    n_loc = NA_WIN_ROWS * GRID_W

    def row(rl, carry):
        r = rt * NA_ROW_TILE + rl
        kr0 = jnp.clip(r - NA_WIN_ROWS // 2, 0, GRID_W - NA_WIN_ROWS)
        dr0 = kr0 - r + NA_WIN_ROWS - 1
        q = q_ref[0, pl.ds(pl.multiple_of(rl * GRID_W, GRID_W), GRID_W), :]
        qs = _stack_heads(q, GRID_W)
        k0 = pl.multiple_of(kr0 * GRID_W, GRID_W)
        ks = k_ref[0, pl.ds(k0, n_loc), :]
        vs = v_ref[0, pl.ds(k0, n_loc), :]
        s_loc = lax.dot_general(qs, ks, _NT, preferred_element_type=F32) * scale + bias_ref[0, dr0]
        s_ctx = lax.dot_general(qs, kc, _NT, preferred_element_type=F32) * scale
        m = jnp.maximum(jnp.max(s_loc, axis=-1, keepdims=True), jnp.max(s_ctx, axis=-1, keepdims=True))
        p_loc = jnp.exp(s_loc - m)
        p_ctx = jnp.exp(s_ctx - m)
        den = jnp.sum(p_loc, axis=-1, keepdims=True) + jnp.sum(p_ctx, axis=-1, keepdims=True)
        o = (jnp.dot(p_loc.astype(BF16), vs, preferred_element_type=F32)
             + jnp.dot(p_ctx.astype(BF16), vc, preferred_element_type=F32)) / den
        o_ref[0, pl.ds(pl.multiple_of(rl * GRID_W, GRID_W), GRID_W), :] = _unstack_heads(o, GRID_W)
        return carry

    lax.fori_loop(0, NA_ROW_TILE, row, 0, unroll=NA_ROW_UNROLL)


def _na_bias_table(rpb):
    n_lyr = rpb.shape[0]
    c = np.arange(GRID_W)
    kc0 = np.clip(c - NA_WIN_COLS // 2, 0, GRID_W - NA_WIN_COLS)
    kc = np.arange(GRID_W)
    valid = (kc[None, :] >= kc0[:, None]) & (kc[None, :] < kc0[:, None] + NA_WIN_COLS)
    dc = kc[None, :] - c[:, None] + NA_WIN_COLS - 1
    onehot = (np.arange(2 * NA_WIN_COLS - 1)[:, None, None] == dc[None]) & valid[None]
    toep = jnp.einsum('lhrd,dck->lhrck', rpb, jnp.asarray(onehot, F32), precision=lax.Precision.HIGHEST)
    toep = jnp.where(jnp.asarray(valid)[None, None, None], toep, NEG_BIG)
    tab = jnp.stack([toep[:, :, d:d + NA_WIN_ROWS] for d in range(NA_WIN_ROWS)], axis=2)
    tab = tab.transpose(0, 1, 2, 4, 3, 5).reshape(n_lyr, NA_HEADS // NA_QUAD, NA_QUAD, NA_WIN_ROWS, GRID_W,
                                                  NA_WIN_ROWS * GRID_W)
    return tab.transpose(0, 1, 3, 2, 4, 5).reshape(n_lyr, NA_HEADS // NA_QUAD, NA_WIN_ROWS, NA_QUAD * GRID_W,
                                                   NA_WIN_ROWS * GRID_W)


def _na_attention(u_na, uc_na, bias_tab, layer):
    B, L, _ = u_na.shape
    C = uc_na.shape[1]
    nq = NA_WIDTH // MXU_DIM
    tq = NA_ROW_TILE * GRID_W
    return pl.pallas_call(
        _na_kernel,
        grid=(B, nq, L // tq),
        in_specs=[pl.BlockSpec((1, tq, MXU_DIM), lambda b, j, i: (b, i, j)),
                  pl.BlockSpec((1, L, MXU_DIM), lambda b, j, i: (b, 0, nq + j)),
                  pl.BlockSpec((1, L, MXU_DIM), lambda b, j, i: (b, 0, 2 * nq + j)),
                  pl.BlockSpec((1, C, MXU_DIM), lambda b, j, i: (b, 0, nq + j)),
                  pl.BlockSpec((1, C, MXU_DIM), lambda b, j, i: (b, 0, 2 * nq + j)),
                  pl.BlockSpec((None, 1, NA_WIN_ROWS, NA_QUAD * GRID_W, NA_WIN_ROWS * GRID_W),
                               lambda b, j, i: (layer, j, 0, 0, 0))],
        out_specs=pl.BlockSpec((1, tq, MXU_DIM), lambda b, j, i: (b, i, j)),
        out_shape=jax.ShapeDtypeStruct((B, L, NA_WIDTH), F32),
        compiler_params=_cparams("arbitrary", "arbitrary", "arbitrary"),
        name="na_attention",
    )(u_na, u_na, u_na, uc_na, uc_na, bias_tab)


def _ctx_attn_kernel(q_ref, k_ref, v_ref, o_ref):
    C = q_ref.shape[1]
    qs = _stack_heads(q_ref[0], C)
    s = lax.dot_general(qs, k_ref[0], _NT, preferred_element_type=F32) * NA_HEAD_DIM ** -0.5
    p = jnp.exp(s - jnp.max(s, axis=-1, keepdims=True))
    den = jnp.sum(p, axis=-1, keepdims=True)
    o = jnp.dot(p.astype(BF16), v_ref[0], preferred_element_type=F32) / den
    o_ref[0] = _unstack_heads(o, C)


def _ctx_attention(uc_na):
    B, C, _ = uc_na.shape
    nq = NA_WIDTH // MXU_DIM
    return pl.pallas_call(
        _ctx_attn_kernel,
        grid=(B, nq),
        in_specs=[pl.BlockSpec((1, C, MXU_DIM), lambda b, j: (b, 0, j)),
                  pl.BlockSpec((1, C, MXU_DIM), lambda b, j: (b, 0, nq + j)),
                  pl.BlockSpec((1, C, MXU_DIM), lambda b, j: (b, 0, 2 * nq + j))],
        out_specs=pl.BlockSpec((1, C, MXU_DIM), lambda b, j: (b, 0, j)),
        out_shape=jax.ShapeDtypeStruct((B, C, NA_WIDTH), F32),
        compiler_params=_cparams("arbitrary", "arbitrary"),
        name="ctx_attention",
    )(uc_na, uc_na, uc_na)


def _layer_norm_rows(y, g, b):
    mu = jnp.mean(y, axis=-1, keepdims=True)
    d = y - mu
    var = jnp.mean(d * d, axis=-1, keepdims=True)
    return d * lax.rsqrt(var + LN_EPS) * g + b


def _outproj_kernel(na_ref, gla_ref, hy_ref, x_ref, g1_ref, w_ref, lg_ref, lb_ref, sc_ref, sh_ref, wr_ref,
                    xo_ref, h_ref, logit_ref):
    w_hy = w_ref[NA_WIDTH + GLA_WIDTH:, :]
    hy_mix = jnp.concatenate(
        [lax.dot_general(hy_ref[0, j].astype(BF16), w_hy, (((0,), (0,)), ((), ())), preferred_element_type=F32)
         for j in range(hy_ref.shape[1])], axis=0)
    mix = (jnp.dot(na_ref[0].astype(BF16), w_ref[0:NA_WIDTH, :], preferred_element_type=F32)
           + jnp.dot(gla_ref[0].astype(BF16), w_ref[NA_WIDTH:NA_WIDTH + GLA_WIDTH, :], preferred_element_type=F32)
           + hy_mix)
    xn = _layer_norm_rows(DEEPNORM_ALPHA * x_ref[0] + g1_ref[0] * mix, lg_ref[...], lb_ref[...])
    xo_ref[0] = xn
    h = xn * (1.0 + sc_ref[0]) + sh_ref[0]
    h_ref[0] = h.astype(BF16)
    logit_ref[...] = lax.dot_general(wr_ref[...], h, _NT, preferred_element_type=F32, precision=lax.Precision.HIGHEST)


def _outproj(na, gla, hy, x, g1, w_out, ln_g, ln_b, sc2, sh2, w_router, per_batch_mod):
    B, T, D = x.shape
    tm = min(TOK_TILE, T)
    mod_idx = (lambda b, i: (b, 0, 0)) if per_batch_mod else (lambda b, i: (0, 0, 0))
    tok = lambda w: pl.BlockSpec((1, tm, w), lambda b, i: (b, i, 0))
    full = lambda s: pl.BlockSpec(s, lambda b, i: (0,) * len(s))
    mod = pl.BlockSpec((1, 1, D), mod_idx)
    return pl.pallas_call(
        _outproj_kernel,
        grid=(B, T // tm),
        in_specs=[tok(NA_WIDTH), tok(GLA_WIDTH),
                  pl.BlockSpec((1, tm // LANE, HY_CH, LANE), lambda b, i: (b, i, 0, 0)), tok(D), mod, full((MIX_WIDTH, D)),
                  full((1, D)), full((1, D)), mod, mod, full((N_EXPERTS, D))],
        out_specs=[tok(D), tok(D), pl.BlockSpec((N_EXPERTS, tm), lambda b, i: (0, b * (T // tm) + i))],
        out_shape=[jax.ShapeDtypeStruct((B, T, D), F32), jax.ShapeDtypeStruct((B, T, D), BF16),
                   jax.ShapeDtypeStruct((N_EXPERTS, B * T), F32)],
        compiler_params=_cparams("arbitrary", "arbitrary"),
        name="outproj_ln1",
    )(na, gla, hy, x, g1, w_out, ln_g, ln_b, sc2, sh2, w_router)


def _moe_kernel(te_ref, nt_ref, xs_ref, wg_ref, wu_ref, wd_ref, ys_ref, wg_s, wu_s, wd_s):
    i = pl.program_id(0)

    @pl.when(jnp.logical_or(i == 0, te_ref[i] != te_ref[jnp.maximum(i - 1, 0)]))
    def _():
        wg_s[...] = wg_ref[0].astype(BF16)
        wu_s[...] = wu_ref[0].astype(BF16)
        wd_s[...] = wd_ref[0].astype(BF16)

    @pl.when(i < nt_ref[0])
    def _():
        xs = xs_ref[...]
        g = jnp.dot(xs, wg_s[...], preferred_element_type=F32)
        u = jnp.dot(xs, wu_s[...], preferred_element_type=F32)
        a = (g * jax.nn.sigmoid(g) * u).astype(BF16)
        ys_ref[...] = jnp.dot(a, wd_s[...], preferred_element_type=F32).astype(ys_ref.dtype)

    @pl.when(i >= nt_ref[0])
    def _():
        ys_ref[...] = jnp.zeros_like(ys_ref)


def _moe_grouped(tile_expert, n_used, xs, we_g, we_u, we_d, layer):
    Mp, D = xs.shape
    n_tiles = Mp // MOE_TILE
    H = EXPERT_HIDDEN
    grid_spec = pltpu.PrefetchScalarGridSpec(
        num_scalar_prefetch=2,
        grid=(n_tiles,),
        in_specs=[pl.BlockSpec((MOE_TILE, D), lambda i, te, nt: (i, 0)),
                  pl.BlockSpec((None, 1, D, H), lambda i, te, nt: (layer, te[i], 0, 0)),
                  pl.BlockSpec((None, 1, D, H), lambda i, te, nt: (layer, te[i], 0, 0)),
                  pl.BlockSpec((None, 1, H, D), lambda i, te, nt: (layer, te[i], 0, 0))],
        out_specs=pl.BlockSpec((MOE_TILE, D), lambda i, te, nt: (i, 0)),
        scratch_shapes=[pltpu.VMEM((D, H), BF16), pltpu.VMEM((D, H), BF16), pltpu.VMEM((H, D), BF16)],
    )
    return pl.pallas_call(
        _moe_kernel,
        grid_spec=grid_spec,
        out_shape=jax.ShapeDtypeStruct((Mp, D), BF16),
        compiler_params=_cparams("arbitrary"),
        name="moe_experts",
    )(tile_expert, n_used, xs, we_g, we_u, we_d)


def _shared_kernel(x_ref, h_ref, pk_ref, gt_ref, g2_ref, wg_ref, wu_ref, wd_ref, lg_ref, lb_ref, o_ref):
    h = h_ref[0]
    g = jnp.dot(h, wg_ref[...], preferred_element_type=F32)
    u = jnp.dot(h, wu_ref[...], preferred_element_type=F32)
    a = (g * jax.nn.sigmoid(g) * u).astype(BF16)
    ff = jnp.dot(a, wd_ref[...], preferred_element_type=F32)
    gates = gt_ref[...]
    for k in range(TOP_K):
        ff = ff + pk_ref[k].astype(F32) * gates[:, k:k + 1]
    o_ref[0] = _layer_norm_rows(DEEPNORM_ALPHA * x_ref[0] + g2_ref[0] * ff, lg_ref[...], lb_ref[...])


def _shared_ln2(x, h, picked, gates, tok_off, g2, ws_g, ws_u, ws_d, ln_g, ln_b, per_batch_mod):
    B, T, D = x.shape
    tm = min(TOK_TILE, T)
    mod_idx = (lambda b, i: (b, 0, 0)) if per_batch_mod else (lambda b, i: (0, 0, 0))
    tok = pl.BlockSpec((1, tm, D), lambda b, i: (b, i, 0))
    full = lambda s: pl.BlockSpec(s, lambda b, i: (0,) * len(s))
    flat = lambda b, i: tok_off // tm + b * (T // tm) + i
    return pl.pallas_call(
        _shared_kernel,
        grid=(B, T // tm),
        in_specs=[tok, tok, pl.BlockSpec((TOP_K, tm, D), lambda b, i: (0, flat(b, i), 0)),
                  pl.BlockSpec((tm, TOP_K), lambda b, i: (flat(b, i), 0)),
                  pl.BlockSpec((1, 1, D), mod_idx), full((D, EXPERT_HIDDEN)), full((D, EXPERT_HIDDEN)),
                  full((EXPERT_HIDDEN, D)), full((1, D)), full((1, D))],
        out_specs=tok,
        out_shape=jax.ShapeDtypeStruct((B, T, D), F32),
        compiler_params=_cparams("arbitrary", "arbitrary"),
        name="shared_ln2",
    )(x, h, picked, gates, g2, ws_g, ws_u, ws_d, ln_g, ln_b)


def _first_max(vals, iota, n):
    m = jnp.max(vals, axis=0, keepdims=True)
    idx = jnp.min(jnp.where(vals == m, iota, n), axis=0, keepdims=True)
    return m, idx


def _route_kernel(lt_ref, b_ref, eidx_ref, w_ref, rank_ref, cnt_ref, base_ref):
    @pl.when(pl.program_id(0) == 0)
    def _():
        base_ref[...] = jnp.zeros_like(base_ref)

    tm = lt_ref.shape[1]
    per = N_EXPERTS // N_GROUPS
    s = jax.nn.sigmoid(lt_ref[...])
    sel = s + b_ref[...]
    io_g = lax.broadcasted_iota(jnp.int32, (per, tm), 0)
    scores = []
    for g in range(N_GROUPS):
        blk = sel[g * per:(g + 1) * per]
        m1, i1 = _first_max(blk, io_g, per)
        m2 = jnp.max(jnp.where(io_g == i1, -jnp.inf, blk), axis=0, keepdims=True)
        scores.append(m1 + m2)
    cur = jnp.concatenate(scores, axis=0)
    io_8 = lax.broadcasted_iota(jnp.int32, (N_GROUPS, tm), 0)
    gmask = jnp.zeros((N_GROUPS, tm), F32)
    for _ in range(TOPK_GROUPS):
        _, gi = _first_max(cur, io_8, N_GROUPS)
        hit = io_8 == gi
        gmask = jnp.where(hit, 1.0, gmask)
        cur = jnp.where(hit, -jnp.inf, cur)
    masked = jnp.concatenate(
        [jnp.where(gmask[g:g + 1] > 0.0, sel[g * per:(g + 1) * per], -jnp.inf) for g in range(N_GROUPS)], axis=0)
    io_e = lax.broadcasted_iota(jnp.int32, (N_EXPERTS, tm), 0)
    chosen = jnp.zeros((N_EXPERTS, tm), F32)
    eidx, gates = [], []
    for _ in range(TOP_K):
        _, ei = _first_max(masked, io_e, N_EXPERTS)
        hit = io_e == ei
        eidx.append(ei)
        gates.append(jnp.sum(jnp.where(hit, s, 0.0), axis=0, keepdims=True))
        masked = jnp.where(hit, -jnp.inf, masked)
        chosen = jnp.where(hit, 1.0, chosen)
    wk = jnp.concatenate(gates, axis=0)
    w_ref[...] = wk / jnp.sum(wk, axis=0, keepdims=True) * ROUTED_SCALE
    eidx_ref[...] = jnp.concatenate(eidx, axis=0)
    earlier = (lax.broadcasted_iota(jnp.int32, (tm, tm), 0) < lax.broadcasted_iota(jnp.int32, (tm, tm), 1))
    pos = jnp.dot(chosen.astype(BF16), jnp.where(earlier, 1.0, 0.0).astype(BF16), preferred_element_type=F32)
    pos = pos + base_ref[...]
    ranks = [jnp.sum(jnp.where(io_e == eidx[k], pos, 0.0), axis=0, keepdims=True) for k in range(TOP_K)]
    rank_ref[...] = jnp.concatenate(ranks, axis=0).astype(jnp.int32)
    base_ref[...] = base_ref[...] + jnp.sum(chosen, axis=1, keepdims=True)
    cnt_ref[...] = base_ref[...]


def _route(logits_t, b_corr):
    E, T = logits_t.shape
    tm = ROUTE_TILE
    tokk = pl.BlockSpec((TOP_K, tm), lambda i: (0, i))
    return pl.pallas_call(
        _route_kernel,
        grid=(T // tm,),
        in_specs=[pl.BlockSpec((E, tm), lambda i: (0, i)), pl.BlockSpec((E, 1), lambda i: (0, 0))],
        out_specs=[tokk, tokk, tokk, pl.BlockSpec((E, 1), lambda i: (0, 0))],
        out_shape=[jax.ShapeDtypeStruct((TOP_K, T), jnp.int32), jax.ShapeDtypeStruct((TOP_K, T), F32),
                   jax.ShapeDtypeStruct((TOP_K, T), jnp.int32), jax.ShapeDtypeStruct((E, 1), F32)],
        scratch_shapes=[pltpu.VMEM((E, 1), F32)],
        compiler_params=_cparams("arbitrary"),
        name="route",
    )(logits_t, b_corr.reshape(E, 1))


def _slot_kernel(eidx_ref, rank_ref, pstart_ref, dest_ref):
    tm = eidx_ref.shape[1]
    io_e = lax.broadcasted_iota(jnp.int32, (N_EXPERTS, tm), 0)
    ei = eidx_ref[...]
    starts = [jnp.sum(jnp.where(io_e == ei[k:k + 1], pstart_ref[...], 0.0), axis=0, keepdims=True)
              for k in range(TOP_K)]
    dest_ref[...] = jnp.concatenate(starts, axis=0).astype(jnp.int32) + rank_ref[...]


def _slots(eidx, rank, pstart):
    K, T = eidx.shape
    tm = ROUTE_TILE
    tokk = pl.BlockSpec((K, tm), lambda i: (0, i))
    return pl.pallas_call(
        _slot_kernel,
        grid=(T // tm,),
        in_specs=[tokk, tokk, pl.BlockSpec((N_EXPERTS, 1), lambda i: (0, 0))],
        out_specs=tokk,
        out_shape=jax.ShapeDtypeStruct((K, T), jnp.int32),
        compiler_params=_cparams("arbitrary"),
        name="route_slots",
    )(eidx, rank, pstart)


def _moe_routed(h_flat, logits_t, b_corr, we_g, we_u, we_d, layer):
    T = h_flat.shape[0]
    n_tiles = T * TOP_K // MOE_TILE + N_EXPERTS
    eidx, gates, rank, counts = _route(logits_t, b_corr)
    counts = counts[:, 0].astype(jnp.int32)
    padded = (counts + MOE_TILE - 1) // MOE_TILE * MOE_TILE
    pends = jnp.cumsum(padded)
    tile_expert = jnp.minimum(jnp.searchsorted(pends, jnp.arange(n_tiles, dtype=jnp.int32) * MOE_TILE, side='right'),
                              N_EXPERTS - 1).astype(jnp.int32)
    n_used = (pends[-1] // MOE_TILE).astype(jnp.int32).reshape(1)
    dest = _slots(eidx, rank, (pends - padded).astype(F32).reshape(N_EXPERTS, 1))
    tok = jnp.broadcast_to(jnp.arange(T, dtype=jnp.int32)[None], (TOP_K, T))
    src_tok = (jnp.arange(n_tiles * MOE_TILE, dtype=jnp.int32) % T).at[dest.reshape(-1)].set(
        tok.reshape(-1), unique_indices=True, mode='promise_in_bounds')
    xs = h_flat.at[src_tok].get(mode='promise_in_bounds')
    ys = _moe_grouped(tile_expert, n_used, xs, we_g, we_u, we_d, layer)
    return ys.at[dest].get(mode='promise_in_bounds'), gates.T


_GLA_QK_BLK, _GLA_V_BLK, _GLA_R_BLK = 0, 1, 2
_GLA_LR_BLK = (2 * GLA_KEY_WIDTH + 2 * GLA_WIDTH) // LANE


def _rope_tables(L):
    t = np.arange(L)
    lane = np.arange(GLA_KEY_WIDTH)
    d = lane % GLA_DK
    pos = np.where(d[None, :] < GLA_DK // 2, (t // GRID_W)[:, None], (t % GRID_W)[:, None]).astype(np.float32)
    quarter = GLA_DK // 4
    inv = ROPE_BASE ** (-jnp.arange(quarter, dtype=F32) / quarter)
    ang = jnp.asarray(pos) * inv[jnp.asarray(d % quarter)][None, :]
    sign = np.where(d % (2 * quarter) < quarter, -1.0, 1.0).astype(np.float32)
    return jnp.cos(ang), jnp.sin(ang) * sign[None, :]


def _rope_partner(x):
    lane = lax.broadcasted_iota(jnp.int32, x.shape, 1)
    quarter = GLA_DK // 4
    return jnp.where(lane % (2 * quarter) < quarter, pltpu.roll(x, GLA_KEY_WIDTH - quarter, 1), pltpu.roll(x, quarter, 1))


def _log_sigmoid(x):
    return jnp.minimum(x, 0.0) - jnp.log(1.0 + jnp.exp(-jnp.abs(x)))


def _gla_kernel(reverse, finalize, *refs):
    if finalize:
        (qk_ref, v_ref, lr_ref, cos_ref, sin_ref, wa_ref, ba_ref, s0_ref, of_ref, r_ref, g_ref,
         o_ref, sfin_ref, s_scr) = refs
    else:
        qk_ref, v_ref, lr_ref, cos_ref, sin_ref, wa_ref, ba_ref, s0_ref, o_ref, sfin_ref, s_scr = refs
    hi = lax.Precision.HIGHEST

    @pl.when(pl.program_id(1) == 0)
    def _():
        s_scr[...] = s0_ref[0]

    tg = qk_ref.shape[1]
    C = GLA_CHUNK
    KW, VW = GLA_KEY_WIDTH, GLA_WIDTH
    qk = qk_ref[0]
    cos, sin = cos_ref[...], sin_ref[...]
    q = qk[:, :KW] * GLA_DK ** -0.5
    k = qk[:, KW:]
    q = q * cos + _rope_partner(q) * sin
    k = k * cos + _rope_partner(k) * sin
    v = v_ref[0]
    logit = jnp.dot(lr_ref[0], wa_ref[...], preferred_element_type=F32, precision=hi) + ba_ref[...]
    la = _log_sigmoid(logit) / GLA_GATE_NORM

    ri = lax.broadcasted_iota(jnp.int32, (C, C), 0)
    ci = lax.broadcasted_iota(jnp.int32, (C, C), 1)
    tri = jnp.where((ci >= ri) if reverse else (ci <= ri), 1.0, 0.0)
    tri_h = jnp.concatenate([tri] * GLA_HEADS, axis=0)
    head_k = lax.broadcasted_iota(jnp.int32, (C, KW), 1) // GLA_DK
    head_v = lax.broadcasted_iota(jnp.int32, (C, VW), 1) // GLA_DV
    own_block = (lax.broadcasted_iota(jnp.int32, (KW, VW), 0) // GLA_DK
                 == lax.broadcasted_iota(jnp.int32, (KW, VW), 1) // GLA_DV)
    eye = lax.broadcasted_iota(jnp.int32, (KW, KW), 0) == lax.broadcasted_iota(jnp.int32, (KW, KW), 1)

    S = s_scr[...]
    nc = tg // C
    outs = [None] * nc
    for c in (range(nc - 1, -1, -1) if reverse else range(nc)):
        sl = slice(c * C, (c + 1) * C)
        b = jnp.dot(tri, la[sl], preferred_element_type=F32, precision=hi)
        qt = q[sl] * jnp.exp(b)
        kt = k[sl] * jnp.exp(-b)
        qs = jnp.concatenate([jnp.where(head_k == h, qt, 0.0) for h in range(GLA_HEADS)], axis=0).astype(BF16)
        att = lax.dot_general(qs, kt.astype(BF16), _NT, preferred_element_type=F32)
        att = jnp.where(tri_h > 0.0, att, 0.0)
        vb = v[sl].astype(BF16)
        oi = jnp.dot(att.astype(BF16), vb, preferred_element_type=F32)
        o_intra = jnp.zeros((C, VW), F32)
        for h in range(GLA_HEADS):
            o_intra = jnp.where(head_v == h, oi[h * C:(h + 1) * C], o_intra)
        o_inter = jnp.dot(qt.astype(BF16), S.astype(BF16), preferred_element_type=F32)
        outs[c] = o_intra + o_inter
        b_last = b[0:1] if reverse else b[C - 1:C]
        kdec = (k[sl] * jnp.exp(b_last - b)).astype(BF16)
        kv = lax.dot_general(kdec, vb, (((0,), (0,)), ((), ())), preferred_element_type=F32)
        decay_col = jnp.sum(jnp.where(eye, jnp.exp(b_last), 0.0), axis=1, keepdims=True)
        S = decay_col * S + jnp.where(own_block, kv, 0.0)
    s_scr[...] = S
    sfin_ref[0] = S
    o = jnp.concatenate(outs, axis=0)
    if finalize:
        o = of_ref[0] + o
        same_head = (lax.broadcasted_iota(jnp.int32, (VW, VW), 0) // GLA_DV
                     == lax.broadcasted_iota(jnp.int32, (VW, VW), 1) // GLA_DV)
        ms = jnp.dot(o * o, jnp.where(same_head, 1.0 / GLA_DV, 0.0), preferred_element_type=F32, precision=hi)
        r = r_ref[0]
        o = o * lax.rsqrt(ms + LN_EPS) * g_ref[...] * (r * jax.nn.sigmoid(r))
    o_ref[0] = o


def _gla_pass(u_rest, cos, sin, wa, ba, s0, reverse, fin=None):
    B, T, _ = u_rest.shape
    tg = min(GLA_TILE, T)
    n = T // tg
    ti = (lambda i: n - 1 - i) if reverse else (lambda i: i)
    KW, VW = GLA_KEY_WIDTH, GLA_WIDTH
    ublk = lambda w, j: pl.BlockSpec((1, tg, w), lambda b, i: (b, ti(i), j))
    full = lambda s: pl.BlockSpec(s, lambda b, i: (0,) * len(s))
    state = pl.BlockSpec((1, KW, VW), lambda b, i: (b, 0, 0))
    tab = pl.BlockSpec((tg, KW), lambda b, i: (ti(i), 0))
    in_specs = [ublk(2 * KW, _GLA_QK_BLK), ublk(VW, _GLA_V_BLK), ublk(LANE, _GLA_LR_BLK), tab, tab,
                full((LANE, KW)), full((1, KW)), state]
    args = [u_rest, u_rest, u_rest, cos, sin, wa, ba, s0]
    if fin is not None:
        in_specs += [ublk(VW, 0), ublk(VW, _GLA_R_BLK), full((1, VW))]
        args += [fin[0], u_rest, fin[1]]
    return pl.pallas_call(
        functools.partial(_gla_kernel, reverse, fin is not None),
        grid=(B, n),
        in_specs=in_specs,
        out_specs=[ublk(VW, 0), state],
        out_shape=[jax.ShapeDtypeStruct((B, T, VW), F32), jax.ShapeDtypeStruct((B, KW, VW), F32)],
        scratch_shapes=[pltpu.VMEM((KW, VW), F32)],
        compiler_params=_cparams("arbitrary", "arbitrary"),
        name="gla_bwd" if reverse else "gla_fwd",
    )(*args)


def _gla_bidir(u_rest, cos, sin, w_a2, b_a2, norm_g, s0_f, s0_b):
    def decay_w(d):
        return jnp.zeros((LANE, GLA_KEY_WIDTH), F32).at[d * GLA_RANK:(d + 1) * GLA_RANK].set(w_a2[d])

    o_f, s_f = _gla_pass(u_rest, cos, sin, decay_w(0), b_a2[0:1], s0_f, False)
    g = jnp.tile(norm_g, GLA_HEADS).reshape(1, GLA_WIDTH)
    o, s_b = _gla_pass(u_rest, cos, sin, decay_w(1), b_a2[1:2], s0_b, True, (o_f, g))
    return o, s_f, s_b


HY_LANES = LANE
HY_CH_TILE = 16
HY_CTX_N1 = 16


def _hy_consts(n1, N1):
    W = HY_LANES
    n1p = max(n1, 16)
    n1o = max(n1, 8)
    a = np.arange(N1)
    th1 = 2.0 * np.pi * ((a[:, None] * a[None, :]) % N1) / N1
    f1 = np.zeros((2 * N1, n1p)); f1[:N1, :n1] = np.cos(th1[:, :n1]); f1[N1:, :n1] = -np.sin(th1[:, :n1])
    f1_full = np.concatenate([np.cos(th1), -np.sin(th1)], axis=0)
    ginv = np.zeros((n1o, 2 * N1)); ginv[:n1, :N1] = np.cos(th1.T[:n1]); ginv[:n1, N1:] = -np.sin(th1.T[:n1])
    r = np.arange(W)
    tht = 2.0 * np.pi * ((a[:, None] * r[None, :]) % (N1 * W)) / (N1 * W)
    tr = np.tile(np.cos(tht), (1, HY_CH_TILE)); ti = np.tile(-np.sin(tht), (1, HY_CH_TILE))
    th2 = 2.0 * np.pi * ((r[:, None] * r[None, :]) % W) / W
    c2, s2 = np.cos(th2), -np.sin(th2)
    m2f = np.block([[c2, s2], [-s2, c2]])
    m2i = np.block([[c2, -s2], [s2, c2]])
    f = lambda m: jnp.asarray(m, F32)
    return dict(n1p=n1p, n1o=n1o, f1=f(f1), f1_full=f(f1_full), ginv=f(ginv), tr=f(tr), ti=f(ti), m2f=f(m2f), m2i=f(m2i))


def _short_conv_rows(u, w_ref, b_ref, n_rows):
    R, Wd = u.shape
    lane = lax.broadcasted_iota(jnp.int32, (R, Wd), 1) % HY_LANES
    row = lax.broadcasted_iota(jnp.int32, (R, Wd), 0)
    up = jnp.where(row == 0, 0.0, pltpu.roll(u, 1, 0))
    dn = jnp.where(row == n_rows - 1, 0.0, pltpu.roll(u, R - 1, 0))
    prev = jnp.where(lane == 0, pltpu.roll(up, Wd - (HY_LANES - 1), 1), pltpu.roll(u, 1, 1))
    nxt = jnp.where(lane == HY_LANES - 1, pltpu.roll(dn, HY_LANES - 1, 1), pltpu.roll(u, Wd - 1, 1))
    return b_ref[...] + prev * w_ref[0:1] + u * w_ref[1:2] + nxt * w_ref[2:3]


def _pad_rows(u, rows):
    return u if u.shape[0] == rows else jnp.concatenate([u, jnp.zeros((rows - u.shape[0], u.shape[1]), u.dtype)], axis=0)


def _hy_chunk_dft(z, f1, tr, ti, N1, prec):
    if prec is None:
        a = jnp.dot(f1.astype(BF16), z.astype(BF16), preferred_element_type=F32)
    else:
        a = jnp.dot(f1, z, preferred_element_type=F32, precision=prec)
    ar, ai = a[:N1], a[N1:]
    a_re, a_im = ar * tr - ai * ti, ar * ti + ai * tr
    W = HY_LANES
    return jnp.concatenate(
        [jnp.concatenate([a_re[:, c * W:(c + 1) * W], a_im[:, c * W:(c + 1) * W]], axis=1) for c in range(z.shape[1] // W)],
        axis=0)


def _hy_long_conv(z, kf, f1, ginv, tr, ti, m2f, m2i, N1):
    W = HY_LANES
    m = z.shape[1] // W
    a = _hy_chunk_dft(z, f1, tr, ti, N1, None).astype(BF16)
    x = jnp.dot(a, m2f.astype(BF16), preferred_element_type=F32)
    xr, xi = x[:, :W], x[:, W:]
    kr, ki = kf[:, :W], kf[:, W:]
    y = jnp.concatenate([xr * kr - xi * ki, xr * ki + xi * kr], axis=1).astype(BF16)
    p = jnp.dot(y, m2i.astype(BF16), preferred_element_type=F32)
    pr = jnp.concatenate([p[c * N1:(c + 1) * N1, :W] for c in range(m)], axis=1)
    pi = jnp.concatenate([p[c * N1:(c + 1) * N1, W:] for c in range(m)], axis=1)
    q = jnp.concatenate([pr * tr + pi * ti, pi * tr - pr * ti], axis=0).astype(BF16)
    return jnp.dot(ginv.astype(BF16), q, preferred_element_type=F32)


def _hy_mixer_kernel(n1, N1, n1p, v_ref, x1_ref, x2_ref, swv_ref, sbv_ref, swx1_ref, sbx1_ref, swx2_ref, sbx2_ref,
                     skip_ref, kf_ref, f1_ref, ginv_ref, tr_ref, ti_ref, m2f_ref, m2i_ref, o_ref):
    rows = max(n1, 8)
    cst = (f1_ref[...], ginv_ref[...], tr_ref[...], ti_ref[...], m2f_ref[...], m2i_ref[...], N1)
    z = _short_conv_rows(_pad_rows(v_ref[0], rows), swv_ref, sbv_ref, n1)
    gates = (_short_conv_rows(_pad_rows(x1_ref[0], rows), swx1_ref, sbx1_ref, n1),
             _short_conv_rows(_pad_rows(x2_ref[0], rows), swx2_ref, sbx2_ref, n1))
    for o in range(HY_ORDER):
        y = _hy_long_conv(_pad_rows(z, n1p), kf_ref[o, 0], *cst)
        z = gates[o] * (y + skip_ref[o:o + 1] * z)
    o_ref[0] = z[:n1]


def _hy_kspec_kernel(N1, scale, k_ref, f1_ref, tr_ref, ti_ref, m2f_ref, kf_ref):
    a = _hy_chunk_dft(k_ref[0], f1_ref[...], tr_ref[...], ti_ref[...], N1, lax.Precision.HIGHEST)
    kf_ref[0, 0] = jnp.dot(a, m2f_ref[...], preferred_element_type=F32, precision=lax.Precision.HIGHEST) * scale


def _hy_filter_spectrum(filt, N1, cst):
    Ls, n_ord, _, C = filt.shape
    W = HY_LANES
    N = N1 * W
    kern = jnp.concatenate([filt[:, :, 0], jnp.zeros((N - 2 * Ls + 1, n_ord, C), F32), filt[:0:-1, :, 1]], axis=0)
    kern = kern.reshape(N1, W, n_ord, C).transpose(2, 0, 3, 1).reshape(n_ord, N1, C * W)
    Wd = HY_CH_TILE * W
    nj = C // HY_CH_TILE
    full = lambda s: pl.BlockSpec(s, lambda o, j: (0,) * len(s))
    return pl.pallas_call(
        functools.partial(_hy_kspec_kernel, N1, 1.0 / N),
        grid=(n_ord, nj),
        in_specs=[pl.BlockSpec((1, N1, Wd), lambda o, j: (o, 0, j)), full((2 * N1, N1)), full((N1, Wd)), full((N1, Wd)),
                  full((2 * W, 2 * W))],
        out_specs=pl.BlockSpec((1, 1, HY_CH_TILE * N1, 2 * W), lambda o, j: (o, j, 0, 0)),
        out_shape=jax.ShapeDtypeStruct((n_ord, nj, HY_CH_TILE * N1, 2 * W), F32),
        compiler_params=_cparams("arbitrary", "arbitrary"),
        name="hyena_filter_spectrum",
    )(kern, cst["f1_full"], cst["tr"], cst["ti"], cst["m2f"])


def _hyena_mixer(hy_t, short_w, short_b, filt, skip, N1):
    B, n1, C3, W = hy_t.shape
    C = C3 // (HY_ORDER + 1)
    cst = _hy_consts(n1, N1)
    kf = _hy_filter_spectrum(filt, N1, cst)
    Wd = HY_CH_TILE * W
    nj = C // HY_CH_TILE
    hy2 = hy_t.reshape(B, n1, C3 * W)
    rep = lambda v: jnp.repeat(v, W, axis=-1)
    sw, sb = rep(short_w), rep(short_b).reshape(1, C3 * W)
    sk = rep(skip)
    chan = lambda part: pl.BlockSpec((1, n1, Wd), lambda j, b: (b, 0, part * nj + j))
    wsp = lambda part: pl.BlockSpec((HY_SHORT, Wd), lambda j, b: (0, part * nj + j))
    bsp = lambda part: pl.BlockSpec((1, Wd), lambda j, b: (0, part * nj + j))
    full = lambda a: pl.BlockSpec(a.shape, lambda j, b: (0,) * a.ndim)
    consts = [cst[k] for k in ("f1", "ginv", "tr", "ti", "m2f", "m2i")]
    out = pl.pallas_call(
        functools.partial(_hy_mixer_kernel, n1, N1, cst["n1p"]),
        grid=(nj, B),
        in_specs=[chan(0), chan(1), chan(2), wsp(0), bsp(0), wsp(1), bsp(1), wsp(2), bsp(2),
                  pl.BlockSpec((HY_ORDER, Wd), lambda j, b: (0, j)),
                  pl.BlockSpec((HY_ORDER, 1, HY_CH_TILE * N1, 2 * W), lambda j, b: (0, j, 0, 0))]
                 + [full(a) for a in consts],
        out_specs=pl.BlockSpec((1, n1, Wd), lambda j, b: (b, 0, j)),
        out_shape=jax.ShapeDtypeStruct((B, n1, C * W), F32),
        compiler_params=_cparams("arbitrary", "arbitrary"),
        name="hyena_mixer",
    )(hy2, hy2, hy2, sw, sb, sw, sb, sw, sb, sk, kf, *consts)
    return out.reshape(B, n1, C, W)


def _hyena_filters(L, w1, b1, f1, w2, b2, f2, w3, b3, decay):
    hi = lax.Precision.HIGHEST
    t = jnp.linspace(0.0, 1.0, L, dtype=F32)
    w = 2.0 * math.pi * jnp.arange(L, dtype=F32) / L
    bands = jnp.linspace(1e-4, HY_BANDS - 1, HY_BANDS, dtype=F32)
    ang = w[:, None] * bands[None]
    z = jnp.concatenate([t[:, None], jnp.cos(ang), -jnp.sin(ang)], -1)
    h = jnp.sin(f1 * (jnp.dot(z, w1, precision=hi) + b1))
    h = jnp.sin(f2 * (jnp.dot(h, w2, precision=hi) + b2))
    h = (jnp.dot(h, w3, precision=hi) + b3).reshape(L, HY_ORDER, 2, HY_CH)
    window = jnp.exp(-t[:, None, None, None] * jnp.abs(decay)[None])
    return h * window


def kernel(x, c, ctx, c_ctx, w_mod, b_mod, w_in, na_rpb, gla_w_a2, gla_b_a2, gla_norm_g, hy_short_w, hy_short_b, hy_w1, hy_b1, hy_f1, hy_w2, hy_b2, hy_f2, hy_w3, hy_b3, hy_decay, hy_skip, w_out, ln1_g, ln1_b, router_w, router_b, we_gate, we_up, we_down, ws_gate, ws_up, ws_down, ln2_g, ln2_b):
    B, L, D = x.shape
    C = ctx.shape[1]
    rope_cos, rope_sin = _rope_tables(L)
    ctx_cos, ctx_sin = jnp.ones((C, GLA_KEY_WIDTH), F32), jnp.zeros((C, GLA_KEY_WIDTH), F32)
    zero_state = jnp.zeros((B, GLA_KEY_WIDTH, GLA_WIDTH), F32)
    na_bias = _na_bias_table(na_rpb)

    n_mod = -(-(B + 1) // 8) * 8
    cs = jnp.zeros((n_mod, D), F32).at[:B].set(c).at[B].set(c_ctx)
    mod_all = _modulation(cs, w_mod, b_mod)

    xc = ctx
    for l in range(DEPTH):
        last = l == DEPTH - 1
        mods = mod_all[l].reshape(n_mod, 6, 1, D)
        lat = lambda j: mods[:B, j]
        cm = lambda j: mods[B:B + 1, j]
        n_tok = NA_COLS + GLA_COLS
        w_pad = jnp.pad(w_in[l][:, :n_tok], ((0, 0), (0, D_IN_PAD - n_tok))).astype(BF16)
        w_hy_t = w_in[l][:, n_tok:].T.astype(BF16)
        w_out_b = w_out[l].astype(BF16)
        lg1, lb1 = ln1_g[l].reshape(1, D), ln1_b[l].reshape(1, D)
        lg2, lb2 = ln2_g[l].reshape(1, D), ln2_b[l].reshape(1, D)

        u_na, u_rest, hy = _inproj(x, lat(1), lat(0), w_pad, w_hy_t, True)
        uc_na, uc_rest, hyc = _inproj(xc, cm(1), cm(0), w_pad, w_hy_t, False)

        na_lat = _na_attention(u_na, uc_na, na_bias, l)

        gla_c, s_ctx_f, s_ctx_b = _gla_bidir(uc_rest, ctx_cos, ctx_sin, gla_w_a2[l], gla_b_a2[l], gla_norm_g[l],
                                             zero_state, zero_state)
        gla_lat, _, _ = _gla_bidir(u_rest, rope_cos, rope_sin, gla_w_a2[l], gla_b_a2[l], gla_norm_g[l],
                                   s_ctx_f, s_ctx_b)

        filt_args = (hy_w1[l], hy_b1[l], hy_f1[l], hy_w2[l], hy_b2[l], hy_f2[l], hy_w3[l], hy_b3[l], hy_decay[l])
        hy_lat = _hyena_mixer(hy, hy_short_w[l], hy_short_b[l], _hyena_filters(L, *filt_args), hy_skip[l],
                              2 * L // HY_LANES)

        wr_t = router_w[l].T
        x, h_lat, logit_lat = _outproj(na_lat, gla_lat, hy_lat, x, lat(2), w_out_b, lg1, lb1, lat(4), lat(3), wr_t, True)
        if not last:
            na_c = _ctx_attention(uc_na)
            hy_c = _hyena_mixer(hyc, hy_short_w[l], hy_short_b[l], _hyena_filters(C, *filt_args), hy_skip[l],
                                HY_CTX_N1)
            xc, h_c, logit_c = _outproj(na_c, gla_c, hy_c, xc, cm(2), w_out_b, lg1, lb1, cm(4), cm(3), wr_t, False)

            h_flat = jnp.concatenate([h_lat.reshape(B * L, D), h_c.reshape(B * C, D)], axis=0)
            logit_t = jnp.concatenate([logit_lat, logit_c], axis=1)
        else:
            h_flat = h_lat.reshape(B * L, D)
            logit_t = logit_lat

        picked, gates = _moe_routed(h_flat, logit_t, router_b[l], we_gate, we_up, we_down, l)
        wsg, wsu, wsd = ws_gate[l].astype(BF16), ws_up[l].astype(BF16), ws_down[l].astype(BF16)
        x = _shared_ln2(x, h_lat, picked, gates, 0, lat(5), wsg, wsu, wsd, lg2, lb2, True)
        if not last:
            xc = _shared_ln2(xc, h_c, picked, gates, B * L, cm(5), wsg, wsu, wsd, lg2, lb2, False)
    return x
```

```python
import functools
import math

import numpy as np
import jax
import jax.numpy as jnp
from jax import lax
from jax.experimental import pallas as pl
from jax.experimental.pallas import tpu as pltpu
from jax.experimental.pallas import tpu_sc as plsc

F32 = jnp.float32
BF16 = jnp.bfloat16

D_MODEL = 1024
DEPTH = 4
GRID_W = 64
CTX_LEN = 256

NA_HEADS = 8
NA_HEAD_DIM = 64
NA_WIDTH = NA_HEADS * NA_HEAD_DIM
NA_WIN_ROWS = 8
NA_WIN_COLS = 16

GLA_HEADS = 4
GLA_DK = 32
GLA_DV = 64
GLA_KEY_WIDTH = GLA_HEADS * GLA_DK
GLA_WIDTH = GLA_HEADS * GLA_DV
GLA_RANK = 16
GLA_GATE_NORM = 16.0
GLA_CHUNK = 64

HY_CH = 256
HY_ORDER = 2
HY_SHORT = 3
HY_BANDS = 16
HY_EMB = 1 + 2 * HY_BANDS

MIX_WIDTH = NA_WIDTH + GLA_WIDTH + HY_CH
IN_SPLITS = (NA_WIDTH, NA_WIDTH, NA_WIDTH, GLA_KEY_WIDTH, GLA_KEY_WIDTH, GLA_WIDTH, GLA_WIDTH,
             2 * GLA_RANK, (HY_ORDER + 1) * HY_CH)
D_IN = sum(IN_SPLITS)
ROPE_BASE = 10000.0

N_EXPERTS = 128
TOP_K = 8
N_GROUPS = 8
TOPK_GROUPS = 4
EXPERT_HIDDEN = 256
ROUTED_SCALE = 2.5

DEEPNORM_ALPHA = (2 * DEPTH) ** 0.25
LN_EPS = 1e-6

LANE = 128
MXU_DIM = 256
VMEM_LIMIT = 48 * 1024 * 1024
SC_CORES = 2
SC_SUBCORES = 16
SC_LANES = 16
SC_SCAN_CHUNK = 16384

NA_COLS = 3 * NA_WIDTH
GLA_COLS = sum(IN_SPLITS[3:8])
HY_COLS = IN_SPLITS[8]
REST_COLS = -(-GLA_COLS // LANE) * LANE
D_IN_PAD = NA_COLS + REST_COLS
NA_QUAD = MXU_DIM // NA_HEAD_DIM
NEG_BIG = -1e30

TOK_TILE = 512
MOE_TILE = 512
NA_ROW_TILE = 8
NA_ROW_UNROLL = 4
ROUTE_TILE = 256
GLA_TILE = 512


def _cparams(*sem):
    return pltpu.CompilerParams(dimension_semantics=sem, vmem_limit_bytes=VMEM_LIMIT)


def _mod_kernel(c_ref, w_ref, b_ref, o_ref):
    c = c_ref[...]
    s = c * jax.nn.sigmoid(c)
    o_ref[0] = jnp.dot(s, w_ref[0], preferred_element_type=F32, precision=lax.Precision.HIGHEST) + b_ref[0]


def _modulation(cs, w_mod, b_mod):
    R = cs.shape[0]
    tn = 1536
    return pl.pallas_call(
        _mod_kernel,
        grid=(DEPTH, 6 * D_MODEL // tn),
        in_specs=[pl.BlockSpec((R, D_MODEL), lambda l, j: (0, 0)),
                  pl.BlockSpec((1, D_MODEL, tn), lambda l, j: (l, 0, j)),
                  pl.BlockSpec((1, 1, tn), lambda l, j: (l, 0, j))],
        out_specs=pl.BlockSpec((1, R, tn), lambda l, j: (l, 0, j)),
        out_shape=jax.ShapeDtypeStruct((DEPTH, R, 6 * D_MODEL), F32),
        compiler_params=_cparams("arbitrary", "arbitrary"),
        name="modulation",
    )(cs, w_mod, b_mod.reshape(DEPTH, 1, 6 * D_MODEL))


def _inproj_kernel(x_ref, sc_ref, sh_ref, w_ref, wh_ref, ona_ref, orest_ref, ohy_ref):
    xm = (x_ref[0] * (1.0 + sc_ref[0]) + sh_ref[0]).astype(BF16)
    step = 512
    for c0 in range(0, NA_COLS, step):
        ona_ref[0, :, c0:c0 + step] = jnp.dot(xm, w_ref[:, c0:c0 + step], preferred_element_type=F32).astype(BF16)
    for c0 in range(0, REST_COLS, step):
        c1 = min(c0 + step, REST_COLS)
        orest_ref[0, :, c0:c1] = jnp.dot(xm, w_ref[:, NA_COLS + c0:NA_COLS + c1], preferred_element_type=F32)
    hy = lax.dot_general(wh_ref[...], xm, _NT, preferred_element_type=F32)
    for j in range(ohy_ref.shape[1]):
        ohy_ref[0, j] = hy[:, j * LANE:(j + 1) * LANE]


def _inproj(x, sc, sh, w_pad, w_hy_t, per_batch_mod):
    B, T, D = x.shape
    tm = min(TOK_TILE, T)
    mod_idx = (lambda b, i: (b, 0, 0)) if per_batch_mod else (lambda b, i: (0, 0, 0))
    return pl.pallas_call(
        _inproj_kernel,
        grid=(B, T // tm),
        in_specs=[pl.BlockSpec((1, tm, D), lambda b, i: (b, i, 0)),
                  pl.BlockSpec((1, 1, D), mod_idx),
                  pl.BlockSpec((1, 1, D), mod_idx),
                  pl.BlockSpec((D, D_IN_PAD), lambda b, i: (0, 0)),
                  pl.BlockSpec((HY_COLS, D), lambda b, i: (0, 0))],
        out_specs=[pl.BlockSpec((1, tm, NA_COLS), lambda b, i: (b, i, 0)),
                   pl.BlockSpec((1, tm, REST_COLS), lambda b, i: (b, i, 0)),
                   pl.BlockSpec((1, tm // LANE, HY_COLS, LANE), lambda b, i: (b, i, 0, 0))],
        out_shape=[jax.ShapeDtypeStruct((B, T, NA_COLS), BF16),
                   jax.ShapeDtypeStruct((B, T, REST_COLS), F32),
                   jax.ShapeDtypeStruct((B, T // LANE, HY_COLS, LANE), F32)],
        compiler_params=_cparams("arbitrary", "arbitrary"),
        name="inproj",
    )(x, sc, sh, w_pad, w_hy_t)


def _stack_heads(q, n_rows):
    head = lax.broadcasted_iota(jnp.int32, (n_rows, MXU_DIM), 1) // NA_HEAD_DIM
    return jnp.concatenate([jnp.where(head == h, q, jnp.zeros_like(q)) for h in range(NA_QUAD)], axis=0)


def _unstack_heads(o, n_rows):
    head = lax.broadcasted_iota(jnp.int32, (n_rows, MXU_DIM), 1) // NA_HEAD_DIM
    out = jnp.zeros((n_rows, MXU_DIM), F32)
    for h in range(NA_QUAD):
        out = jnp.where(head == h, o[h * n_rows:(h + 1) * n_rows], out)
    return out


_NT = (((1,), (1,)), ((), ()))


def _na_kernel(q_ref, k_ref, v_ref, kc_ref, vc_ref, bias_ref, o_ref):
    rt = pl.program_id(2)
    scale = NA_HEAD_DIM ** -0.5
    kc = kc_ref[0]
    vc = vc_ref[0]
    n_loc = NA_WIN_ROWS * GRID_W

    def row(rl, carry):
        r = rt * NA_ROW_TILE + rl
        kr0 = jnp.clip(r - NA_WIN_ROWS // 2, 0, GRID_W - NA_WIN_ROWS)
        dr0 = kr0 - r + NA_WIN_ROWS - 1
        q = q_ref[0, pl.ds(pl.multiple_of(rl * GRID_W, GRID_W), GRID_W), :]
        qs = _stack_heads(q, GRID_W)
        k0 = pl.multiple_of(kr0 * GRID_W, GRID_W)
        ks = k_ref[0, pl.ds(k0, n_loc), :]
        vs = v_ref[0, pl.ds(k0, n_loc), :]
        s_loc = lax.dot_general(qs, ks, _NT, preferred_element_type=F32) * scale + bias_ref[0, dr0]
        s_ctx = lax.dot_general(qs, kc, _NT, preferred_element_type=F32) * scale
        m = jnp.maximum(jnp.max(s_loc, axis=-1, keepdims=True), jnp.max(s_ctx, axis=-1, keepdims=True))
        p_loc = jnp.exp(s_loc - m)
        p_ctx = jnp.exp(s_ctx - m)
        den = jnp.sum(p_loc, axis=-1, keepdims=True) + jnp.sum(p_ctx, axis=-1, keepdims=True)
        o = (jnp.dot(p_loc.astype(BF16), vs, preferred_element_type=F32)
             + jnp.dot(p_ctx.astype(BF16), vc, preferred_element_type=F32)) / den
        o_ref[0, pl.ds(pl.multiple_of(rl * GRID_W, GRID_W), GRID_W), :] = _unstack_heads(o, GRID_W)
        return carry

    lax.fori_loop(0, NA_ROW_TILE, row, 0, unroll=NA_ROW_UNROLL)


def _na_bias_table(rpb):
    n_lyr = rpb.shape[0]
    c = np.arange(GRID_W)
    kc0 = np.clip(c - NA_WIN_COLS // 2, 0, GRID_W - NA_WIN_COLS)
    kc = np.arange(GRID_W)
    valid = (kc[None, :] >= kc0[:, None]) & (kc[None, :] < kc0[:, None] + NA_WIN_COLS)
    dc = kc[None, :] - c[:, None] + NA_WIN_COLS - 1
    onehot = (np.arange(2 * NA_WIN_COLS - 1)[:, None, None] == dc[None]) & valid[None]
    toep = jnp.einsum('lhrd,dck->lhrck', rpb, jnp.asarray(onehot, F32), precision=lax.Precision.HIGHEST)
    toep = jnp.where(jnp.asarray(valid)[None, None, None], toep, NEG_BIG)
    tab = jnp.stack([toep[:, :, d:d + NA_WIN_ROWS] for d in range(NA_WIN_ROWS)], axis=2)
    tab = tab.transpose(0, 1, 2, 4, 3, 5).reshape(n_lyr, NA_HEADS // NA_QUAD, NA_QUAD, NA_WIN_ROWS, GRID_W,
                                                  NA_WIN_ROWS * GRID_W)
    return tab.transpose(0, 1, 3, 2, 4, 5).reshape(n_lyr, NA_HEADS // NA_QUAD, NA_WIN_ROWS, NA_QUAD * GRID_W,
                                                   NA_WIN_ROWS * GRID_W)


def _na_attention(u_na, uc_na, bias_tab, layer):
    B, L, _ = u_na.shape
    C = uc_na.shape[1]
    nq = NA_WIDTH // MXU_DIM
    tq = NA_ROW_TILE * GRID_W
    return pl.pallas_call(
        _na_kernel,
        grid=(B, nq, L // tq),
        in_specs=[pl.BlockSpec((1, tq, MXU_DIM), lambda b, j, i: (b, i, j)),
                  pl.BlockSpec((1, L, MXU_DIM), lambda b, j, i: (b, 0, nq + j)),
                  pl.BlockSpec((1, L, MXU_DIM), lambda b, j, i: (b, 0, 2 * nq + j)),
                  pl.BlockSpec((1, C, MXU_DIM), lambda b, j, i: (b, 0, nq + j)),
                  pl.BlockSpec((1, C, MXU_DIM), lambda b, j, i: (b, 0, 2 * nq + j)),
                  pl.BlockSpec((None, 1, NA_WIN_ROWS, NA_QUAD * GRID_W, NA_WIN_ROWS * GRID_W),
                               lambda b, j, i: (layer, j, 0, 0, 0))],
        out_specs=pl.BlockSpec((1, tq, MXU_DIM), lambda b, j, i: (b, i, j)),
        out_shape=jax.ShapeDtypeStruct((B, L, NA_WIDTH), F32),
        compiler_params=_cparams("arbitrary", "arbitrary", "arbitrary"),
        name="na_attention",
    )(u_na, u_na, u_na, uc_na, uc_na, bias_tab)


def _ctx_attn_kernel(q_ref, k_ref, v_ref, o_ref):
    C = q_ref.shape[1]
    qs = _stack_heads(q_ref[0], C)
    s = lax.dot_general(qs, k_ref[0], _NT, preferred_element_type=F32) * NA_HEAD_DIM ** -0.5
    p = jnp.exp(s - jnp.max(s, axis=-1, keepdims=True))
    den = jnp.sum(p, axis=-1, keepdims=True)
    o = jnp.dot(p.astype(BF16), v_ref[0], preferred_element_type=F32) / den
    o_ref[0] = _unstack_heads(o, C)


def _ctx_attention(uc_na):
    B, C, _ = uc_na.shape
    nq = NA_WIDTH // MXU_DIM
    return pl.pallas_call(
        _ctx_attn_kernel,
        grid=(B, nq),
        in_specs=[pl.BlockSpec((1, C, MXU_DIM), lambda b, j: (b, 0, j)),
                  pl.BlockSpec((1, C, MXU_DIM), lambda b, j: (b, 0, nq + j)),
                  pl.BlockSpec((1, C, MXU_DIM), lambda b, j: (b, 0, 2 * nq + j))],
        out_specs=pl.BlockSpec((1, C, MXU_DIM), lambda b, j: (b, 0, j)),
        out_shape=jax.ShapeDtypeStruct((B, C, NA_WIDTH), F32),
        compiler_params=_cparams("arbitrary", "arbitrary"),
        name="ctx_attention",
    )(uc_na, uc_na, uc_na)


def _layer_norm_rows(y, g, b):
    mu = jnp.mean(y, axis=-1, keepdims=True)
    d = y - mu
    var = jnp.mean(d * d, axis=-1, keepdims=True)
    return d * lax.rsqrt(var + LN_EPS) * g + b


def _outproj_kernel(na_ref, gla_ref, hy_ref, x_ref, g1_ref, w_ref, lg_ref, lb_ref, sc_ref, sh_ref, wr_ref,
                    xo_ref, h_ref, logit_ref):
    w_hy = w_ref[NA_WIDTH + GLA_WIDTH:, :]
    hy_mix = jnp.concatenate(
        [lax.dot_general(hy_ref[0, j].astype(BF16), w_hy, (((0,), (0,)), ((), ())), preferred_element_type=F32)
         for j in range(hy_ref.shape[1])], axis=0)
    mix = (jnp.dot(na_ref[0].astype(BF16), w_ref[0:NA_WIDTH, :], preferred_element_type=F32)
           + jnp.dot(gla_ref[0].astype(BF16), w_ref[NA_WIDTH:NA_WIDTH + GLA_WIDTH, :], preferred_element_type=F32)
           + hy_mix)
    xn = _layer_norm_rows(DEEPNORM_ALPHA * x_ref[0] + g1_ref[0] * mix, lg_ref[...], lb_ref[...])
    xo_ref[0] = xn
    h = xn * (1.0 + sc_ref[0]) + sh_ref[0]
    h_ref[0] = h.astype(BF16)
    logit_ref[...] = lax.dot_general(wr_ref[...], h, _NT, preferred_element_type=F32, precision=lax.Precision.HIGHEST)


def _outproj(na, gla, hy, x, g1, w_out, ln_g, ln_b, sc2, sh2, w_router, per_batch_mod):
    B, T, D = x.shape
    tm = min(TOK_TILE, T)
    mod_idx = (lambda b, i: (b, 0, 0)) if per_batch_mod else (lambda b, i: (0, 0, 0))
    tok = lambda w: pl.BlockSpec((1, tm, w), lambda b, i: (b, i, 0))
    full = lambda s: pl.BlockSpec(s, lambda b, i: (0,) * len(s))
    mod = pl.BlockSpec((1, 1, D), mod_idx)
    return pl.pallas_call(
        _outproj_kernel,
        grid=(B, T // tm),
        in_specs=[tok(NA_WIDTH), tok(GLA_WIDTH),
                  pl.BlockSpec((1, tm // LANE, HY_CH, LANE), lambda b, i: (b, i, 0, 0)), tok(D), mod, full((MIX_WIDTH, D)),
                  full((1, D)), full((1, D)), mod, mod, full((N_EXPERTS, D))],
        out_specs=[tok(D), tok(D), pl.BlockSpec((N_EXPERTS, tm), lambda b, i: (0, b * (T // tm) + i))],
        out_shape=[jax.ShapeDtypeStruct((B, T, D), F32), jax.ShapeDtypeStruct((B, T, D), BF16),
                   jax.ShapeDtypeStruct((N_EXPERTS, B * T), F32)],
        compiler_params=_cparams("arbitrary", "arbitrary"),
        name="outproj_ln1",
    )(na, gla, hy, x, g1, w_out, ln_g, ln_b, sc2, sh2, w_router)


def _moe_kernel(te_ref, nt_ref, xs_ref, wg_ref, wu_ref, wd_ref, ys_ref, wg_s, wu_s, wd_s):
    i = pl.program_id(0)

    @pl.when(jnp.logical_or(i == 0, te_ref[i] != te_ref[jnp.maximum(i - 1, 0)]))
    def _():
        wg_s[...] = wg_ref[0].astype(BF16)
        wu_s[...] = wu_ref[0].astype(BF16)
        wd_s[...] = wd_ref[0].astype(BF16)

    @pl.when(i < nt_ref[0])
    def _():
        xs = xs_ref[...]
        g = jnp.dot(xs, wg_s[...], preferred_element_type=F32)
        u = jnp.dot(xs, wu_s[...], preferred_element_type=F32)
        a = (g * jax.nn.sigmoid(g) * u).astype(BF16)
        ys_ref[...] = jnp.dot(a, wd_s[...], preferred_element_type=F32).astype(ys_ref.dtype)

    @pl.when(i >= nt_ref[0])
    def _():
        ys_ref[...] = jnp.zeros_like(ys_ref)


def _moe_grouped(tile_expert, n_used, xs, we_g, we_u, we_d, layer):
    Mp, D = xs.shape
    n_tiles = Mp // MOE_TILE
    H = EXPERT_HIDDEN
    grid_spec = pltpu.PrefetchScalarGridSpec(
        num_scalar_prefetch=2,
        grid=(n_tiles,),
        in_specs=[pl.BlockSpec((MOE_TILE, D), lambda i, te, nt: (i, 0)),
                  pl.BlockSpec((None, 1, D, H), lambda i, te, nt: (layer, te[i], 0, 0)),
                  pl.BlockSpec((None, 1, D, H), lambda i, te, nt: (layer, te[i], 0, 0)),
                  pl.BlockSpec((None, 1, H, D), lambda i, te, nt: (layer, te[i], 0, 0))],
        out_specs=pl.BlockSpec((MOE_TILE, D), lambda i, te, nt: (i, 0)),
        scratch_shapes=[pltpu.VMEM((D, H), BF16), pltpu.VMEM((D, H), BF16), pltpu.VMEM((H, D), BF16)],
    )
    return pl.pallas_call(
        _moe_kernel,
        grid_spec=grid_spec,
        out_shape=jax.ShapeDtypeStruct((Mp, D), BF16),
        compiler_params=_cparams("arbitrary"),
        name="moe_experts",
    )(tile_expert, n_used, xs, we_g, we_u, we_d)


def _shared_kernel(x_ref, h_ref, pk_ref, gt_ref, g2_ref, wg_ref, wu_ref, wd_ref, lg_ref, lb_ref, o_ref):
    h = h_ref[0]
    g = jnp.dot(h, wg_ref[...], preferred_element_type=F32)
    u = jnp.dot(h, wu_ref[...], preferred_element_type=F32)
    a = (g * jax.nn.sigmoid(g) * u).astype(BF16)
    ff = jnp.dot(a, wd_ref[...], preferred_element_type=F32)
    gates = gt_ref[...]
    for k in range(TOP_K):
        ff = ff + pk_ref[k].astype(F32) * gates[:, k:k + 1]
    o_ref[0] = _layer_norm_rows(DEEPNORM_ALPHA * x_ref[0] + g2_ref[0] * ff, lg_ref[...], lb_ref[...])


def _shared_ln2(x, h, picked, gates, tok_off, g2, ws_g, ws_u, ws_d, ln_g, ln_b, per_batch_mod):
    B, T, D = x.shape
    tm = min(TOK_TILE, T)
    mod_idx = (lambda b, i: (b, 0, 0)) if per_batch_mod else (lambda b, i: (0, 0, 0))
    tok = pl.BlockSpec((1, tm, D), lambda b, i: (b, i, 0))
    full = lambda s: pl.BlockSpec(s, lambda b, i: (0,) * len(s))
    flat = lambda b, i: tok_off // tm + b * (T // tm) + i
    return pl.pallas_call(
        _shared_kernel,
        grid=(B, T // tm),
        in_specs=[tok, tok, pl.BlockSpec((TOP_K, tm, D), lambda b, i: (0, flat(b, i), 0)),
                  pl.BlockSpec((tm, TOP_K), lambda b, i: (flat(b, i), 0)),
                  pl.BlockSpec((1, 1, D), mod_idx), full((D, EXPERT_HIDDEN)), full((D, EXPERT_HIDDEN)),
                  full((EXPERT_HIDDEN, D)), full((1, D)), full((1, D))],
        out_specs=tok,
        out_shape=jax.ShapeDtypeStruct((B, T, D), F32),
        compiler_params=_cparams("arbitrary", "arbitrary"),
        name="shared_ln2",
    )(x, h, picked, gates, g2, ws_g, ws_u, ws_d, ln_g, ln_b)


def _first_max(vals, iota, n):
    m = jnp.max(vals, axis=0, keepdims=True)
    idx = jnp.min(jnp.where(vals == m, iota, n), axis=0, keepdims=True)
    return m, idx


def _route_kernel(lt_ref, b_ref, eidx_ref, w_ref, rank_ref, cnt_ref, base_ref):
    @pl.when(pl.program_id(0) == 0)
    def _():
        base_ref[...] = jnp.zeros_like(base_ref)

    tm = lt_ref.shape[1]
    per = N_EXPERTS // N_GROUPS
    s = jax.nn.sigmoid(lt_ref[...])
    sel = s + b_ref[...]
    io_g = lax.broadcasted_iota(jnp.int32, (per, tm), 0)
    scores = []
    for g in range(N_GROUPS):
        blk = sel[g * per:(g + 1) * per]
        m1, i1 = _first_max(blk, io_g, per)
        m2 = jnp.max(jnp.where(io_g == i1, -jnp.inf, blk), axis=0, keepdims=True)
        scores.append(m1 + m2)
    cur = jnp.concatenate(scores, axis=0)
    io_8 = lax.broadcasted_iota(jnp.int32, (N_GROUPS, tm), 0)
    gmask = jnp.zeros((N_GROUPS, tm), F32)
    for _ in range(TOPK_GROUPS):
        _, gi = _first_max(cur, io_8, N_GROUPS)
        hit = io_8 == gi
        gmask = jnp.where(hit, 1.0, gmask)
        cur = jnp.where(hit, -jnp.inf, cur)
    masked = jnp.concatenate(
        [jnp.where(gmask[g:g + 1] > 0.0, sel[g * per:(g + 1) * per], -jnp.inf) for g in range(N_GROUPS)], axis=0)
    io_e = lax.broadcasted_iota(jnp.int32, (N_EXPERTS, tm), 0)
    chosen = jnp.zeros((N_EXPERTS, tm), F32)
    eidx, gates = [], []
    for _ in range(TOP_K):
        _, ei = _first_max(masked, io_e, N_EXPERTS)
        hit = io_e == ei
        eidx.append(ei)
        gates.append(jnp.sum(jnp.where(hit, s, 0.0), axis=0, keepdims=True))
        masked = jnp.where(hit, -jnp.inf, masked)
        chosen = jnp.where(hit, 1.0, chosen)
    wk = jnp.concatenate(gates, axis=0)
    w_ref[...] = wk / jnp.sum(wk, axis=0, keepdims=True) * ROUTED_SCALE
    eidx_ref[...] = jnp.concatenate(eidx, axis=0)
    earlier = (lax.broadcasted_iota(jnp.int32, (tm, tm), 0) < lax.broadcasted_iota(jnp.int32, (tm, tm), 1))
    pos = jnp.dot(chosen.astype(BF16), jnp.where(earlier, 1.0, 0.0).astype(BF16), preferred_element_type=F32)
    pos = pos + base_ref[...]
    ranks = [jnp.sum(jnp.where(io_e == eidx[k], pos, 0.0), axis=0, keepdims=True) for k in range(TOP_K)]
    rank_ref[...] = jnp.concatenate(ranks, axis=0).astype(jnp.int32)
    base_ref[...] = base_ref[...] + jnp.sum(chosen, axis=1, keepdims=True)
    cnt_ref[...] = base_ref[...]


def _route(logits_t, b_corr):
    E, T = logits_t.shape
    tm = ROUTE_TILE
    tokk = pl.BlockSpec((TOP_K, tm), lambda i: (0, i))
    return pl.pallas_call(
        _route_kernel,
        grid=(T // tm,),
        in_specs=[pl.BlockSpec((E, tm), lambda i: (0, i)), pl.BlockSpec((E, 1), lambda i: (0, 0))],
        out_specs=[tokk, tokk, tokk, pl.BlockSpec((E, 1), lambda i: (0, 0))],
        out_shape=[jax.ShapeDtypeStruct((TOP_K, T), jnp.int32), jax.ShapeDtypeStruct((TOP_K, T), F32),
                   jax.ShapeDtypeStruct((TOP_K, T), jnp.int32), jax.ShapeDtypeStruct((E, 1), F32)],
        scratch_shapes=[pltpu.VMEM((E, 1), F32)],
        compiler_params=_cparams("arbitrary"),
        name="route",
    )(logits_t, b_corr.reshape(E, 1))


def _slot_kernel(eidx_ref, rank_ref, pstart_ref, dest_ref):
    tm = eidx_ref.shape[1]
    io_e = lax.broadcasted_iota(jnp.int32, (N_EXPERTS, tm), 0)
    ei = eidx_ref[...]
    starts = [jnp.sum(jnp.where(io_e == ei[k:k + 1], pstart_ref[...], 0.0), axis=0, keepdims=True)
              for k in range(TOP_K)]
    dest_ref[...] = jnp.concatenate(starts, axis=0).astype(jnp.int32) + rank_ref[...]


def _slots(eidx, rank, pstart):
    K, T = eidx.shape
    tm = ROUTE_TILE
    tokk = pl.BlockSpec((K, tm), lambda i: (0, i))
    return pl.pallas_call(
        _slot_kernel,
        grid=(T // tm,),
        in_specs=[tokk, tokk, pl.BlockSpec((N_EXPERTS, 1), lambda i: (0, 0))],
        out_specs=tokk,
        out_shape=jax.ShapeDtypeStruct((K, T), jnp.int32),
        compiler_params=_cparams("arbitrary"),
        name="route_slots",
    )(eidx, rank, pstart)


def _slot_tokens(dest, tok, n_slots, n_tok):
    n_asg = dest.shape[0]
    n_sub = SC_CORES * SC_SUBCORES
    per = n_slots // n_sub
    chunk = SC_SCAN_CHUNK
    assert n_slots % n_sub == 0 and per % SC_LANES == 0 and n_asg % chunk == 0 and chunk % SC_LANES == 0
    mesh = plsc.VectorSubcoreMesh(core_axis_name="c", subcore_axis_name="s", num_cores=SC_CORES, num_subcores=SC_SUBCORES)

    def body(dest_hbm, tok_hbm, out_hbm, loc, dbuf, tbuf):
        s0 = (lax.axis_index("c") * SC_SUBCORES + lax.axis_index("s")) * per
        lanes = lax.iota(jnp.int32, SC_LANES)

        @pl.loop(0, per, step=SC_LANES)
        def _(i):
            loc[pl.ds(i, SC_LANES)] = lax.rem(s0 + i, n_tok - SC_LANES) + lanes

        @pl.loop(0, n_asg, step=chunk)
        def _(c):
            pltpu.sync_copy(dest_hbm.at[pl.ds(c, chunk)], dbuf)
            pltpu.sync_copy(tok_hbm.at[pl.ds(c, chunk)], tbuf)

            @pl.loop(0, chunk, step=SC_LANES)
            def _(i):
                d = dbuf[pl.ds(i, SC_LANES)] - s0
                mine = jnp.logical_and(d >= 0, d < per)
                plsc.store_scatter(loc, [d], tbuf[pl.ds(i, SC_LANES)], mask=mine)

        pltpu.sync_copy(loc, out_hbm.at[pl.ds(s0, per)])

    return pl.kernel(
        body, out_type=jax.ShapeDtypeStruct((n_slots,), jnp.int32), mesh=mesh,
        scratch_types=[pltpu.VMEM((per,), jnp.int32), pltpu.VMEM((chunk,), jnp.int32), pltpu.VMEM((chunk,), jnp.int32)],
        compiler_params=pltpu.CompilerParams(needs_layout_passes=False),
        name="slot_tokens",
    )(dest, tok)


def _moe_routed(h_flat, logits_t, b_corr, we_g, we_u, we_d, layer):
    T = h_flat.shape[0]
    n_tiles = T * TOP_K // MOE_TILE + N_EXPERTS
    eidx, gates, rank, counts = _route(logits_t, b_corr)
    counts = counts[:, 0].astype(jnp.int32)
    padded = (counts + MOE_TILE - 1) // MOE_TILE * MOE_TILE
    pends = jnp.cumsum(padded)
    tile_expert = jnp.minimum(jnp.searchsorted(pends, jnp.arange(n_tiles, dtype=jnp.int32) * MOE_TILE, side='right'),
                              N_EXPERTS - 1).astype(jnp.int32)
    n_used = (pends[-1] // MOE_TILE).astype(jnp.int32).reshape(1)
    dest = _slots(eidx, rank, (pends - padded).astype(F32).reshape(N_EXPERTS, 1))
    tok = jnp.broadcast_to(jnp.arange(T, dtype=jnp.int32)[None], (TOP_K, T))
    src_tok = _slot_tokens(dest.reshape(-1), tok.reshape(-1), n_tiles * MOE_TILE, T)
    xs = h_flat.at[src_tok].get(mode='promise_in_bounds')
    ys = _moe_grouped(tile_expert, n_used, xs, we_g, we_u, we_d, layer)
    return ys.at[dest].get(mode='promise_in_bounds'), gates.T


_GLA_QK_BLK, _GLA_V_BLK, _GLA_R_BLK = 0, 1, 2
_GLA_LR_BLK = (2 * GLA_KEY_WIDTH + 2 * GLA_WIDTH) // LANE


def _rope_tables(L):
    t = np.arange(L)
    lane = np.arange(GLA_KEY_WIDTH)
    d = lane % GLA_DK
    pos = np.where(d[None, :] < GLA_DK // 2, (t // GRID_W)[:, None], (t % GRID_W)[:, None]).astype(np.float32)
    quarter = GLA_DK // 4
    inv = ROPE_BASE ** (-jnp.arange(quarter, dtype=F32) / quarter)
    ang = jnp.asarray(pos) * inv[jnp.asarray(d % quarter)][None, :]
    sign = np.where(d % (2 * quarter) < quarter, -1.0, 1.0).astype(np.float32)
    return jnp.cos(ang), jnp.sin(ang) * sign[None, :]


def _rope_partner(x):
    lane = lax.broadcasted_iota(jnp.int32, x.shape, 1)
    quarter = GLA_DK // 4
    return jnp.where(lane % (2 * quarter) < quarter, pltpu.roll(x, GLA_KEY_WIDTH - quarter, 1), pltpu.roll(x, quarter, 1))


def _log_sigmoid(x):
    return jnp.minimum(x, 0.0) - jnp.log(1.0 + jnp.exp(-jnp.abs(x)))


def _gla_kernel(reverse, finalize, *refs):
    if finalize:
        (qk_ref, v_ref, lr_ref, cos_ref, sin_ref, wa_ref, ba_ref, s0_ref, of_ref, r_ref, g_ref,
         o_ref, sfin_ref, s_scr) = refs
    else:
        qk_ref, v_ref, lr_ref, cos_ref, sin_ref, wa_ref, ba_ref, s0_ref, o_ref, sfin_ref, s_scr = refs
    hi = lax.Precision.HIGHEST

    @pl.when(pl.program_id(1) == 0)
    def _():
        s_scr[...] = s0_ref[0]

    tg = qk_ref.shape[1]
    C = GLA_CHUNK
    KW, VW = GLA_KEY_WIDTH, GLA_WIDTH
    qk = qk_ref[0]
    cos, sin = cos_ref[...], sin_ref[...]
    q = qk[:, :KW] * GLA_DK ** -0.5
    k = qk[:, KW:]
    q = q * cos + _rope_partner(q) * sin
    k = k * cos + _rope_partner(k) * sin
    v = v_ref[0]
    logit = jnp.dot(lr_ref[0], wa_ref[...], preferred_element_type=F32, precision=hi) + ba_ref[...]
    la = _log_sigmoid(logit) / GLA_GATE_NORM

    ri = lax.broadcasted_iota(jnp.int32, (C, C), 0)
    ci = lax.broadcasted_iota(jnp.int32, (C, C), 1)
    tri = jnp.where((ci >= ri) if reverse else (ci <= ri), 1.0, 0.0)
    tri_h = jnp.concatenate([tri] * GLA_HEADS, axis=0)
    head_k = lax.broadcasted_iota(jnp.int32, (C, KW), 1) // GLA_DK
    head_v = lax.broadcasted_iota(jnp.int32, (C, VW), 1) // GLA_DV
    own_block = (lax.broadcasted_iota(jnp.int32, (KW, VW), 0) // GLA_DK
                 == lax.broadcasted_iota(jnp.int32, (KW, VW), 1) // GLA_DV)
    eye = lax.broadcasted_iota(jnp.int32, (KW, KW), 0) == lax.broadcasted_iota(jnp.int32, (KW, KW), 1)

    S = s_scr[...]
    nc = tg // C
    outs = [None] * nc
    for c in (range(nc - 1, -1, -1) if reverse else range(nc)):
        sl = slice(c * C, (c + 1) * C)
        b = jnp.dot(tri, la[sl], preferred_element_type=F32, precision=hi)
        qt = q[sl] * jnp.exp(b)
        kt = k[sl] * jnp.exp(-b)
        qs = jnp.concatenate([jnp.where(head_k == h, qt, 0.0) for h in range(GLA_HEADS)], axis=0).astype(BF16)
        att = lax.dot_general(qs, kt.astype(BF16), _NT, preferred_element_type=F32)
        att = jnp.where(tri_h > 0.0, att, 0.0)
        vb = v[sl].astype(BF16)
        oi = jnp.dot(att.astype(BF16), vb, preferred_element_type=F32)
        o_intra = jnp.zeros((C, VW), F32)
        for h in range(GLA_HEADS):
            o_intra = jnp.where(head_v == h, oi[h * C:(h + 1) * C], o_intra)
        o_inter = jnp.dot(qt.astype(BF16), S.astype(BF16), preferred_element_type=F32)
        outs[c] = o_intra + o_inter
        b_last = b[0:1] if reverse else b[C - 1:C]
        kdec = (k[sl] * jnp.exp(b_last - b)).astype(BF16)
        kv = lax.dot_general(kdec, vb, (((0,), (0,)), ((), ())), preferred_element_type=F32)
        decay_col = jnp.sum(jnp.where(eye, jnp.exp(b_last), 0.0), axis=1, keepdims=True)
        S = decay_col * S + jnp.where(own_block, kv, 0.0)
    s_scr[...] = S
    sfin_ref[0] = S
    o = jnp.concatenate(outs, axis=0)
    if finalize:
        o = of_ref[0] + o
        same_head = (lax.broadcasted_iota(jnp.int32, (VW, VW), 0) // GLA_DV
                     == lax.broadcasted_iota(jnp.int32, (VW, VW), 1) // GLA_DV)
        ms = jnp.dot(o * o, jnp.where(same_head, 1.0 / GLA_DV, 0.0), preferred_element_type=F32, precision=hi)
        r = r_ref[0]
        o = o * lax.rsqrt(ms + LN_EPS) * g_ref[...] * (r * jax.nn.sigmoid(r))
    o_ref[0] = o


def _gla_pass(u_rest, cos, sin, wa, ba, s0, reverse, fin=None):
    B, T, _ = u_rest.shape
    tg = min(GLA_TILE, T)
    n = T // tg
    ti = (lambda i: n - 1 - i) if reverse else (lambda i: i)
    KW, VW = GLA_KEY_WIDTH, GLA_WIDTH
    ublk = lambda w, j: pl.BlockSpec((1, tg, w), lambda b, i: (b, ti(i), j))
    full = lambda s: pl.BlockSpec(s, lambda b, i: (0,) * len(s))
    state = pl.BlockSpec((1, KW, VW), lambda b, i: (b, 0, 0))
    tab = pl.BlockSpec((tg, KW), lambda b, i: (ti(i), 0))
    in_specs = [ublk(2 * KW, _GLA_QK_BLK), ublk(VW, _GLA_V_BLK), ublk(LANE, _GLA_LR_BLK), tab, tab,
                full((LANE, KW)), full((1, KW)), state]
    args = [u_rest, u_rest, u_rest, cos, sin, wa, ba, s0]
    if fin is not None:
        in_specs += [ublk(VW, 0), ublk(VW, _GLA_R_BLK), full((1, VW))]
        args += [fin[0], u_rest, fin[1]]
    return pl.pallas_call(
        functools.partial(_gla_kernel, reverse, fin is not None),
        grid=(B, n),
        in_specs=in_specs,
        out_specs=[ublk(VW, 0), state],
        out_shape=[jax.ShapeDtypeStruct((B, T, VW), F32), jax.ShapeDtypeStruct((B, KW, VW), F32)],
        scratch_shapes=[pltpu.VMEM((KW, VW), F32)],
        compiler_params=_cparams("arbitrary", "arbitrary"),
        name="gla_bwd" if reverse else "gla_fwd",
    )(*args)


def _gla_bidir(u_rest, cos, sin, w_a2, b_a2, norm_g, s0_f, s0_b):
    def decay_w(d):
        return jnp.zeros((LANE, GLA_KEY_WIDTH), F32).at[d * GLA_RANK:(d + 1) * GLA_RANK].set(w_a2[d])

    o_f, s_f = _gla_pass(u_rest, cos, sin, decay_w(0), b_a2[0:1], s0_f, False)
    g = jnp.tile(norm_g, GLA_HEADS).reshape(1, GLA_WIDTH)
    o, s_b = _gla_pass(u_rest, cos, sin, decay_w(1), b_a2[1:2], s0_b, True, (o_f, g))
    return o, s_f, s_b


HY_LANES = LANE
HY_CH_TILE = 16
HY_CTX_N1 = 16


def _hy_consts(n1, N1):
    W = HY_LANES
    n1p = max(n1, 16)
    n1o = max(n1, 8)
    a = np.arange(N1)
    th1 = 2.0 * np.pi * ((a[:, None] * a[None, :]) % N1) / N1
    f1 = np.zeros((2 * N1, n1p)); f1[:N1, :n1] = np.cos(th1[:, :n1]); f1[N1:, :n1] = -np.sin(th1[:, :n1])
    f1_full = np.concatenate([np.cos(th1), -np.sin(th1)], axis=0)
    ginv = np.zeros((n1o, 2 * N1)); ginv[:n1, :N1] = np.cos(th1.T[:n1]); ginv[:n1, N1:] = -np.sin(th1.T[:n1])
    r = np.arange(W)
    tht = 2.0 * np.pi * ((a[:, None] * r[None, :]) % (N1 * W)) / (N1 * W)
    tr = np.tile(np.cos(tht), (1, HY_CH_TILE)); ti = np.tile(-np.sin(tht), (1, HY_CH_TILE))
    th2 = 2.0 * np.pi * ((r[:, None] * r[None, :]) % W) / W
    c2, s2 = np.cos(th2), -np.sin(th2)
    m2f = np.block([[c2, s2], [-s2, c2]])
    m2i = np.block([[c2, -s2], [s2, c2]])
    f = lambda m: jnp.asarray(m, F32)
    return dict(n1p=n1p, n1o=n1o, f1=f(f1), f1_full=f(f1_full), ginv=f(ginv), tr=f(tr), ti=f(ti), m2f=f(m2f), m2i=f(m2i))


def _short_conv_rows(u, w_ref, b_ref, n_rows):
    R, Wd = u.shape
    lane = lax.broadcasted_iota(jnp.int32, (R, Wd), 1) % HY_LANES
    row = lax.broadcasted_iota(jnp.int32, (R, Wd), 0)
    up = jnp.where(row == 0, 0.0, pltpu.roll(u, 1, 0))
    dn = jnp.where(row == n_rows - 1, 0.0, pltpu.roll(u, R - 1, 0))
    prev = jnp.where(lane == 0, pltpu.roll(up, Wd - (HY_LANES - 1), 1), pltpu.roll(u, 1, 1))
    nxt = jnp.where(lane == HY_LANES - 1, pltpu.roll(dn, HY_LANES - 1, 1), pltpu.roll(u, Wd - 1, 1))
    return b_ref[...] + prev * w_ref[0:1] + u * w_ref[1:2] + nxt * w_ref[2:3]


def _pad_rows(u, rows):
    return u if u.shape[0] == rows else jnp.concatenate([u, jnp.zeros((rows - u.shape[0], u.shape[1]), u.dtype)], axis=0)


def _hy_chunk_dft(z, f1, tr, ti, N1, prec):
    if prec is None:
        a = jnp.dot(f1.astype(BF16), z.astype(BF16), preferred_element_type=F32)
    else:
        a = jnp.dot(f1, z, preferred_element_type=F32, precision=prec)
    ar, ai = a[:N1], a[N1:]
    a_re, a_im = ar * tr - ai * ti, ar * ti + ai * tr
    W = HY_LANES
    return jnp.concatenate(
        [jnp.concatenate([a_re[:, c * W:(c + 1) * W], a_im[:, c * W:(c + 1) * W]], axis=1) for c in range(z.shape[1] // W)],
        axis=0)


def _hy_long_conv(z, kf, f1, ginv, tr, ti, m2f, m2i, N1):
    W = HY_LANES
    m = z.shape[1] // W
    a = _hy_chunk_dft(z, f1, tr, ti, N1, None).astype(BF16)
    x = jnp.dot(a, m2f.astype(BF16), preferred_element_type=F32)
    xr, xi = x[:, :W], x[:, W:]
    kr, ki = kf[:, :W], kf[:, W:]
    y = jnp.concatenate([xr * kr - xi * ki, xr * ki + xi * kr], axis=1).astype(BF16)
    p = jnp.dot(y, m2i.astype(BF16), preferred_element_type=F32)
    pr = jnp.concatenate([p[c * N1:(c + 1) * N1, :W] for c in range(m)], axis=1)
    pi = jnp.concatenate([p[c * N1:(c + 1) * N1, W:] for c in range(m)], axis=1)
    q = jnp.concatenate([pr * tr + pi * ti, pi * tr - pr * ti], axis=0).astype(BF16)
    return jnp.dot(ginv.astype(BF16), q, preferred_element_type=F32)


def _hy_mixer_kernel(n1, N1, n1p, v_ref, x1_ref, x2_ref, swv_ref, sbv_ref, swx1_ref, sbx1_ref, swx2_ref, sbx2_ref,
                     skip_ref, kf_ref, f1_ref, ginv_ref, tr_ref, ti_ref, m2f_ref, m2i_ref, o_ref):
    rows = max(n1, 8)
    cst = (f1_ref[...], ginv_ref[...], tr_ref[...], ti_ref[...], m2f_ref[...], m2i_ref[...], N1)
    z = _short_conv_rows(_pad_rows(v_ref[0], rows), swv_ref, sbv_ref, n1)
    gates = (_short_conv_rows(_pad_rows(x1_ref[0], rows), swx1_ref, sbx1_ref, n1),
             _short_conv_rows(_pad_rows(x2_ref[0], rows), swx2_ref, sbx2_ref, n1))
    for o in range(HY_ORDER):
        y = _hy_long_conv(_pad_rows(z, n1p), kf_ref[o, 0], *cst)
        z = gates[o] * (y + skip_ref[o:o + 1] * z)
    o_ref[0] = z[:n1]


def _hy_kspec_kernel(N1, scale, k_ref, f1_ref, tr_ref, ti_ref, m2f_ref, kf_ref):
    a = _hy_chunk_dft(k_ref[0], f1_ref[...], tr_ref[...], ti_ref[...], N1, lax.Precision.HIGHEST)
    kf_ref[0, 0] = jnp.dot(a, m2f_ref[...], preferred_element_type=F32, precision=lax.Precision.HIGHEST) * scale


def _hy_filter_spectrum(filt, N1, cst):
    Ls, n_ord, _, C = filt.shape
    W = HY_LANES
    N = N1 * W
    kern = jnp.concatenate([filt[:, :, 0], jnp.zeros((N - 2 * Ls + 1, n_ord, C), F32), filt[:0:-1, :, 1]], axis=0)
    kern = kern.reshape(N1, W, n_ord, C).transpose(2, 0, 3, 1).reshape(n_ord, N1, C * W)
    Wd = HY_CH_TILE * W
    nj = C // HY_CH_TILE
    full = lambda s: pl.BlockSpec(s, lambda o, j: (0,) * len(s))
    return pl.pallas_call(
        functools.partial(_hy_kspec_kernel, N1, 1.0 / N),
        grid=(n_ord, nj),
        in_specs=[pl.BlockSpec((1, N1, Wd), lambda o, j: (o, 0, j)), full((2 * N1, N1)), full((N1, Wd)), full((N1, Wd)),
                  full((2 * W, 2 * W))],
        out_specs=pl.BlockSpec((1, 1, HY_CH_TILE * N1, 2 * W), lambda o, j: (o, j, 0, 0)),
        out_shape=jax.ShapeDtypeStruct((n_ord, nj, HY_CH_TILE * N1, 2 * W), F32),
        compiler_params=_cparams("arbitrary", "arbitrary"),
        name="hyena_filter_spectrum",
    )(kern, cst["f1_full"], cst["tr"], cst["ti"], cst["m2f"])


def _hyena_mixer(hy_t, short_w, short_b, filt, skip, N1):
    B, n1, C3, W = hy_t.shape
    C = C3 // (HY_ORDER + 1)
    cst = _hy_consts(n1, N1)
    kf = _hy_filter_spectrum(filt, N1, cst)
    Wd = HY_CH_TILE * W
    nj = C // HY_CH_TILE
    hy2 = hy_t.reshape(B, n1, C3 * W)
    rep = lambda v: jnp.repeat(v, W, axis=-1)
    sw, sb = rep(short_w), rep(short_b).reshape(1, C3 * W)
    sk = rep(skip)
    chan = lambda part: pl.BlockSpec((1, n1, Wd), lambda j, b: (b, 0, part * nj + j))
    wsp = lambda part: pl.BlockSpec((HY_SHORT, Wd), lambda j, b: (0, part * nj + j))
    bsp = lambda part: pl.BlockSpec((1, Wd), lambda j, b: (0, part * nj + j))
    full = lambda a: pl.BlockSpec(a.shape, lambda j, b: (0,) * a.ndim)
    consts = [cst[k] for k in ("f1", "ginv", "tr", "ti", "m2f", "m2i")]
    out = pl.pallas_call(
        functools.partial(_hy_mixer_kernel, n1, N1, cst["n1p"]),
        grid=(nj, B),
        in_specs=[chan(0), chan(1), chan(2), wsp(0), bsp(0), wsp(1), bsp(1), wsp(2), bsp(2),
                  pl.BlockSpec((HY_ORDER, Wd), lambda j, b: (0, j)),
                  pl.BlockSpec((HY_ORDER, 1, HY_CH_TILE * N1, 2 * W), lambda j, b: (0, j, 0, 0))]
                 + [full(a) for a in consts],
        out_specs=pl.BlockSpec((1, n1, Wd), lambda j, b: (b, 0, j)),
        out_shape=jax.ShapeDtypeStruct((B, n1, C * W), F32),
        compiler_params=_cparams("arbitrary", "arbitrary"),
        name="hyena_mixer",
    )(hy2, hy2, hy2, sw, sb, sw, sb, sw, sb, sk, kf, *consts)
    return out.reshape(B, n1, C, W)


def _hyena_filters(L, w1, b1, f1, w2, b2, f2, w3, b3, decay):
    hi = lax.Precision.HIGHEST
    t = jnp.linspace(0.0, 1.0, L, dtype=F32)
    w = 2.0 * math.pi * jnp.arange(L, dtype=F32) / L
    bands = jnp.linspace(1e-4, HY_BANDS - 1, HY_BANDS, dtype=F32)
    ang = w[:, None] * bands[None]
    z = jnp.concatenate([t[:, None], jnp.cos(ang), -jnp.sin(ang)], -1)
    h = jnp.sin(f1 * (jnp.dot(z, w1, precision=hi) + b1))
    h = jnp.sin(f2 * (jnp.dot(h, w2, precision=hi) + b2))
    h = (jnp.dot(h, w3, precision=hi) + b3).reshape(L, HY_ORDER, 2, HY_CH)
    window = jnp.exp(-t[:, None, None, None] * jnp.abs(decay)[None])
    return h * window


def kernel(x, c, ctx, c_ctx, w_mod, b_mod, w_in, na_rpb, gla_w_a2, gla_b_a2, gla_norm_g, hy_short_w, hy_short_b, hy_w1, hy_b1, hy_f1, hy_w2, hy_b2, hy_f2, hy_w3, hy_b3, hy_decay, hy_skip, w_out, ln1_g, ln1_b, router_w, router_b, we_gate, we_up, we_down, ws_gate, ws_up, ws_down, ln2_g, ln2_b):
    B, L, D = x.shape
    C = ctx.shape[1]
    rope_cos, rope_sin = _rope_tables(L)
    ctx_cos, ctx_sin = jnp.ones((C, GLA_KEY_WIDTH), F32), jnp.zeros((C, GLA_KEY_WIDTH), F32)
    zero_state = jnp.zeros((B, GLA_KEY_WIDTH, GLA_WIDTH), F32)
    na_bias = _na_bias_table(na_rpb)

    n_mod = -(-(B + 1) // 8) * 8
    cs = jnp.zeros((n_mod, D), F32).at[:B].set(c).at[B].set(c_ctx)
    mod_all = _modulation(cs, w_mod, b_mod)

    xc = ctx
    for l in range(DEPTH):
        last = l == DEPTH - 1
        mods = mod_all[l].reshape(n_mod, 6, 1, D)
        lat = lambda j: mods[:B, j]
        cm = lambda j: mods[B:B + 1, j]
        n_tok = NA_COLS + GLA_COLS
        w_pad = jnp.pad(w_in[l][:, :n_tok], ((0, 0), (0, D_IN_PAD - n_tok))).astype(BF16)
        w_hy_t = w_in[l][:, n_tok:].T.astype(BF16)
        w_out_b = w_out[l].astype(BF16)
        lg1, lb1 = ln1_g[l].reshape(1, D), ln1_b[l].reshape(1, D)
        lg2, lb2 = ln2_g[l].reshape(1, D), ln2_b[l].reshape(1, D)

        u_na, u_rest, hy = _inproj(x, lat(1), lat(0), w_pad, w_hy_t, True)
        uc_na, uc_rest, hyc = _inproj(xc, cm(1), cm(0), w_pad, w_hy_t, False)

        na_lat = _na_attention(u_na, uc_na, na_bias, l)

        gla_c, s_ctx_f, s_ctx_b = _gla_bidir(uc_rest, ctx_cos, ctx_sin, gla_w_a2[l], gla_b_a2[l], gla_norm_g[l],
                                             zero_state, zero_state)
        gla_lat, _, _ = _gla_bidir(u_rest, rope_cos, rope_sin, gla_w_a2[l], gla_b_a2[l], gla_norm_g[l],
                                   s_ctx_f, s_ctx_b)

        filt_args = (hy_w1[l], hy_b1[l], hy_f1[l], hy_w2[l], hy_b2[l], hy_f2[l], hy_w3[l], hy_b3[l], hy_decay[l])
        hy_lat = _hyena_mixer(hy, hy_short_w[l], hy_short_b[l], _hyena_filters(L, *filt_args), hy_skip[l],
                              2 * L // HY_LANES)

        wr_t = router_w[l].T
        x, h_lat, logit_lat = _outproj(na_lat, gla_lat, hy_lat, x, lat(2), w_out_b, lg1, lb1, lat(4), lat(3), wr_t, True)
        if not last:
            na_c = _ctx_attention(uc_na)
            hy_c = _hyena_mixer(hyc, hy_short_w[l], hy_short_b[l], _hyena_filters(C, *filt_args), hy_skip[l],
                                HY_CTX_N1)
            xc, h_c, logit_c = _outproj(na_c, gla_c, hy_c, xc, cm(2), w_out_b, lg1, lb1, cm(4), cm(3), wr_t, False)

            h_flat = jnp.concatenate([h_lat.reshape(B * L, D), h_c.reshape(B * C, D)], axis=0)
            logit_t = jnp.concatenate([logit_lat, logit_c], axis=1)
        else:
            h_flat = h_lat.reshape(B * L, D)
            logit_t = logit_lat

        picked, gates = _moe_routed(h_flat, logit_t, router_b[l], we_gate, we_up, we_down, l)
        wsg, wsu, wsd = ws_gate[l].astype(BF16), ws_up[l].astype(BF16), ws_down[l].astype(BF16)
        x = _shared_ln2(x, h_lat, picked, gates, 0, lat(5), wsg, wsu, wsd, lg2, lb2, True)
        if not last:
            xc = _shared_ln2(xc, h_c, picked, gates, B * L, cm(5), wsg, wsu, wsd, lg2, lb2, False)
    return x
```

```python
import functools
import math

import numpy as np
import jax
import jax.numpy as jnp
from jax import lax
from jax.experimental import pallas as pl
from jax.experimental.pallas import tpu as pltpu
from jax.experimental.pallas import tpu_sc as plsc

F32 = jnp.float32
BF16 = jnp.bfloat16

D_MODEL = 1024
DEPTH = 4
GRID_W = 64
CTX_LEN = 256

NA_HEADS = 8
NA_HEAD_DIM = 64
NA_WIDTH = NA_HEADS * NA_HEAD_DIM
NA_WIN_ROWS = 8
NA_WIN_COLS = 16

GLA_HEADS = 4
GLA_DK = 32
GLA_DV = 64
GLA_KEY_WIDTH = GLA_HEADS * GLA_DK
GLA_WIDTH = GLA_HEADS * GLA_DV
GLA_RANK = 16
GLA_GATE_NORM = 16.0
GLA_CHUNK = 64

HY_CH = 256
HY_ORDER = 2
HY_SHORT = 3
HY_BANDS = 16
HY_EMB = 1 + 2 * HY_BANDS

MIX_WIDTH = NA_WIDTH + GLA_WIDTH + HY_CH
IN_SPLITS = (NA_WIDTH, NA_WIDTH, NA_WIDTH, GLA_KEY_WIDTH, GLA_KEY_WIDTH, GLA_WIDTH, GLA_WIDTH,
             2 * GLA_RANK, (HY_ORDER + 1) * HY_CH)
D_IN = sum(IN_SPLITS)
ROPE_BASE = 10000.0

N_EXPERTS = 128
TOP_K = 8
N_GROUPS = 8
TOPK_GROUPS = 4
EXPERT_HIDDEN = 256
ROUTED_SCALE = 2.5

DEEPNORM_ALPHA = (2 * DEPTH) ** 0.25
LN_EPS = 1e-6

LANE = 128
MXU_DIM = 256
VMEM_LIMIT = 48 * 1024 * 1024
SC_CORES = 2
SC_SUBCORES = 16
SC_LANES = 16
SC_SCAN_CHUNK = 16384

NA_COLS = 3 * NA_WIDTH
GLA_COLS = sum(IN_SPLITS[3:8])
HY_COLS = IN_SPLITS[8]
REST_COLS = -(-GLA_COLS // LANE) * LANE
D_IN_PAD = NA_COLS + REST_COLS
NA_QUAD = MXU_DIM // NA_HEAD_DIM
NEG_BIG = -1e30

TOK_TILE = 512
MOE_TILE = 512
NA_ROW_TILE = 8
NA_ROW_UNROLL = 4
ROUTE_TILE = 256
GLA_TILE = 512
GLA_BATCH = 2


def _cparams(*sem):
    return pltpu.CompilerParams(dimension_semantics=sem, vmem_limit_bytes=VMEM_LIMIT)


def _mod_kernel(c_ref, w_ref, b_ref, o_ref):
    c = c_ref[...]
    s = c * jax.nn.sigmoid(c)
    o_ref[0] = jnp.dot(s, w_ref[0], preferred_element_type=F32, precision=lax.Precision.HIGHEST) + b_ref[0]


def _modulation(cs, w_mod, b_mod):
    R = cs.shape[0]
    tn = 1536
    return pl.pallas_call(
        _mod_kernel,
        grid=(DEPTH, 6 * D_MODEL // tn),
        in_specs=[pl.BlockSpec((R, D_MODEL), lambda l, j: (0, 0)),
                  pl.BlockSpec((1, D_MODEL, tn), lambda l, j: (l, 0, j)),
                  pl.BlockSpec((1, 1, tn), lambda l, j: (l, 0, j))],
        out_specs=pl.BlockSpec((1, R, tn), lambda l, j: (l, 0, j)),
        out_shape=jax.ShapeDtypeStruct((DEPTH, R, 6 * D_MODEL), F32),
        compiler_params=_cparams("arbitrary", "arbitrary"),
        name="modulation",
    )(cs, w_mod, b_mod.reshape(DEPTH, 1, 6 * D_MODEL))


def _inproj_kernel(x_ref, sc_ref, sh_ref, w_ref, wh_ref, ona_ref, orest_ref, ohy_ref):
    xm = (x_ref[0] * (1.0 + sc_ref[0]) + sh_ref[0]).astype(BF16)
    step = 512
    for c0 in range(0, NA_COLS, step):
        ona_ref[0, :, c0:c0 + step] = jnp.dot(xm, w_ref[:, c0:c0 + step], preferred_element_type=F32).astype(BF16)
    for c0 in range(0, REST_COLS, step):
        c1 = min(c0 + step, REST_COLS)
        orest_ref[0, :, c0:c1] = jnp.dot(xm, w_ref[:, NA_COLS + c0:NA_COLS + c1], preferred_element_type=F32)
    hy = lax.dot_general(wh_ref[...], xm, _NT, preferred_element_type=F32)
    for j in range(ohy_ref.shape[1]):
        ohy_ref[0, j] = hy[:, j * LANE:(j + 1) * LANE]


def _inproj(x, sc, sh, w_pad, w_hy_t, per_batch_mod):
    B, T, D = x.shape
    tm = min(TOK_TILE, T)
    mod_idx = (lambda b, i: (b, 0, 0)) if per_batch_mod else (lambda b, i: (0, 0, 0))
    return pl.pallas_call(
        _inproj_kernel,
        grid=(B, T // tm),
        in_specs=[pl.BlockSpec((1, tm, D), lambda b, i: (b, i, 0)),
                  pl.BlockSpec((1, 1, D), mod_idx),
                  pl.BlockSpec((1, 1, D), mod_idx),
                  pl.BlockSpec((D, D_IN_PAD), lambda b, i: (0, 0)),
                  pl.BlockSpec((HY_COLS, D), lambda b, i: (0, 0))],
        out_specs=[pl.BlockSpec((1, tm, NA_COLS), lambda b, i: (b, i, 0)),
                   pl.BlockSpec((1, tm, REST_COLS), lambda b, i: (b, i, 0)),
                   pl.BlockSpec((1, tm // LANE, HY_COLS, LANE), lambda b, i: (b, i, 0, 0))],
        out_shape=[jax.ShapeDtypeStruct((B, T, NA_COLS), BF16),
                   jax.ShapeDtypeStruct((B, T, REST_COLS), F32),
                   jax.ShapeDtypeStruct((B, T // LANE, HY_COLS, LANE), F32)],
        compiler_params=_cparams("arbitrary", "arbitrary"),
        name="inproj",
    )(x, sc, sh, w_pad, w_hy_t)


def _stack_heads(q, n_rows):
    head = lax.broadcasted_iota(jnp.int32, (n_rows, MXU_DIM), 1) // NA_HEAD_DIM
    return jnp.concatenate([jnp.where(head == h, q, jnp.zeros_like(q)) for h in range(NA_QUAD)], axis=0)


def _unstack_heads(o, n_rows):
    head = lax.broadcasted_iota(jnp.int32, (n_rows, MXU_DIM), 1) // NA_HEAD_DIM
    out = jnp.zeros((n_rows, MXU_DIM), F32)
    for h in range(NA_QUAD):
        out = jnp.where(head == h, o[h * n_rows:(h + 1) * n_rows], out)
    return out


_NT = (((1,), (1,)), ((), ()))


def _na_kernel(q_ref, k_ref, v_ref, kc_ref, vc_ref, bias_ref, o_ref):
    rt = pl.program_id(2)
    scale = NA_HEAD_DIM ** -0.5
    kc = kc_ref[0]
    vc = vc_ref[0]
    n_loc = NA_WIN_ROWS * GRID_W

    def row(rl, carry):
        r = rt * NA_ROW_TILE + rl
        kr0 = jnp.clip(r - NA_WIN_ROWS // 2, 0, GRID_W - NA_WIN_ROWS)
        dr0 = kr0 - r + NA_WIN_ROWS - 1
        q = q_ref[0, pl.ds(pl.multiple_of(rl * GRID_W, GRID_W), GRID_W), :]
        qs = _stack_heads(q, GRID_W)
        k0 = pl.multiple_of(kr0 * GRID_W, GRID_W)
        ks = k_ref[0, pl.ds(k0, n_loc), :]
        vs = v_ref[0, pl.ds(k0, n_loc), :]
        s_loc = lax.dot_general(qs, ks, _NT, preferred_element_type=F32) * scale + bias_ref[0, dr0]
        s_ctx = lax.dot_general(qs, kc, _NT, preferred_element_type=F32) * scale
        m = jnp.maximum(jnp.max(s_loc, axis=-1, keepdims=True), jnp.max(s_ctx, axis=-1, keepdims=True))
        p_loc = jnp.exp(s_loc - m)
        p_ctx = jnp.exp(s_ctx - m)
        den = jnp.sum(p_loc, axis=-1, keepdims=True) + jnp.sum(p_ctx, axis=-1, keepdims=True)
        o = (jnp.dot(p_loc.astype(BF16), vs, preferred_element_type=F32)
             + jnp.dot(p_ctx.astype(BF16), vc, preferred_element_type=F32)) / den
        o_ref[0, pl.ds(pl.multiple_of(rl * GRID_W, GRID_W), GRID_W), :] = _unstack_heads(o, GRID_W)
        return carry

    lax.fori_loop(0, NA_ROW_TILE, row, 0, unroll=NA_ROW_UNROLL)


def _na_bias_table(rpb):
    n_lyr = rpb.shape[0]
    c = np.arange(GRID_W)
    kc0 = np.clip(c - NA_WIN_COLS // 2, 0, GRID_W - NA_WIN_COLS)
    kc = np.arange(GRID_W)
    valid = (kc[None, :] >= kc0[:, None]) & (kc[None, :] < kc0[:, None] + NA_WIN_COLS)
    dc = kc[None, :] - c[:, None] + NA_WIN_COLS - 1
    onehot = (np.arange(2 * NA_WIN_COLS - 1)[:, None, None] == dc[None]) & valid[None]
    toep = jnp.einsum('lhrd,dck->lhrck', rpb, jnp.asarray(onehot, F32), precision=lax.Precision.HIGHEST)
    toep = jnp.where(jnp.asarray(valid)[None, None, None], toep, NEG_BIG)
    tab = jnp.stack([toep[:, :, d:d + NA_WIN_ROWS] for d in range(NA_WIN_ROWS)], axis=2)
    tab = tab.transpose(0, 1, 2, 4, 3, 5).reshape(n_lyr, NA_HEADS // NA_QUAD, NA_QUAD, NA_WIN_ROWS, GRID_W,
                                                  NA_WIN_ROWS * GRID_W)
    return tab.transpose(0, 1, 3, 2, 4, 5).reshape(n_lyr, NA_HEADS // NA_QUAD, NA_WIN_ROWS, NA_QUAD * GRID_W,
                                                   NA_WIN_ROWS * GRID_W)


def _na_attention(u_na, uc_na, bias_tab, layer):
    B, L, _ = u_na.shape
    C = uc_na.shape[1]
    nq = NA_WIDTH // MXU_DIM
    tq = NA_ROW_TILE * GRID_W
    return pl.pallas_call(
        _na_kernel,
        grid=(B, nq, L // tq),
        in_specs=[pl.BlockSpec((1, tq, MXU_DIM), lambda b, j, i: (b, i, j)),
                  pl.BlockSpec((1, L, MXU_DIM), lambda b, j, i: (b, 0, nq + j)),
                  pl.BlockSpec((1, L, MXU_DIM), lambda b, j, i: (b, 0, 2 * nq + j)),
                  pl.BlockSpec((1, C, MXU_DIM), lambda b, j, i: (b, 0, nq + j)),
                  pl.BlockSpec((1, C, MXU_DIM), lambda b, j, i: (b, 0, 2 * nq + j)),
                  pl.BlockSpec((None, 1, NA_WIN_ROWS, NA_QUAD * GRID_W, NA_WIN_ROWS * GRID_W),
                               lambda b, j, i: (layer, j, 0, 0, 0))],
        out_specs=pl.BlockSpec((1, tq, MXU_DIM), lambda b, j, i: (b, i, j)),
        out_shape=jax.ShapeDtypeStruct((B, L, NA_WIDTH), F32),
        compiler_params=_cparams("arbitrary", "arbitrary", "arbitrary"),
        name="na_attention",
    )(u_na, u_na, u_na, uc_na, uc_na, bias_tab)


def _ctx_attn_kernel(q_ref, k_ref, v_ref, o_ref):
    C = q_ref.shape[1]
    qs = _stack_heads(q_ref[0], C)
    s = lax.dot_general(qs, k_ref[0], _NT, preferred_element_type=F32) * NA_HEAD_DIM ** -0.5
    p = jnp.exp(s - jnp.max(s, axis=-1, keepdims=True))
    den = jnp.sum(p, axis=-1, keepdims=True)
    o = jnp.dot(p.astype(BF16), v_ref[0], preferred_element_type=F32) / den
    o_ref[0] = _unstack_heads(o, C)


def _ctx_attention(uc_na):
    B, C, _ = uc_na.shape
    nq = NA_WIDTH // MXU_DIM
    return pl.pallas_call(
        _ctx_attn_kernel,
        grid=(B, nq),
        in_specs=[pl.BlockSpec((1, C, MXU_DIM), lambda b, j: (b, 0, j)),
                  pl.BlockSpec((1, C, MXU_DIM), lambda b, j: (b, 0, nq + j)),
                  pl.BlockSpec((1, C, MXU_DIM), lambda b, j: (b, 0, 2 * nq + j))],
        out_specs=pl.BlockSpec((1, C, MXU_DIM), lambda b, j: (b, 0, j)),
        out_shape=jax.ShapeDtypeStruct((B, C, NA_WIDTH), F32),
        compiler_params=_cparams("arbitrary", "arbitrary"),
        name="ctx_attention",
    )(uc_na, uc_na, uc_na)


def _layer_norm_rows(y, g, b):
    mu = jnp.mean(y, axis=-1, keepdims=True)
    d = y - mu
    var = jnp.mean(d * d, axis=-1, keepdims=True)
    return d * lax.rsqrt(var + LN_EPS) * g + b


def _outproj_kernel(na_ref, gla_ref, hy_ref, x_ref, g1_ref, w_ref, lg_ref, lb_ref, sc_ref, sh_ref, wr_ref,
                    xo_ref, h_ref, logit_ref):
    w_hy = w_ref[NA_WIDTH + GLA_WIDTH:, :]
    hy_mix = jnp.concatenate(
        [lax.dot_general(hy_ref[0, j].astype(BF16), w_hy, (((0,), (0,)), ((), ())), preferred_element_type=F32)
         for j in range(hy_ref.shape[1])], axis=0)
    mix = (jnp.dot(na_ref[0].astype(BF16), w_ref[0:NA_WIDTH, :], preferred_element_type=F32)
           + jnp.dot(gla_ref[0].astype(BF16), w_ref[NA_WIDTH:NA_WIDTH + GLA_WIDTH, :], preferred_element_type=F32)
           + hy_mix)
    xn = _layer_norm_rows(DEEPNORM_ALPHA * x_ref[0] + g1_ref[0] * mix, lg_ref[...], lb_ref[...])
    xo_ref[0] = xn
    h = xn * (1.0 + sc_ref[0]) + sh_ref[0]
    h_ref[0] = h.astype(BF16)
    logit_ref[...] = _dot_split(wr_ref[...], h, dims=_NT)


def _outproj(na, gla, hy, x, g1, w_out, ln_g, ln_b, sc2, sh2, w_router, per_batch_mod):
    B, T, D = x.shape
    tm = min(TOK_TILE, T)
    mod_idx = (lambda b, i: (b, 0, 0)) if per_batch_mod else (lambda b, i: (0, 0, 0))
    tok = lambda w: pl.BlockSpec((1, tm, w), lambda b, i: (b, i, 0))
    full = lambda s: pl.BlockSpec(s, lambda b, i: (0,) * len(s))
    mod = pl.BlockSpec((1, 1, D), mod_idx)
    return pl.pallas_call(
        _outproj_kernel,
        grid=(B, T // tm),
        in_specs=[tok(NA_WIDTH), tok(GLA_WIDTH),
                  pl.BlockSpec((1, tm // LANE, HY_CH, LANE), lambda b, i: (b, i, 0, 0)), tok(D), mod, full((MIX_WIDTH, D)),
                  full((1, D)), full((1, D)), mod, mod, full((N_EXPERTS, D))],
        out_specs=[tok(D), tok(D), pl.BlockSpec((N_EXPERTS, tm), lambda b, i: (0, b * (T // tm) + i))],
        out_shape=[jax.ShapeDtypeStruct((B, T, D), F32), jax.ShapeDtypeStruct((B, T, D), BF16),
                   jax.ShapeDtypeStruct((N_EXPERTS, B * T), F32)],
        compiler_params=_cparams("arbitrary", "arbitrary"),
        name="outproj_ln1",
    )(na, gla, hy, x, g1, w_out, ln_g, ln_b, sc2, sh2, w_router)


def _moe_kernel(te_ref, nt_ref, xs_ref, wg_ref, wu_ref, wd_ref, ys_ref, wg_s, wu_s, wd_s):
    i = pl.program_id(0)

    @pl.when(jnp.logical_or(i == 0, te_ref[i] != te_ref[jnp.maximum(i - 1, 0)]))
    def _():
        wg_s[...] = wg_ref[0].astype(BF16)
        wu_s[...] = wu_ref[0].astype(BF16)
        wd_s[...] = wd_ref[0].astype(BF16)

    @pl.when(i < nt_ref[0])
    def _():
        xs = xs_ref[...]
        g = jnp.dot(xs, wg_s[...], preferred_element_type=F32)
        u = jnp.dot(xs, wu_s[...], preferred_element_type=F32)
        a = (g * jax.nn.sigmoid(g) * u).astype(BF16)
        ys_ref[...] = jnp.dot(a, wd_s[...], preferred_element_type=F32).astype(ys_ref.dtype)

    @pl.when(i >= nt_ref[0])
    def _():
        ys_ref[...] = jnp.zeros_like(ys_ref)


def _moe_grouped(tile_expert, n_used, xs, we_g, we_u, we_d, layer):
    Mp, D = xs.shape
    n_tiles = Mp // MOE_TILE
    H = EXPERT_HIDDEN
    grid_spec = pltpu.PrefetchScalarGridSpec(
        num_scalar_prefetch=2,
        grid=(n_tiles,),
        in_specs=[pl.BlockSpec((MOE_TILE, D), lambda i, te, nt: (i, 0)),
                  pl.BlockSpec((None, 1, D, H), lambda i, te, nt: (layer, te[i], 0, 0)),
                  pl.BlockSpec((None, 1, D, H), lambda i, te, nt: (layer, te[i], 0, 0)),
                  pl.BlockSpec((None, 1, H, D), lambda i, te, nt: (layer, te[i], 0, 0))],
        out_specs=pl.BlockSpec((MOE_TILE, D), lambda i, te, nt: (i, 0)),
        scratch_shapes=[pltpu.VMEM((D, H), BF16), pltpu.VMEM((D, H), BF16), pltpu.VMEM((H, D), BF16)],
    )
    return pl.pallas_call(
        _moe_kernel,
        grid_spec=grid_spec,
        out_shape=jax.ShapeDtypeStruct((Mp, D), BF16),
        compiler_params=_cparams("arbitrary"),
        name="moe_experts",
    )(tile_expert, n_used, xs, we_g, we_u, we_d)


def _shared_kernel(x_ref, h_ref, pk_ref, gt_ref, g2_ref, wg_ref, wu_ref, wd_ref, lg_ref, lb_ref, o_ref):
    h = h_ref[0]
    g = jnp.dot(h, wg_ref[...], preferred_element_type=F32)
    u = jnp.dot(h, wu_ref[...], preferred_element_type=F32)
    a = (g * jax.nn.sigmoid(g) * u).astype(BF16)
    ff = jnp.dot(a, wd_ref[...], preferred_element_type=F32)
    gates = gt_ref[...]
    for k in range(TOP_K):
        ff = ff + pk_ref[k].astype(F32) * gates[:, k:k + 1]
    o_ref[0] = _layer_norm_rows(DEEPNORM_ALPHA * x_ref[0] + g2_ref[0] * ff, lg_ref[...], lb_ref[...])


def _shared_ln2(x, h, picked, gates, tok_off, g2, ws_g, ws_u, ws_d, ln_g, ln_b, per_batch_mod):
    B, T, D = x.shape
    tm = min(TOK_TILE, T)
    mod_idx = (lambda b, i: (b, 0, 0)) if per_batch_mod else (lambda b, i: (0, 0, 0))
    tok = pl.BlockSpec((1, tm, D), lambda b, i: (b, i, 0))
    full = lambda s: pl.BlockSpec(s, lambda b, i: (0,) * len(s))
    flat = lambda b, i: tok_off // tm + b * (T // tm) + i
    return pl.pallas_call(
        _shared_kernel,
        grid=(B, T // tm),
        in_specs=[tok, tok, pl.BlockSpec((TOP_K, tm, D), lambda b, i: (0, flat(b, i), 0)),
                  pl.BlockSpec((tm, TOP_K), lambda b, i: (flat(b, i), 0)),
                  pl.BlockSpec((1, 1, D), mod_idx), full((D, EXPERT_HIDDEN)), full((D, EXPERT_HIDDEN)),
                  full((EXPERT_HIDDEN, D)), full((1, D)), full((1, D))],
        out_specs=tok,
        out_shape=jax.ShapeDtypeStruct((B, T, D), F32),
        compiler_params=_cparams("arbitrary", "arbitrary"),
        name="shared_ln2",
    )(x, h, picked, gates, g2, ws_g, ws_u, ws_d, ln_g, ln_b)


def _first_max(vals, iota, n):
    m = jnp.max(vals, axis=0, keepdims=True)
    idx = jnp.min(jnp.where(vals == m, iota, n), axis=0, keepdims=True)
    return m, idx


def _route_kernel(lt_ref, b_ref, eidx_ref, w_ref, rank_ref, cnt_ref, base_ref):
    @pl.when(pl.program_id(0) == 0)
    def _():
        base_ref[...] = jnp.zeros_like(base_ref)

    tm = lt_ref.shape[1]
    per = N_EXPERTS // N_GROUPS
    s = jax.nn.sigmoid(lt_ref[...])
    sel = s + b_ref[...]
    io_g = lax.broadcasted_iota(jnp.int32, (per, tm), 0)
    scores = []
    for g in range(N_GROUPS):
        blk = sel[g * per:(g + 1) * per]
        m1, i1 = _first_max(blk, io_g, per)
        m2 = jnp.max(jnp.where(io_g == i1, -jnp.inf, blk), axis=0, keepdims=True)
        scores.append(m1 + m2)
    cur = jnp.concatenate(scores, axis=0)
    io_8 = lax.broadcasted_iota(jnp.int32, (N_GROUPS, tm), 0)
    gmask = jnp.zeros((N_GROUPS, tm), F32)
    for _ in range(TOPK_GROUPS):
        _, gi = _first_max(cur, io_8, N_GROUPS)
        hit = io_8 == gi
        gmask = jnp.where(hit, 1.0, gmask)
        cur = jnp.where(hit, -jnp.inf, cur)
    masked = jnp.concatenate(
        [jnp.where(gmask[g:g + 1] > 0.0, sel[g * per:(g + 1) * per], -jnp.inf) for g in range(N_GROUPS)], axis=0)
    io_e = lax.broadcasted_iota(jnp.int32, (N_EXPERTS, tm), 0)
    chosen = jnp.zeros((N_EXPERTS, tm), F32)
    eidx, gates = [], []
    for _ in range(TOP_K):
        _, ei = _first_max(masked, io_e, N_EXPERTS)
        hit = io_e == ei
        eidx.append(ei)
        gates.append(jnp.sum(jnp.where(hit, s, 0.0), axis=0, keepdims=True))
        masked = jnp.where(hit, -jnp.inf, masked)
        chosen = jnp.where(hit, 1.0, chosen)
    wk = jnp.concatenate(gates, axis=0)
    w_ref[...] = wk / jnp.sum(wk, axis=0, keepdims=True) * ROUTED_SCALE
    eidx_ref[...] = jnp.concatenate(eidx, axis=0)
    earlier = (lax.broadcasted_iota(jnp.int32, (tm, tm), 0) < lax.broadcasted_iota(jnp.int32, (tm, tm), 1))
    pos = jnp.dot(chosen.astype(BF16), jnp.where(earlier, 1.0, 0.0).astype(BF16), preferred_element_type=F32)
    pos = pos + base_ref[...]
    ranks = [jnp.sum(jnp.where(io_e == eidx[k], pos, 0.0), axis=0, keepdims=True) for k in range(TOP_K)]
    rank_ref[...] = jnp.concatenate(ranks, axis=0).astype(jnp.int32)
    base_ref[...] = base_ref[...] + jnp.sum(chosen, axis=1, keepdims=True)
    cnt_ref[...] = base_ref[...]


def _route(logits_t, b_corr):
    E, T = logits_t.shape
    tm = ROUTE_TILE
    tokk = pl.BlockSpec((TOP_K, tm), lambda i: (0, i))
    return pl.pallas_call(
        _route_kernel,
        grid=(T // tm,),
        in_specs=[pl.BlockSpec((E, tm), lambda i: (0, i)), pl.BlockSpec((E, 1), lambda i: (0, 0))],
        out_specs=[tokk, tokk, tokk, pl.BlockSpec((E, 1), lambda i: (0, 0))],
        out_shape=[jax.ShapeDtypeStruct((TOP_K, T), jnp.int32), jax.ShapeDtypeStruct((TOP_K, T), F32),
                   jax.ShapeDtypeStruct((TOP_K, T), jnp.int32), jax.ShapeDtypeStruct((E, 1), F32)],
        scratch_shapes=[pltpu.VMEM((E, 1), F32)],
        compiler_params=_cparams("arbitrary"),
        name="route",
    )(logits_t, b_corr.reshape(E, 1))


def _slot_kernel(eidx_ref, rank_ref, pstart_ref, dest_ref):
    tm = eidx_ref.shape[1]
    io_e = lax.broadcasted_iota(jnp.int32, (N_EXPERTS, tm), 0)
    ei = eidx_ref[...]
    starts = [jnp.sum(jnp.where(io_e == ei[k:k + 1], pstart_ref[...], 0.0), axis=0, keepdims=True)
              for k in range(TOP_K)]
    dest_ref[...] = jnp.concatenate(starts, axis=0).astype(jnp.int32) + rank_ref[...]


def _slots(eidx, rank, pstart):
    K, T = eidx.shape
    tm = ROUTE_TILE
    tokk = pl.BlockSpec((K, tm), lambda i: (0, i))
    return pl.pallas_call(
        _slot_kernel,
        grid=(T // tm,),
        in_specs=[tokk, tokk, pl.BlockSpec((N_EXPERTS, 1), lambda i: (0, 0))],
        out_specs=tokk,
        out_shape=jax.ShapeDtypeStruct((K, T), jnp.int32),
        compiler_params=_cparams("arbitrary"),
        name="route_slots",
    )(eidx, rank, pstart)


def _slot_tokens(dest, tok, n_slots, n_tok):
    n_asg = dest.shape[0]
    n_sub = SC_CORES * SC_SUBCORES
    per = n_slots // n_sub
    chunk = SC_SCAN_CHUNK
    assert n_slots % n_sub == 0 and per % SC_LANES == 0 and n_asg % chunk == 0 and chunk % SC_LANES == 0
    mesh = plsc.VectorSubcoreMesh(core_axis_name="c", subcore_axis_name="s", num_cores=SC_CORES, num_subcores=SC_SUBCORES)

    def body(dest_hbm, tok_hbm, out_hbm, loc, dbuf, tbuf):
        s0 = (lax.axis_index("c") * SC_SUBCORES + lax.axis_index("s")) * per
        lanes = lax.iota(jnp.int32, SC_LANES)

        @pl.loop(0, per, step=SC_LANES)
        def _(i):
            loc[pl.ds(i, SC_LANES)] = lax.rem(s0 + i, n_tok - SC_LANES) + lanes

        @pl.loop(0, n_asg, step=chunk)
        def _(c):
            pltpu.sync_copy(dest_hbm.at[pl.ds(c, chunk)], dbuf)
            pltpu.sync_copy(tok_hbm.at[pl.ds(c, chunk)], tbuf)

            @pl.loop(0, chunk, step=SC_LANES)
            def _(i):
                d = dbuf[pl.ds(i, SC_LANES)] - s0
                mine = jnp.logical_and(d >= 0, d < per)
                plsc.store_scatter(loc, [d], tbuf[pl.ds(i, SC_LANES)], mask=mine)

        pltpu.sync_copy(loc, out_hbm.at[pl.ds(s0, per)])

    return pl.kernel(
        body, out_type=jax.ShapeDtypeStruct((n_slots,), jnp.int32), mesh=mesh,
        scratch_types=[pltpu.VMEM((per,), jnp.int32), pltpu.VMEM((chunk,), jnp.int32), pltpu.VMEM((chunk,), jnp.int32)],
        compiler_params=pltpu.CompilerParams(needs_layout_passes=False),
        name="slot_tokens",
    )(dest, tok)


def _moe_routed(h_flat, logits_t, b_corr, we_g, we_u, we_d, layer):
    T = h_flat.shape[0]
    n_tiles = T * TOP_K // MOE_TILE + N_EXPERTS
    eidx, gates, rank, counts = _route(logits_t, b_corr)
    counts = counts[:, 0].astype(jnp.int32)
    padded = (counts + MOE_TILE - 1) // MOE_TILE * MOE_TILE
    pends = jnp.cumsum(padded)
    tile_start = jnp.arange(n_tiles, dtype=jnp.int32) * MOE_TILE
    tile_expert = jnp.minimum(jnp.sum((pends[None, :] <= tile_start[:, None]).astype(jnp.int32), axis=1), N_EXPERTS - 1)
    n_used = (pends[-1] // MOE_TILE).astype(jnp.int32).reshape(1)
    dest = _slots(eidx, rank, (pends - padded).astype(F32).reshape(N_EXPERTS, 1))
    tok = jnp.broadcast_to(jnp.arange(T, dtype=jnp.int32)[None], (TOP_K, T))
    src_tok = _slot_tokens(dest.reshape(-1), tok.reshape(-1), n_tiles * MOE_TILE, T)
    xs = h_flat.at[src_tok].get(mode='promise_in_bounds')
    ys = _moe_grouped(tile_expert, n_used, xs, we_g, we_u, we_d, layer)
    return ys.at[dest].get(mode='promise_in_bounds'), gates.T


_GLA_QK_BLK, _GLA_V_BLK, _GLA_R_BLK = 0, 1, 2
_GLA_LR_BLK = (2 * GLA_KEY_WIDTH + 2 * GLA_WIDTH) // LANE


def _rope_tables(L):
    t = np.arange(L)
    lane = np.arange(GLA_KEY_WIDTH)
    d = lane % GLA_DK
    pos = np.where(d[None, :] < GLA_DK // 2, (t // GRID_W)[:, None], (t % GRID_W)[:, None]).astype(np.float32)
    quarter = GLA_DK // 4
    inv = ROPE_BASE ** (-jnp.arange(quarter, dtype=F32) / quarter)
    ang = jnp.asarray(pos) * inv[jnp.asarray(d % quarter)][None, :]
    sign = np.where(d % (2 * quarter) < quarter, -1.0, 1.0).astype(np.float32)
    return jnp.cos(ang), jnp.sin(ang) * sign[None, :]


def _rope_partner(x):
    lane = lax.broadcasted_iota(jnp.int32, x.shape, 1)
    quarter = GLA_DK // 4
    return jnp.where(lane % (2 * quarter) < quarter, pltpu.roll(x, GLA_KEY_WIDTH - quarter, 1), pltpu.roll(x, quarter, 1))


_NN = (((1,), (0,)), ((), ()))


def _dot_split(a, b, rhs_exact=False, dims=_NN):
    dot = lambda x, y: lax.dot_general(x, y, dims, preferred_element_type=F32)
    a_hi = a.astype(BF16)
    a_lo = (a - a_hi.astype(F32)).astype(BF16)
    b_hi = b.astype(BF16)
    out = dot(a_hi, b_hi) + dot(a_lo, b_hi)
    if not rhs_exact:
        b_lo = (b - b_hi.astype(F32)).astype(BF16)
        out = out + dot(a_hi, b_lo)
    return out


def _log_sigmoid(x):
    return jnp.minimum(x, 0.0) - jnp.log(1.0 + jnp.exp(-jnp.abs(x)))


def _gla_kernel(reverse, finalize, *refs):
    if finalize:
        (qk_ref, v_ref, lr_ref, cos_ref, sin_ref, wa_ref, ba_ref, s0_ref, of_ref, r_ref, g_ref,
         o_ref, sfin_ref, s_scr) = refs
    else:
        qk_ref, v_ref, lr_ref, cos_ref, sin_ref, wa_ref, ba_ref, s0_ref, o_ref, sfin_ref, s_scr = refs
    hi = lax.Precision.HIGHEST

    @pl.when(pl.program_id(1) == 0)
    def _():
        s_scr[...] = s0_ref[...]

    tg = qk_ref.shape[1]
    C = GLA_CHUNK
    KW, VW = GLA_KEY_WIDTH, GLA_WIDTH
    cos, sin = cos_ref[...], sin_ref[...]

    ri = lax.broadcasted_iota(jnp.int32, (C, C), 0)
    ci = lax.broadcasted_iota(jnp.int32, (C, C), 1)
    tri = jnp.where((ci >= ri) if reverse else (ci <= ri), 1.0, 0.0)
    tri_h = jnp.concatenate([tri] * GLA_HEADS, axis=0)
    head_k = lax.broadcasted_iota(jnp.int32, (C, KW), 1) // GLA_DK
    head_v = lax.broadcasted_iota(jnp.int32, (C, VW), 1) // GLA_DV
    own_block = (lax.broadcasted_iota(jnp.int32, (KW, VW), 0) // GLA_DK
                 == lax.broadcasted_iota(jnp.int32, (KW, VW), 1) // GLA_DV)
    eye = lax.broadcasted_iota(jnp.int32, (KW, KW), 0) == lax.broadcasted_iota(jnp.int32, (KW, KW), 1)
    for bb in range(qk_ref.shape[0]):
        _gla_sample(reverse, finalize, bb, refs, cos, sin, tri, tri_h, head_k, head_v, own_block, eye)


def _gla_sample(reverse, finalize, bb, refs, cos, sin, tri, tri_h, head_k, head_v, own_block, eye):
    if finalize:
        (qk_ref, v_ref, lr_ref, _, _, wa_ref, ba_ref, _, of_ref, r_ref, g_ref, o_ref, sfin_ref, s_scr) = refs
    else:
        qk_ref, v_ref, lr_ref, _, _, wa_ref, ba_ref, _, o_ref, sfin_ref, s_scr = refs
    hi = lax.Precision.HIGHEST
    tg = qk_ref.shape[1]
    C = GLA_CHUNK
    KW, VW = GLA_KEY_WIDTH, GLA_WIDTH
    qk = qk_ref[bb]
    q = qk[:, :KW] * GLA_DK ** -0.5
    k = qk[:, KW:]
    q = q * cos + _rope_partner(q) * sin
    k = k * cos + _rope_partner(k) * sin
    v = v_ref[bb]
    logit = _dot_split(lr_ref[bb], wa_ref[...]) + ba_ref[...]
    la = _log_sigmoid(logit) / GLA_GATE_NORM
    pos = lax.broadcasted_iota(jnp.int32, (tg, KW), 0) % C
    cum = la
    step = 1
    while step < C:
        if reverse:
            cum = cum + jnp.where(pos < C - step, pltpu.roll(cum, tg - step, 0), 0.0)
        else:
            cum = cum + jnp.where(pos >= step, pltpu.roll(cum, step, 0), 0.0)
        step *= 2

    S = s_scr[bb]
    nc = tg // C
    outs = [None] * nc
    for c in (range(nc - 1, -1, -1) if reverse else range(nc)):
        sl = slice(c * C, (c + 1) * C)
        b = cum[sl]
        qt = q[sl] * jnp.exp(b)
        kt = k[sl] * jnp.exp(-b)
        qs = jnp.concatenate([jnp.where(head_k == h, qt, 0.0) for h in range(GLA_HEADS)], axis=0).astype(BF16)
        att = lax.dot_general(qs, kt.astype(BF16), _NT, preferred_element_type=F32)
        att = jnp.where(tri_h > 0.0, att, 0.0)
        vb = v[sl].astype(BF16)
        oi = jnp.dot(att.astype(BF16), vb, preferred_element_type=F32)
        o_intra = jnp.zeros((C, VW), F32)
        for h in range(GLA_HEADS):
            o_intra = jnp.where(head_v == h, oi[h * C:(h + 1) * C], o_intra)
        o_inter = jnp.dot(qt.astype(BF16), S.astype(BF16), preferred_element_type=F32)
        outs[c] = o_intra + o_inter
        b_last = b[0:1] if reverse else b[C - 1:C]
        kdec = (k[sl] * jnp.exp(b_last - b)).astype(BF16)
        kv = lax.dot_general(kdec, vb, (((0,), (0,)), ((), ())), preferred_element_type=F32)
        decay_col = jnp.sum(jnp.where(eye, jnp.exp(b_last), 0.0), axis=1, keepdims=True)
        S = decay_col * S + jnp.where(own_block, kv, 0.0)
    s_scr[bb] = S
    sfin_ref[bb] = S
    o = jnp.concatenate(outs, axis=0)
    if finalize:
        o = of_ref[bb] + o
        same_head = (lax.broadcasted_iota(jnp.int32, (VW, VW), 0) // GLA_DV
                     == lax.broadcasted_iota(jnp.int32, (VW, VW), 1) // GLA_DV)
        ms = _dot_split(o * o, jnp.where(same_head, 1.0 / GLA_DV, 0.0), rhs_exact=True)
        r = r_ref[bb]
        o = o * lax.rsqrt(ms + LN_EPS) * g_ref[...] * (r * jax.nn.sigmoid(r))
    o_ref[bb] = o


def _gla_pass(u_rest, cos, sin, wa, ba, s0, reverse, fin=None):
    B, T, _ = u_rest.shape
    tg = min(GLA_TILE, T)
    n = T // tg
    ti = (lambda i: n - 1 - i) if reverse else (lambda i: i)
    KW, VW = GLA_KEY_WIDTH, GLA_WIDTH
    gb = GLA_BATCH if B % GLA_BATCH == 0 else 1
    ublk = lambda w, j: pl.BlockSpec((gb, tg, w), lambda b, i: (b, ti(i), j))
    full = lambda s: pl.BlockSpec(s, lambda b, i: (0,) * len(s))
    state = pl.BlockSpec((gb, KW, VW), lambda b, i: (b, 0, 0))
    tab = pl.BlockSpec((tg, KW), lambda b, i: (ti(i), 0))
    in_specs = [ublk(2 * KW, _GLA_QK_BLK), ublk(VW, _GLA_V_BLK), ublk(LANE, _GLA_LR_BLK), tab, tab,
                full((LANE, KW)), full((1, KW)), state]
    args = [u_rest, u_rest, u_rest, cos, sin, wa, ba, s0]
    if fin is not None:
        in_specs += [ublk(VW, 0), ublk(VW, _GLA_R_BLK), full((1, VW))]
        args += [fin[0], u_rest, fin[1]]
    return pl.pallas_call(
        functools.partial(_gla_kernel, reverse, fin is not None),
        grid=(B // gb, n),
        in_specs=in_specs,
        out_specs=[ublk(VW, 0), state],
        out_shape=[jax.ShapeDtypeStruct((B, T, VW), F32), jax.ShapeDtypeStruct((B, KW, VW), F32)],
        scratch_shapes=[pltpu.VMEM((gb, KW, VW), F32)],
        compiler_params=_cparams("arbitrary", "arbitrary"),
        name="gla_bwd" if reverse else "gla_fwd",
    )(*args)


def _gla_bidir(u_rest, cos, sin, w_a2, b_a2, norm_g, s0_f, s0_b):
    def decay_w(d):
        return jnp.zeros((LANE, GLA_KEY_WIDTH), F32).at[d * GLA_RANK:(d + 1) * GLA_RANK].set(w_a2[d])

    o_f, s_f = _gla_pass(u_rest, cos, sin, decay_w(0), b_a2[0:1], s0_f, False)
    g = jnp.tile(norm_g, GLA_HEADS).reshape(1, GLA_WIDTH)
    o, s_b = _gla_pass(u_rest, cos, sin, decay_w(1), b_a2[1:2], s0_b, True, (o_f, g))
    return o, s_f, s_b


HY_LANES = LANE
HY_CH_TILE = 16
HY_CTX_N1 = 16


def _hy_consts(n1, N1):
    W = HY_LANES
    n1p = max(n1, 16)
    n1o = max(n1, 8)
    a = np.arange(N1)
    th1 = 2.0 * np.pi * ((a[:, None] * a[None, :]) % N1) / N1
    f1 = np.zeros((2 * N1, n1p)); f1[:N1, :n1] = np.cos(th1[:, :n1]); f1[N1:, :n1] = -np.sin(th1[:, :n1])
    f1_full = np.concatenate([np.cos(th1), -np.sin(th1)], axis=0)
    ginv = np.zeros((n1o, 2 * N1)); ginv[:n1, :N1] = np.cos(th1.T[:n1]); ginv[:n1, N1:] = -np.sin(th1.T[:n1])
    r = np.arange(W)
    tht = 2.0 * np.pi * ((a[:, None] * r[None, :]) % (N1 * W)) / (N1 * W)
    tr = np.tile(np.cos(tht), (1, HY_CH_TILE)); ti = np.tile(-np.sin(tht), (1, HY_CH_TILE))
    th2 = 2.0 * np.pi * ((r[:, None] * r[None, :]) % W) / W
    c2, s2 = np.cos(th2), -np.sin(th2)
    m2f = np.block([[c2, s2], [-s2, c2]])
    m2i = np.block([[c2, -s2], [s2, c2]])
    f = lambda m: jnp.asarray(m, F32)
    return dict(n1p=n1p, n1o=n1o, f1=f(f1), f1_full=f(f1_full), ginv=f(ginv), tr=f(tr), ti=f(ti), m2f=f(m2f), m2i=f(m2i))


def _short_conv_rows(u, w_ref, b_ref, n_rows):
    R, Wd = u.shape
    lane = lax.broadcasted_iota(jnp.int32, (R, Wd), 1) % HY_LANES
    row = lax.broadcasted_iota(jnp.int32, (R, Wd), 0)
    up = jnp.where(row == 0, 0.0, pltpu.roll(u, 1, 0))
    dn = jnp.where(row == n_rows - 1, 0.0, pltpu.roll(u, R - 1, 0))
    prev = jnp.where(lane == 0, pltpu.roll(up, Wd - (HY_LANES - 1), 1), pltpu.roll(u, 1, 1))
    nxt = jnp.where(lane == HY_LANES - 1, pltpu.roll(dn, HY_LANES - 1, 1), pltpu.roll(u, Wd - 1, 1))
    return b_ref[...] + prev * w_ref[0:1] + u * w_ref[1:2] + nxt * w_ref[2:3]


def _pad_rows(u, rows):
    return u if u.shape[0] == rows else jnp.concatenate([u, jnp.zeros((rows - u.shape[0], u.shape[1]), u.dtype)], axis=0)


def _hy_chunk_dft(z, f1, tr, ti, N1, prec):
    if prec is None:
        a = jnp.dot(f1.astype(BF16), z.astype(BF16), preferred_element_type=F32)
    else:
        a = jnp.dot(f1, z, preferred_element_type=F32, precision=prec)
    ar, ai = a[:N1], a[N1:]
    a_re, a_im = ar * tr - ai * ti, ar * ti + ai * tr
    W = HY_LANES
    return jnp.concatenate(
        [jnp.concatenate([a_re[:, c * W:(c + 1) * W], a_im[:, c * W:(c + 1) * W]], axis=1) for c in range(z.shape[1] // W)],
        axis=0)


def _hy_long_conv(z, kf, f1, ginv, tr, ti, m2f, m2i, N1):
    W = HY_LANES
    m = z.shape[1] // W
    a = _hy_chunk_dft(z, f1, tr, ti, N1, None).astype(BF16)
    x = jnp.dot(a, m2f.astype(BF16), preferred_element_type=F32)
    xr, xi = x[:, :W], x[:, W:]
    kr, ki = kf[:, :W], kf[:, W:]
    y = jnp.concatenate([xr * kr - xi * ki, xr * ki + xi * kr], axis=1).astype(BF16)
    p = jnp.dot(y, m2i.astype(BF16), preferred_element_type=F32)
    pr = jnp.concatenate([p[c * N1:(c + 1) * N1, :W] for c in range(m)], axis=1)
    pi = jnp.concatenate([p[c * N1:(c + 1) * N1, W:] for c in range(m)], axis=1)
    q = jnp.concatenate([pr * tr + pi * ti, pi * tr - pr * ti], axis=0).astype(BF16)
    return jnp.dot(ginv.astype(BF16), q, preferred_element_type=F32)


def _hy_mixer_kernel(n1, N1, n1p, v_ref, x1_ref, x2_ref, swv_ref, sbv_ref, swx1_ref, sbx1_ref, swx2_ref, sbx2_ref,
                     skip_ref, kf_ref, f1_ref, ginv_ref, tr_ref, ti_ref, m2f_ref, m2i_ref, o_ref):
    rows = max(n1, 8)
    cst = (f1_ref[...], ginv_ref[...], tr_ref[...], ti_ref[...], m2f_ref[...], m2i_ref[...], N1)
    z = _short_conv_rows(_pad_rows(v_ref[0], rows), swv_ref, sbv_ref, n1)
    gates = (_short_conv_rows(_pad_rows(x1_ref[0], rows), swx1_ref, sbx1_ref, n1),
             _short_conv_rows(_pad_rows(x2_ref[0], rows), swx2_ref, sbx2_ref, n1))
    for o in range(HY_ORDER):
        y = _hy_long_conv(_pad_rows(z, n1p), kf_ref[o, 0], *cst)
        z = gates[o] * (y + skip_ref[o:o + 1] * z)
    o_ref[0] = z[:n1]


def _hy_kspec_kernel(N1, scale, k_ref, f1_ref, tr_ref, ti_ref, m2f_ref, kf_ref):
    a = _hy_chunk_dft(k_ref[0], f1_ref[...], tr_ref[...], ti_ref[...], N1, lax.Precision.HIGHEST)
    kf_ref[0, 0] = jnp.dot(a, m2f_ref[...], preferred_element_type=F32, precision=lax.Precision.HIGHEST) * scale


def _hy_filter_spectrum(filt, N1, cst):
    Ls, n_ord, _, C = filt.shape
    W = HY_LANES
    N = N1 * W
    kern = jnp.concatenate([filt[:, :, 0], jnp.zeros((N - 2 * Ls + 1, n_ord, C), F32), filt[:0:-1, :, 1]], axis=0)
    kern = kern.reshape(N1, W, n_ord, C).transpose(2, 0, 3, 1).reshape(n_ord, N1, C * W)
    Wd = HY_CH_TILE * W
    nj = C // HY_CH_TILE
    full = lambda s: pl.BlockSpec(s, lambda o, j: (0,) * len(s))
    return pl.pallas_call(
        functools.partial(_hy_kspec_kernel, N1, 1.0 / N),
        grid=(n_ord, nj),
        in_specs=[pl.BlockSpec((1, N1, Wd), lambda o, j: (o, 0, j)), full((2 * N1, N1)), full((N1, Wd)), full((N1, Wd)),
                  full((2 * W, 2 * W))],
        out_specs=pl.BlockSpec((1, 1, HY_CH_TILE * N1, 2 * W), lambda o, j: (o, j, 0, 0)),
        out_shape=jax.ShapeDtypeStruct((n_ord, nj, HY_CH_TILE * N1, 2 * W), F32),
        compiler_params=_cparams("arbitrary", "arbitrary"),
        name="hyena_filter_spectrum",
    )(kern, cst["f1_full"], cst["tr"], cst["ti"], cst["m2f"])


def _hyena_mixer(hy_t, short_w, short_b, filt, skip, N1):
    B, n1, C3, W = hy_t.shape
    C = C3 // (HY_ORDER + 1)
    cst = _hy_consts(n1, N1)
    kf = _hy_filter_spectrum(filt, N1, cst)
    Wd = HY_CH_TILE * W
    nj = C // HY_CH_TILE
    hy2 = hy_t.reshape(B, n1, C3 * W)
    rep = lambda v: jnp.repeat(v, W, axis=-1)
    sw, sb = rep(short_w), rep(short_b).reshape(1, C3 * W)
    sk = rep(skip)
    chan = lambda part: pl.BlockSpec((1, n1, Wd), lambda j, b: (b, 0, part * nj + j))
    wsp = lambda part: pl.BlockSpec((HY_SHORT, Wd), lambda j, b: (0, part * nj + j))
    bsp = lambda part: pl.BlockSpec((1, Wd), lambda j, b: (0, part * nj + j))
    full = lambda a: pl.BlockSpec(a.shape, lambda j, b: (0,) * a.ndim)
    consts = [cst[k] for k in ("f1", "ginv", "tr", "ti", "m2f", "m2i")]
    out = pl.pallas_call(
        functools.partial(_hy_mixer_kernel, n1, N1, cst["n1p"]),
        grid=(nj, B),
        in_specs=[chan(0), chan(1), chan(2), wsp(0), bsp(0), wsp(1), bsp(1), wsp(2), bsp(2),
                  pl.BlockSpec((HY_ORDER, Wd), lambda j, b: (0, j)),
                  pl.BlockSpec((HY_ORDER, 1, HY_CH_TILE * N1, 2 * W), lambda j, b: (0, j, 0, 0))]
                 + [full(a) for a in consts],
        out_specs=pl.BlockSpec((1, n1, Wd), lambda j, b: (b, 0, j)),
        out_shape=jax.ShapeDtypeStruct((B, n1, C * W), F32),
        compiler_params=_cparams("arbitrary", "arbitrary"),
        name="hyena_mixer",
    )(hy2, hy2, hy2, sw, sb, sw, sb, sw, sb, sk, kf, *consts)
    return out.reshape(B, n1, C, W)


def _hyena_filters(L, w1, b1, f1, w2, b2, f2, w3, b3, decay):
    hi = lax.Precision.HIGHEST
    t = jnp.linspace(0.0, 1.0, L, dtype=F32)
    w = 2.0 * math.pi * jnp.arange(L, dtype=F32) / L
    bands = jnp.linspace(1e-4, HY_BANDS - 1, HY_BANDS, dtype=F32)
    ang = w[:, None] * bands[None]
    z = jnp.concatenate([t[:, None], jnp.cos(ang), -jnp.sin(ang)], -1)
    h = jnp.sin(f1 * (jnp.dot(z, w1, precision=hi) + b1))
    h = jnp.sin(f2 * (jnp.dot(h, w2, precision=hi) + b2))
    h = (jnp.dot(h, w3, precision=hi) + b3).reshape(L, HY_ORDER, 2, HY_CH)
    window = jnp.exp(-t[:, None, None, None] * jnp.abs(decay)[None])
    return h * window


def kernel(x, c, ctx, c_ctx, w_mod, b_mod, w_in, na_rpb, gla_w_a2, gla_b_a2, gla_norm_g, hy_short_w, hy_short_b, hy_w1, hy_b1, hy_f1, hy_w2, hy_b2, hy_f2, hy_w3, hy_b3, hy_decay, hy_skip, w_out, ln1_g, ln1_b, router_w, router_b, we_gate, we_up, we_down, ws_gate, ws_up, ws_down, ln2_g, ln2_b):
    B, L, D = x.shape
    C = ctx.shape[1]
    rope_cos, rope_sin = _rope_tables(L)
    ctx_cos, ctx_sin = jnp.ones((C, GLA_KEY_WIDTH), F32), jnp.zeros((C, GLA_KEY_WIDTH), F32)
    zero_state = jnp.zeros((B, GLA_KEY_WIDTH, GLA_WIDTH), F32)
    na_bias = _na_bias_table(na_rpb)

    n_mod = -(-(B + 1) // 8) * 8
    cs = jnp.zeros((n_mod, D), F32).at[:B].set(c).at[B].set(c_ctx)
    mod_all = _modulation(cs, w_mod, b_mod)

    xc = ctx
    for l in range(DEPTH):
        last = l == DEPTH - 1
        mods = mod_all[l].reshape(n_mod, 6, 1, D)
        lat = lambda j: mods[:B, j]
        cm = lambda j: mods[B:B + 1, j]
        n_tok = NA_COLS + GLA_COLS
        w_pad = jnp.pad(w_in[l][:, :n_tok], ((0, 0), (0, D_IN_PAD - n_tok))).astype(BF16)
        w_hy_t = w_in[l][:, n_tok:].T.astype(BF16)
        w_out_b = w_out[l].astype(BF16)
        lg1, lb1 = ln1_g[l].reshape(1, D), ln1_b[l].reshape(1, D)
        lg2, lb2 = ln2_g[l].reshape(1, D), ln2_b[l].reshape(1, D)

        u_na, u_rest, hy = _inproj(x, lat(1), lat(0), w_pad, w_hy_t, True)
        uc_na, uc_rest, hyc = _inproj(xc, cm(1), cm(0), w_pad, w_hy_t, False)

        na_lat = _na_attention(u_na, uc_na, na_bias, l)

        gla_c, s_ctx_f, s_ctx_b = _gla_bidir(uc_rest, ctx_cos, ctx_sin, gla_w_a2[l], gla_b_a2[l], gla_norm_g[l],
                                             zero_state, zero_state)
        gla_lat, _, _ = _gla_bidir(u_rest, rope_cos, rope_sin, gla_w_a2[l], gla_b_a2[l], gla_norm_g[l],
                                   s_ctx_f, s_ctx_b)

        filt_args = (hy_w1[l], hy_b1[l], hy_f1[l], hy_w2[l], hy_b2[l], hy_f2[l], hy_w3[l], hy_b3[l], hy_decay[l])
        hy_lat = _hyena_mixer(hy, hy_short_w[l], hy_short_b[l], _hyena_filters(L, *filt_args), hy_skip[l],
                              2 * L // HY_LANES)

        wr_t = router_w[l].T
        x, h_lat, logit_lat = _outproj(na_lat, gla_lat, hy_lat, x, lat(2), w_out_b, lg1, lb1, lat(4), lat(3), wr_t, True)
        if not last:
            na_c = _ctx_attention(uc_na)
            hy_c = _hyena_mixer(hyc, hy_short_w[l], hy_short_b[l], _hyena_filters(C, *filt_args), hy_skip[l],
                                HY_CTX_N1)
            xc, h_c, logit_c = _outproj(na_c, gla_c, hy_c, xc, cm(2), w_out_b, lg1, lb1, cm(4), cm(3), wr_t, False)

            h_flat = jnp.concatenate([h_lat.reshape(B * L, D), h_c.reshape(B * C, D)], axis=0)
            logit_t = jnp.concatenate([logit_lat, logit_c], axis=1)
        else:
            h_flat = h_lat.reshape(B * L, D)
            logit_t = logit_lat

        picked, gates = _moe_routed(h_flat, logit_t, router_b[l], we_gate, we_up, we_down, l)
        wsg, wsu, wsd = ws_gate[l].astype(BF16), ws_up[l].astype(BF16), ws_down[l].astype(BF16)
        x = _shared_ln2(x, h_lat, picked, gates, 0, lat(5), wsg, wsu, wsd, lg2, lb2, True)
        if not last:
            xc = _shared_ln2(xc, h_c, picked, gates, B * L, cm(5), wsg, wsu, wsd, lg2, lb2, False)
    return x
```

```python
import functools
import math

import numpy as np
import jax
import jax.numpy as jnp
from jax import lax
from jax.experimental import pallas as pl
from jax.experimental.pallas import tpu as pltpu
from jax.experimental.pallas import tpu_sc as plsc

F32 = jnp.float32
BF16 = jnp.bfloat16

D_MODEL = 1024
DEPTH = 4
GRID_W = 64
CTX_LEN = 256

NA_HEADS = 8
NA_HEAD_DIM = 64
NA_WIDTH = NA_HEADS * NA_HEAD_DIM
NA_WIN_ROWS = 8
NA_WIN_COLS = 16

GLA_HEADS = 4
GLA_DK = 32
GLA_DV = 64
GLA_KEY_WIDTH = GLA_HEADS * GLA_DK
GLA_WIDTH = GLA_HEADS * GLA_DV
GLA_RANK = 16
GLA_GATE_NORM = 16.0
GLA_CHUNK = 64

HY_CH = 256
HY_ORDER = 2
HY_SHORT = 3
HY_BANDS = 16
HY_EMB = 1 + 2 * HY_BANDS

MIX_WIDTH = NA_WIDTH + GLA_WIDTH + HY_CH
IN_SPLITS = (NA_WIDTH, NA_WIDTH, NA_WIDTH, GLA_KEY_WIDTH, GLA_KEY_WIDTH, GLA_WIDTH, GLA_WIDTH,
             2 * GLA_RANK, (HY_ORDER + 1) * HY_CH)
ROPE_BASE = 10000.0

N_EXPERTS = 128
TOP_K = 8
N_GROUPS = 8
TOPK_GROUPS = 4
EXPERT_HIDDEN = 256
ROUTED_SCALE = 2.5

DEEPNORM_ALPHA = (2 * DEPTH) ** 0.25
LN_EPS = 1e-6

LANE = 128
MXU_DIM = 256
VMEM_LIMIT = 48 * 1024 * 1024
SC_CORES = 2
SC_SUBCORES = 16
SC_LANES = 16
SC_SCAN_CHUNK = 16384

NA_COLS = 3 * NA_WIDTH
GLA_COLS = sum(IN_SPLITS[3:8])
HY_COLS = IN_SPLITS[8]
REST_COLS = -(-GLA_COLS // LANE) * LANE
D_IN_PAD = NA_COLS + REST_COLS
NA_QUAD = MXU_DIM // NA_HEAD_DIM
NEG_BIG = -1e30

TOK_TILE = 512
MOE_TILE = 512
NA_ROW_TILE = 8
NA_ROW_UNROLL = 4
ROUTE_TILE = 256
GLA_TILE = 512
GLA_BATCH = 2


def _cparams(*sem):
    return pltpu.CompilerParams(dimension_semantics=sem, vmem_limit_bytes=VMEM_LIMIT)


def _mod_kernel(c_ref, w_ref, b_ref, o_ref):
    c = c_ref[...]
    s = c * jax.nn.sigmoid(c)
    o_ref[0] = jnp.dot(s, w_ref[0], preferred_element_type=F32, precision=lax.Precision.HIGHEST) + b_ref[0]


def _modulation(cs, w_mod, b_mod):
    R = cs.shape[0]
    tn = 1536
    return pl.pallas_call(
        _mod_kernel,
        grid=(DEPTH, 6 * D_MODEL // tn),
        in_specs=[pl.BlockSpec((R, D_MODEL), lambda l, j: (0, 0)),
                  pl.BlockSpec((1, D_MODEL, tn), lambda l, j: (l, 0, j)),
                  pl.BlockSpec((1, 1, tn), lambda l, j: (l, 0, j))],
        out_specs=pl.BlockSpec((1, R, tn), lambda l, j: (l, 0, j)),
        out_shape=jax.ShapeDtypeStruct((DEPTH, R, 6 * D_MODEL), F32),
        compiler_params=_cparams("arbitrary", "arbitrary"),
        name="modulation",
    )(cs, w_mod, b_mod.reshape(DEPTH, 1, 6 * D_MODEL))


def _inproj_kernel(x_ref, sc_ref, sh_ref, w_ref, wh_ref, ona_ref, orest_ref, ohy_ref):
    xm = (x_ref[0] * (1.0 + sc_ref[0]) + sh_ref[0]).astype(BF16)
    step = 512
    for c0 in range(0, NA_COLS, step):
        ona_ref[0, :, c0:c0 + step] = jnp.dot(xm, w_ref[:, c0:c0 + step], preferred_element_type=F32).astype(BF16)
    for c0 in range(0, REST_COLS, step):
        c1 = min(c0 + step, REST_COLS)
        orest_ref[0, :, c0:c1] = jnp.dot(xm, w_ref[:, NA_COLS + c0:NA_COLS + c1], preferred_element_type=F32)
    hy = lax.dot_general(wh_ref[...], xm, _NT, preferred_element_type=F32)
    for j in range(ohy_ref.shape[1]):
        ohy_ref[0, j] = hy[:, j * LANE:(j + 1) * LANE]


def _inproj(x, sc, sh, w_pad, w_hy_t, per_batch_mod):
    B, T, D = x.shape
    tm = min(TOK_TILE, T)
    mod_idx = (lambda b, i: (b, 0, 0)) if per_batch_mod else (lambda b, i: (0, 0, 0))
    return pl.pallas_call(
        _inproj_kernel,
        grid=(B, T // tm),
        in_specs=[pl.BlockSpec((1, tm, D), lambda b, i: (b, i, 0)),
                  pl.BlockSpec((1, 1, D), mod_idx),
                  pl.BlockSpec((1, 1, D), mod_idx),
                  pl.BlockSpec((D, D_IN_PAD), lambda b, i: (0, 0)),
                  pl.BlockSpec((HY_COLS, D), lambda b, i: (0, 0))],
        out_specs=[pl.BlockSpec((1, tm, NA_COLS), lambda b, i: (b, i, 0)),
                   pl.BlockSpec((1, tm, REST_COLS), lambda b, i: (b, i, 0)),
                   pl.BlockSpec((1, tm // LANE, HY_COLS, LANE), lambda b, i: (b, i, 0, 0))],
        out_shape=[jax.ShapeDtypeStruct((B, T, NA_COLS), BF16),
                   jax.ShapeDtypeStruct((B, T, REST_COLS), F32),
                   jax.ShapeDtypeStruct((B, T // LANE, HY_COLS, LANE), F32)],
        compiler_params=_cparams("arbitrary", "arbitrary"),
        name="inproj",
    )(x, sc, sh, w_pad, w_hy_t)


def _stack_heads(q, n_rows):
    head = lax.broadcasted_iota(jnp.int32, (n_rows, MXU_DIM), 1) // NA_HEAD_DIM
    return jnp.concatenate([jnp.where(head == h, q, jnp.zeros_like(q)) for h in range(NA_QUAD)], axis=0)


def _unstack_heads(o, n_rows):
    head = lax.broadcasted_iota(jnp.int32, (n_rows, MXU_DIM), 1) // NA_HEAD_DIM
    out = jnp.zeros((n_rows, MXU_DIM), F32)
    for h in range(NA_QUAD):
        out = jnp.where(head == h, o[h * n_rows:(h + 1) * n_rows], out)
    return out


_NT = (((1,), (1,)), ((), ()))


def _na_kernel(q_ref, k_ref, v_ref, kc_ref, vc_ref, bias_ref, o_ref):
    rt = pl.program_id(2)
    scale = NA_HEAD_DIM ** -0.5
    kc = kc_ref[0]
    vc = vc_ref[0]
    n_loc = NA_WIN_ROWS * GRID_W

    def row(rl, carry):
        r = rt * NA_ROW_TILE + rl
        kr0 = jnp.clip(r - NA_WIN_ROWS // 2, 0, GRID_W - NA_WIN_ROWS)
        dr0 = kr0 - r + NA_WIN_ROWS - 1
        q = q_ref[0, pl.ds(pl.multiple_of(rl * GRID_W, GRID_W), GRID_W), :]
        qs = _stack_heads(q, GRID_W)
        k0 = pl.multiple_of(kr0 * GRID_W, GRID_W)
        ks = k_ref[0, pl.ds(k0, n_loc), :]
        vs = v_ref[0, pl.ds(k0, n_loc), :]
        s_loc = lax.dot_general(qs, ks, _NT, preferred_element_type=F32) * scale + bias_ref[0, dr0]
        s_ctx = lax.dot_general(qs, kc, _NT, preferred_element_type=F32) * scale
        m = jnp.maximum(jnp.max(s_loc, axis=-1, keepdims=True), jnp.max(s_ctx, axis=-1, keepdims=True))
        p_loc = jnp.exp(s_loc - m)
        p_ctx = jnp.exp(s_ctx - m)
        den = jnp.sum(p_loc, axis=-1, keepdims=True) + jnp.sum(p_ctx, axis=-1, keepdims=True)
        o = (jnp.dot(p_loc.astype(BF16), vs, preferred_element_type=F32)
             + jnp.dot(p_ctx.astype(BF16), vc, preferred_element_type=F32)) / den
        o_ref[0, pl.ds(pl.multiple_of(rl * GRID_W, GRID_W), GRID_W), :] = _unstack_heads(o, GRID_W)
        return carry

    lax.fori_loop(0, NA_ROW_TILE, row, 0, unroll=NA_ROW_UNROLL)


def _na_bias_table(rpb):
    n_lyr = rpb.shape[0]
    c = np.arange(GRID_W)
    kc0 = np.clip(c - NA_WIN_COLS // 2, 0, GRID_W - NA_WIN_COLS)
    kc = np.arange(GRID_W)
    valid = (kc[None, :] >= kc0[:, None]) & (kc[None, :] < kc0[:, None] + NA_WIN_COLS)
    dc = kc[None, :] - c[:, None] + NA_WIN_COLS - 1
    onehot = (np.arange(2 * NA_WIN_COLS - 1)[:, None, None] == dc[None]) & valid[None]
    toep = jnp.einsum('lhrd,dck->lhrck', rpb, jnp.asarray(onehot, F32), precision=lax.Precision.HIGHEST)
    toep = jnp.where(jnp.asarray(valid)[None, None, None], toep, NEG_BIG)
    tab = jnp.stack([toep[:, :, d:d + NA_WIN_ROWS] for d in range(NA_WIN_ROWS)], axis=2)
    tab = tab.transpose(0, 1, 2, 4, 3, 5).reshape(n_lyr, NA_HEADS // NA_QUAD, NA_QUAD, NA_WIN_ROWS, GRID_W,
                                                  NA_WIN_ROWS * GRID_W)
    return tab.transpose(0, 1, 3, 2, 4, 5).reshape(n_lyr, NA_HEADS // NA_QUAD, NA_WIN_ROWS, NA_QUAD * GRID_W,
                                                   NA_WIN_ROWS * GRID_W)


def _na_attention(u_na, uc_na, bias_tab, layer):
    B, L, _ = u_na.shape
    C = uc_na.shape[1]
    assert L == GRID_W * GRID_W and GRID_W % NA_ROW_TILE == 0
    nq = NA_WIDTH // MXU_DIM
    tq = NA_ROW_TILE * GRID_W
    return pl.pallas_call(
        _na_kernel,
        grid=(B, nq, L // tq),
        in_specs=[pl.BlockSpec((1, tq, MXU_DIM), lambda b, j, i: (b, i, j)),
                  pl.BlockSpec((1, L, MXU_DIM), lambda b, j, i: (b, 0, nq + j)),
                  pl.BlockSpec((1, L, MXU_DIM), lambda b, j, i: (b, 0, 2 * nq + j)),
                  pl.BlockSpec((1, C, MXU_DIM), lambda b, j, i: (b, 0, nq + j)),
                  pl.BlockSpec((1, C, MXU_DIM), lambda b, j, i: (b, 0, 2 * nq + j)),
                  pl.BlockSpec((None, 1, NA_WIN_ROWS, NA_QUAD * GRID_W, NA_WIN_ROWS * GRID_W),
                               lambda b, j, i: (layer, j, 0, 0, 0))],
        out_specs=pl.BlockSpec((1, tq, MXU_DIM), lambda b, j, i: (b, i, j)),
        out_shape=jax.ShapeDtypeStruct((B, L, NA_WIDTH), F32),
        compiler_params=_cparams("arbitrary", "arbitrary", "arbitrary"),
        name="na_attention",
    )(u_na, u_na, u_na, uc_na, uc_na, bias_tab)


def _ctx_attn_kernel(q_ref, k_ref, v_ref, o_ref):
    C = q_ref.shape[1]
    qs = _stack_heads(q_ref[0], C)
    s = lax.dot_general(qs, k_ref[0], _NT, preferred_element_type=F32) * NA_HEAD_DIM ** -0.5
    p = jnp.exp(s - jnp.max(s, axis=-1, keepdims=True))
    den = jnp.sum(p, axis=-1, keepdims=True)
    o = jnp.dot(p.astype(BF16), v_ref[0], preferred_element_type=F32) / den
    o_ref[0] = _unstack_heads(o, C)


def _ctx_attention(uc_na):
    B, C, _ = uc_na.shape
    nq = NA_WIDTH // MXU_DIM
    return pl.pallas_call(
        _ctx_attn_kernel,
        grid=(B, nq),
        in_specs=[pl.BlockSpec((1, C, MXU_DIM), lambda b, j: (b, 0, j)),
                  pl.BlockSpec((1, C, MXU_DIM), lambda b, j: (b, 0, nq + j)),
                  pl.BlockSpec((1, C, MXU_DIM), lambda b, j: (b, 0, 2 * nq + j))],
        out_specs=pl.BlockSpec((1, C, MXU_DIM), lambda b, j: (b, 0, j)),
        out_shape=jax.ShapeDtypeStruct((B, C, NA_WIDTH), F32),
        compiler_params=_cparams("arbitrary", "arbitrary"),
        name="ctx_attention",
    )(uc_na, uc_na, uc_na)


def _layer_norm_rows(y, g, b):
    mu = jnp.mean(y, axis=-1, keepdims=True)
    d = y - mu
    var = jnp.mean(d * d, axis=-1, keepdims=True)
    return d * lax.rsqrt(var + LN_EPS) * g + b


def _outproj_kernel(na_ref, gla_ref, hy_ref, x_ref, g1_ref, w_ref, lg_ref, lb_ref, sc_ref, sh_ref, wr_ref,
                    xo_ref, h_ref, logit_ref):
    w_hy = w_ref[NA_WIDTH + GLA_WIDTH:, :]
    hy_mix = jnp.concatenate(
        [lax.dot_general(hy_ref[0, j].astype(BF16), w_hy, (((0,), (0,)), ((), ())), preferred_element_type=F32)
         for j in range(hy_ref.shape[1])], axis=0)
    mix = (jnp.dot(na_ref[0].astype(BF16), w_ref[0:NA_WIDTH, :], preferred_element_type=F32)
           + jnp.dot(gla_ref[0].astype(BF16), w_ref[NA_WIDTH:NA_WIDTH + GLA_WIDTH, :], preferred_element_type=F32)
           + hy_mix)
    xn = _layer_norm_rows(DEEPNORM_ALPHA * x_ref[0] + g1_ref[0] * mix, lg_ref[...], lb_ref[...])
    xo_ref[0] = xn
    h = xn * (1.0 + sc_ref[0]) + sh_ref[0]
    h_ref[0] = h.astype(BF16)
    logit_ref[...] = _dot_split(wr_ref[...], h, dims=_NT)


def _outproj(na, gla, hy, x, g1, w_out, ln_g, ln_b, sc2, sh2, w_router, per_batch_mod):
    B, T, D = x.shape
    tm = min(TOK_TILE, T)
    mod_idx = (lambda b, i: (b, 0, 0)) if per_batch_mod else (lambda b, i: (0, 0, 0))
    tok = lambda w: pl.BlockSpec((1, tm, w), lambda b, i: (b, i, 0))
    full = lambda s: pl.BlockSpec(s, lambda b, i: (0,) * len(s))
    mod = pl.BlockSpec((1, 1, D), mod_idx)
    return pl.pallas_call(
        _outproj_kernel,
        grid=(B, T // tm),
        in_specs=[tok(NA_WIDTH), tok(GLA_WIDTH),
                  pl.BlockSpec((1, tm // LANE, HY_CH, LANE), lambda b, i: (b, i, 0, 0)), tok(D), mod, full((MIX_WIDTH, D)),
                  full((1, D)), full((1, D)), mod, mod, full((N_EXPERTS, D))],
        out_specs=[tok(D), tok(D), pl.BlockSpec((N_EXPERTS, tm), lambda b, i: (0, b * (T // tm) + i))],
        out_shape=[jax.ShapeDtypeStruct((B, T, D), F32), jax.ShapeDtypeStruct((B, T, D), BF16),
                   jax.ShapeDtypeStruct((N_EXPERTS, B * T), F32)],
        compiler_params=_cparams("arbitrary", "arbitrary"),
        name="outproj_ln1",
    )(na, gla, hy, x, g1, w_out, ln_g, ln_b, sc2, sh2, w_router)


def _moe_kernel(te_ref, nt_ref, xs_ref, wg_ref, wu_ref, wd_ref, ys_ref, wg_s, wu_s, wd_s):
    i = pl.program_id(0)

    @pl.when(jnp.logical_or(i == 0, te_ref[i] != te_ref[jnp.maximum(i - 1, 0)]))
    def _():
        wg_s[...] = wg_ref[0].astype(BF16)
        wu_s[...] = wu_ref[0].astype(BF16)
        wd_s[...] = wd_ref[0].astype(BF16)

    @pl.when(i < nt_ref[0])
    def _():
        xs = xs_ref[...]
        g = jnp.dot(xs, wg_s[...], preferred_element_type=F32)
        u = jnp.dot(xs, wu_s[...], preferred_element_type=F32)
        a = (g * jax.nn.sigmoid(g) * u).astype(BF16)
        ys_ref[...] = jnp.dot(a, wd_s[...], preferred_element_type=F32).astype(ys_ref.dtype)

    @pl.when(i >= nt_ref[0])
    def _():
        ys_ref[...] = jnp.zeros_like(ys_ref)


def _moe_grouped(tile_expert, n_used, xs, we_g, we_u, we_d, layer):
    Mp, D = xs.shape
    n_tiles = Mp // MOE_TILE
    H = EXPERT_HIDDEN
    grid_spec = pltpu.PrefetchScalarGridSpec(
        num_scalar_prefetch=2,
        grid=(n_tiles,),
        in_specs=[pl.BlockSpec((MOE_TILE, D), lambda i, te, nt: (i, 0)),
                  pl.BlockSpec((None, 1, D, H), lambda i, te, nt: (layer, te[i], 0, 0)),
                  pl.BlockSpec((None, 1, D, H), lambda i, te, nt: (layer, te[i], 0, 0)),
                  pl.BlockSpec((None, 1, H, D), lambda i, te, nt: (layer, te[i], 0, 0))],
        out_specs=pl.BlockSpec((MOE_TILE, D), lambda i, te, nt: (i, 0)),
        scratch_shapes=[pltpu.VMEM((D, H), BF16), pltpu.VMEM((D, H), BF16), pltpu.VMEM((H, D), BF16)],
    )
    return pl.pallas_call(
        _moe_kernel,
        grid_spec=grid_spec,
        out_shape=jax.ShapeDtypeStruct((Mp, D), BF16),
        compiler_params=_cparams("arbitrary"),
        name="moe_experts",
    )(tile_expert, n_used, xs, we_g, we_u, we_d)


def _shared_kernel(x_ref, h_ref, pk_ref, gt_ref, g2_ref, wg_ref, wu_ref, wd_ref, lg_ref, lb_ref, o_ref):
    h = h_ref[0]
    g = jnp.dot(h, wg_ref[...], preferred_element_type=F32)
    u = jnp.dot(h, wu_ref[...], preferred_element_type=F32)
    a = (g * jax.nn.sigmoid(g) * u).astype(BF16)
    ff = jnp.dot(a, wd_ref[...], preferred_element_type=F32)
    gates = gt_ref[...]
    for k in range(TOP_K):
        ff = ff + pk_ref[k].astype(F32) * gates[:, k:k + 1]
    o_ref[0] = _layer_norm_rows(DEEPNORM_ALPHA * x_ref[0] + g2_ref[0] * ff, lg_ref[...], lb_ref[...])


def _shared_ln2(x, h, picked, gates, tok_off, g2, ws_g, ws_u, ws_d, ln_g, ln_b, per_batch_mod):
    B, T, D = x.shape
    tm = min(TOK_TILE, T)
    mod_idx = (lambda b, i: (b, 0, 0)) if per_batch_mod else (lambda b, i: (0, 0, 0))
    tok = pl.BlockSpec((1, tm, D), lambda b, i: (b, i, 0))
    full = lambda s: pl.BlockSpec(s, lambda b, i: (0,) * len(s))
    flat = lambda b, i: tok_off // tm + b * (T // tm) + i
    return pl.pallas_call(
        _shared_kernel,
        grid=(B, T // tm),
        in_specs=[tok, tok, pl.BlockSpec((TOP_K, tm, D), lambda b, i: (0, flat(b, i), 0)),
                  pl.BlockSpec((tm, TOP_K), lambda b, i: (flat(b, i), 0)),
                  pl.BlockSpec((1, 1, D), mod_idx), full((D, EXPERT_HIDDEN)), full((D, EXPERT_HIDDEN)),
                  full((EXPERT_HIDDEN, D)), full((1, D)), full((1, D))],
        out_specs=tok,
        out_shape=jax.ShapeDtypeStruct((B, T, D), F32),
        compiler_params=_cparams("arbitrary", "arbitrary"),
        name="shared_ln2",
    )(x, h, picked, gates, g2, ws_g, ws_u, ws_d, ln_g, ln_b)


def _first_max(vals, iota, n):
    m = jnp.max(vals, axis=0, keepdims=True)
    idx = jnp.min(jnp.where(vals == m, iota, n), axis=0, keepdims=True)
    return m, idx


def _route_kernel(lt_ref, b_ref, eidx_ref, w_ref, rank_ref, cnt_ref, base_ref):
    @pl.when(pl.program_id(0) == 0)
    def _():
        base_ref[...] = jnp.zeros_like(base_ref)

    tm = lt_ref.shape[1]
    per = N_EXPERTS // N_GROUPS
    s = jax.nn.sigmoid(lt_ref[...])
    sel = s + b_ref[...]
    io_g = lax.broadcasted_iota(jnp.int32, (per, tm), 0)
    scores = []
    for g in range(N_GROUPS):
        blk = sel[g * per:(g + 1) * per]
        m1, i1 = _first_max(blk, io_g, per)
        m2 = jnp.max(jnp.where(io_g == i1, -jnp.inf, blk), axis=0, keepdims=True)
        scores.append(m1 + m2)
    cur = jnp.concatenate(scores, axis=0)
    io_8 = lax.broadcasted_iota(jnp.int32, (N_GROUPS, tm), 0)
    gmask = jnp.zeros((N_GROUPS, tm), F32)
    for _ in range(TOPK_GROUPS):
        _, gi = _first_max(cur, io_8, N_GROUPS)
        hit = io_8 == gi
        gmask = jnp.where(hit, 1.0, gmask)
        cur = jnp.where(hit, -jnp.inf, cur)
    masked = jnp.concatenate(
        [jnp.where(gmask[g:g + 1] > 0.0, sel[g * per:(g + 1) * per], -jnp.inf) for g in range(N_GROUPS)], axis=0)
    io_e = lax.broadcasted_iota(jnp.int32, (N_EXPERTS, tm), 0)
    chosen = jnp.zeros((N_EXPERTS, tm), F32)
    eidx, gates = [], []
    for _ in range(TOP_K):
        _, ei = _first_max(masked, io_e, N_EXPERTS)
        hit = io_e == ei
        eidx.append(ei)
        gates.append(jnp.sum(jnp.where(hit, s, 0.0), axis=0, keepdims=True))
        masked = jnp.where(hit, -jnp.inf, masked)
        chosen = jnp.where(hit, 1.0, chosen)
    wk = jnp.concatenate(gates, axis=0)
    w_ref[...] = wk / jnp.sum(wk, axis=0, keepdims=True) * ROUTED_SCALE
    eidx_ref[...] = jnp.concatenate(eidx, axis=0)
    earlier = (lax.broadcasted_iota(jnp.int32, (tm, tm), 0) < lax.broadcasted_iota(jnp.int32, (tm, tm), 1))
    pos = jnp.dot(chosen.astype(BF16), jnp.where(earlier, 1.0, 0.0).astype(BF16), preferred_element_type=F32)
    pos = pos + base_ref[...]
    ranks = [jnp.sum(jnp.where(io_e == eidx[k], pos, 0.0), axis=0, keepdims=True) for k in range(TOP_K)]
    rank_ref[...] = jnp.concatenate(ranks, axis=0).astype(jnp.int32)
    base_ref[...] = base_ref[...] + jnp.sum(chosen, axis=1, keepdims=True)
    cnt_ref[...] = base_ref[...]


def _route(logits_t, b_corr):
    E, T = logits_t.shape
    tm = ROUTE_TILE
    tokk = pl.BlockSpec((TOP_K, tm), lambda i: (0, i))
    return pl.pallas_call(
        _route_kernel,
        grid=(T // tm,),
        in_specs=[pl.BlockSpec((E, tm), lambda i: (0, i)), pl.BlockSpec((E, 1), lambda i: (0, 0))],
        out_specs=[tokk, tokk, tokk, pl.BlockSpec((E, 1), lambda i: (0, 0))],
        out_shape=[jax.ShapeDtypeStruct((TOP_K, T), jnp.int32), jax.ShapeDtypeStruct((TOP_K, T), F32),
                   jax.ShapeDtypeStruct((TOP_K, T), jnp.int32), jax.ShapeDtypeStruct((E, 1), F32)],
        scratch_shapes=[pltpu.VMEM((E, 1), F32)],
        compiler_params=_cparams("arbitrary"),
        name="route",
    )(logits_t, b_corr.reshape(E, 1))


def _slot_kernel(eidx_ref, rank_ref, pstart_ref, dest_ref):
    tm = eidx_ref.shape[1]
    io_e = lax.broadcasted_iota(jnp.int32, (N_EXPERTS, tm), 0)
    ei = eidx_ref[...]
    starts = [jnp.sum(jnp.where(io_e == ei[k:k + 1], pstart_ref[...], 0.0), axis=0, keepdims=True)
              for k in range(TOP_K)]
    dest_ref[...] = jnp.concatenate(starts, axis=0).astype(jnp.int32) + rank_ref[...]


def _slots(eidx, rank, pstart):
    K, T = eidx.shape
    tm = ROUTE_TILE
    tokk = pl.BlockSpec((K, tm), lambda i: (0, i))
    return pl.pallas_call(
        _slot_kernel,
        grid=(T // tm,),
        in_specs=[tokk, tokk, pl.BlockSpec((N_EXPERTS, 1), lambda i: (0, 0))],
        out_specs=tokk,
        out_shape=jax.ShapeDtypeStruct((K, T), jnp.int32),
        compiler_params=_cparams("arbitrary"),
        name="route_slots",
    )(eidx, rank, pstart)


def _slot_tokens(dest, tok, n_slots, n_tok):
    n_asg = dest.shape[0]
    n_sub = SC_CORES * SC_SUBCORES
    per = n_slots // n_sub
    chunk = SC_SCAN_CHUNK
    assert n_slots % n_sub == 0 and per % SC_LANES == 0 and n_asg % chunk == 0 and chunk % SC_LANES == 0
    mesh = plsc.VectorSubcoreMesh(core_axis_name="c", subcore_axis_name="s", num_cores=SC_CORES, num_subcores=SC_SUBCORES)

    def body(dest_hbm, tok_hbm, out_hbm, loc, dbuf, tbuf):
        s0 = (lax.axis_index("c") * SC_SUBCORES + lax.axis_index("s")) * per
        lanes = lax.iota(jnp.int32, SC_LANES)

        @pl.loop(0, per, step=SC_LANES)
        def _(i):
            loc[pl.ds(i, SC_LANES)] = lax.rem(s0 + i, n_tok - SC_LANES) + lanes

        @pl.loop(0, n_asg, step=chunk)
        def _(c):
            pltpu.sync_copy(dest_hbm.at[pl.ds(c, chunk)], dbuf)
            pltpu.sync_copy(tok_hbm.at[pl.ds(c, chunk)], tbuf)

            @pl.loop(0, chunk, step=SC_LANES)
            def _(i):
                d = dbuf[pl.ds(i, SC_LANES)] - s0
                mine = jnp.logical_and(d >= 0, d < per)
                plsc.store_scatter(loc, [d], tbuf[pl.ds(i, SC_LANES)], mask=mine)

        pltpu.sync_copy(loc, out_hbm.at[pl.ds(s0, per)])

    return pl.kernel(
        body, out_type=jax.ShapeDtypeStruct((n_slots,), jnp.int32), mesh=mesh,
        scratch_types=[pltpu.VMEM((per,), jnp.int32), pltpu.VMEM((chunk,), jnp.int32), pltpu.VMEM((chunk,), jnp.int32)],
        compiler_params=pltpu.CompilerParams(needs_layout_passes=False),
        name="slot_tokens",
    )(dest, tok)


def _moe_routed(h_flat, logits_t, b_corr, we_g, we_u, we_d, layer):
    T = h_flat.shape[0]
    n_tiles = T * TOP_K // MOE_TILE + N_EXPERTS
    eidx, gates, rank, counts = _route(logits_t, b_corr)
    counts = counts[:, 0].astype(jnp.int32)
    padded = (counts + MOE_TILE - 1) // MOE_TILE * MOE_TILE
    pends = jnp.cumsum(padded)
    tile_start = jnp.arange(n_tiles, dtype=jnp.int32) * MOE_TILE
    tile_expert = jnp.minimum(jnp.sum((pends[None, :] <= tile_start[:, None]).astype(jnp.int32), axis=1), N_EXPERTS - 1)
    n_used = (pends[-1] // MOE_TILE).astype(jnp.int32).reshape(1)
    dest = _slots(eidx, rank, (pends - padded).astype(F32).reshape(N_EXPERTS, 1))
    tok = jnp.broadcast_to(jnp.arange(T, dtype=jnp.int32)[None], (TOP_K, T))
    src_tok = _slot_tokens(dest.reshape(-1), tok.reshape(-1), n_tiles * MOE_TILE, T)
    xs = h_flat.at[src_tok].get(mode='promise_in_bounds')
    ys = _moe_grouped(tile_expert, n_used, xs, we_g, we_u, we_d, layer)
    return ys.at[dest].get(mode='promise_in_bounds'), gates.T


_GLA_QK_BLK, _GLA_V_BLK, _GLA_R_BLK = 0, 1, 2
_GLA_LR_BLK = (2 * GLA_KEY_WIDTH + 2 * GLA_WIDTH) // LANE


def _rope_tables(L):
    t = np.arange(L)
    lane = np.arange(GLA_KEY_WIDTH)
    d = lane % GLA_DK
    pos = np.where(d[None, :] < GLA_DK // 2, (t // GRID_W)[:, None], (t % GRID_W)[:, None]).astype(np.float32)
    quarter = GLA_DK // 4
    inv = ROPE_BASE ** (-jnp.arange(quarter, dtype=F32) / quarter)
    ang = jnp.asarray(pos) * inv[jnp.asarray(d % quarter)][None, :]
    sign = np.where(d % (2 * quarter) < quarter, -1.0, 1.0).astype(np.float32)
    return jnp.cos(ang), jnp.sin(ang) * sign[None, :]


def _rope_partner(x):
    lane = lax.broadcasted_iota(jnp.int32, x.shape, 1)
    quarter = GLA_DK // 4
    return jnp.where(lane % (2 * quarter) < quarter, pltpu.roll(x, GLA_KEY_WIDTH - quarter, 1), pltpu.roll(x, quarter, 1))


_NN = (((1,), (0,)), ((), ()))


def _dot_split(a, b, rhs_exact=False, dims=_NN):
    dot = lambda x, y: lax.dot_general(x, y, dims, preferred_element_type=F32)
    a_hi = a.astype(BF16)
    a_lo = (a - a_hi.astype(F32)).astype(BF16)
    b_hi = b.astype(BF16)
    out = dot(a_hi, b_hi) + dot(a_lo, b_hi)
    if not rhs_exact:
        b_lo = (b - b_hi.astype(F32)).astype(BF16)
        out = out + dot(a_hi, b_lo)
    return out


def _log_sigmoid(x):
    return jnp.minimum(x, 0.0) - jnp.log(1.0 + jnp.exp(-jnp.abs(x)))


def _gla_kernel(reverse, finalize, *refs):
    if finalize:
        (qk_ref, v_ref, lr_ref, cos_ref, sin_ref, wa_ref, ba_ref, s0_ref, of_ref, r_ref, g_ref,
         o_ref, sfin_ref, s_scr) = refs
    else:
        qk_ref, v_ref, lr_ref, cos_ref, sin_ref, wa_ref, ba_ref, s0_ref, o_ref, sfin_ref, s_scr = refs
    hi = lax.Precision.HIGHEST

    @pl.when(pl.program_id(1) == 0)
    def _():
        s_scr[...] = s0_ref[...]

    tg = qk_ref.shape[1]
    C = GLA_CHUNK
    KW, VW = GLA_KEY_WIDTH, GLA_WIDTH
    cos, sin = cos_ref[...], sin_ref[...]

    ri = lax.broadcasted_iota(jnp.int32, (C, C), 0)
    ci = lax.broadcasted_iota(jnp.int32, (C, C), 1)
    tri = jnp.where((ci >= ri) if reverse else (ci <= ri), 1.0, 0.0)
    tri_h = jnp.concatenate([tri] * GLA_HEADS, axis=0)
    head_k = lax.broadcasted_iota(jnp.int32, (C, KW), 1) // GLA_DK
    head_v = lax.broadcasted_iota(jnp.int32, (C, VW), 1) // GLA_DV
    own_block = (lax.broadcasted_iota(jnp.int32, (KW, VW), 0) // GLA_DK
                 == lax.broadcasted_iota(jnp.int32, (KW, VW), 1) // GLA_DV)
    eye = lax.broadcasted_iota(jnp.int32, (KW, KW), 0) == lax.broadcasted_iota(jnp.int32, (KW, KW), 1)
    for bb in range(qk_ref.shape[0]):
        _gla_sample(reverse, finalize, bb, refs, cos, sin, tri, tri_h, head_k, head_v, own_block, eye)


def _gla_sample(reverse, finalize, bb, refs, cos, sin, tri, tri_h, head_k, head_v, own_block, eye):
    if finalize:
        (qk_ref, v_ref, lr_ref, _, _, wa_ref, ba_ref, _, of_ref, r_ref, g_ref, o_ref, sfin_ref, s_scr) = refs
    else:
        qk_ref, v_ref, lr_ref, _, _, wa_ref, ba_ref, _, o_ref, sfin_ref, s_scr = refs
    hi = lax.Precision.HIGHEST
    tg = qk_ref.shape[1]
    C = GLA_CHUNK
    KW, VW = GLA_KEY_WIDTH, GLA_WIDTH
    qk = qk_ref[bb]
    q = qk[:, :KW] * GLA_DK ** -0.5
    k = qk[:, KW:]
    q = q * cos + _rope_partner(q) * sin
    k = k * cos + _rope_partner(k) * sin
    v = v_ref[bb]
    logit = _dot_split(lr_ref[bb], wa_ref[...]) + ba_ref[...]
    la = _log_sigmoid(logit) / GLA_GATE_NORM
    pos = lax.broadcasted_iota(jnp.int32, (tg, KW), 0) % C
    cum = la
    step = 1
    while step < C:
        if reverse:
            cum = cum + jnp.where(pos < C - step, pltpu.roll(cum, tg - step, 0), 0.0)
        else:
            cum = cum + jnp.where(pos >= step, pltpu.roll(cum, step, 0), 0.0)
        step *= 2

    S = s_scr[bb]
    nc = tg // C
    outs = [None] * nc
    for c in (range(nc - 1, -1, -1) if reverse else range(nc)):
        sl = slice(c * C, (c + 1) * C)
        b = cum[sl]
        qt = q[sl] * jnp.exp(b)
        kt = k[sl] * jnp.exp(-b)
        qs = jnp.concatenate([jnp.where(head_k == h, qt, 0.0) for h in range(GLA_HEADS)], axis=0).astype(BF16)
        att = lax.dot_general(qs, kt.astype(BF16), _NT, preferred_element_type=F32)
        att = jnp.where(tri_h > 0.0, att, 0.0)
        vb = v[sl].astype(BF16)
        oi = jnp.dot(att.astype(BF16), vb, preferred_element_type=F32)
        o_intra = jnp.zeros((C, VW), F32)
        for h in range(GLA_HEADS):
            o_intra = jnp.where(head_v == h, oi[h * C:(h + 1) * C], o_intra)
        o_inter = jnp.dot(qt.astype(BF16), S.astype(BF16), preferred_element_type=F32)
        outs[c] = o_intra + o_inter
        b_last = b[0:1] if reverse else b[C - 1:C]
        kdec = (k[sl] * jnp.exp(b_last - b)).astype(BF16)
        kv = lax.dot_general(kdec, vb, (((0,), (0,)), ((), ())), preferred_element_type=F32)
        decay_col = jnp.sum(jnp.where(eye, jnp.exp(b_last), 0.0), axis=1, keepdims=True)
        S = decay_col * S + jnp.where(own_block, kv, 0.0)
    s_scr[bb] = S
    sfin_ref[bb] = S
    o = jnp.concatenate(outs, axis=0)
    if finalize:
        o = of_ref[bb] + o
        same_head = (lax.broadcasted_iota(jnp.int32, (VW, VW), 0) // GLA_DV
                     == lax.broadcasted_iota(jnp.int32, (VW, VW), 1) // GLA_DV)
        ms = _dot_split(o * o, jnp.where(same_head, 1.0 / GLA_DV, 0.0), rhs_exact=True)
        r = r_ref[bb]
        o = o * lax.rsqrt(ms + LN_EPS) * g_ref[...] * (r * jax.nn.sigmoid(r))
    o_ref[bb] = o


def _gla_pass(u_rest, cos, sin, wa, ba, s0, reverse, fin=None):
    B, T, _ = u_rest.shape
    tg = min(GLA_TILE, T)
    n = T // tg
    ti = (lambda i: n - 1 - i) if reverse else (lambda i: i)
    KW, VW = GLA_KEY_WIDTH, GLA_WIDTH
    gb = GLA_BATCH if B % GLA_BATCH == 0 else 1
    ublk = lambda w, j: pl.BlockSpec((gb, tg, w), lambda b, i: (b, ti(i), j))
    full = lambda s: pl.BlockSpec(s, lambda b, i: (0,) * len(s))
    state = pl.BlockSpec((gb, KW, VW), lambda b, i: (b, 0, 0))
    tab = pl.BlockSpec((tg, KW), lambda b, i: (ti(i), 0))
    in_specs = [ublk(2 * KW, _GLA_QK_BLK), ublk(VW, _GLA_V_BLK), ublk(LANE, _GLA_LR_BLK), tab, tab,
                full((LANE, KW)), full((1, KW)), state]
    args = [u_rest, u_rest, u_rest, cos, sin, wa, ba, s0]
    if fin is not None:
        in_specs += [ublk(VW, 0), ublk(VW, _GLA_R_BLK), full((1, VW))]
        args += [fin[0], u_rest, fin[1]]
    return pl.pallas_call(
        functools.partial(_gla_kernel, reverse, fin is not None),
        grid=(B // gb, n),
        in_specs=in_specs,
        out_specs=[ublk(VW, 0), state],
        out_shape=[jax.ShapeDtypeStruct((B, T, VW), F32), jax.ShapeDtypeStruct((B, KW, VW), F32)],
        scratch_shapes=[pltpu.VMEM((gb, KW, VW), F32)],
        compiler_params=_cparams("arbitrary", "arbitrary"),
        name="gla_bwd" if reverse else "gla_fwd",
    )(*args)


def _gla_bidir(u_rest, cos, sin, w_a2, b_a2, norm_g, s0_f, s0_b):
    def decay_w(d):
        return jnp.zeros((LANE, GLA_KEY_WIDTH), F32).at[d * GLA_RANK:(d + 1) * GLA_RANK].set(w_a2[d])

    o_f, s_f = _gla_pass(u_rest, cos, sin, decay_w(0), b_a2[0:1], s0_f, False)
    g = jnp.tile(norm_g, GLA_HEADS).reshape(1, GLA_WIDTH)
    o, s_b = _gla_pass(u_rest, cos, sin, decay_w(1), b_a2[1:2], s0_b, True, (o_f, g))
    return o, s_f, s_b


HY_LANES = LANE
HY_CH_TILE = 16
HY_CTX_N1 = 16


def _hy_consts(n1, N1):
    W = HY_LANES
    n1p = max(n1, 16)
    n1o = max(n1, 8)
    a = np.arange(N1)
    th1 = 2.0 * np.pi * ((a[:, None] * a[None, :]) % N1) / N1
    f1 = np.zeros((2 * N1, n1p)); f1[:N1, :n1] = np.cos(th1[:, :n1]); f1[N1:, :n1] = -np.sin(th1[:, :n1])
    f1_full = np.concatenate([np.cos(th1), -np.sin(th1)], axis=0)
    ginv = np.zeros((n1o, 2 * N1)); ginv[:n1, :N1] = np.cos(th1.T[:n1]); ginv[:n1, N1:] = -np.sin(th1.T[:n1])
    r = np.arange(W)
    tht = 2.0 * np.pi * ((a[:, None] * r[None, :]) % (N1 * W)) / (N1 * W)
    tr = np.tile(np.cos(tht), (1, HY_CH_TILE)); ti = np.tile(-np.sin(tht), (1, HY_CH_TILE))
    th2 = 2.0 * np.pi * ((r[:, None] * r[None, :]) % W) / W
    c2, s2 = np.cos(th2), -np.sin(th2)
    m2f = np.block([[c2, s2], [-s2, c2]])
    m2i = np.block([[c2, -s2], [s2, c2]])
    f = lambda m: jnp.asarray(m, F32)
    return dict(n1p=n1p, n1o=n1o, f1=f(f1), f1_full=f(f1_full), ginv=f(ginv), tr=f(tr), ti=f(ti), m2f=f(m2f), m2i=f(m2i))


def _short_conv_rows(u, w_ref, b_ref, n_rows):
    R, Wd = u.shape
    lane = lax.broadcasted_iota(jnp.int32, (R, Wd), 1) % HY_LANES
    row = lax.broadcasted_iota(jnp.int32, (R, Wd), 0)
    up = jnp.where(row == 0, 0.0, pltpu.roll(u, 1, 0))
    dn = jnp.where(row == n_rows - 1, 0.0, pltpu.roll(u, R - 1, 0))
    prev = jnp.where(lane == 0, pltpu.roll(up, Wd - (HY_LANES - 1), 1), pltpu.roll(u, 1, 1))
    nxt = jnp.where(lane == HY_LANES - 1, pltpu.roll(dn, HY_LANES - 1, 1), pltpu.roll(u, Wd - 1, 1))
    return b_ref[...] + prev * w_ref[0:1] + u * w_ref[1:2] + nxt * w_ref[2:3]


def _pad_rows(u, rows):
    return u if u.shape[0] == rows else jnp.concatenate([u, jnp.zeros((rows - u.shape[0], u.shape[1]), u.dtype)], axis=0)


def _hy_chunk_dft(z, f1, tr, ti, N1, prec):
    if prec is None:
        a = jnp.dot(f1.astype(BF16), z.astype(BF16), preferred_element_type=F32)
    else:
        a = jnp.dot(f1, z, preferred_element_type=F32, precision=prec)
    ar, ai = a[:N1], a[N1:]
    a_re, a_im = ar * tr - ai * ti, ar * ti + ai * tr
    W = HY_LANES
    return jnp.concatenate(
        [jnp.concatenate([a_re[:, c * W:(c + 1) * W], a_im[:, c * W:(c + 1) * W]], axis=1) for c in range(z.shape[1] // W)],
        axis=0)


def _hy_long_conv(z, kf, f1, ginv, tr, ti, m2f, m2i, N1):
    W = HY_LANES
    m = z.shape[1] // W
    a = _hy_chunk_dft(z, f1, tr, ti, N1, None).astype(BF16)
    x = jnp.dot(a, m2f.astype(BF16), preferred_element_type=F32)
    xr, xi = x[:, :W], x[:, W:]
    kr, ki = kf[:, :W], kf[:, W:]
    y = jnp.concatenate([xr * kr - xi * ki, xr * ki + xi * kr], axis=1).astype(BF16)
    p = jnp.dot(y, m2i.astype(BF16), preferred_element_type=F32)
    pr = jnp.concatenate([p[c * N1:(c + 1) * N1, :W] for c in range(m)], axis=1)
    pi = jnp.concatenate([p[c * N1:(c + 1) * N1, W:] for c in range(m)], axis=1)
    q = jnp.concatenate([pr * tr + pi * ti, pi * tr - pr * ti], axis=0).astype(BF16)
    return jnp.dot(ginv.astype(BF16), q, preferred_element_type=F32)


def _hy_mixer_kernel(n1, N1, n1p, v_ref, x1_ref, x2_ref, swv_ref, sbv_ref, swx1_ref, sbx1_ref, swx2_ref, sbx2_ref,
                     skip_ref, kf_ref, f1_ref, ginv_ref, tr_ref, ti_ref, m2f_ref, m2i_ref, o_ref):
    rows = max(n1, 8)
    cst = (f1_ref[...], ginv_ref[...], tr_ref[...], ti_ref[...], m2f_ref[...], m2i_ref[...], N1)
    z = _short_conv_rows(_pad_rows(v_ref[0], rows), swv_ref, sbv_ref, n1)
    gates = (_short_conv_rows(_pad_rows(x1_ref[0], rows), swx1_ref, sbx1_ref, n1),
             _short_conv_rows(_pad_rows(x2_ref[0], rows), swx2_ref, sbx2_ref, n1))
    for o in range(HY_ORDER):
        y = _hy_long_conv(_pad_rows(z, n1p), kf_ref[o, 0], *cst)
        z = gates[o] * (y + skip_ref[o:o + 1] * z)
    o_ref[0] = z[:n1]


HY_EMB_PAD = -(-HY_EMB // 8) * 8


def _hy_positions(Ls, N):
    m = np.arange(N)
    fwd = m < Ls
    bwd = m > N - Ls
    lag = np.where(fwd, m, np.where(bwd, N - m, 0))
    t = np.linspace(0.0, 1.0, Ls, dtype=np.float32)[lag]
    w = (2.0 * math.pi * np.arange(Ls, dtype=np.float32) / Ls).astype(np.float32)[lag]
    bands = np.linspace(1e-4, HY_BANDS - 1, HY_BANDS, dtype=np.float32)
    ang = jnp.asarray(w[None, :] * bands[:, None])
    z = jnp.concatenate([jnp.asarray(t)[None, :], jnp.cos(ang), -jnp.sin(ang),
                         jnp.zeros((HY_EMB_PAD - HY_EMB, N), F32)], axis=0)
    f = lambda a: jnp.asarray(a[None, :].astype(np.float32))
    return z, f(t), f(fwd), f(bwd)


def _hy_mlp_kernel(z_ref, w1_ref, b1_ref, f1_ref, w2_ref, b2_ref, f2_ref, h_ref):
    hi = lax.Precision.HIGHEST
    h = jnp.sin(f1_ref[...] * (jnp.dot(w1_ref[...], z_ref[...], preferred_element_type=F32, precision=hi) + b1_ref[...]))
    h_ref[...] = jnp.sin(f2_ref[...] * (jnp.dot(w2_ref[...], h, preferred_element_type=F32, precision=hi) + b2_ref[...]))


def _hy_kspec_kernel(N1, scale, h_ref, w3f_ref, w3b_ref, b3f_ref, b3b_ref, df_ref, db_ref, t_ref, vf_ref, vb_ref,
                     f1_ref, tr_ref, ti_ref, m2f_ref, kf_ref):
    hi = lax.Precision.HIGHEST
    W = HY_LANES
    h, t = h_ref[...], t_ref[...]
    kf_t = ((jnp.dot(w3f_ref[...], h, preferred_element_type=F32, precision=hi) + b3f_ref[...])
            * jnp.exp(-t * jnp.abs(df_ref[...])) * vf_ref[...]
            + (jnp.dot(w3b_ref[...], h, preferred_element_type=F32, precision=hi) + b3b_ref[...])
            * jnp.exp(-t * jnp.abs(db_ref[...])) * vb_ref[...])
    kern = jnp.concatenate(
        [jnp.concatenate([kf_t[c:c + 1, a * W:(a + 1) * W] for c in range(kf_t.shape[0])], axis=1) for a in range(N1)],
        axis=0)
    a = _hy_chunk_dft(kern, f1_ref[...], tr_ref[...], ti_ref[...], N1, hi)
    kf_ref[0, 0] = jnp.dot(a, m2f_ref[...], preferred_element_type=F32, precision=hi) * scale


def _hy_filter_spectrum(Ls, N1, cst, w1, b1, f1, w2, b2, f2, w3, b3, decay):
    W = HY_LANES
    N = N1 * W
    C = HY_CH
    hid = w2.shape[0]
    z, t_row, v_f, v_b = _hy_positions(Ls, N)
    col = lambda v: v.reshape(-1, 1)
    w1_t = jnp.pad(w1, ((0, HY_EMB_PAD - HY_EMB), (0, 0))).T
    full1 = lambda a: pl.BlockSpec(a.shape, lambda i: (0,) * a.ndim)
    mlp_args = (z, w1_t, col(b1), col(f1), w2.T, col(b2), col(f2))
    h = pl.pallas_call(
        _hy_mlp_kernel, grid=(1,), in_specs=[full1(a) for a in mlp_args],
        out_specs=pl.BlockSpec((hid, N), lambda i: (0, 0)), out_shape=jax.ShapeDtypeStruct((hid, N), F32),
        compiler_params=_cparams("arbitrary"), name="hyena_filter_mlp",
    )(*mlp_args)
    ct = HY_CH_TILE
    nj = C // ct
    nb = C // ct
    full = lambda a: pl.BlockSpec(a.shape, lambda o, j: (0,) * a.ndim)
    rows = lambda width, d: pl.BlockSpec((ct, width), lambda o, j: ((o * 2 + d) * nb + j, 0))
    w3_t, b3_c, dec_c = w3.T, col(b3), col(decay)
    consts = (t_row, v_f, v_b, cst["f1_full"], cst["tr"], cst["ti"], cst["m2f"])
    return pl.pallas_call(
        functools.partial(_hy_kspec_kernel, N1, 1.0 / N),
        grid=(HY_ORDER, nj),
        in_specs=[full(h), rows(hid, 0), rows(hid, 1), rows(1, 0), rows(1, 1), rows(1, 0), rows(1, 1)]
                 + [full(a) for a in consts],
        out_specs=pl.BlockSpec((1, 1, ct * N1, 2 * W), lambda o, j: (o, j, 0, 0)),
        out_shape=jax.ShapeDtypeStruct((HY_ORDER, nj, ct * N1, 2 * W), F32),
        compiler_params=_cparams("arbitrary", "arbitrary"),
        name="hyena_filter_spectrum",
    )(h, w3_t, w3_t, b3_c, b3_c, dec_c, dec_c, *consts)


def _hyena_mixer(hy_t, short_w, short_b, filt_params, skip, N1):
    B, n1, C3, W = hy_t.shape
    C = C3 // (HY_ORDER + 1)
    assert N1 >= 2 * n1 and C % HY_CH_TILE == 0, "circular length must cover the two-sided linear convolution"
    cst = _hy_consts(n1, N1)
    kf = _hy_filter_spectrum(n1 * W, N1, cst, *filt_params)
    Wd = HY_CH_TILE * W
    nj = C // HY_CH_TILE
    hy2 = hy_t.reshape(B, n1, C3 * W)
    rep = lambda v: jnp.repeat(v, W, axis=-1)
    sw, sb = rep(short_w), rep(short_b).reshape(1, C3 * W)
    sk = rep(skip)
    chan = lambda part: pl.BlockSpec((1, n1, Wd), lambda j, b: (b, 0, part * nj + j))
    wsp = lambda part: pl.BlockSpec((HY_SHORT, Wd), lambda j, b: (0, part * nj + j))
    bsp = lambda part: pl.BlockSpec((1, Wd), lambda j, b: (0, part * nj + j))
    full = lambda a: pl.BlockSpec(a.shape, lambda j, b: (0,) * a.ndim)
    consts = [cst[k] for k in ("f1", "ginv", "tr", "ti", "m2f", "m2i")]
    out = pl.pallas_call(
        functools.partial(_hy_mixer_kernel, n1, N1, cst["n1p"]),
        grid=(nj, B),
        in_specs=[chan(0), chan(1), chan(2), wsp(0), bsp(0), wsp(1), bsp(1), wsp(2), bsp(2),
                  pl.BlockSpec((HY_ORDER, Wd), lambda j, b: (0, j)),
                  pl.BlockSpec((HY_ORDER, 1, HY_CH_TILE * N1, 2 * W), lambda j, b: (0, j, 0, 0))]
                 + [full(a) for a in consts],
        out_specs=pl.BlockSpec((1, n1, Wd), lambda j, b: (b, 0, j)),
        out_shape=jax.ShapeDtypeStruct((B, n1, C * W), F32),
        compiler_params=_cparams("arbitrary", "arbitrary"),
        name="hyena_mixer",
    )(hy2, hy2, hy2, sw, sb, sw, sb, sw, sb, sk, kf, *consts)
    return out.reshape(B, n1, C, W)


def kernel(x, c, ctx, c_ctx, w_mod, b_mod, w_in, na_rpb, gla_w_a2, gla_b_a2, gla_norm_g, hy_short_w, hy_short_b, hy_w1, hy_b1, hy_f1, hy_w2, hy_b2, hy_f2, hy_w3, hy_b3, hy_decay, hy_skip, w_out, ln1_g, ln1_b, router_w, router_b, we_gate, we_up, we_down, ws_gate, ws_up, ws_down, ln2_g, ln2_b):
    B, L, D = x.shape
    C = ctx.shape[1]
    assert D == D_MODEL and L % TOK_TILE == 0 and C % HY_LANES == 0 and C <= TOK_TILE and w_mod.shape[0] == DEPTH
    rope_cos, rope_sin = _rope_tables(L)
    ctx_cos, ctx_sin = jnp.ones((C, GLA_KEY_WIDTH), F32), jnp.zeros((C, GLA_KEY_WIDTH), F32)
    zero_state = jnp.zeros((B, GLA_KEY_WIDTH, GLA_WIDTH), F32)
    na_bias = _na_bias_table(na_rpb)

    n_mod = -(-(B + 1) // 8) * 8
    cs = jnp.zeros((n_mod, D), F32).at[:B].set(c).at[B].set(c_ctx)
    mod_all = _modulation(cs, w_mod, b_mod)

    xc = ctx
    for l in range(DEPTH):
        last = l == DEPTH - 1
        mods = mod_all[l].reshape(n_mod, 6, 1, D)
        lat = lambda j: mods[:B, j]
        cm = lambda j: mods[B:B + 1, j]
        n_tok = NA_COLS + GLA_COLS
        w_pad = jnp.pad(w_in[l][:, :n_tok], ((0, 0), (0, D_IN_PAD - n_tok))).astype(BF16)
        w_hy_t = w_in[l][:, n_tok:].T.astype(BF16)
        w_out_b = w_out[l].astype(BF16)
        lg1, lb1 = ln1_g[l].reshape(1, D), ln1_b[l].reshape(1, D)
        lg2, lb2 = ln2_g[l].reshape(1, D), ln2_b[l].reshape(1, D)

        u_na, u_rest, hy = _inproj(x, lat(1), lat(0), w_pad, w_hy_t, True)
        uc_na, uc_rest, hyc = _inproj(xc, cm(1), cm(0), w_pad, w_hy_t, False)

        na_lat = _na_attention(u_na, uc_na, na_bias, l)

        gla_c, s_ctx_f, s_ctx_b = _gla_bidir(uc_rest, ctx_cos, ctx_sin, gla_w_a2[l], gla_b_a2[l], gla_norm_g[l],
                                             zero_state, zero_state)
        gla_lat, _, _ = _gla_bidir(u_rest, rope_cos, rope_sin, gla_w_a2[l], gla_b_a2[l], gla_norm_g[l],
                                   s_ctx_f, s_ctx_b)

        filt_args = (hy_w1[l], hy_b1[l], hy_f1[l], hy_w2[l], hy_b2[l], hy_f2[l], hy_w3[l], hy_b3[l], hy_decay[l])
        hy_lat = _hyena_mixer(hy, hy_short_w[l], hy_short_b[l], filt_args, hy_skip[l], 2 * L // HY_LANES)

        wr_t = router_w[l].T
        x, h_lat, logit_lat = _outproj(na_lat, gla_lat, hy_lat, x, lat(2), w_out_b, lg1, lb1, lat(4), lat(3), wr_t, True)
        if not last:
            na_c = _ctx_attention(uc_na)
            hy_c = _hyena_mixer(hyc, hy_short_w[l], hy_short_b[l], filt_args, hy_skip[l], HY_CTX_N1)
            xc, h_c, logit_c = _outproj(na_c, gla_c, hy_c, xc, cm(2), w_out_b, lg1, lb1, cm(4), cm(3), wr_t, False)

            h_flat = jnp.concatenate([h_lat.reshape(B * L, D), h_c.reshape(B * C, D)], axis=0)
            logit_t = jnp.concatenate([logit_lat, logit_c], axis=1)
        else:
            h_flat = h_lat.reshape(B * L, D)
            logit_t = logit_lat

        picked, gates = _moe_routed(h_flat, logit_t, router_b[l], we_gate, we_up, we_down, l)
        wsg, wsu, wsd = ws_gate[l].astype(BF16), ws_up[l].astype(BF16), ws_down[l].astype(BF16)
        x = _shared_ln2(x, h_lat, picked, gates, 0, lat(5), wsg, wsu, wsd, lg2, lb2, True)
        if not last:
            xc = _shared_ln2(xc, h_c, picked, gates, B * L, cm(5), wsg, wsu, wsd, lg2, lb2, False)
    return x
```

```python
import functools
import math

import numpy as np
import jax
import jax.numpy as jnp
from jax import lax
from jax.experimental import pallas as pl
from jax.experimental.pallas import tpu as pltpu
from jax.experimental.pallas import tpu_sc as plsc

F32 = jnp.float32
BF16 = jnp.bfloat16

D_MODEL = 1024
DEPTH = 4
GRID_W = 64
CTX_LEN = 256

NA_HEADS = 8
NA_HEAD_DIM = 64
NA_WIDTH = NA_HEADS * NA_HEAD_DIM
NA_WIN_ROWS = 8
NA_WIN_COLS = 16

GLA_HEADS = 4
GLA_DK = 32
GLA_DV = 64
GLA_KEY_WIDTH = GLA_HEADS * GLA_DK
GLA_WIDTH = GLA_HEADS * GLA_DV
GLA_RANK = 16
GLA_GATE_NORM = 16.0
GLA_CHUNK = 64

HY_CH = 256
HY_ORDER = 2
HY_SHORT = 3
HY_BANDS = 16
HY_EMB = 1 + 2 * HY_BANDS

MIX_WIDTH = NA_WIDTH + GLA_WIDTH + HY_CH
IN_SPLITS = (NA_WIDTH, NA_WIDTH, NA_WIDTH, GLA_KEY_WIDTH, GLA_KEY_WIDTH, GLA_WIDTH, GLA_WIDTH,
             2 * GLA_RANK, (HY_ORDER + 1) * HY_CH)
ROPE_BASE = 10000.0

N_EXPERTS = 128
TOP_K = 8
N_GROUPS = 8
TOPK_GROUPS = 4
EXPERT_HIDDEN = 256
ROUTED_SCALE = 2.5

DEEPNORM_ALPHA = (2 * DEPTH) ** 0.25
LN_EPS = 1e-6

LANE = 128
MXU_DIM = 256
VMEM_LIMIT = 48 * 1024 * 1024
SC_CORES = 2
SC_SUBCORES = 16
SC_LANES = 16
SC_SCAN_CHUNK = 16384

NA_COLS = 3 * NA_WIDTH
GLA_COLS = sum(IN_SPLITS[3:8])
HY_COLS = IN_SPLITS[8]
REST_COLS = -(-GLA_COLS // LANE) * LANE
D_IN_PAD = NA_COLS + REST_COLS
NA_QUAD = MXU_DIM // NA_HEAD_DIM
NEG_BIG = -1e30

TOK_TILE = 512
MOE_TILE = 512
NA_ROW_TILE = 8
NA_ROW_UNROLL = 8
ROUTE_TILE = 256
GLA_TILE = 512
GLA_BATCH = 2


def _cparams(*sem):
    return pltpu.CompilerParams(dimension_semantics=sem, vmem_limit_bytes=VMEM_LIMIT)


def _mod_kernel(c_ref, w_ref, b_ref, o_ref):
    c = c_ref[...]
    s = c * jax.nn.sigmoid(c)
    o_ref[0] = jnp.dot(s, w_ref[0], preferred_element_type=F32, precision=lax.Precision.HIGHEST) + b_ref[0]


def _modulation(cs, w_mod, b_mod):
    R = cs.shape[0]
    tn = 1536
    return pl.pallas_call(
        _mod_kernel,
        grid=(DEPTH, 6 * D_MODEL // tn),
        in_specs=[pl.BlockSpec((R, D_MODEL), lambda l, j: (0, 0)),
                  pl.BlockSpec((1, D_MODEL, tn), lambda l, j: (l, 0, j)),
                  pl.BlockSpec((1, 1, tn), lambda l, j: (l, 0, j))],
        out_specs=pl.BlockSpec((1, R, tn), lambda l, j: (l, 0, j)),
        out_shape=jax.ShapeDtypeStruct((DEPTH, R, 6 * D_MODEL), F32),
        compiler_params=_cparams("arbitrary", "arbitrary"),
        name="modulation",
    )(cs, w_mod, b_mod.reshape(DEPTH, 1, 6 * D_MODEL))


def _inproj_kernel(x_ref, sc_ref, sh_ref, w_ref, wh_ref, ona_ref, orest_ref, ohy_ref):
    xm = (x_ref[0] * (1.0 + sc_ref[0]) + sh_ref[0]).astype(BF16)
    step = 512
    for c0 in range(0, NA_COLS, step):
        ona_ref[0, :, c0:c0 + step] = jnp.dot(xm, w_ref[:, c0:c0 + step], preferred_element_type=F32).astype(BF16)
    for c0 in range(0, REST_COLS, step):
        c1 = min(c0 + step, REST_COLS)
        orest_ref[0, :, c0:c1] = jnp.dot(xm, w_ref[:, NA_COLS + c0:NA_COLS + c1], preferred_element_type=F32)
    hy = lax.dot_general(wh_ref[...], xm, _NT, preferred_element_type=F32)
    for j in range(ohy_ref.shape[1]):
        ohy_ref[0, j] = hy[:, j * LANE:(j + 1) * LANE]


def _inproj(x, sc, sh, w_pad, w_hy_t, per_batch_mod):
    B, T, D = x.shape
    tm = min(TOK_TILE, T)
    mod_idx = (lambda b, i: (b, 0, 0)) if per_batch_mod else (lambda b, i: (0, 0, 0))
    return pl.pallas_call(
        _inproj_kernel,
        grid=(B, T // tm),
        in_specs=[pl.BlockSpec((1, tm, D), lambda b, i: (b, i, 0)),
                  pl.BlockSpec((1, 1, D), mod_idx),
                  pl.BlockSpec((1, 1, D), mod_idx),
                  pl.BlockSpec((D, D_IN_PAD), lambda b, i: (0, 0)),
                  pl.BlockSpec((HY_COLS, D), lambda b, i: (0, 0))],
        out_specs=[pl.BlockSpec((1, tm, NA_COLS), lambda b, i: (b, i, 0)),
                   pl.BlockSpec((1, tm, REST_COLS), lambda b, i: (b, i, 0)),
                   pl.BlockSpec((1, tm // LANE, HY_COLS, LANE), lambda b, i: (b, i, 0, 0))],
        out_shape=[jax.ShapeDtypeStruct((B, T, NA_COLS), BF16),
                   jax.ShapeDtypeStruct((B, T, REST_COLS), F32),
                   jax.ShapeDtypeStruct((B, T // LANE, HY_COLS, LANE), F32)],
        compiler_params=_cparams("arbitrary", "arbitrary"),
        name="inproj",
    )(x, sc, sh, w_pad, w_hy_t)


def _stack_heads(q, n_rows):
    head = lax.broadcasted_iota(jnp.int32, (n_rows, MXU_DIM), 1) // NA_HEAD_DIM
    return jnp.concatenate([jnp.where(head == h, q, jnp.zeros_like(q)) for h in range(NA_QUAD)], axis=0)


def _unstack_heads(o, n_rows):
    head = lax.broadcasted_iota(jnp.int32, (n_rows, MXU_DIM), 1) // NA_HEAD_DIM
    out = jnp.zeros((n_rows, MXU_DIM), F32)
    for h in range(NA_QUAD):
        out = jnp.where(head == h, o[h * n_rows:(h + 1) * n_rows], out)
    return out


_NT = (((1,), (1,)), ((), ()))


def _na_kernel(q_ref, k_ref, v_ref, kc_ref, vc_ref, bias_ref, o_ref):
    rt = pl.program_id(2)
    scale = NA_HEAD_DIM ** -0.5
    kc = kc_ref[0]
    vc = vc_ref[0]
    n_loc = NA_WIN_ROWS * GRID_W

    def row(rl, carry):
        r = rt * NA_ROW_TILE + rl
        kr0 = jnp.clip(r - NA_WIN_ROWS // 2, 0, GRID_W - NA_WIN_ROWS)
        dr0 = kr0 - r + NA_WIN_ROWS - 1
        q = q_ref[0, pl.ds(pl.multiple_of(rl * GRID_W, GRID_W), GRID_W), :]
        qs = _stack_heads(q, GRID_W)
        k0 = pl.multiple_of(kr0 * GRID_W, GRID_W)
        ks = k_ref[0, pl.ds(k0, n_loc), :]
        vs = v_ref[0, pl.ds(k0, n_loc), :]
        s_loc = lax.dot_general(qs, ks, _NT, preferred_element_type=F32) * scale + bias_ref[0, dr0]
        s_ctx = lax.dot_general(qs, kc, _NT, preferred_element_type=F32) * scale
        m = jnp.maximum(jnp.max(s_loc, axis=-1, keepdims=True), jnp.max(s_ctx, axis=-1, keepdims=True))
        p_loc = jnp.exp(s_loc - m)
        p_ctx = jnp.exp(s_ctx - m)
        den = jnp.sum(p_loc, axis=-1, keepdims=True) + jnp.sum(p_ctx, axis=-1, keepdims=True)
        o = (jnp.dot(p_loc.astype(BF16), vs, preferred_element_type=F32)
             + jnp.dot(p_ctx.astype(BF16), vc, preferred_element_type=F32)) / den
        o_ref[0, pl.ds(pl.multiple_of(rl * GRID_W, GRID_W), GRID_W), :] = _unstack_heads(o, GRID_W)
        return carry

    lax.fori_loop(0, NA_ROW_TILE, row, 0, unroll=NA_ROW_UNROLL)


def _na_bias_table(rpb):
    n_lyr = rpb.shape[0]
    c = np.arange(GRID_W)
    kc0 = np.clip(c - NA_WIN_COLS // 2, 0, GRID_W - NA_WIN_COLS)
    kc = np.arange(GRID_W)
    valid = (kc[None, :] >= kc0[:, None]) & (kc[None, :] < kc0[:, None] + NA_WIN_COLS)
    dc = kc[None, :] - c[:, None] + NA_WIN_COLS - 1
    onehot = (np.arange(2 * NA_WIN_COLS - 1)[:, None, None] == dc[None]) & valid[None]
    toep = jnp.einsum('lhrd,dck->lhrck', rpb, jnp.asarray(onehot, F32), precision=lax.Precision.HIGHEST)
    toep = jnp.where(jnp.asarray(valid)[None, None, None], toep, NEG_BIG)
    tab = jnp.stack([toep[:, :, d:d + NA_WIN_ROWS] for d in range(NA_WIN_ROWS)], axis=2)
    tab = tab.transpose(0, 1, 2, 4, 3, 5).reshape(n_lyr, NA_HEADS // NA_QUAD, NA_QUAD, NA_WIN_ROWS, GRID_W,
                                                  NA_WIN_ROWS * GRID_W)
    return tab.transpose(0, 1, 3, 2, 4, 5).reshape(n_lyr, NA_HEADS // NA_QUAD, NA_WIN_ROWS, NA_QUAD * GRID_W,
                                                   NA_WIN_ROWS * GRID_W)


def _na_attention(u_na, uc_na, bias_tab, layer):
    B, L, _ = u_na.shape
    C = uc_na.shape[1]
    assert L == GRID_W * GRID_W and GRID_W % NA_ROW_TILE == 0
    nq = NA_WIDTH // MXU_DIM
    tq = NA_ROW_TILE * GRID_W
    return pl.pallas_call(
        _na_kernel,
        grid=(B, nq, L // tq),
        in_specs=[pl.BlockSpec((1, tq, MXU_DIM), lambda b, j, i: (b, i, j)),
                  pl.BlockSpec((1, L, MXU_DIM), lambda b, j, i: (b, 0, nq + j)),
                  pl.BlockSpec((1, L, MXU_DIM), lambda b, j, i: (b, 0, 2 * nq + j)),
                  pl.BlockSpec((1, C, MXU_DIM), lambda b, j, i: (b, 0, nq + j)),
                  pl.BlockSpec((1, C, MXU_DIM), lambda b, j, i: (b, 0, 2 * nq + j)),
                  pl.BlockSpec((None, 1, NA_WIN_ROWS, NA_QUAD * GRID_W, NA_WIN_ROWS * GRID_W),
                               lambda b, j, i: (layer, j, 0, 0, 0))],
        out_specs=pl.BlockSpec((1, tq, MXU_DIM), lambda b, j, i: (b, i, j)),
        out_shape=jax.ShapeDtypeStruct((B, L, NA_WIDTH), F32),
        compiler_params=_cparams("arbitrary", "arbitrary", "arbitrary"),
        name="na_attention",
    )(u_na, u_na, u_na, uc_na, uc_na, bias_tab)


def _ctx_attn_kernel(q_ref, k_ref, v_ref, o_ref):
    C = q_ref.shape[1]
    qs = _stack_heads(q_ref[0], C)
    s = lax.dot_general(qs, k_ref[0], _NT, preferred_element_type=F32) * NA_HEAD_DIM ** -0.5
    p = jnp.exp(s - jnp.max(s, axis=-1, keepdims=True))
    den = jnp.sum(p, axis=-1, keepdims=True)
    o = jnp.dot(p.astype(BF16), v_ref[0], preferred_element_type=F32) / den
    o_ref[0] = _unstack_heads(o, C)


def _ctx_attention(uc_na):
    B, C, _ = uc_na.shape
    nq = NA_WIDTH // MXU_DIM
    return pl.pallas_call(
        _ctx_attn_kernel,
        grid=(B, nq),
        in_specs=[pl.BlockSpec((1, C, MXU_DIM), lambda b, j: (b, 0, j)),
                  pl.BlockSpec((1, C, MXU_DIM), lambda b, j: (b, 0, nq + j)),
                  pl.BlockSpec((1, C, MXU_DIM), lambda b, j: (b, 0, 2 * nq + j))],
        out_specs=pl.BlockSpec((1, C, MXU_DIM), lambda b, j: (b, 0, j)),
        out_shape=jax.ShapeDtypeStruct((B, C, NA_WIDTH), F32),
        compiler_params=_cparams("arbitrary", "arbitrary"),
        name="ctx_attention",
    )(uc_na, uc_na, uc_na)


def _layer_norm_rows(y, g, b):
    mu = jnp.mean(y, axis=-1, keepdims=True)
    d = y - mu
    var = jnp.mean(d * d, axis=-1, keepdims=True)
    return d * lax.rsqrt(var + LN_EPS) * g + b


def _outproj_kernel(na_ref, gla_ref, hy_ref, x_ref, g1_ref, w_ref, lg_ref, lb_ref, sc_ref, sh_ref, wr_ref,
                    xo_ref, h_ref, logit_ref):
    w_hy = w_ref[NA_WIDTH + GLA_WIDTH:, :]
    hy_mix = jnp.concatenate(
        [lax.dot_general(hy_ref[0, j].astype(BF16), w_hy, (((0,), (0,)), ((), ())), preferred_element_type=F32)
         for j in range(hy_ref.shape[1])], axis=0)
    mix = (jnp.dot(na_ref[0].astype(BF16), w_ref[0:NA_WIDTH, :], preferred_element_type=F32)
           + jnp.dot(gla_ref[0].astype(BF16), w_ref[NA_WIDTH:NA_WIDTH + GLA_WIDTH, :], preferred_element_type=F32)
           + hy_mix)
    xn = _layer_norm_rows(DEEPNORM_ALPHA * x_ref[0] + g1_ref[0] * mix, lg_ref[...], lb_ref[...])
    xo_ref[0] = xn
    h = xn * (1.0 + sc_ref[0]) + sh_ref[0]
    h_ref[0] = h.astype(BF16)
    logit_ref[...] = _dot_split(wr_ref[...], h, dims=_NT)


def _outproj(na, gla, hy, x, g1, w_out, ln_g, ln_b, sc2, sh2, w_router, per_batch_mod):
    B, T, D = x.shape
    tm = min(TOK_TILE, T)
    mod_idx = (lambda b, i: (b, 0, 0)) if per_batch_mod else (lambda b, i: (0, 0, 0))
    tok = lambda w: pl.BlockSpec((1, tm, w), lambda b, i: (b, i, 0))
    full = lambda s: pl.BlockSpec(s, lambda b, i: (0,) * len(s))
    mod = pl.BlockSpec((1, 1, D), mod_idx)
    return pl.pallas_call(
        _outproj_kernel,
        grid=(B, T // tm),
        in_specs=[tok(NA_WIDTH), tok(GLA_WIDTH),
                  pl.BlockSpec((1, tm // LANE, HY_CH, LANE), lambda b, i: (b, i, 0, 0)), tok(D), mod, full((MIX_WIDTH, D)),
                  full((1, D)), full((1, D)), mod, mod, full((N_EXPERTS, D))],
        out_specs=[tok(D), tok(D), pl.BlockSpec((N_EXPERTS, tm), lambda b, i: (0, b * (T // tm) + i))],
        out_shape=[jax.ShapeDtypeStruct((B, T, D), F32), jax.ShapeDtypeStruct((B, T, D), BF16),
                   jax.ShapeDtypeStruct((N_EXPERTS, B * T), F32)],
        compiler_params=_cparams("arbitrary", "arbitrary"),
        name="outproj_ln1",
    )(na, gla, hy, x, g1, w_out, ln_g, ln_b, sc2, sh2, w_router)


def _moe_kernel(te_ref, nt_ref, xs_ref, wg_ref, wu_ref, wd_ref, ys_ref, wg_s, wu_s, wd_s):
    i = pl.program_id(0)

    @pl.when(jnp.logical_or(i == 0, te_ref[i] != te_ref[jnp.maximum(i - 1, 0)]))
    def _():
        wg_s[...] = wg_ref[0].astype(BF16)
        wu_s[...] = wu_ref[0].astype(BF16)
        wd_s[...] = wd_ref[0].astype(BF16)

    @pl.when(i < nt_ref[0])
    def _():
        xs = xs_ref[...]
        g = jnp.dot(xs, wg_s[...], preferred_element_type=F32)
        u = jnp.dot(xs, wu_s[...], preferred_element_type=F32)
        a = (g * jax.nn.sigmoid(g) * u).astype(BF16)
        ys_ref[...] = jnp.dot(a, wd_s[...], preferred_element_type=F32).astype(ys_ref.dtype)

    @pl.when(i >= nt_ref[0])
    def _():
        ys_ref[...] = jnp.zeros_like(ys_ref)


def _moe_grouped(tile_expert, n_used, xs, we_g, we_u, we_d, layer):
    Mp, D = xs.shape
    n_tiles = Mp // MOE_TILE
    H = EXPERT_HIDDEN
    grid_spec = pltpu.PrefetchScalarGridSpec(
        num_scalar_prefetch=2,
        grid=(n_tiles,),
        in_specs=[pl.BlockSpec((MOE_TILE, D), lambda i, te, nt: (i, 0)),
                  pl.BlockSpec((None, 1, D, H), lambda i, te, nt: (layer, te[i], 0, 0)),
                  pl.BlockSpec((None, 1, D, H), lambda i, te, nt: (layer, te[i], 0, 0)),
                  pl.BlockSpec((None, 1, H, D), lambda i, te, nt: (layer, te[i], 0, 0))],
        out_specs=pl.BlockSpec((MOE_TILE, D), lambda i, te, nt: (i, 0)),
        scratch_shapes=[pltpu.VMEM((D, H), BF16), pltpu.VMEM((D, H), BF16), pltpu.VMEM((H, D), BF16)],
    )
    return pl.pallas_call(
        _moe_kernel,
        grid_spec=grid_spec,
        out_shape=jax.ShapeDtypeStruct((Mp, D), BF16),
        compiler_params=_cparams("arbitrary"),
        name="moe_experts",
    )(tile_expert, n_used, xs, we_g, we_u, we_d)


def _shared_kernel(x_ref, h_ref, pk_ref, gt_ref, g2_ref, wg_ref, wu_ref, wd_ref, lg_ref, lb_ref, o_ref):
    h = h_ref[0]
    g = jnp.dot(h, wg_ref[...], preferred_element_type=F32)
    u = jnp.dot(h, wu_ref[...], preferred_element_type=F32)
    a = (g * jax.nn.sigmoid(g) * u).astype(BF16)
    ff = jnp.dot(a, wd_ref[...], preferred_element_type=F32)
    gates = gt_ref[...]
    for k in range(TOP_K):
        ff = ff + pk_ref[k].astype(F32) * gates[:, k:k + 1]
    o_ref[0] = _layer_norm_rows(DEEPNORM_ALPHA * x_ref[0] + g2_ref[0] * ff, lg_ref[...], lb_ref[...])


def _shared_ln2(x, h, picked, gates, tok_off, g2, ws_g, ws_u, ws_d, ln_g, ln_b, per_batch_mod):
    B, T, D = x.shape
    tm = min(TOK_TILE, T)
    mod_idx = (lambda b, i: (b, 0, 0)) if per_batch_mod else (lambda b, i: (0, 0, 0))
    tok = pl.BlockSpec((1, tm, D), lambda b, i: (b, i, 0))
    full = lambda s: pl.BlockSpec(s, lambda b, i: (0,) * len(s))
    flat = lambda b, i: tok_off // tm + b * (T // tm) + i
    return pl.pallas_call(
        _shared_kernel,
        grid=(B, T // tm),
        in_specs=[tok, tok, pl.BlockSpec((TOP_K, tm, D), lambda b, i: (0, flat(b, i), 0)),
                  pl.BlockSpec((tm, TOP_K), lambda b, i: (flat(b, i), 0)),
                  pl.BlockSpec((1, 1, D), mod_idx), full((D, EXPERT_HIDDEN)), full((D, EXPERT_HIDDEN)),
                  full((EXPERT_HIDDEN, D)), full((1, D)), full((1, D))],
        out_specs=tok,
        out_shape=jax.ShapeDtypeStruct((B, T, D), F32),
        compiler_params=_cparams("arbitrary", "arbitrary"),
        name="shared_ln2",
    )(x, h, picked, gates, g2, ws_g, ws_u, ws_d, ln_g, ln_b)


def _first_max(vals, iota, n):
    m = jnp.max(vals, axis=0, keepdims=True)
    idx = jnp.min(jnp.where(vals == m, iota, n), axis=0, keepdims=True)
    return m, idx


def _route_kernel(lt_ref, b_ref, eidx_ref, w_ref, rank_ref, cnt_ref, base_ref):
    @pl.when(pl.program_id(0) == 0)
    def _():
        base_ref[...] = jnp.zeros_like(base_ref)

    tm = lt_ref.shape[1]
    per = N_EXPERTS // N_GROUPS
    s = jax.nn.sigmoid(lt_ref[...])
    sel = s + b_ref[...]
    io_g = lax.broadcasted_iota(jnp.int32, (per, tm), 0)
    scores = []
    for g in range(N_GROUPS):
        blk = sel[g * per:(g + 1) * per]
        m1, i1 = _first_max(blk, io_g, per)
        m2 = jnp.max(jnp.where(io_g == i1, -jnp.inf, blk), axis=0, keepdims=True)
        scores.append(m1 + m2)
    cur = jnp.concatenate(scores, axis=0)
    io_8 = lax.broadcasted_iota(jnp.int32, (N_GROUPS, tm), 0)
    gmask = jnp.zeros((N_GROUPS, tm), F32)
    for _ in range(TOPK_GROUPS):
        _, gi = _first_max(cur, io_8, N_GROUPS)
        hit = io_8 == gi
        gmask = jnp.where(hit, 1.0, gmask)
        cur = jnp.where(hit, -jnp.inf, cur)
    masked = jnp.concatenate(
        [jnp.where(gmask[g:g + 1] > 0.0, sel[g * per:(g + 1) * per], -jnp.inf) for g in range(N_GROUPS)], axis=0)
    io_e = lax.broadcasted_iota(jnp.int32, (N_EXPERTS, tm), 0)
    chosen = jnp.zeros((N_EXPERTS, tm), F32)
    eidx, gates = [], []
    for _ in range(TOP_K):
        _, ei = _first_max(masked, io_e, N_EXPERTS)
        hit = io_e == ei
        eidx.append(ei)
        gates.append(jnp.sum(jnp.where(hit, s, 0.0), axis=0, keepdims=True))
        masked = jnp.where(hit, -jnp.inf, masked)
        chosen = jnp.where(hit, 1.0, chosen)
    wk = jnp.concatenate(gates, axis=0)
    w_ref[...] = wk / jnp.sum(wk, axis=0, keepdims=True) * ROUTED_SCALE
    eidx_ref[...] = jnp.concatenate(eidx, axis=0)
    earlier = (lax.broadcasted_iota(jnp.int32, (tm, tm), 0) < lax.broadcasted_iota(jnp.int32, (tm, tm), 1))
    pos = jnp.dot(chosen.astype(BF16), jnp.where(earlier, 1.0, 0.0).astype(BF16), preferred_element_type=F32)
    pos = pos + base_ref[...]
    ranks = [jnp.sum(jnp.where(io_e == eidx[k], pos, 0.0), axis=0, keepdims=True) for k in range(TOP_K)]
    rank_ref[...] = jnp.concatenate(ranks, axis=0).astype(jnp.int32)
    base_ref[...] = base_ref[...] + jnp.sum(chosen, axis=1, keepdims=True)
    cnt_ref[...] = base_ref[...]


def _route(logits_t, b_corr):
    E, T = logits_t.shape
    tm = ROUTE_TILE
    tokk = pl.BlockSpec((TOP_K, tm), lambda i: (0, i))
    return pl.pallas_call(
        _route_kernel,
        grid=(T // tm,),
        in_specs=[pl.BlockSpec((E, tm), lambda i: (0, i)), pl.BlockSpec((E, 1), lambda i: (0, 0))],
        out_specs=[tokk, tokk, tokk, pl.BlockSpec((E, 1), lambda i: (0, 0))],
        out_shape=[jax.ShapeDtypeStruct((TOP_K, T), jnp.int32), jax.ShapeDtypeStruct((TOP_K, T), F32),
                   jax.ShapeDtypeStruct((TOP_K, T), jnp.int32), jax.ShapeDtypeStruct((E, 1), F32)],
        scratch_shapes=[pltpu.VMEM((E, 1), F32)],
        compiler_params=_cparams("arbitrary"),
        name="route",
    )(logits_t, b_corr.reshape(E, 1))


def _slot_kernel(eidx_ref, rank_ref, pstart_ref, dest_ref):
    tm = eidx_ref.shape[1]
    io_e = lax.broadcasted_iota(jnp.int32, (N_EXPERTS, tm), 0)
    ei = eidx_ref[...]
    starts = [jnp.sum(jnp.where(io_e == ei[k:k + 1], pstart_ref[...], 0.0), axis=0, keepdims=True)
              for k in range(TOP_K)]
    dest_ref[...] = jnp.concatenate(starts, axis=0).astype(jnp.int32) + rank_ref[...]


def _slots(eidx, rank, pstart):
    K, T = eidx.shape
    tm = ROUTE_TILE
    tokk = pl.BlockSpec((K, tm), lambda i: (0, i))
    return pl.pallas_call(
        _slot_kernel,
        grid=(T // tm,),
        in_specs=[tokk, tokk, pl.BlockSpec((N_EXPERTS, 1), lambda i: (0, 0))],
        out_specs=tokk,
        out_shape=jax.ShapeDtypeStruct((K, T), jnp.int32),
        compiler_params=_cparams("arbitrary"),
        name="route_slots",
    )(eidx, rank, pstart)


def _slot_tokens(dest, tok, n_slots, n_tok):
    n_asg = dest.shape[0]
    n_sub = SC_CORES * SC_SUBCORES
    per = n_slots // n_sub
    chunk = SC_SCAN_CHUNK
    assert n_slots % n_sub == 0 and per % SC_LANES == 0 and n_asg % chunk == 0 and chunk % SC_LANES == 0
    mesh = plsc.VectorSubcoreMesh(core_axis_name="c", subcore_axis_name="s", num_cores=SC_CORES, num_subcores=SC_SUBCORES)

    def body(dest_hbm, tok_hbm, out_hbm, loc, dbuf, tbuf):
        s0 = (lax.axis_index("c") * SC_SUBCORES + lax.axis_index("s")) * per
        lanes = lax.iota(jnp.int32, SC_LANES)

        @pl.loop(0, per, step=SC_LANES)
        def _(i):
            loc[pl.ds(i, SC_LANES)] = lax.rem(s0 + i, n_tok - SC_LANES) + lanes

        @pl.loop(0, n_asg, step=chunk)
        def _(c):
            pltpu.sync_copy(dest_hbm.at[pl.ds(c, chunk)], dbuf)
            pltpu.sync_copy(tok_hbm.at[pl.ds(c, chunk)], tbuf)

            @pl.loop(0, chunk, step=SC_LANES)
            def _(i):
                d = dbuf[pl.ds(i, SC_LANES)] - s0
                mine = jnp.logical_and(d >= 0, d < per)
                plsc.store_scatter(loc, [d], tbuf[pl.ds(i, SC_LANES)], mask=mine)

        pltpu.sync_copy(loc, out_hbm.at[pl.ds(s0, per)])

    return pl.kernel(
        body, out_type=jax.ShapeDtypeStruct((n_slots,), jnp.int32), mesh=mesh,
        scratch_types=[pltpu.VMEM((per,), jnp.int32), pltpu.VMEM((chunk,), jnp.int32), pltpu.VMEM((chunk,), jnp.int32)],
        compiler_params=pltpu.CompilerParams(needs_layout_passes=False),
        name="slot_tokens",
    )(dest, tok)


def _moe_routed(h_flat, logits_t, b_corr, we_g, we_u, we_d, layer):
    T = h_flat.shape[0]
    n_tiles = T * TOP_K // MOE_TILE + N_EXPERTS
    eidx, gates, rank, counts = _route(logits_t, b_corr)
    counts = counts[:, 0].astype(jnp.int32)
    padded = (counts + MOE_TILE - 1) // MOE_TILE * MOE_TILE
    pends = jnp.cumsum(padded)
    tile_start = jnp.arange(n_tiles, dtype=jnp.int32) * MOE_TILE
    tile_expert = jnp.minimum(jnp.sum((pends[None, :] <= tile_start[:, None]).astype(jnp.int32), axis=1), N_EXPERTS - 1)
    n_used = (pends[-1] // MOE_TILE).astype(jnp.int32).reshape(1)
    dest = _slots(eidx, rank, (pends - padded).astype(F32).reshape(N_EXPERTS, 1))
    tok = jnp.broadcast_to(jnp.arange(T, dtype=jnp.int32)[None], (TOP_K, T))
    src_tok = _slot_tokens(dest.reshape(-1), tok.reshape(-1), n_tiles * MOE_TILE, T)
    xs = h_flat.at[src_tok].get(mode='promise_in_bounds')
    ys = _moe_grouped(tile_expert, n_used, xs, we_g, we_u, we_d, layer)
    return ys.at[dest].get(mode='promise_in_bounds'), gates.T


_GLA_QK_BLK, _GLA_V_BLK, _GLA_R_BLK = 0, 1, 2
_GLA_LR_BLK = (2 * GLA_KEY_WIDTH + 2 * GLA_WIDTH) // LANE


def _rope_tables(L):
    t = np.arange(L)
    lane = np.arange(GLA_KEY_WIDTH)
    d = lane % GLA_DK
    pos = np.where(d[None, :] < GLA_DK // 2, (t // GRID_W)[:, None], (t % GRID_W)[:, None]).astype(np.float32)
    quarter = GLA_DK // 4
    inv = ROPE_BASE ** (-jnp.arange(quarter, dtype=F32) / quarter)
    ang = jnp.asarray(pos) * inv[jnp.asarray(d % quarter)][None, :]
    sign = np.where(d % (2 * quarter) < quarter, -1.0, 1.0).astype(np.float32)
    return jnp.cos(ang), jnp.sin(ang) * sign[None, :]


def _rope_partner(x):
    lane = lax.broadcasted_iota(jnp.int32, x.shape, 1)
    quarter = GLA_DK // 4
    return jnp.where(lane % (2 * quarter) < quarter, pltpu.roll(x, GLA_KEY_WIDTH - quarter, 1), pltpu.roll(x, quarter, 1))


_NN = (((1,), (0,)), ((), ()))


def _dot_split(a, b, rhs_exact=False, dims=_NN):
    dot = lambda x, y: lax.dot_general(x, y, dims, preferred_element_type=F32)
    a_hi = a.astype(BF16)
    a_lo = (a - a_hi.astype(F32)).astype(BF16)
    b_hi = b.astype(BF16)
    out = dot(a_hi, b_hi) + dot(a_lo, b_hi)
    if not rhs_exact:
        b_lo = (b - b_hi.astype(F32)).astype(BF16)
        out = out + dot(a_hi, b_lo)
    return out


def _log_sigmoid(x):
    return jnp.minimum(x, 0.0) - jnp.log(1.0 + jnp.exp(-jnp.abs(x)))


def _gla_kernel(reverse, finalize, *refs):
    if finalize:
        (qk_ref, v_ref, lr_ref, cos_ref, sin_ref, wa_ref, ba_ref, s0_ref, of_ref, r_ref, g_ref,
         o_ref, sfin_ref, s_scr) = refs
    else:
        qk_ref, v_ref, lr_ref, cos_ref, sin_ref, wa_ref, ba_ref, s0_ref, o_ref, sfin_ref, s_scr = refs

    @pl.when(pl.program_id(1) == 0)
    def _():
        s_scr[...] = s0_ref[...]

    C = GLA_CHUNK
    KW, VW = GLA_KEY_WIDTH, GLA_WIDTH
    cos, sin = cos_ref[...], sin_ref[...]

    ri = lax.broadcasted_iota(jnp.int32, (C, C), 0)
    ci = lax.broadcasted_iota(jnp.int32, (C, C), 1)
    tri = jnp.where((ci >= ri) if reverse else (ci <= ri), 1.0, 0.0)
    tri_h = jnp.concatenate([tri] * GLA_HEADS, axis=0)
    head_k = lax.broadcasted_iota(jnp.int32, (C, KW), 1) // GLA_DK
    head_v = lax.broadcasted_iota(jnp.int32, (C, VW), 1) // GLA_DV
    own_block = (lax.broadcasted_iota(jnp.int32, (KW, VW), 0) // GLA_DK
                 == lax.broadcasted_iota(jnp.int32, (KW, VW), 1) // GLA_DV)
    eye = lax.broadcasted_iota(jnp.int32, (KW, KW), 0) == lax.broadcasted_iota(jnp.int32, (KW, KW), 1)
    for bb in range(qk_ref.shape[0]):
        _gla_sample(reverse, finalize, bb, refs, cos, sin, tri_h, head_k, head_v, own_block, eye)


def _gla_sample(reverse, finalize, bb, refs, cos, sin, tri_h, head_k, head_v, own_block, eye):
    if finalize:
        (qk_ref, v_ref, lr_ref, _, _, wa_ref, ba_ref, _, of_ref, r_ref, g_ref, o_ref, sfin_ref, s_scr) = refs
    else:
        qk_ref, v_ref, lr_ref, _, _, wa_ref, ba_ref, _, o_ref, sfin_ref, s_scr = refs
    tg = qk_ref.shape[1]
    C = GLA_CHUNK
    KW, VW = GLA_KEY_WIDTH, GLA_WIDTH
    qk = qk_ref[bb]
    q = qk[:, :KW] * GLA_DK ** -0.5
    k = qk[:, KW:]
    q = q * cos + _rope_partner(q) * sin
    k = k * cos + _rope_partner(k) * sin
    v = v_ref[bb]
    logit = _dot_split(lr_ref[bb], wa_ref[...]) + ba_ref[...]
    la = _log_sigmoid(logit) / GLA_GATE_NORM
    pos = lax.broadcasted_iota(jnp.int32, (tg, KW), 0) % C
    cum = la
    step = 1
    while step < C:
        if reverse:
            cum = cum + jnp.where(pos < C - step, pltpu.roll(cum, tg - step, 0), 0.0)
        else:
            cum = cum + jnp.where(pos >= step, pltpu.roll(cum, step, 0), 0.0)
        step *= 2

    S = s_scr[bb]
    nc = tg // C
    outs = [None] * nc
    for c in (range(nc - 1, -1, -1) if reverse else range(nc)):
        sl = slice(c * C, (c + 1) * C)
        b = cum[sl]
        qt = q[sl] * jnp.exp(b)
        kt = k[sl] * jnp.exp(-b)
        qs = jnp.concatenate([jnp.where(head_k == h, qt, 0.0) for h in range(GLA_HEADS)], axis=0).astype(BF16)
        att = lax.dot_general(qs, kt.astype(BF16), _NT, preferred_element_type=F32)
        att = jnp.where(tri_h > 0.0, att, 0.0)
        vb = v[sl].astype(BF16)
        oi = jnp.dot(att.astype(BF16), vb, preferred_element_type=F32)
        o_intra = jnp.zeros((C, VW), F32)
        for h in range(GLA_HEADS):
            o_intra = jnp.where(head_v == h, oi[h * C:(h + 1) * C], o_intra)
        o_inter = jnp.dot(qt.astype(BF16), S.astype(BF16), preferred_element_type=F32)
        outs[c] = o_intra + o_inter
        b_last = b[0:1] if reverse else b[C - 1:C]
        kdec = (k[sl] * jnp.exp(b_last - b)).astype(BF16)
        kv = lax.dot_general(kdec, vb, (((0,), (0,)), ((), ())), preferred_element_type=F32)
        decay_col = jnp.sum(jnp.where(eye, jnp.exp(b_last), 0.0), axis=1, keepdims=True)
        S = decay_col * S + jnp.where(own_block, kv, 0.0)
    s_scr[bb] = S
    sfin_ref[bb] = S
    o = jnp.concatenate(outs, axis=0)
    if finalize:
        o = of_ref[bb] + o
        same_head = (lax.broadcasted_iota(jnp.int32, (VW, VW), 0) // GLA_DV
                     == lax.broadcasted_iota(jnp.int32, (VW, VW), 1) // GLA_DV)
        ms = _dot_split(o * o, jnp.where(same_head, 1.0 / GLA_DV, 0.0), rhs_exact=True)
        r = r_ref[bb]
        o = o * lax.rsqrt(ms + LN_EPS) * g_ref[...] * (r * jax.nn.sigmoid(r))
    o_ref[bb] = o


def _gla_pass(u_rest, cos, sin, wa, ba, s0, reverse, fin=None):
    B, T, _ = u_rest.shape
    tg = min(GLA_TILE, T)
    n = T // tg
    ti = (lambda i: n - 1 - i) if reverse else (lambda i: i)
    KW, VW = GLA_KEY_WIDTH, GLA_WIDTH
    gb = GLA_BATCH if B % GLA_BATCH == 0 else 1
    ublk = lambda w, j: pl.BlockSpec((gb, tg, w), lambda b, i: (b, ti(i), j))
    full = lambda s: pl.BlockSpec(s, lambda b, i: (0,) * len(s))
    state = pl.BlockSpec((gb, KW, VW), lambda b, i: (b, 0, 0))
    tab = pl.BlockSpec((tg, KW), lambda b, i: (ti(i), 0))
    in_specs = [ublk(2 * KW, _GLA_QK_BLK), ublk(VW, _GLA_V_BLK), ublk(LANE, _GLA_LR_BLK), tab, tab,
                full((LANE, KW)), full((1, KW)), state]
    args = [u_rest, u_rest, u_rest, cos, sin, wa, ba, s0]
    if fin is not None:
        in_specs += [ublk(VW, 0), ublk(VW, _GLA_R_BLK), full((1, VW))]
        args += [fin[0], u_rest, fin[1]]
    return pl.pallas_call(
        functools.partial(_gla_kernel, reverse, fin is not None),
        grid=(B // gb, n),
        in_specs=in_specs,
        out_specs=[ublk(VW, 0), state],
        out_shape=[jax.ShapeDtypeStruct((B, T, VW), F32), jax.ShapeDtypeStruct((B, KW, VW), F32)],
        scratch_shapes=[pltpu.VMEM((gb, KW, VW), F32)],
        compiler_params=_cparams("arbitrary", "arbitrary"),
        name="gla_bwd" if reverse else "gla_fwd",
    )(*args)


def _gla_bidir(u_rest, cos, sin, w_a2, b_a2, norm_g, s0_f, s0_b):
    def decay_w(d):
        return jnp.zeros((LANE, GLA_KEY_WIDTH), F32).at[d * GLA_RANK:(d + 1) * GLA_RANK].set(w_a2[d])

    o_f, s_f = _gla_pass(u_rest, cos, sin, decay_w(0), b_a2[0:1], s0_f, False)
    g = jnp.tile(norm_g, GLA_HEADS).reshape(1, GLA_WIDTH)
    o, s_b = _gla_pass(u_rest, cos, sin, decay_w(1), b_a2[1:2], s0_b, True, (o_f, g))
    return o, s_f, s_b


HY_LANES = LANE
HY_CH_TILE = 16
HY_CTX_N1 = 16


def _hy_consts(n1, N1):
    W = HY_LANES
    n1p = max(n1, 16)
    n1o = max(n1, 8)
    a = np.arange(N1)
    th1 = 2.0 * np.pi * ((a[:, None] * a[None, :]) % N1) / N1
    f1 = np.zeros((2 * N1, n1p)); f1[:N1, :n1] = np.cos(th1[:, :n1]); f1[N1:, :n1] = -np.sin(th1[:, :n1])
    f1_full = np.concatenate([np.cos(th1), -np.sin(th1)], axis=0)
    ginv = np.zeros((n1o, 2 * N1)); ginv[:n1, :N1] = np.cos(th1.T[:n1]); ginv[:n1, N1:] = -np.sin(th1.T[:n1])
    r = np.arange(W)
    tht = 2.0 * np.pi * ((a[:, None] * r[None, :]) % (N1 * W)) / (N1 * W)
    tr = np.tile(np.cos(tht), (1, HY_CH_TILE)); ti = np.tile(-np.sin(tht), (1, HY_CH_TILE))
    th2 = 2.0 * np.pi * ((r[:, None] * r[None, :]) % W) / W
    c2, s2 = np.cos(th2), -np.sin(th2)
    m2f = np.block([[c2, s2], [-s2, c2]])
    m2i = np.block([[c2, -s2], [s2, c2]])
    f = lambda m: jnp.asarray(m, F32)
    return dict(n1p=n1p, n1o=n1o, f1=f(f1), f1_full=f(f1_full), ginv=f(ginv), tr=f(tr), ti=f(ti), m2f=f(m2f), m2i=f(m2i))


def _short_conv_rows(u, w_ref, b_ref, n_rows):
    R, Wd = u.shape
    lane = lax.broadcasted_iota(jnp.int32, (R, Wd), 1) % HY_LANES
    row = lax.broadcasted_iota(jnp.int32, (R, Wd), 0)
    up = jnp.where(row == 0, 0.0, pltpu.roll(u, 1, 0))
    dn = jnp.where(row == n_rows - 1, 0.0, pltpu.roll(u, R - 1, 0))
    prev = jnp.where(lane == 0, pltpu.roll(up, Wd - (HY_LANES - 1), 1), pltpu.roll(u, 1, 1))
    nxt = jnp.where(lane == HY_LANES - 1, pltpu.roll(dn, HY_LANES - 1, 1), pltpu.roll(u, Wd - 1, 1))
    return b_ref[...] + prev * w_ref[0:1] + u * w_ref[1:2] + nxt * w_ref[2:3]


def _pad_rows(u, rows):
    return u if u.shape[0] == rows else jnp.concatenate([u, jnp.zeros((rows - u.shape[0], u.shape[1]), u.dtype)], axis=0)


def _hy_chunk_dft(z, f1, tr, ti, N1, prec):
    if prec is None:
        a = jnp.dot(f1.astype(BF16), z.astype(BF16), preferred_element_type=F32)
    else:
        a = jnp.dot(f1, z, preferred_element_type=F32, precision=prec)
    ar, ai = a[:N1], a[N1:]
    a_re, a_im = ar * tr - ai * ti, ar * ti + ai * tr
    W = HY_LANES
    return jnp.concatenate(
        [jnp.concatenate([a_re[:, c * W:(c + 1) * W], a_im[:, c * W:(c + 1) * W]], axis=1) for c in range(z.shape[1] // W)],
        axis=0)


def _hy_long_conv(z, kf, f1, ginv, tr, ti, m2f, m2i, N1):
    W = HY_LANES
    m = z.shape[1] // W
    a = _hy_chunk_dft(z, f1, tr, ti, N1, None).astype(BF16)
    x = jnp.dot(a, m2f.astype(BF16), preferred_element_type=F32)
    xr, xi = x[:, :W], x[:, W:]
    kr, ki = kf[:, :W], kf[:, W:]
    y = jnp.concatenate([xr * kr - xi * ki, xr * ki + xi * kr], axis=1).astype(BF16)
    p = jnp.dot(y, m2i.astype(BF16), preferred_element_type=F32)
    pr = jnp.concatenate([p[c * N1:(c + 1) * N1, :W] for c in range(m)], axis=1)
    pi = jnp.concatenate([p[c * N1:(c + 1) * N1, W:] for c in range(m)], axis=1)
    q = jnp.concatenate([pr * tr + pi * ti, pi * tr - pr * ti], axis=0).astype(BF16)
    return jnp.dot(ginv.astype(BF16), q, preferred_element_type=F32)


def _hy_mixer_kernel(n1, N1, n1p, v_ref, x1_ref, x2_ref, swv_ref, sbv_ref, swx1_ref, sbx1_ref, swx2_ref, sbx2_ref,
                     skip_ref, kf_ref, f1_ref, ginv_ref, tr_ref, ti_ref, m2f_ref, m2i_ref, o_ref):
    rows = max(n1, 8)
    cst = (f1_ref[...], ginv_ref[...], tr_ref[...], ti_ref[...], m2f_ref[...], m2i_ref[...], N1)
    z = _short_conv_rows(_pad_rows(v_ref[0], rows), swv_ref, sbv_ref, n1)
    gates = (_short_conv_rows(_pad_rows(x1_ref[0], rows), swx1_ref, sbx1_ref, n1),
             _short_conv_rows(_pad_rows(x2_ref[0], rows), swx2_ref, sbx2_ref, n1))
    for o in range(HY_ORDER):
        y = _hy_long_conv(_pad_rows(z, n1p), kf_ref[o, 0], *cst)
        z = gates[o] * (y + skip_ref[o:o + 1] * z)
    o_ref[0] = z[:n1]


HY_EMB_PAD = -(-HY_EMB // 8) * 8


def _hy_positions(Ls, N):
    m = np.arange(N)
    fwd = m < Ls
    bwd = m > N - Ls
    lag = np.where(fwd, m, np.where(bwd, N - m, 0))
    t = np.linspace(0.0, 1.0, Ls, dtype=np.float32)[lag]
    w = (2.0 * math.pi * np.arange(Ls, dtype=np.float32) / Ls).astype(np.float32)[lag]
    bands = np.linspace(1e-4, HY_BANDS - 1, HY_BANDS, dtype=np.float32)
    ang = jnp.asarray(w[None, :] * bands[:, None])
    z = jnp.concatenate([jnp.asarray(t)[None, :], jnp.cos(ang), -jnp.sin(ang),
                         jnp.zeros((HY_EMB_PAD - HY_EMB, N), F32)], axis=0)
    f = lambda a: jnp.asarray(a[None, :].astype(np.float32))
    return z, f(t), f(fwd), f(bwd)


def _hy_mlp_kernel(z_ref, w1_ref, b1_ref, f1_ref, w2_ref, b2_ref, f2_ref, h_ref):
    hi = lax.Precision.HIGHEST
    h = jnp.sin(f1_ref[...] * (jnp.dot(w1_ref[...], z_ref[...], preferred_element_type=F32, precision=hi) + b1_ref[...]))
    h_ref[...] = jnp.sin(f2_ref[...] * (jnp.dot(w2_ref[...], h, preferred_element_type=F32, precision=hi) + b2_ref[...]))


def _hy_kspec_kernel(N1, scale, h_ref, w3f_ref, w3b_ref, b3f_ref, b3b_ref, df_ref, db_ref, t_ref, vf_ref, vb_ref,
                     f1_ref, tr_ref, ti_ref, m2f_ref, kf_ref):
    hi = lax.Precision.HIGHEST
    W = HY_LANES
    h, t = h_ref[...], t_ref[...]
    kf_t = ((jnp.dot(w3f_ref[...], h, preferred_element_type=F32, precision=hi) + b3f_ref[...])
            * jnp.exp(-t * jnp.abs(df_ref[...])) * vf_ref[...]
            + (jnp.dot(w3b_ref[...], h, preferred_element_type=F32, precision=hi) + b3b_ref[...])
            * jnp.exp(-t * jnp.abs(db_ref[...])) * vb_ref[...])
    kern = jnp.concatenate(
        [jnp.concatenate([kf_t[c:c + 1, a * W:(a + 1) * W] for c in range(kf_t.shape[0])], axis=1) for a in range(N1)],
        axis=0)
    a = _hy_chunk_dft(kern, f1_ref[...], tr_ref[...], ti_ref[...], N1, hi)
    kf_ref[0, 0] = jnp.dot(a, m2f_ref[...], preferred_element_type=F32, precision=hi) * scale


def _hy_filter_spectrum(Ls, N1, cst, w1, b1, f1, w2, b2, f2, w3, b3, decay):
    W = HY_LANES
    N = N1 * W
    C = HY_CH
    hid = w2.shape[0]
    z, t_row, v_f, v_b = _hy_positions(Ls, N)
    col = lambda v: v.reshape(-1, 1)
    w1_t = jnp.pad(w1, ((0, HY_EMB_PAD - HY_EMB), (0, 0))).T
    full1 = lambda a: pl.BlockSpec(a.shape, lambda i: (0,) * a.ndim)
    mlp_args = (z, w1_t, col(b1), col(f1), w2.T, col(b2), col(f2))
    h = pl.pallas_call(
        _hy_mlp_kernel, grid=(1,), in_specs=[full1(a) for a in mlp_args],
        out_specs=pl.BlockSpec((hid, N), lambda i: (0, 0)), out_shape=jax.ShapeDtypeStruct((hid, N), F32),
        compiler_params=_cparams("arbitrary"), name="hyena_filter_mlp",
    )(*mlp_args)
    ct = HY_CH_TILE
    nj = C // ct
    nb = C // ct
    full = lambda a: pl.BlockSpec(a.shape, lambda o, j: (0,) * a.ndim)
    rows = lambda width, d: pl.BlockSpec((ct, width), lambda o, j: ((o * 2 + d) * nb + j, 0))
    w3_t, b3_c, dec_c = w3.T, col(b3), col(decay)
    consts = (t_row, v_f, v_b, cst["f1_full"], cst["tr"], cst["ti"], cst["m2f"])
    return pl.pallas_call(
        functools.partial(_hy_kspec_kernel, N1, 1.0 / N),
        grid=(HY_ORDER, nj),
        in_specs=[full(h), rows(hid, 0), rows(hid, 1), rows(1, 0), rows(1, 1), rows(1, 0), rows(1, 1)]
                 + [full(a) for a in consts],
        out_specs=pl.BlockSpec((1, 1, ct * N1, 2 * W), lambda o, j: (o, j, 0, 0)),
        out_shape=jax.ShapeDtypeStruct((HY_ORDER, nj, ct * N1, 2 * W), F32),
        compiler_params=_cparams("arbitrary", "arbitrary"),
        name="hyena_filter_spectrum",
    )(h, w3_t, w3_t, b3_c, b3_c, dec_c, dec_c, *consts)


def _hyena_mixer(hy_t, short_w, short_b, filt_params, skip, N1):
    B, n1, C3, W = hy_t.shape
    C = C3 // (HY_ORDER + 1)
    assert N1 >= 2 * n1 and C % HY_CH_TILE == 0, "circular length must cover the two-sided linear convolution"
    cst = _hy_consts(n1, N1)
    kf = _hy_filter_spectrum(n1 * W, N1, cst, *filt_params)
    Wd = HY_CH_TILE * W
    nj = C // HY_CH_TILE
    hy2 = hy_t.reshape(B, n1, C3 * W)
    rep = lambda v: jnp.repeat(v, W, axis=-1)
    sw, sb = rep(short_w), rep(short_b).reshape(1, C3 * W)
    sk = rep(skip)
    chan = lambda part: pl.BlockSpec((1, n1, Wd), lambda j, b: (b, 0, part * nj + j))
    wsp = lambda part: pl.BlockSpec((HY_SHORT, Wd), lambda j, b: (0, part * nj + j))
    bsp = lambda part: pl.BlockSpec((1, Wd), lambda j, b: (0, part * nj + j))
    full = lambda a: pl.BlockSpec(a.shape, lambda j, b: (0,) * a.ndim)
    consts = [cst[k] for k in ("f1", "ginv", "tr", "ti", "m2f", "m2i")]
    out = pl.pallas_call(
        functools.partial(_hy_mixer_kernel, n1, N1, cst["n1p"]),
        grid=(nj, B),
        in_specs=[chan(0), chan(1), chan(2), wsp(0), bsp(0), wsp(1), bsp(1), wsp(2), bsp(2),
                  pl.BlockSpec((HY_ORDER, Wd), lambda j, b: (0, j)),
                  pl.BlockSpec((HY_ORDER, 1, HY_CH_TILE * N1, 2 * W), lambda j, b: (0, j, 0, 0))]
                 + [full(a) for a in consts],
        out_specs=pl.BlockSpec((1, n1, Wd), lambda j, b: (b, 0, j)),
        out_shape=jax.ShapeDtypeStruct((B, n1, C * W), F32),
        compiler_params=_cparams("arbitrary", "arbitrary"),
        name="hyena_mixer",
    )(hy2, hy2, hy2, sw, sb, sw, sb, sw, sb, sk, kf, *consts)
    return out.reshape(B, n1, C, W)


def kernel(x, c, ctx, c_ctx, w_mod, b_mod, w_in, na_rpb, gla_w_a2, gla_b_a2, gla_norm_g, hy_short_w, hy_short_b, hy_w1, hy_b1, hy_f1, hy_w2, hy_b2, hy_f2, hy_w3, hy_b3, hy_decay, hy_skip, w_out, ln1_g, ln1_b, router_w, router_b, we_gate, we_up, we_down, ws_gate, ws_up, ws_down, ln2_g, ln2_b):
    B, L, D = x.shape
    C = ctx.shape[1]
    assert D == D_MODEL and L % TOK_TILE == 0 and C % HY_LANES == 0 and C <= TOK_TILE and w_mod.shape[0] == DEPTH
    rope_cos, rope_sin = _rope_tables(L)
    ctx_cos, ctx_sin = jnp.ones((C, GLA_KEY_WIDTH), F32), jnp.zeros((C, GLA_KEY_WIDTH), F32)
    zero_state = jnp.zeros((B, GLA_KEY_WIDTH, GLA_WIDTH), F32)
    na_bias = _na_bias_table(na_rpb)

    n_mod = -(-(B + 1) // 8) * 8
    cs = jnp.zeros((n_mod, D), F32).at[:B].set(c).at[B].set(c_ctx)
    mod_all = _modulation(cs, w_mod, b_mod)

    xc = ctx
    for l in range(DEPTH):
        last = l == DEPTH - 1
        mods = mod_all[l].reshape(n_mod, 6, 1, D)
        lat = lambda j: mods[:B, j]
        cm = lambda j: mods[B:B + 1, j]
        n_tok = NA_COLS + GLA_COLS
        w_pad = jnp.pad(w_in[l][:, :n_tok], ((0, 0), (0, D_IN_PAD - n_tok))).astype(BF16)
        w_hy_t = w_in[l][:, n_tok:].T.astype(BF16)
        w_out_b = w_out[l].astype(BF16)
        lg1, lb1 = ln1_g[l].reshape(1, D), ln1_b[l].reshape(1, D)
        lg2, lb2 = ln2_g[l].reshape(1, D), ln2_b[l].reshape(1, D)

        u_na, u_rest, hy = _inproj(x, lat(1), lat(0), w_pad, w_hy_t, True)
        uc_na, uc_rest, hyc = _inproj(xc, cm(1), cm(0), w_pad, w_hy_t, False)

        na_lat = _na_attention(u_na, uc_na, na_bias, l)

        gla_c, s_ctx_f, s_ctx_b = _gla_bidir(uc_rest, ctx_cos, ctx_sin, gla_w_a2[l], gla_b_a2[l], gla_norm_g[l],
                                             zero_state, zero_state)
        gla_lat, _, _ = _gla_bidir(u_rest, rope_cos, rope_sin, gla_w_a2[l], gla_b_a2[l], gla_norm_g[l],
                                   s_ctx_f, s_ctx_b)

        filt_args = (hy_w1[l], hy_b1[l], hy_f1[l], hy_w2[l], hy_b2[l], hy_f2[l], hy_w3[l], hy_b3[l], hy_decay[l])
        hy_lat = _hyena_mixer(hy, hy_short_w[l], hy_short_b[l], filt_args, hy_skip[l], 2 * L // HY_LANES)

        wr_t = router_w[l].T
        x, h_lat, logit_lat = _outproj(na_lat, gla_lat, hy_lat, x, lat(2), w_out_b, lg1, lb1, lat(4), lat(3), wr_t, True)
        if not last:
            na_c = _ctx_attention(uc_na)
            hy_c = _hyena_mixer(hyc, hy_short_w[l], hy_short_b[l], filt_args, hy_skip[l], HY_CTX_N1)
            xc, h_c, logit_c = _outproj(na_c, gla_c, hy_c, xc, cm(2), w_out_b, lg1, lb1, cm(4), cm(3), wr_t, False)

            h_flat = jnp.concatenate([h_lat.reshape(B * L, D), h_c.reshape(B * C, D)], axis=0)
            logit_t = jnp.concatenate([logit_lat, logit_c], axis=1)
        else:
            h_flat = h_lat.reshape(B * L, D)
            logit_t = logit_lat

        picked, gates = _moe_routed(h_flat, logit_t, router_b[l], we_gate, we_up, we_down, l)
        wsg, wsu, wsd = ws_gate[l].astype(BF16), ws_up[l].astype(BF16), ws_down[l].astype(BF16)
        x = _shared_ln2(x, h_lat, picked, gates, 0, lat(5), wsg, wsu, wsd, lg2, lb2, True)
        if not last:
            xc = _shared_ln2(xc, h_c, picked, gates, B * L, cm(5), wsg, wsu, wsd, lg2, lb2, False)
    return x
```

```python
import functools
import math

import numpy as np
import jax
import jax.numpy as jnp
from jax import lax
from jax.experimental import pallas as pl
from jax.experimental.pallas import tpu as pltpu
from jax.experimental.pallas import tpu_sc as plsc

F32 = jnp.float32
BF16 = jnp.bfloat16

D_MODEL = 1024
DEPTH = 4
GRID_W = 64
CTX_LEN = 256

NA_HEADS = 8
NA_HEAD_DIM = 64
NA_WIDTH = NA_HEADS * NA_HEAD_DIM
NA_WIN_ROWS = 8
NA_WIN_COLS = 16

GLA_HEADS = 4
GLA_DK = 32
GLA_DV = 64
GLA_KEY_WIDTH = GLA_HEADS * GLA_DK
GLA_WIDTH = GLA_HEADS * GLA_DV
GLA_RANK = 16
GLA_GATE_NORM = 16.0
GLA_CHUNK = 64

HY_CH = 256
HY_ORDER = 2
HY_SHORT = 3
HY_BANDS = 16
HY_EMB = 1 + 2 * HY_BANDS

MIX_WIDTH = NA_WIDTH + GLA_WIDTH + HY_CH
IN_SPLITS = (NA_WIDTH, NA_WIDTH, NA_WIDTH, GLA_KEY_WIDTH, GLA_KEY_WIDTH, GLA_WIDTH, GLA_WIDTH,
             2 * GLA_RANK, (HY_ORDER + 1) * HY_CH)
ROPE_BASE = 10000.0

N_EXPERTS = 128
TOP_K = 8
N_GROUPS = 8
TOPK_GROUPS = 4
EXPERT_HIDDEN = 256
ROUTED_SCALE = 2.5

DEEPNORM_ALPHA = (2 * DEPTH) ** 0.25
LN_EPS = 1e-6

LANE = 128
MXU_DIM = 256
VMEM_LIMIT = 48 * 1024 * 1024
SC_CORES = 2
SC_SUBCORES = 16
SC_LANES = 16
SC_SCAN_CHUNK = 16384

NA_COLS = 3 * NA_WIDTH
GLA_COLS = sum(IN_SPLITS[3:8])
HY_COLS = IN_SPLITS[8]
REST_COLS = -(-GLA_COLS // LANE) * LANE
D_IN_PAD = NA_COLS + REST_COLS
NA_QUAD = MXU_DIM // NA_HEAD_DIM
NEG_BIG = -1e30

TOK_TILE = 512
MOE_TILE = 512
NA_ROW_TILE = 8
NA_ROW_UNROLL = 8
ROUTE_TILE = 256
GLA_TILE = 512
GLA_BATCH = 2


def _cparams(*sem):
    return pltpu.CompilerParams(dimension_semantics=sem, vmem_limit_bytes=VMEM_LIMIT)


def _mod_kernel(c_ref, w_ref, b_ref, o_ref):
    c = c_ref[...]
    s = c * jax.nn.sigmoid(c)
    o_ref[0] = jnp.dot(s, w_ref[0], preferred_element_type=F32, precision=lax.Precision.HIGHEST) + b_ref[0]


def _modulation(cs, w_mod, b_mod):
    R = cs.shape[0]
    tn = 1536
    return pl.pallas_call(
        _mod_kernel,
        grid=(DEPTH, 6 * D_MODEL // tn),
        in_specs=[pl.BlockSpec((R, D_MODEL), lambda l, j: (0, 0)),
                  pl.BlockSpec((1, D_MODEL, tn), lambda l, j: (l, 0, j)),
                  pl.BlockSpec((1, 1, tn), lambda l, j: (l, 0, j))],
        out_specs=pl.BlockSpec((1, R, tn), lambda l, j: (l, 0, j)),
        out_shape=jax.ShapeDtypeStruct((DEPTH, R, 6 * D_MODEL), F32),
        compiler_params=_cparams("arbitrary", "arbitrary"),
        name="modulation",
    )(cs, w_mod, b_mod.reshape(DEPTH, 1, 6 * D_MODEL))


def _inproj_kernel(x_ref, sc_ref, sh_ref, w_ref, wh_ref, ona_ref, orest_ref, ohy_ref):
    xm = (x_ref[0] * (1.0 + sc_ref[0]) + sh_ref[0]).astype(BF16)
    step = 512
    for c0 in range(0, NA_COLS, step):
        ona_ref[0, :, c0:c0 + step] = jnp.dot(xm, w_ref[:, c0:c0 + step], preferred_element_type=F32).astype(BF16)
    for c0 in range(0, REST_COLS, step):
        c1 = min(c0 + step, REST_COLS)
        orest_ref[0, :, c0:c1] = jnp.dot(xm, w_ref[:, NA_COLS + c0:NA_COLS + c1], preferred_element_type=F32)
    hy = lax.dot_general(wh_ref[...], xm, _NT, preferred_element_type=F32)
    for j in range(ohy_ref.shape[1]):
        ohy_ref[0, j] = hy[:, j * LANE:(j + 1) * LANE]


def _inproj(x, sc, sh, w_pad, w_hy_t, per_batch_mod):
    B, T, D = x.shape
    tm = min(TOK_TILE, T)
    mod_idx = (lambda b, i: (b, 0, 0)) if per_batch_mod else (lambda b, i: (0, 0, 0))
    return pl.pallas_call(
        _inproj_kernel,
        grid=(B, T // tm),
        in_specs=[pl.BlockSpec((1, tm, D), lambda b, i: (b, i, 0)),
                  pl.BlockSpec((1, 1, D), mod_idx),
                  pl.BlockSpec((1, 1, D), mod_idx),
                  pl.BlockSpec((D, D_IN_PAD), lambda b, i: (0, 0)),
                  pl.BlockSpec((HY_COLS, D), lambda b, i: (0, 0))],
        out_specs=[pl.BlockSpec((1, tm, NA_COLS), lambda b, i: (b, i, 0)),
                   pl.BlockSpec((1, tm, REST_COLS), lambda b, i: (b, i, 0)),
                   pl.BlockSpec((1, tm // LANE, HY_COLS, LANE), lambda b, i: (b, i, 0, 0))],
        out_shape=[jax.ShapeDtypeStruct((B, T, NA_COLS), BF16),
                   jax.ShapeDtypeStruct((B, T, REST_COLS), F32),
                   jax.ShapeDtypeStruct((B, T // LANE, HY_COLS, LANE), F32)],
        compiler_params=_cparams("arbitrary", "arbitrary"),
        name="inproj",
    )(x, sc, sh, w_pad, w_hy_t)


def _stack_heads(q, n_rows):
    head = lax.broadcasted_iota(jnp.int32, (n_rows, MXU_DIM), 1) // NA_HEAD_DIM
    return jnp.concatenate([jnp.where(head == h, q, jnp.zeros_like(q)) for h in range(NA_QUAD)], axis=0)


def _unstack_heads(o, n_rows):
    head = lax.broadcasted_iota(jnp.int32, (n_rows, MXU_DIM), 1) // NA_HEAD_DIM
    out = jnp.zeros((n_rows, MXU_DIM), F32)
    for h in range(NA_QUAD):
        out = jnp.where(head == h, o[h * n_rows:(h + 1) * n_rows], out)
    return out


_NT = (((1,), (1,)), ((), ()))


def _na_kernel(q_ref, k_ref, v_ref, kc_ref, vc_ref, bias_ref, o_ref):
    rt = pl.program_id(2)
    scale = NA_HEAD_DIM ** -0.5
    kc = kc_ref[0]
    vc = vc_ref[0]
    n_loc = NA_WIN_ROWS * GRID_W

    def row(rl, carry):
        r = rt * NA_ROW_TILE + rl
        kr0 = jnp.clip(r - NA_WIN_ROWS // 2, 0, GRID_W - NA_WIN_ROWS)
        dr0 = kr0 - r + NA_WIN_ROWS - 1
        q = q_ref[0, pl.ds(pl.multiple_of(rl * GRID_W, GRID_W), GRID_W), :]
        qs = _stack_heads(q, GRID_W)
        k0 = pl.multiple_of(kr0 * GRID_W, GRID_W)
        ks = k_ref[0, pl.ds(k0, n_loc), :]
        vs = v_ref[0, pl.ds(k0, n_loc), :]
        s_loc = lax.dot_general(qs, ks, _NT, preferred_element_type=F32) * scale + bias_ref[0, dr0]
        s_ctx = lax.dot_general(qs, kc, _NT, preferred_element_type=F32) * scale
        m = jnp.maximum(jnp.max(s_loc, axis=-1, keepdims=True), jnp.max(s_ctx, axis=-1, keepdims=True))
        p_loc = jnp.exp(s_loc - m)
        p_ctx = jnp.exp(s_ctx - m)
        den = jnp.sum(p_loc, axis=-1, keepdims=True) + jnp.sum(p_ctx, axis=-1, keepdims=True)
        o = (jnp.dot(p_loc.astype(BF16), vs, preferred_element_type=F32)
             + jnp.dot(p_ctx.astype(BF16), vc, preferred_element_type=F32)) / den
        o_ref[0, pl.ds(pl.multiple_of(rl * GRID_W, GRID_W), GRID_W), :] = _unstack_heads(o, GRID_W)
        return carry

    lax.fori_loop(0, NA_ROW_TILE, row, 0, unroll=NA_ROW_UNROLL)


def _na_bias_table(rpb):
    n_lyr = rpb.shape[0]
    c = np.arange(GRID_W)
    kc0 = np.clip(c - NA_WIN_COLS // 2, 0, GRID_W - NA_WIN_COLS)
    kc = np.arange(GRID_W)
    valid = (kc[None, :] >= kc0[:, None]) & (kc[None, :] < kc0[:, None] + NA_WIN_COLS)
    dc = kc[None, :] - c[:, None] + NA_WIN_COLS - 1
    onehot = (np.arange(2 * NA_WIN_COLS - 1)[:, None, None] == dc[None]) & valid[None]
    toep = jnp.einsum('lhrd,dck->lhrck', rpb, jnp.asarray(onehot, F32), precision=lax.Precision.HIGHEST)
    toep = jnp.where(jnp.asarray(valid)[None, None, None], toep, NEG_BIG)
    tab = jnp.stack([toep[:, :, d:d + NA_WIN_ROWS] for d in range(NA_WIN_ROWS)], axis=2)
    tab = tab.transpose(0, 1, 2, 4, 3, 5).reshape(n_lyr, NA_HEADS // NA_QUAD, NA_QUAD, NA_WIN_ROWS, GRID_W,
                                                  NA_WIN_ROWS * GRID_W)
    return tab.transpose(0, 1, 3, 2, 4, 5).reshape(n_lyr, NA_HEADS // NA_QUAD, NA_WIN_ROWS, NA_QUAD * GRID_W,
                                                   NA_WIN_ROWS * GRID_W)


def _na_attention(u_na, uc_na, bias_tab, layer):
    B, L, _ = u_na.shape
    C = uc_na.shape[1]
    assert L == GRID_W * GRID_W and GRID_W % NA_ROW_TILE == 0
    nq = NA_WIDTH // MXU_DIM
    tq = NA_ROW_TILE * GRID_W
    return pl.pallas_call(
        _na_kernel,
        grid=(B, nq, L // tq),
        in_specs=[pl.BlockSpec((1, tq, MXU_DIM), lambda b, j, i: (b, i, j)),
                  pl.BlockSpec((1, L, MXU_DIM), lambda b, j, i: (b, 0, nq + j)),
                  pl.BlockSpec((1, L, MXU_DIM), lambda b, j, i: (b, 0, 2 * nq + j)),
                  pl.BlockSpec((1, C, MXU_DIM), lambda b, j, i: (b, 0, nq + j)),
                  pl.BlockSpec((1, C, MXU_DIM), lambda b, j, i: (b, 0, 2 * nq + j)),
                  pl.BlockSpec((None, 1, NA_WIN_ROWS, NA_QUAD * GRID_W, NA_WIN_ROWS * GRID_W),
                               lambda b, j, i: (layer, j, 0, 0, 0))],
        out_specs=pl.BlockSpec((1, tq, MXU_DIM), lambda b, j, i: (b, i, j)),
        out_shape=jax.ShapeDtypeStruct((B, L, NA_WIDTH), F32),
        compiler_params=_cparams("arbitrary", "arbitrary", "arbitrary"),
        name="na_attention",
    )(u_na, u_na, u_na, uc_na, uc_na, bias_tab)


def _ctx_attn_kernel(q_ref, k_ref, v_ref, o_ref):
    C = q_ref.shape[1]
    qs = _stack_heads(q_ref[0], C)
    s = lax.dot_general(qs, k_ref[0], _NT, preferred_element_type=F32) * NA_HEAD_DIM ** -0.5
    p = jnp.exp(s - jnp.max(s, axis=-1, keepdims=True))
    den = jnp.sum(p, axis=-1, keepdims=True)
    o = jnp.dot(p.astype(BF16), v_ref[0], preferred_element_type=F32) / den
    o_ref[0] = _unstack_heads(o, C)


def _ctx_attention(uc_na):
    B, C, _ = uc_na.shape
    nq = NA_WIDTH // MXU_DIM
    return pl.pallas_call(
        _ctx_attn_kernel,
        grid=(B, nq),
        in_specs=[pl.BlockSpec((1, C, MXU_DIM), lambda b, j: (b, 0, j)),
                  pl.BlockSpec((1, C, MXU_DIM), lambda b, j: (b, 0, nq + j)),
                  pl.BlockSpec((1, C, MXU_DIM), lambda b, j: (b, 0, 2 * nq + j))],
        out_specs=pl.BlockSpec((1, C, MXU_DIM), lambda b, j: (b, 0, j)),
        out_shape=jax.ShapeDtypeStruct((B, C, NA_WIDTH), F32),
        compiler_params=_cparams("arbitrary", "arbitrary"),
        name="ctx_attention",
    )(uc_na, uc_na, uc_na)


def _layer_norm_rows(y, g, b):
    mu = jnp.mean(y, axis=-1, keepdims=True)
    d = y - mu
    var = jnp.mean(d * d, axis=-1, keepdims=True)
    return d * lax.rsqrt(var + LN_EPS) * g + b


def _outproj_kernel(na_ref, gla_ref, hy_ref, x_ref, g1_ref, w_ref, lg_ref, lb_ref, sc_ref, sh_ref, wr_ref,
                    xo_ref, h_ref, logit_ref):
    w_hy = w_ref[NA_WIDTH + GLA_WIDTH:, :]
    hy_mix = jnp.concatenate(
        [lax.dot_general(hy_ref[0, j].astype(BF16), w_hy, (((0,), (0,)), ((), ())), preferred_element_type=F32)
         for j in range(hy_ref.shape[1])], axis=0)
    mix = (jnp.dot(na_ref[0].astype(BF16), w_ref[0:NA_WIDTH, :], preferred_element_type=F32)
           + jnp.dot(gla_ref[0].astype(BF16), w_ref[NA_WIDTH:NA_WIDTH + GLA_WIDTH, :], preferred_element_type=F32)
           + hy_mix)
    xn = _layer_norm_rows(DEEPNORM_ALPHA * x_ref[0] + g1_ref[0] * mix, lg_ref[...], lb_ref[...])
    xo_ref[0] = xn
    h = xn * (1.0 + sc_ref[0]) + sh_ref[0]
    h_ref[0] = h.astype(BF16)
    logit_ref[...] = _dot_split(wr_ref[...], h, dims=_NT)


def _outproj(na, gla, hy, x, g1, w_out, ln_g, ln_b, sc2, sh2, w_router, per_batch_mod):
    B, T, D = x.shape
    tm = min(TOK_TILE, T)
    mod_idx = (lambda b, i: (b, 0, 0)) if per_batch_mod else (lambda b, i: (0, 0, 0))
    tok = lambda w: pl.BlockSpec((1, tm, w), lambda b, i: (b, i, 0))
    full = lambda s: pl.BlockSpec(s, lambda b, i: (0,) * len(s))
    mod = pl.BlockSpec((1, 1, D), mod_idx)
    return pl.pallas_call(
        _outproj_kernel,
        grid=(B, T // tm),
        in_specs=[tok(NA_WIDTH), tok(GLA_WIDTH),
                  pl.BlockSpec((1, tm // LANE, HY_CH, LANE), lambda b, i: (b, i, 0, 0)), tok(D), mod, full((MIX_WIDTH, D)),
                  full((1, D)), full((1, D)), mod, mod, full((N_EXPERTS, D))],
        out_specs=[tok(D), tok(D), pl.BlockSpec((N_EXPERTS, tm), lambda b, i: (0, b * (T // tm) + i))],
        out_shape=[jax.ShapeDtypeStruct((B, T, D), F32), jax.ShapeDtypeStruct((B, T, D), BF16),
                   jax.ShapeDtypeStruct((N_EXPERTS, B * T), F32)],
        compiler_params=_cparams("arbitrary", "arbitrary"),
        name="outproj_ln1",
    )(na, gla, hy, x, g1, w_out, ln_g, ln_b, sc2, sh2, w_router)


def _moe_kernel(tile_off, te_ref, nt_ref, xs_ref, wg_ref, wu_ref, wd_ref, *rest):
    ys_ref, wg_s, wu_s, wd_s = rest[-4:]
    step = pl.program_id(0)
    i = step + tile_off

    @pl.when(jnp.logical_or(step == 0, te_ref[i] != te_ref[jnp.maximum(i - 1, 0)]))
    def _():
        wg_s[...] = wg_ref[0].astype(BF16)
        wu_s[...] = wu_ref[0].astype(BF16)
        wd_s[...] = wd_ref[0].astype(BF16)

    @pl.when(i < nt_ref[0])
    def _():
        xs = xs_ref[...]
        g = jnp.dot(xs, wg_s[...], preferred_element_type=F32)
        u = jnp.dot(xs, wu_s[...], preferred_element_type=F32)
        a = (g * jax.nn.sigmoid(g) * u).astype(BF16)
        ys_ref[...] = jnp.dot(a, wd_s[...], preferred_element_type=F32).astype(ys_ref.dtype)

    @pl.when(i >= nt_ref[0])
    def _():
        ys_ref[...] = jnp.zeros_like(ys_ref)


def _moe_grouped(tile_expert, n_used, xs, we_g, we_u, we_d, layer, tile_off, n_tiles_all, ys_prev=None):
    rows, D = xs.shape
    H = EXPERT_HIDDEN
    off = tile_off
    in_specs = [pl.BlockSpec((MOE_TILE, D), lambda i, te, nt: (i, 0)),
                pl.BlockSpec((None, 1, D, H), lambda i, te, nt: (layer, te[i + off], 0, 0)),
                pl.BlockSpec((None, 1, D, H), lambda i, te, nt: (layer, te[i + off], 0, 0)),
                pl.BlockSpec((None, 1, H, D), lambda i, te, nt: (layer, te[i + off], 0, 0))]
    args = [tile_expert, n_used, xs, we_g, we_u, we_d]
    aliases = {}
    if ys_prev is not None:
        in_specs.append(pl.BlockSpec(memory_space=pl.ANY))
        args.append(ys_prev)
        aliases = {len(args) - 1: 0}
    grid_spec = pltpu.PrefetchScalarGridSpec(
        num_scalar_prefetch=2,
        grid=(rows // MOE_TILE,),
        in_specs=in_specs,
        out_specs=pl.BlockSpec((MOE_TILE, D), lambda i, te, nt: (i + off, 0)),
        scratch_shapes=[pltpu.VMEM((D, H), BF16), pltpu.VMEM((D, H), BF16), pltpu.VMEM((H, D), BF16)],
    )
    return pl.pallas_call(
        functools.partial(_moe_kernel, tile_off),
        grid_spec=grid_spec,
        out_shape=jax.ShapeDtypeStruct((n_tiles_all * MOE_TILE, D), BF16),
        input_output_aliases=aliases,
        compiler_params=_cparams("arbitrary"),
        name="moe_experts",
    )(*args)


def _shared_kernel(x_ref, h_ref, pk_ref, gt_ref, g2_ref, wg_ref, wu_ref, wd_ref, lg_ref, lb_ref, o_ref):
    h = h_ref[0]
    g = jnp.dot(h, wg_ref[...], preferred_element_type=F32)
    u = jnp.dot(h, wu_ref[...], preferred_element_type=F32)
    a = (g * jax.nn.sigmoid(g) * u).astype(BF16)
    ff = jnp.dot(a, wd_ref[...], preferred_element_type=F32)
    gates = gt_ref[...]
    for k in range(TOP_K):
        ff = ff + pk_ref[k].astype(F32) * gates[:, k:k + 1]
    o_ref[0] = _layer_norm_rows(DEEPNORM_ALPHA * x_ref[0] + g2_ref[0] * ff, lg_ref[...], lb_ref[...])


def _shared_ln2(x, h, picked, gates, tok_off, g2, ws_g, ws_u, ws_d, ln_g, ln_b, per_batch_mod):
    B, T, D = x.shape
    tm = min(TOK_TILE, T)
    mod_idx = (lambda b, i: (b, 0, 0)) if per_batch_mod else (lambda b, i: (0, 0, 0))
    tok = pl.BlockSpec((1, tm, D), lambda b, i: (b, i, 0))
    full = lambda s: pl.BlockSpec(s, lambda b, i: (0,) * len(s))
    flat = lambda b, i: tok_off // tm + b * (T // tm) + i
    return pl.pallas_call(
        _shared_kernel,
        grid=(B, T // tm),
        in_specs=[tok, tok, pl.BlockSpec((TOP_K, tm, D), lambda b, i: (0, flat(b, i), 0)),
                  pl.BlockSpec((tm, TOP_K), lambda b, i: (flat(b, i), 0)),
                  pl.BlockSpec((1, 1, D), mod_idx), full((D, EXPERT_HIDDEN)), full((D, EXPERT_HIDDEN)),
                  full((EXPERT_HIDDEN, D)), full((1, D)), full((1, D))],
        out_specs=tok,
        out_shape=jax.ShapeDtypeStruct((B, T, D), F32),
        compiler_params=_cparams("arbitrary", "arbitrary"),
        name="shared_ln2",
    )(x, h, picked, gates, g2, ws_g, ws_u, ws_d, ln_g, ln_b)


def _first_max(vals, iota, n):
    m = jnp.max(vals, axis=0, keepdims=True)
    idx = jnp.min(jnp.where(vals == m, iota, n), axis=0, keepdims=True)
    return m, idx


def _route_kernel(lt_ref, b_ref, eidx_ref, w_ref, rank_ref, cnt_ref, base_ref):
    @pl.when(pl.program_id(0) == 0)
    def _():
        base_ref[...] = jnp.zeros_like(base_ref)

    tm = lt_ref.shape[1]
    per = N_EXPERTS // N_GROUPS
    s = jax.nn.sigmoid(lt_ref[...])
    sel = s + b_ref[...]
    io_g = lax.broadcasted_iota(jnp.int32, (per, tm), 0)
    scores = []
    for g in range(N_GROUPS):
        blk = sel[g * per:(g + 1) * per]
        m1, i1 = _first_max(blk, io_g, per)
        m2 = jnp.max(jnp.where(io_g == i1, -jnp.inf, blk), axis=0, keepdims=True)
        scores.append(m1 + m2)
    cur = jnp.concatenate(scores, axis=0)
    io_8 = lax.broadcasted_iota(jnp.int32, (N_GROUPS, tm), 0)
    gmask = jnp.zeros((N_GROUPS, tm), F32)
    for _ in range(TOPK_GROUPS):
        _, gi = _first_max(cur, io_8, N_GROUPS)
        hit = io_8 == gi
        gmask = jnp.where(hit, 1.0, gmask)
        cur = jnp.where(hit, -jnp.inf, cur)
    masked = jnp.concatenate(
        [jnp.where(gmask[g:g + 1] > 0.0, sel[g * per:(g + 1) * per], -jnp.inf) for g in range(N_GROUPS)], axis=0)
    io_e = lax.broadcasted_iota(jnp.int32, (N_EXPERTS, tm), 0)
    chosen = jnp.zeros((N_EXPERTS, tm), F32)
    eidx, gates = [], []
    for _ in range(TOP_K):
        _, ei = _first_max(masked, io_e, N_EXPERTS)
        hit = io_e == ei
        eidx.append(ei)
        gates.append(jnp.sum(jnp.where(hit, s, 0.0), axis=0, keepdims=True))
        masked = jnp.where(hit, -jnp.inf, masked)
        chosen = jnp.where(hit, 1.0, chosen)
    wk = jnp.concatenate(gates, axis=0)
    w_ref[...] = wk / jnp.sum(wk, axis=0, keepdims=True) * ROUTED_SCALE
    eidx_ref[...] = jnp.concatenate(eidx, axis=0)
    earlier = (lax.broadcasted_iota(jnp.int32, (tm, tm), 0) < lax.broadcasted_iota(jnp.int32, (tm, tm), 1))
    pos = jnp.dot(chosen.astype(BF16), jnp.where(earlier, 1.0, 0.0).astype(BF16), preferred_element_type=F32)
    pos = pos + base_ref[...]
    ranks = [jnp.sum(jnp.where(io_e == eidx[k], pos, 0.0), axis=0, keepdims=True) for k in range(TOP_K)]
    rank_ref[...] = jnp.concatenate(ranks, axis=0).astype(jnp.int32)
    base_ref[...] = base_ref[...] + jnp.sum(chosen, axis=1, keepdims=True)
    cnt_ref[...] = base_ref[...]


def _route(logits_t, b_corr):
    E, T = logits_t.shape
    tm = ROUTE_TILE
    tokk = pl.BlockSpec((TOP_K, tm), lambda i: (0, i))
    return pl.pallas_call(
        _route_kernel,
        grid=(T // tm,),
        in_specs=[pl.BlockSpec((E, tm), lambda i: (0, i)), pl.BlockSpec((E, 1), lambda i: (0, 0))],
        out_specs=[tokk, tokk, tokk, pl.BlockSpec((E, 1), lambda i: (0, 0))],
        out_shape=[jax.ShapeDtypeStruct((TOP_K, T), jnp.int32), jax.ShapeDtypeStruct((TOP_K, T), F32),
                   jax.ShapeDtypeStruct((TOP_K, T), jnp.int32), jax.ShapeDtypeStruct((E, 1), F32)],
        scratch_shapes=[pltpu.VMEM((E, 1), F32)],
        compiler_params=_cparams("arbitrary"),
        name="route",
    )(logits_t, b_corr.reshape(E, 1))


def _slot_kernel(eidx_ref, rank_ref, pstart_ref, dest_ref):
    tm = eidx_ref.shape[1]
    io_e = lax.broadcasted_iota(jnp.int32, (N_EXPERTS, tm), 0)
    ei = eidx_ref[...]
    starts = [jnp.sum(jnp.where(io_e == ei[k:k + 1], pstart_ref[...], 0.0), axis=0, keepdims=True)
              for k in range(TOP_K)]
    dest_ref[...] = jnp.concatenate(starts, axis=0).astype(jnp.int32) + rank_ref[...]


def _slots(eidx, rank, pstart):
    K, T = eidx.shape
    tm = ROUTE_TILE
    tokk = pl.BlockSpec((K, tm), lambda i: (0, i))
    return pl.pallas_call(
        _slot_kernel,
        grid=(T // tm,),
        in_specs=[tokk, tokk, pl.BlockSpec((N_EXPERTS, 1), lambda i: (0, 0))],
        out_specs=tokk,
        out_shape=jax.ShapeDtypeStruct((K, T), jnp.int32),
        compiler_params=_cparams("arbitrary"),
        name="route_slots",
    )(eidx, rank, pstart)


def _slot_tokens(dest, tok, n_slots, n_tok):
    n_asg = dest.shape[0]
    n_sub = SC_CORES * SC_SUBCORES
    per = n_slots // n_sub
    chunk = SC_SCAN_CHUNK
    assert n_slots % n_sub == 0 and per % SC_LANES == 0 and n_asg % chunk == 0 and chunk % SC_LANES == 0
    mesh = plsc.VectorSubcoreMesh(core_axis_name="c", subcore_axis_name="s", num_cores=SC_CORES, num_subcores=SC_SUBCORES)

    def body(dest_hbm, tok_hbm, out_hbm, loc, dbuf, tbuf):
        s0 = (lax.axis_index("c") * SC_SUBCORES + lax.axis_index("s")) * per
        lanes = lax.iota(jnp.int32, SC_LANES)

        @pl.loop(0, per, step=SC_LANES)
        def _(i):
            loc[pl.ds(i, SC_LANES)] = lax.rem(s0 + i, n_tok - SC_LANES) + lanes

        @pl.loop(0, n_asg, step=chunk)
        def _(c):
            pltpu.sync_copy(dest_hbm.at[pl.ds(c, chunk)], dbuf)
            pltpu.sync_copy(tok_hbm.at[pl.ds(c, chunk)], tbuf)

            @pl.loop(0, chunk, step=SC_LANES)
            def _(i):
                d = dbuf[pl.ds(i, SC_LANES)] - s0
                mine = jnp.logical_and(d >= 0, d < per)
                plsc.store_scatter(loc, [d], tbuf[pl.ds(i, SC_LANES)], mask=mine)

        pltpu.sync_copy(loc, out_hbm.at[pl.ds(s0, per)])

    return pl.kernel(
        body, out_type=jax.ShapeDtypeStruct((n_slots,), jnp.int32), mesh=mesh,
        scratch_types=[pltpu.VMEM((per,), jnp.int32), pltpu.VMEM((chunk,), jnp.int32), pltpu.VMEM((chunk,), jnp.int32)],
        compiler_params=pltpu.CompilerParams(needs_layout_passes=False),
        name="slot_tokens",
    )(dest, tok)


def _moe_routed(h_flat, logits_t, b_corr, we_g, we_u, we_d, layer):
    T = h_flat.shape[0]
    n_tiles = T * TOP_K // MOE_TILE + N_EXPERTS
    eidx, gates, rank, counts = _route(logits_t, b_corr)
    counts = counts[:, 0].astype(jnp.int32)
    padded = (counts + MOE_TILE - 1) // MOE_TILE * MOE_TILE
    pends = jnp.cumsum(padded)
    tile_start = jnp.arange(n_tiles, dtype=jnp.int32) * MOE_TILE
    tile_expert = jnp.minimum(jnp.sum((pends[None, :] <= tile_start[:, None]).astype(jnp.int32), axis=1), N_EXPERTS - 1)
    n_used = (pends[-1] // MOE_TILE).astype(jnp.int32).reshape(1)
    dest = _slots(eidx, rank, (pends - padded).astype(F32).reshape(N_EXPERTS, 1))
    tok = jnp.broadcast_to(jnp.arange(T, dtype=jnp.int32)[None], (TOP_K, T))
    src_tok = _slot_tokens(dest.reshape(-1), tok.reshape(-1), n_tiles * MOE_TILE, T)
    assert n_tiles % 2 == 0
    half = n_tiles // 2
    take = lambda idx: h_flat.at[idx].get(mode='promise_in_bounds')
    ys = _moe_grouped(tile_expert, n_used, take(src_tok[:half * MOE_TILE]), we_g, we_u, we_d, layer, 0, n_tiles)
    ys = _moe_grouped(tile_expert, n_used, take(src_tok[half * MOE_TILE:]), we_g, we_u, we_d, layer, half, n_tiles, ys)
    return ys.at[dest].get(mode='promise_in_bounds'), gates.T


_GLA_QK_BLK, _GLA_V_BLK, _GLA_R_BLK = 0, 1, 2
_GLA_LR_BLK = (2 * GLA_KEY_WIDTH + 2 * GLA_WIDTH) // LANE


def _rope_tables(L):
    t = np.arange(L)
    lane = np.arange(GLA_KEY_WIDTH)
    d = lane % GLA_DK
    pos = np.where(d[None, :] < GLA_DK // 2, (t // GRID_W)[:, None], (t % GRID_W)[:, None]).astype(np.float32)
    quarter = GLA_DK // 4
    inv = ROPE_BASE ** (-jnp.arange(quarter, dtype=F32) / quarter)
    ang = jnp.asarray(pos) * inv[jnp.asarray(d % quarter)][None, :]
    sign = np.where(d % (2 * quarter) < quarter, -1.0, 1.0).astype(np.float32)
    return jnp.cos(ang), jnp.sin(ang) * sign[None, :]


def _rope_partner(x):
    lane = lax.broadcasted_iota(jnp.int32, x.shape, 1)
    quarter = GLA_DK // 4
    return jnp.where(lane % (2 * quarter) < quarter, pltpu.roll(x, GLA_KEY_WIDTH - quarter, 1), pltpu.roll(x, quarter, 1))


_NN = (((1,), (0,)), ((), ()))


def _dot_split(a, b, rhs_exact=False, dims=_NN):
    dot = lambda x, y: lax.dot_general(x, y, dims, preferred_element_type=F32)
    a_hi = a.astype(BF16)
    a_lo = (a - a_hi.astype(F32)).astype(BF16)
    b_hi = b.astype(BF16)
    out = dot(a_hi, b_hi) + dot(a_lo, b_hi)
    if not rhs_exact:
        b_lo = (b - b_hi.astype(F32)).astype(BF16)
        out = out + dot(a_hi, b_lo)
    return out


def _log_sigmoid(x):
    return jnp.minimum(x, 0.0) - jnp.log(1.0 + jnp.exp(-jnp.abs(x)))


def _gla_kernel(reverse, finalize, *refs):
    if finalize:
        (qk_ref, v_ref, lr_ref, cos_ref, sin_ref, wa_ref, ba_ref, s0_ref, of_ref, r_ref, g_ref,
         o_ref, sfin_ref, s_scr) = refs
    else:
        qk_ref, v_ref, lr_ref, cos_ref, sin_ref, wa_ref, ba_ref, s0_ref, o_ref, sfin_ref, s_scr = refs

    @pl.when(pl.program_id(1) == 0)
    def _():
        s_scr[...] = s0_ref[...]

    C = GLA_CHUNK
    KW, VW = GLA_KEY_WIDTH, GLA_WIDTH
    cos, sin = cos_ref[...], sin_ref[...]

    ri = lax.broadcasted_iota(jnp.int32, (C, C), 0)
    ci = lax.broadcasted_iota(jnp.int32, (C, C), 1)
    tri = jnp.where((ci >= ri) if reverse else (ci <= ri), 1.0, 0.0)
    tri_h = jnp.concatenate([tri] * GLA_HEADS, axis=0)
    head_k = lax.broadcasted_iota(jnp.int32, (C, KW), 1) // GLA_DK
    head_v = lax.broadcasted_iota(jnp.int32, (C, VW), 1) // GLA_DV
    own_block = (lax.broadcasted_iota(jnp.int32, (KW, VW), 0) // GLA_DK
                 == lax.broadcasted_iota(jnp.int32, (KW, VW), 1) // GLA_DV)
    eye = lax.broadcasted_iota(jnp.int32, (KW, KW), 0) == lax.broadcasted_iota(jnp.int32, (KW, KW), 1)
    for bb in range(qk_ref.shape[0]):
        _gla_sample(reverse, finalize, bb, refs, cos, sin, tri_h, head_k, head_v, own_block, eye)


def _gla_sample(reverse, finalize, bb, refs, cos, sin, tri_h, head_k, head_v, own_block, eye):
    if finalize:
        (qk_ref, v_ref, lr_ref, _, _, wa_ref, ba_ref, _, of_ref, r_ref, g_ref, o_ref, sfin_ref, s_scr) = refs
    else:
        qk_ref, v_ref, lr_ref, _, _, wa_ref, ba_ref, _, o_ref, sfin_ref, s_scr = refs
    tg = qk_ref.shape[1]
    C = GLA_CHUNK
    KW, VW = GLA_KEY_WIDTH, GLA_WIDTH
    qk = qk_ref[bb]
    q = qk[:, :KW] * GLA_DK ** -0.5
    k = qk[:, KW:]
    q = q * cos + _rope_partner(q) * sin
    k = k * cos + _rope_partner(k) * sin
    v = v_ref[bb]
    logit = _dot_split(lr_ref[bb], wa_ref[...]) + ba_ref[...]
    la = _log_sigmoid(logit) / GLA_GATE_NORM
    pos = lax.broadcasted_iota(jnp.int32, (tg, KW), 0) % C
    cum = la
    step = 1
    while step < C:
        if reverse:
            cum = cum + jnp.where(pos < C - step, pltpu.roll(cum, tg - step, 0), 0.0)
        else:
            cum = cum + jnp.where(pos >= step, pltpu.roll(cum, step, 0), 0.0)
        step *= 2

    S = s_scr[bb]
    nc = tg // C
    outs = [None] * nc
    for c in (range(nc - 1, -1, -1) if reverse else range(nc)):
        sl = slice(c * C, (c + 1) * C)
        b = cum[sl]
        qt = q[sl] * jnp.exp(b)
        kt = k[sl] * jnp.exp(-b)
        qs = jnp.concatenate([jnp.where(head_k == h, qt, 0.0) for h in range(GLA_HEADS)], axis=0).astype(BF16)
        att = lax.dot_general(qs, kt.astype(BF16), _NT, preferred_element_type=F32)
        att = jnp.where(tri_h > 0.0, att, 0.0)
        vb = v[sl].astype(BF16)
        oi = jnp.dot(att.astype(BF16), vb, preferred_element_type=F32)
        o_intra = jnp.zeros((C, VW), F32)
        for h in range(GLA_HEADS):
            o_intra = jnp.where(head_v == h, oi[h * C:(h + 1) * C], o_intra)
        o_inter = jnp.dot(qt.astype(BF16), S.astype(BF16), preferred_element_type=F32)
        outs[c] = o_intra + o_inter
        b_last = b[0:1] if reverse else b[C - 1:C]
        kdec = (k[sl] * jnp.exp(b_last - b)).astype(BF16)
        kv = lax.dot_general(kdec, vb, (((0,), (0,)), ((), ())), preferred_element_type=F32)
        decay_col = jnp.sum(jnp.where(eye, jnp.exp(b_last), 0.0), axis=1, keepdims=True)
        S = decay_col * S + jnp.where(own_block, kv, 0.0)
    s_scr[bb] = S
    sfin_ref[bb] = S
    o = jnp.concatenate(outs, axis=0)
    if finalize:
        o = of_ref[bb] + o
        same_head = (lax.broadcasted_iota(jnp.int32, (VW, VW), 0) // GLA_DV
                     == lax.broadcasted_iota(jnp.int32, (VW, VW), 1) // GLA_DV)
        ms = _dot_split(o * o, jnp.where(same_head, 1.0 / GLA_DV, 0.0), rhs_exact=True)
        r = r_ref[bb]
        o = o * lax.rsqrt(ms + LN_EPS) * g_ref[...] * (r * jax.nn.sigmoid(r))
    o_ref[bb] = o


def _gla_pass(u_rest, cos, sin, wa, ba, s0, reverse, fin=None):
    B, T, _ = u_rest.shape
    tg = min(GLA_TILE, T)
    n = T // tg
    ti = (lambda i: n - 1 - i) if reverse else (lambda i: i)
    KW, VW = GLA_KEY_WIDTH, GLA_WIDTH
    gb = GLA_BATCH if B % GLA_BATCH == 0 else 1
    ublk = lambda w, j: pl.BlockSpec((gb, tg, w), lambda b, i: (b, ti(i), j))
    full = lambda s: pl.BlockSpec(s, lambda b, i: (0,) * len(s))
    state = pl.BlockSpec((gb, KW, VW), lambda b, i: (b, 0, 0))
    tab = pl.BlockSpec((tg, KW), lambda b, i: (ti(i), 0))
    in_specs = [ublk(2 * KW, _GLA_QK_BLK), ublk(VW, _GLA_V_BLK), ublk(LANE, _GLA_LR_BLK), tab, tab,
                full((LANE, KW)), full((1, KW)), state]
    args = [u_rest, u_rest, u_rest, cos, sin, wa, ba, s0]
    if fin is not None:
        in_specs += [ublk(VW, 0), ublk(VW, _GLA_R_BLK), full((1, VW))]
        args += [fin[0], u_rest, fin[1]]
    return pl.pallas_call(
        functools.partial(_gla_kernel, reverse, fin is not None),
        grid=(B // gb, n),
        in_specs=in_specs,
        out_specs=[ublk(VW, 0), state],
        out_shape=[jax.ShapeDtypeStruct((B, T, VW), F32), jax.ShapeDtypeStruct((B, KW, VW), F32)],
        scratch_shapes=[pltpu.VMEM((gb, KW, VW), F32)],
        compiler_params=_cparams("arbitrary", "arbitrary"),
        name="gla_bwd" if reverse else "gla_fwd",
    )(*args)


def _gla_bidir(u_rest, cos, sin, w_a2, b_a2, norm_g, s0_f, s0_b):
    def decay_w(d):
        return jnp.zeros((LANE, GLA_KEY_WIDTH), F32).at[d * GLA_RANK:(d + 1) * GLA_RANK].set(w_a2[d])

    o_f, s_f = _gla_pass(u_rest, cos, sin, decay_w(0), b_a2[0:1], s0_f, False)
    g = jnp.tile(norm_g, GLA_HEADS).reshape(1, GLA_WIDTH)
    o, s_b = _gla_pass(u_rest, cos, sin, decay_w(1), b_a2[1:2], s0_b, True, (o_f, g))
    return o, s_f, s_b


HY_LANES = LANE
HY_CH_TILE = 16
HY_CTX_N1 = 16


def _hy_consts(n1, N1):
    W = HY_LANES
    n1p = max(n1, 16)
    n1o = max(n1, 8)
    a = np.arange(N1)
    th1 = 2.0 * np.pi * ((a[:, None] * a[None, :]) % N1) / N1
    f1 = np.zeros((2 * N1, n1p)); f1[:N1, :n1] = np.cos(th1[:, :n1]); f1[N1:, :n1] = -np.sin(th1[:, :n1])
    f1_full = np.concatenate([np.cos(th1), -np.sin(th1)], axis=0)
    ginv = np.zeros((n1o, 2 * N1)); ginv[:n1, :N1] = np.cos(th1.T[:n1]); ginv[:n1, N1:] = -np.sin(th1.T[:n1])
    r = np.arange(W)
    tht = 2.0 * np.pi * ((a[:, None] * r[None, :]) % (N1 * W)) / (N1 * W)
    tr = np.tile(np.cos(tht), (1, HY_CH_TILE)); ti = np.tile(-np.sin(tht), (1, HY_CH_TILE))
    th2 = 2.0 * np.pi * ((r[:, None] * r[None, :]) % W) / W
    c2, s2 = np.cos(th2), -np.sin(th2)
    m2f = np.block([[c2, s2], [-s2, c2]])
    m2i = np.block([[c2, -s2], [s2, c2]])
    f = lambda m: jnp.asarray(m, F32)
    return dict(n1p=n1p, n1o=n1o, f1=f(f1), f1_full=f(f1_full), ginv=f(ginv), tr=f(tr), ti=f(ti), m2f=f(m2f), m2i=f(m2i))


def _short_conv_rows(u, w_ref, b_ref, n_rows):
    R, Wd = u.shape
    lane = lax.broadcasted_iota(jnp.int32, (R, Wd), 1) % HY_LANES
    row = lax.broadcasted_iota(jnp.int32, (R, Wd), 0)
    up = jnp.where(row == 0, 0.0, pltpu.roll(u, 1, 0))
    dn = jnp.where(row == n_rows - 1, 0.0, pltpu.roll(u, R - 1, 0))
    prev = jnp.where(lane == 0, pltpu.roll(up, Wd - (HY_LANES - 1), 1), pltpu.roll(u, 1, 1))
    nxt = jnp.where(lane == HY_LANES - 1, pltpu.roll(dn, HY_LANES - 1, 1), pltpu.roll(u, Wd - 1, 1))
    return b_ref[...] + prev * w_ref[0:1] + u * w_ref[1:2] + nxt * w_ref[2:3]


def _pad_rows(u, rows):
    return u if u.shape[0] == rows else jnp.concatenate([u, jnp.zeros((rows - u.shape[0], u.shape[1]), u.dtype)], axis=0)


def _hy_chunk_dft(z, f1, tr, ti, N1, prec):
    if prec is None:
        a = jnp.dot(f1.astype(BF16), z.astype(BF16), preferred_element_type=F32)
    else:
        a = jnp.dot(f1, z, preferred_element_type=F32, precision=prec)
    ar, ai = a[:N1], a[N1:]
    a_re, a_im = ar * tr - ai * ti, ar * ti + ai * tr
    W = HY_LANES
    return jnp.concatenate(
        [jnp.concatenate([a_re[:, c * W:(c + 1) * W], a_im[:, c * W:(c + 1) * W]], axis=1) for c in range(z.shape[1] // W)],
        axis=0)


def _hy_long_conv(z, kf, f1, ginv, tr, ti, m2f, m2i, N1):
    W = HY_LANES
    m = z.shape[1] // W
    a = _hy_chunk_dft(z, f1, tr, ti, N1, None).astype(BF16)
    x = jnp.dot(a, m2f.astype(BF16), preferred_element_type=F32)
    xr, xi = x[:, :W], x[:, W:]
    kr, ki = kf[:, :W], kf[:, W:]
    y = jnp.concatenate([xr * kr - xi * ki, xr * ki + xi * kr], axis=1).astype(BF16)
    p = jnp.dot(y, m2i.astype(BF16), preferred_element_type=F32)
    pr = jnp.concatenate([p[c * N1:(c + 1) * N1, :W] for c in range(m)], axis=1)
    pi = jnp.concatenate([p[c * N1:(c + 1) * N1, W:] for c in range(m)], axis=1)
    q = jnp.concatenate([pr * tr + pi * ti, pi * tr - pr * ti], axis=0).astype(BF16)
    return jnp.dot(ginv.astype(BF16), q, preferred_element_type=F32)


def _hy_mixer_kernel(n1, N1, n1p, v_ref, x1_ref, x2_ref, swv_ref, sbv_ref, swx1_ref, sbx1_ref, swx2_ref, sbx2_ref,
                     skip_ref, kf_ref, f1_ref, ginv_ref, tr_ref, ti_ref, m2f_ref, m2i_ref, o_ref):
    rows = max(n1, 8)
    cst = (f1_ref[...], ginv_ref[...], tr_ref[...], ti_ref[...], m2f_ref[...], m2i_ref[...], N1)
    z = _short_conv_rows(_pad_rows(v_ref[0], rows), swv_ref, sbv_ref, n1)
    gates = (_short_conv_rows(_pad_rows(x1_ref[0], rows), swx1_ref, sbx1_ref, n1),
             _short_conv_rows(_pad_rows(x2_ref[0], rows), swx2_ref, sbx2_ref, n1))
    for o in range(HY_ORDER):
        y = _hy_long_conv(_pad_rows(z, n1p), kf_ref[o, 0], *cst)
        z = gates[o] * (y + skip_ref[o:o + 1] * z)
    o_ref[0] = z[:n1]


HY_EMB_PAD = -(-HY_EMB // 8) * 8


def _hy_positions(Ls, N):
    m = np.arange(N)
    fwd = m < Ls
    bwd = m > N - Ls
    lag = np.where(fwd, m, np.where(bwd, N - m, 0))
    t = np.linspace(0.0, 1.0, Ls, dtype=np.float32)[lag]
    w = (2.0 * math.pi * np.arange(Ls, dtype=np.float32) / Ls).astype(np.float32)[lag]
    bands = np.linspace(1e-4, HY_BANDS - 1, HY_BANDS, dtype=np.float32)
    ang = jnp.asarray(w[None, :] * bands[:, None])
    z = jnp.concatenate([jnp.asarray(t)[None, :], jnp.cos(ang), -jnp.sin(ang),
                         jnp.zeros((HY_EMB_PAD - HY_EMB, N), F32)], axis=0)
    f = lambda a: jnp.asarray(a[None, :].astype(np.float32))
    return z, f(t), f(fwd), f(bwd)


def _hy_mlp_kernel(z_ref, w1_ref, b1_ref, f1_ref, w2_ref, b2_ref, f2_ref, h_ref):
    hi = lax.Precision.HIGHEST
    h = jnp.sin(f1_ref[...] * (jnp.dot(w1_ref[...], z_ref[...], preferred_element_type=F32, precision=hi) + b1_ref[...]))
    h_ref[...] = jnp.sin(f2_ref[...] * (jnp.dot(w2_ref[...], h, preferred_element_type=F32, precision=hi) + b2_ref[...]))


def _hy_kspec_kernel(N1, scale, h_ref, w3f_ref, w3b_ref, b3f_ref, b3b_ref, df_ref, db_ref, t_ref, vf_ref, vb_ref,
                     f1_ref, tr_ref, ti_ref, m2f_ref, kf_ref):
    hi = lax.Precision.HIGHEST
    W = HY_LANES
    h, t = h_ref[...], t_ref[...]
    kf_t = ((jnp.dot(w3f_ref[...], h, preferred_element_type=F32, precision=hi) + b3f_ref[...])
            * jnp.exp(-t * jnp.abs(df_ref[...])) * vf_ref[...]
            + (jnp.dot(w3b_ref[...], h, preferred_element_type=F32, precision=hi) + b3b_ref[...])
            * jnp.exp(-t * jnp.abs(db_ref[...])) * vb_ref[...])
    kern = jnp.concatenate(
        [jnp.concatenate([kf_t[c:c + 1, a * W:(a + 1) * W] for c in range(kf_t.shape[0])], axis=1) for a in range(N1)],
        axis=0)
    a = _hy_chunk_dft(kern, f1_ref[...], tr_ref[...], ti_ref[...], N1, hi)
    kf_ref[0, 0] = jnp.dot(a, m2f_ref[...], preferred_element_type=F32, precision=hi) * scale


def _hy_filter_spectrum(Ls, N1, cst, w1, b1, f1, w2, b2, f2, w3, b3, decay):
    W = HY_LANES
    N = N1 * W
    C = HY_CH
    hid = w2.shape[0]
    z, t_row, v_f, v_b = _hy_positions(Ls, N)
    col = lambda v: v.reshape(-1, 1)
    w1_t = jnp.pad(w1, ((0, HY_EMB_PAD - HY_EMB), (0, 0))).T
    full1 = lambda a: pl.BlockSpec(a.shape, lambda i: (0,) * a.ndim)
    mlp_args = (z, w1_t, col(b1), col(f1), w2.T, col(b2), col(f2))
    h = pl.pallas_call(
        _hy_mlp_kernel, grid=(1,), in_specs=[full1(a) for a in mlp_args],
        out_specs=pl.BlockSpec((hid, N), lambda i: (0, 0)), out_shape=jax.ShapeDtypeStruct((hid, N), F32),
        compiler_params=_cparams("arbitrary"), name="hyena_filter_mlp",
    )(*mlp_args)
    ct = HY_CH_TILE
    nj = C // ct
    nb = C // ct
    full = lambda a: pl.BlockSpec(a.shape, lambda o, j: (0,) * a.ndim)
    rows = lambda width, d: pl.BlockSpec((ct, width), lambda o, j: ((o * 2 + d) * nb + j, 0))
    w3_t, b3_c, dec_c = w3.T, col(b3), col(decay)
    consts = (t_row, v_f, v_b, cst["f1_full"], cst["tr"], cst["ti"], cst["m2f"])
    return pl.pallas_call(
        functools.partial(_hy_kspec_kernel, N1, 1.0 / N),
        grid=(HY_ORDER, nj),
        in_specs=[full(h), rows(hid, 0), rows(hid, 1), rows(1, 0), rows(1, 1), rows(1, 0), rows(1, 1)]
                 + [full(a) for a in consts],
        out_specs=pl.BlockSpec((1, 1, ct * N1, 2 * W), lambda o, j: (o, j, 0, 0)),
        out_shape=jax.ShapeDtypeStruct((HY_ORDER, nj, ct * N1, 2 * W), F32),
        compiler_params=_cparams("arbitrary", "arbitrary"),
        name="hyena_filter_spectrum",
    )(h, w3_t, w3_t, b3_c, b3_c, dec_c, dec_c, *consts)


def _hyena_mixer(hy_t, short_w, short_b, filt_params, skip, N1):
    B, n1, C3, W = hy_t.shape
    C = C3 // (HY_ORDER + 1)
    assert N1 >= 2 * n1 and C % HY_CH_TILE == 0, "circular length must cover the two-sided linear convolution"
    cst = _hy_consts(n1, N1)
    kf = _hy_filter_spectrum(n1 * W, N1, cst, *filt_params)
    Wd = HY_CH_TILE * W
    nj = C // HY_CH_TILE
    hy2 = hy_t.reshape(B, n1, C3 * W)
    rep = lambda v: jnp.repeat(v, W, axis=-1)
    sw, sb = rep(short_w), rep(short_b).reshape(1, C3 * W)
    sk = rep(skip)
    chan = lambda part: pl.BlockSpec((1, n1, Wd), lambda j, b: (b, 0, part * nj + j))
    wsp = lambda part: pl.BlockSpec((HY_SHORT, Wd), lambda j, b: (0, part * nj + j))
    bsp = lambda part: pl.BlockSpec((1, Wd), lambda j, b: (0, part * nj + j))
    full = lambda a: pl.BlockSpec(a.shape, lambda j, b: (0,) * a.ndim)
    consts = [cst[k] for k in ("f1", "ginv", "tr", "ti", "m2f", "m2i")]
    out = pl.pallas_call(
        functools.partial(_hy_mixer_kernel, n1, N1, cst["n1p"]),
        grid=(nj, B),
        in_specs=[chan(0), chan(1), chan(2), wsp(0), bsp(0), wsp(1), bsp(1), wsp(2), bsp(2),
                  pl.BlockSpec((HY_ORDER, Wd), lambda j, b: (0, j)),
                  pl.BlockSpec((HY_ORDER, 1, HY_CH_TILE * N1, 2 * W), lambda j, b: (0, j, 0, 0))]
                 + [full(a) for a in consts],
        out_specs=pl.BlockSpec((1, n1, Wd), lambda j, b: (b, 0, j)),
        out_shape=jax.ShapeDtypeStruct((B, n1, C * W), F32),
        compiler_params=_cparams("arbitrary", "arbitrary"),
        name="hyena_mixer",
    )(hy2, hy2, hy2, sw, sb, sw, sb, sw, sb, sk, kf, *consts)
    return out.reshape(B, n1, C, W)


def kernel(x, c, ctx, c_ctx, w_mod, b_mod, w_in, na_rpb, gla_w_a2, gla_b_a2, gla_norm_g, hy_short_w, hy_short_b, hy_w1, hy_b1, hy_f1, hy_w2, hy_b2, hy_f2, hy_w3, hy_b3, hy_decay, hy_skip, w_out, ln1_g, ln1_b, router_w, router_b, we_gate, we_up, we_down, ws_gate, ws_up, ws_down, ln2_g, ln2_b):
    B, L, D = x.shape
    C = ctx.shape[1]
    assert D == D_MODEL and L % TOK_TILE == 0 and C % HY_LANES == 0 and C <= TOK_TILE and w_mod.shape[0] == DEPTH
    rope_cos, rope_sin = _rope_tables(L)
    ctx_cos, ctx_sin = jnp.ones((C, GLA_KEY_WIDTH), F32), jnp.zeros((C, GLA_KEY_WIDTH), F32)
    zero_state = jnp.zeros((B, GLA_KEY_WIDTH, GLA_WIDTH), F32)
    na_bias = _na_bias_table(na_rpb)

    n_mod = -(-(B + 1) // 8) * 8
    cs = jnp.zeros((n_mod, D), F32).at[:B].set(c).at[B].set(c_ctx)
    mod_all = _modulation(cs, w_mod, b_mod)

    xc = ctx
    for l in range(DEPTH):
        last = l == DEPTH - 1
        mods = mod_all[l].reshape(n_mod, 6, 1, D)
        lat = lambda j: mods[:B, j]
        cm = lambda j: mods[B:B + 1, j]
        n_tok = NA_COLS + GLA_COLS
        w_pad = jnp.pad(w_in[l][:, :n_tok], ((0, 0), (0, D_IN_PAD - n_tok))).astype(BF16)
        w_hy_t = w_in[l][:, n_tok:].T.astype(BF16)
        w_out_b = w_out[l].astype(BF16)
        lg1, lb1 = ln1_g[l].reshape(1, D), ln1_b[l].reshape(1, D)
        lg2, lb2 = ln2_g[l].reshape(1, D), ln2_b[l].reshape(1, D)

        u_na, u_rest, hy = _inproj(x, lat(1), lat(0), w_pad, w_hy_t, True)
        uc_na, uc_rest, hyc = _inproj(xc, cm(1), cm(0), w_pad, w_hy_t, False)

        na_lat = _na_attention(u_na, uc_na, na_bias, l)

        gla_c, s_ctx_f, s_ctx_b = _gla_bidir(uc_rest, ctx_cos, ctx_sin, gla_w_a2[l], gla_b_a2[l], gla_norm_g[l],
                                             zero_state, zero_state)
        gla_lat, _, _ = _gla_bidir(u_rest, rope_cos, rope_sin, gla_w_a2[l], gla_b_a2[l], gla_norm_g[l],
                                   s_ctx_f, s_ctx_b)

        filt_args = (hy_w1[l], hy_b1[l], hy_f1[l], hy_w2[l], hy_b2[l], hy_f2[l], hy_w3[l], hy_b3[l], hy_decay[l])
        hy_lat = _hyena_mixer(hy, hy_short_w[l], hy_short_b[l], filt_args, hy_skip[l], 2 * L // HY_LANES)

        wr_t = router_w[l].T
        x, h_lat, logit_lat = _outproj(na_lat, gla_lat, hy_lat, x, lat(2), w_out_b, lg1, lb1, lat(4), lat(3), wr_t, True)
        if not last:
            na_c = _ctx_attention(uc_na)
            hy_c = _hyena_mixer(hyc, hy_short_w[l], hy_short_b[l], filt_args, hy_skip[l], HY_CTX_N1)
            xc, h_c, logit_c = _outproj(na_c, gla_c, hy_c, xc, cm(2), w_out_b, lg1, lb1, cm(4), cm(3), wr_t, False)

            h_flat = jnp.concatenate([h_lat.reshape(B * L, D), h_c.reshape(B * C, D)], axis=0)
            logit_t = jnp.concatenate([logit_lat, logit_c], axis=1)
        else:
            h_flat = h_lat.reshape(B * L, D)
            logit_t = logit_lat

        picked, gates = _moe_routed(h_flat, logit_t, router_b[l], we_gate, we_up, we_down, l)
        wsg, wsu, wsd = ws_gate[l].astype(BF16), ws_up[l].astype(BF16), ws_down[l].astype(BF16)
        x = _shared_ln2(x, h_lat, picked, gates, 0, lat(5), wsg, wsu, wsd, lg2, lb2, True)
        if not last:
            xc = _shared_ln2(xc, h_c, picked, gates, B * L, cm(5), wsg, wsu, wsd, lg2, lb2, False)
    return x
```

```python
import functools
import math

import numpy as np
import jax
import jax.numpy as jnp
from jax import lax
from jax.experimental import pallas as pl
from jax.experimental.pallas import tpu as pltpu
from jax.experimental.pallas import tpu_sc as plsc

F32 = jnp.float32
BF16 = jnp.bfloat16

D_MODEL = 1024
DEPTH = 4
GRID_W = 64
CTX_LEN = 256

NA_HEADS = 8
NA_HEAD_DIM = 64
NA_WIDTH = NA_HEADS * NA_HEAD_DIM
NA_WIN_ROWS = 8
NA_WIN_COLS = 16

GLA_HEADS = 4
GLA_DK = 32
GLA_DV = 64
GLA_KEY_WIDTH = GLA_HEADS * GLA_DK
GLA_WIDTH = GLA_HEADS * GLA_DV
GLA_RANK = 16
GLA_GATE_NORM = 16.0
GLA_CHUNK = 64

HY_CH = 256
HY_ORDER = 2
HY_SHORT = 3
HY_BANDS = 16
HY_EMB = 1 + 2 * HY_BANDS

MIX_WIDTH = NA_WIDTH + GLA_WIDTH + HY_CH
IN_SPLITS = (NA_WIDTH, NA_WIDTH, NA_WIDTH, GLA_KEY_WIDTH, GLA_KEY_WIDTH, GLA_WIDTH, GLA_WIDTH,
             2 * GLA_RANK, (HY_ORDER + 1) * HY_CH)
ROPE_BASE = 10000.0

N_EXPERTS = 128
TOP_K = 8
N_GROUPS = 8
TOPK_GROUPS = 4
EXPERT_HIDDEN = 256
ROUTED_SCALE = 2.5

DEEPNORM_ALPHA = (2 * DEPTH) ** 0.25
LN_EPS = 1e-6

LANE = 128
MXU_DIM = 256
VMEM_LIMIT = 48 * 1024 * 1024
SC_CORES = 2
SC_SUBCORES = 16
SC_LANES = 16
SC_SCAN_CHUNK = 16384

NA_COLS = 3 * NA_WIDTH
GLA_COLS = sum(IN_SPLITS[3:8])
HY_COLS = IN_SPLITS[8]
REST_COLS = -(-GLA_COLS // LANE) * LANE
D_IN_PAD = NA_COLS + REST_COLS
NA_QUAD = MXU_DIM // NA_HEAD_DIM
NEG_BIG = -1e30

TOK_TILE = 512
MOE_TILE = 512
NA_ROW_TILE = 8
NA_ROW_UNROLL = 8
ROUTE_TILE = 256
GLA_TILE = 512
GLA_BATCH = 2


def _cparams(*sem):
    return pltpu.CompilerParams(dimension_semantics=sem, vmem_limit_bytes=VMEM_LIMIT)


def _mod_kernel(c_ref, w_ref, b_ref, o_ref):
    c = c_ref[...]
    s = c * jax.nn.sigmoid(c)
    o_ref[0] = jnp.dot(s, w_ref[0], preferred_element_type=F32, precision=lax.Precision.HIGHEST) + b_ref[0]


def _modulation(cs, w_mod, b_mod):
    R = cs.shape[0]
    tn = 1536
    return pl.pallas_call(
        _mod_kernel,
        grid=(DEPTH, 6 * D_MODEL // tn),
        in_specs=[pl.BlockSpec((R, D_MODEL), lambda l, j: (0, 0)),
                  pl.BlockSpec((1, D_MODEL, tn), lambda l, j: (l, 0, j)),
                  pl.BlockSpec((1, 1, tn), lambda l, j: (l, 0, j))],
        out_specs=pl.BlockSpec((1, R, tn), lambda l, j: (l, 0, j)),
        out_shape=jax.ShapeDtypeStruct((DEPTH, R, 6 * D_MODEL), F32),
        compiler_params=_cparams("arbitrary", "arbitrary"),
        name="modulation",
    )(cs, w_mod, b_mod.reshape(DEPTH, 1, 6 * D_MODEL))


def _inproj_kernel(x_ref, sc_ref, sh_ref, w_ref, wh_ref, ona_ref, orest_ref, ohy_ref):
    xm = (x_ref[0] * (1.0 + sc_ref[0]) + sh_ref[0]).astype(BF16)
    step = 512
    for c0 in range(0, NA_COLS, step):
        ona_ref[0, :, c0:c0 + step] = jnp.dot(xm, w_ref[:, c0:c0 + step], preferred_element_type=F32).astype(BF16)
    for c0 in range(0, REST_COLS, step):
        c1 = min(c0 + step, REST_COLS)
        orest_ref[0, :, c0:c1] = jnp.dot(xm, w_ref[:, NA_COLS + c0:NA_COLS + c1], preferred_element_type=F32)
    hy = lax.dot_general(wh_ref[...], xm, _NT, preferred_element_type=F32)
    for j in range(ohy_ref.shape[1]):
        ohy_ref[0, j] = hy[:, j * LANE:(j + 1) * LANE]


def _inproj(x, sc, sh, w_pad, w_hy_t, per_batch_mod):
    B, T, D = x.shape
    tm = min(TOK_TILE, T)
    mod_idx = (lambda b, i: (b, 0, 0)) if per_batch_mod else (lambda b, i: (0, 0, 0))
    return pl.pallas_call(
        _inproj_kernel,
        grid=(B, T // tm),
        in_specs=[pl.BlockSpec((1, tm, D), lambda b, i: (b, i, 0)),
                  pl.BlockSpec((1, 1, D), mod_idx),
                  pl.BlockSpec((1, 1, D), mod_idx),
                  pl.BlockSpec((D, D_IN_PAD), lambda b, i: (0, 0)),
                  pl.BlockSpec((HY_COLS, D), lambda b, i: (0, 0))],
        out_specs=[pl.BlockSpec((1, tm, NA_COLS), lambda b, i: (b, i, 0)),
                   pl.BlockSpec((1, tm, REST_COLS), lambda b, i: (b, i, 0)),
                   pl.BlockSpec((1, tm // LANE, HY_COLS, LANE), lambda b, i: (b, i, 0, 0))],
        out_shape=[jax.ShapeDtypeStruct((B, T, NA_COLS), BF16),
                   jax.ShapeDtypeStruct((B, T, REST_COLS), F32),
                   jax.ShapeDtypeStruct((B, T // LANE, HY_COLS, LANE), F32)],
        compiler_params=_cparams("arbitrary", "arbitrary"),
        name="inproj",
    )(x, sc, sh, w_pad, w_hy_t)


def _stack_heads(q, n_rows):
    head = lax.broadcasted_iota(jnp.int32, (n_rows, MXU_DIM), 1) // NA_HEAD_DIM
    return jnp.concatenate([jnp.where(head == h, q, jnp.zeros_like(q)) for h in range(NA_QUAD)], axis=0)


def _unstack_heads(o, n_rows):
    head = lax.broadcasted_iota(jnp.int32, (n_rows, MXU_DIM), 1) // NA_HEAD_DIM
    out = jnp.zeros((n_rows, MXU_DIM), F32)
    for h in range(NA_QUAD):
        out = jnp.where(head == h, o[h * n_rows:(h + 1) * n_rows], out)
    return out


_NT = (((1,), (1,)), ((), ()))


def _na_kernel(q_ref, k_ref, v_ref, kc_ref, vc_ref, bias_ref, o_ref):
    rt = pl.program_id(2)
    scale = NA_HEAD_DIM ** -0.5
    kc = kc_ref[0]
    vc = vc_ref[0]
    n_loc = NA_WIN_ROWS * GRID_W

    def row(rl, carry):
        r = rt * NA_ROW_TILE + rl
        kr0 = jnp.clip(r - NA_WIN_ROWS // 2, 0, GRID_W - NA_WIN_ROWS)
        dr0 = kr0 - r + NA_WIN_ROWS - 1
        q = q_ref[0, pl.ds(pl.multiple_of(rl * GRID_W, GRID_W), GRID_W), :]
        qs = _stack_heads(q, GRID_W)
        k0 = pl.multiple_of(kr0 * GRID_W, GRID_W)
        ks = k_ref[0, pl.ds(k0, n_loc), :]
        vs = v_ref[0, pl.ds(k0, n_loc), :]
        s_loc = lax.dot_general(qs, ks, _NT, preferred_element_type=F32) * scale + bias_ref[0, dr0]
        s_ctx = lax.dot_general(qs, kc, _NT, preferred_element_type=F32) * scale
        m = jnp.maximum(jnp.max(s_loc, axis=-1, keepdims=True), jnp.max(s_ctx, axis=-1, keepdims=True))
        p_loc = jnp.exp(s_loc - m)
        p_ctx = jnp.exp(s_ctx - m)
        den = jnp.sum(p_loc, axis=-1, keepdims=True) + jnp.sum(p_ctx, axis=-1, keepdims=True)
        o = (jnp.dot(p_loc.astype(BF16), vs, preferred_element_type=F32)
             + jnp.dot(p_ctx.astype(BF16), vc, preferred_element_type=F32)) / den
        o_ref[0, pl.ds(pl.multiple_of(rl * GRID_W, GRID_W), GRID_W), :] = _unstack_heads(o, GRID_W)
        return carry

    lax.fori_loop(0, NA_ROW_TILE, row, 0, unroll=NA_ROW_UNROLL)


def _na_bias_table(rpb):
    n_lyr = rpb.shape[0]
    c = np.arange(GRID_W)
    kc0 = np.clip(c - NA_WIN_COLS // 2, 0, GRID_W - NA_WIN_COLS)
    kc = np.arange(GRID_W)
    valid = (kc[None, :] >= kc0[:, None]) & (kc[None, :] < kc0[:, None] + NA_WIN_COLS)
    dc = kc[None, :] - c[:, None] + NA_WIN_COLS - 1
    onehot = (np.arange(2 * NA_WIN_COLS - 1)[:, None, None] == dc[None]) & valid[None]
    toep = jnp.einsum('lhrd,dck->lhrck', rpb, jnp.asarray(onehot, F32), precision=lax.Precision.HIGHEST)
    toep = jnp.where(jnp.asarray(valid)[None, None, None], toep, NEG_BIG)
    tab = jnp.stack([toep[:, :, d:d + NA_WIN_ROWS] for d in range(NA_WIN_ROWS)], axis=2)
    tab = tab.transpose(0, 1, 2, 4, 3, 5).reshape(n_lyr, NA_HEADS // NA_QUAD, NA_QUAD, NA_WIN_ROWS, GRID_W,
                                                  NA_WIN_ROWS * GRID_W)
    return tab.transpose(0, 1, 3, 2, 4, 5).reshape(n_lyr, NA_HEADS // NA_QUAD, NA_WIN_ROWS, NA_QUAD * GRID_W,
                                                   NA_WIN_ROWS * GRID_W)


def _na_attention(u_na, uc_na, bias_tab, layer):
    B, L, _ = u_na.shape
    C = uc_na.shape[1]
    assert L == GRID_W * GRID_W and GRID_W % NA_ROW_TILE == 0
    nq = NA_WIDTH // MXU_DIM
    tq = NA_ROW_TILE * GRID_W
    return pl.pallas_call(
        _na_kernel,
        grid=(B, nq, L // tq),
        in_specs=[pl.BlockSpec((1, tq, MXU_DIM), lambda b, j, i: (b, i, j)),
                  pl.BlockSpec((1, L, MXU_DIM), lambda b, j, i: (b, 0, nq + j)),
                  pl.BlockSpec((1, L, MXU_DIM), lambda b, j, i: (b, 0, 2 * nq + j)),
                  pl.BlockSpec((1, C, MXU_DIM), lambda b, j, i: (b, 0, nq + j)),
                  pl.BlockSpec((1, C, MXU_DIM), lambda b, j, i: (b, 0, 2 * nq + j)),
                  pl.BlockSpec((None, 1, NA_WIN_ROWS, NA_QUAD * GRID_W, NA_WIN_ROWS * GRID_W),
                               lambda b, j, i: (layer, j, 0, 0, 0))],
        out_specs=pl.BlockSpec((1, tq, MXU_DIM), lambda b, j, i: (b, i, j)),
        out_shape=jax.ShapeDtypeStruct((B, L, NA_WIDTH), F32),
        compiler_params=_cparams("arbitrary", "arbitrary", "arbitrary"),
        name="na_attention",
    )(u_na, u_na, u_na, uc_na, uc_na, bias_tab)


def _ctx_attn_kernel(q_ref, k_ref, v_ref, o_ref):
    C = q_ref.shape[1]
    qs = _stack_heads(q_ref[0], C)
    s = lax.dot_general(qs, k_ref[0], _NT, preferred_element_type=F32) * NA_HEAD_DIM ** -0.5
    p = jnp.exp(s - jnp.max(s, axis=-1, keepdims=True))
    den = jnp.sum(p, axis=-1, keepdims=True)
    o = jnp.dot(p.astype(BF16), v_ref[0], preferred_element_type=F32) / den
    o_ref[0] = _unstack_heads(o, C)


def _ctx_attention(uc_na):
    B, C, _ = uc_na.shape
    nq = NA_WIDTH // MXU_DIM
    return pl.pallas_call(
        _ctx_attn_kernel,
        grid=(B, nq),
        in_specs=[pl.BlockSpec((1, C, MXU_DIM), lambda b, j: (b, 0, j)),
                  pl.BlockSpec((1, C, MXU_DIM), lambda b, j: (b, 0, nq + j)),
                  pl.BlockSpec((1, C, MXU_DIM), lambda b, j: (b, 0, 2 * nq + j))],
        out_specs=pl.BlockSpec((1, C, MXU_DIM), lambda b, j: (b, 0, j)),
        out_shape=jax.ShapeDtypeStruct((B, C, NA_WIDTH), F32),
        compiler_params=_cparams("arbitrary", "arbitrary"),
        name="ctx_attention",
    )(uc_na, uc_na, uc_na)


def _layer_norm_rows(y, g, b):
    mu = jnp.mean(y, axis=-1, keepdims=True)
    d = y - mu
    var = jnp.mean(d * d, axis=-1, keepdims=True)
    return d * lax.rsqrt(var + LN_EPS) * g + b


def _outproj_kernel(na_ref, gla_ref, hy_ref, x_ref, g1_ref, w_ref, lg_ref, lb_ref, sc_ref, sh_ref, wr_ref,
                    xo_ref, h_ref, logit_ref):
    w_hy = w_ref[NA_WIDTH + GLA_WIDTH:, :]
    hy_mix = jnp.concatenate(
        [lax.dot_general(hy_ref[0, j].astype(BF16), w_hy, (((0,), (0,)), ((), ())), preferred_element_type=F32)
         for j in range(hy_ref.shape[1])], axis=0)
    mix = (jnp.dot(na_ref[0].astype(BF16), w_ref[0:NA_WIDTH, :], preferred_element_type=F32)
           + jnp.dot(gla_ref[0].astype(BF16), w_ref[NA_WIDTH:NA_WIDTH + GLA_WIDTH, :], preferred_element_type=F32)
           + hy_mix)
    xn = _layer_norm_rows(DEEPNORM_ALPHA * x_ref[0] + g1_ref[0] * mix, lg_ref[...], lb_ref[...])
    xo_ref[0] = xn
    h = xn * (1.0 + sc_ref[0]) + sh_ref[0]
    h_ref[0] = h.astype(BF16)
    logit_ref[...] = _dot_split(wr_ref[...], h, dims=_NT)


def _outproj(na, gla, hy, x, g1, w_out, ln_g, ln_b, sc2, sh2, w_router, per_batch_mod):
    B, T, D = x.shape
    tm = min(TOK_TILE, T)
    mod_idx = (lambda b, i: (b, 0, 0)) if per_batch_mod else (lambda b, i: (0, 0, 0))
    tok = lambda w: pl.BlockSpec((1, tm, w), lambda b, i: (b, i, 0))
    full = lambda s: pl.BlockSpec(s, lambda b, i: (0,) * len(s))
    mod = pl.BlockSpec((1, 1, D), mod_idx)
    return pl.pallas_call(
        _outproj_kernel,
        grid=(B, T // tm),
        in_specs=[tok(NA_WIDTH), tok(GLA_WIDTH),
                  pl.BlockSpec((1, tm // LANE, HY_CH, LANE), lambda b, i: (b, i, 0, 0)), tok(D), mod, full((MIX_WIDTH, D)),
                  full((1, D)), full((1, D)), mod, mod, full((N_EXPERTS, D))],
        out_specs=[tok(D), tok(D), pl.BlockSpec((N_EXPERTS, tm), lambda b, i: (0, b * (T // tm) + i))],
        out_shape=[jax.ShapeDtypeStruct((B, T, D), F32), jax.ShapeDtypeStruct((B, T, D), BF16),
                   jax.ShapeDtypeStruct((N_EXPERTS, B * T), F32)],
        compiler_params=_cparams("arbitrary", "arbitrary"),
        name="outproj_ln1",
    )(na, gla, hy, x, g1, w_out, ln_g, ln_b, sc2, sh2, w_router)


def _moe_kernel(tile_off, te_ref, nt_ref, xs_ref, wg_ref, wu_ref, wd_ref, *rest):
    ys_ref, wg_s, wu_s, wd_s = rest[-4:]
    step = pl.program_id(0)
    i = step + tile_off

    @pl.when(jnp.logical_or(step == 0, te_ref[i] != te_ref[jnp.maximum(i - 1, 0)]))
    def _():
        wg_s[...] = wg_ref[0].astype(BF16)
        wu_s[...] = wu_ref[0].astype(BF16)
        wd_s[...] = wd_ref[0].astype(BF16)

    @pl.when(i < nt_ref[0])
    def _():
        xs = xs_ref[...]
        g = jnp.dot(xs, wg_s[...], preferred_element_type=F32)
        u = jnp.dot(xs, wu_s[...], preferred_element_type=F32)
        a = (g * jax.nn.sigmoid(g) * u).astype(BF16)
        ys_ref[...] = jnp.dot(a, wd_s[...], preferred_element_type=F32).astype(ys_ref.dtype)

    @pl.when(i >= nt_ref[0])
    def _():
        ys_ref[...] = jnp.zeros_like(ys_ref)


def _moe_grouped(tile_expert, n_used, xs, we_g, we_u, we_d, layer, tile_off, n_tiles_all, ys_prev=None):
    rows, D = xs.shape
    H = EXPERT_HIDDEN
    off = tile_off
    in_specs = [pl.BlockSpec((MOE_TILE, D), lambda i, te, nt: (i, 0)),
                pl.BlockSpec((None, 1, D, H), lambda i, te, nt: (layer, te[i + off], 0, 0)),
                pl.BlockSpec((None, 1, D, H), lambda i, te, nt: (layer, te[i + off], 0, 0)),
                pl.BlockSpec((None, 1, H, D), lambda i, te, nt: (layer, te[i + off], 0, 0))]
    args = [tile_expert, n_used, xs, we_g, we_u, we_d]
    aliases = {}
    if ys_prev is not None:
        in_specs.append(pl.BlockSpec(memory_space=pl.ANY))
        args.append(ys_prev)
        aliases = {len(args) - 1: 0}
    grid_spec = pltpu.PrefetchScalarGridSpec(
        num_scalar_prefetch=2,
        grid=(rows // MOE_TILE,),
        in_specs=in_specs,
        out_specs=pl.BlockSpec((MOE_TILE, D), lambda i, te, nt: (i + off, 0)),
        scratch_shapes=[pltpu.VMEM((D, H), BF16), pltpu.VMEM((D, H), BF16), pltpu.VMEM((H, D), BF16)],
    )
    return pl.pallas_call(
        functools.partial(_moe_kernel, tile_off),
        grid_spec=grid_spec,
        out_shape=jax.ShapeDtypeStruct((n_tiles_all * MOE_TILE, D), BF16),
        input_output_aliases=aliases,
        compiler_params=_cparams("arbitrary"),
        name="moe_experts",
    )(*args)


def _shared_kernel(x_ref, h_ref, pk_ref, gt_ref, g2_ref, wg_ref, wu_ref, wd_ref, lg_ref, lb_ref, *rest):
    o_ref = rest[-1]
    h = h_ref[0]
    g = jnp.dot(h, wg_ref[...], preferred_element_type=F32)
    u = jnp.dot(h, wu_ref[...], preferred_element_type=F32)
    a = (g * jax.nn.sigmoid(g) * u).astype(BF16)
    ff = jnp.dot(a, wd_ref[...], preferred_element_type=F32)
    gates = gt_ref[...]
    for k in range(TOP_K):
        ff = ff + pk_ref[k].astype(F32) * gates[:, k:k + 1]
    o_ref[0] = _layer_norm_rows(DEEPNORM_ALPHA * x_ref[0] + g2_ref[0] * ff, lg_ref[...], lb_ref[...])


def _shared_ln2(x, h, picked, gates, pk_off, gt_off, g2, ws_g, ws_u, ws_d, ln_g, ln_b, per_batch_mod,
                b0=0, nb=None, prev=None):
    B, T, D = x.shape
    nb = B if nb is None else nb
    tm = min(TOK_TILE, T)
    mod_idx = (lambda b, i: (b + b0, 0, 0)) if per_batch_mod else (lambda b, i: (0, 0, 0))
    tok = pl.BlockSpec((1, tm, D), lambda b, i: (b + b0, i, 0))
    full = lambda s: pl.BlockSpec(s, lambda b, i: (0,) * len(s))
    flat = lambda off: (lambda b, i: off // tm + b * (T // tm) + i)
    pk_idx, gt_idx = flat(pk_off), flat(gt_off)
    in_specs = [tok, tok, pl.BlockSpec((TOP_K, tm, D), lambda b, i: (0, pk_idx(b, i), 0)),
                pl.BlockSpec((tm, TOP_K), lambda b, i: (gt_idx(b, i), 0)),
                pl.BlockSpec((1, 1, D), mod_idx), full((D, EXPERT_HIDDEN)), full((D, EXPERT_HIDDEN)),
                full((EXPERT_HIDDEN, D)), full((1, D)), full((1, D))]
    args = [x, h, picked, gates, g2, ws_g, ws_u, ws_d, ln_g, ln_b]
    aliases = {}
    if prev is not None:
        in_specs.append(pl.BlockSpec(memory_space=pl.ANY))
        args.append(prev)
        aliases = {len(args) - 1: 0}
    return pl.pallas_call(
        _shared_kernel,
        grid=(nb, T // tm),
        in_specs=in_specs,
        out_specs=tok,
        out_shape=jax.ShapeDtypeStruct((B, T, D), F32),
        input_output_aliases=aliases,
        compiler_params=_cparams("arbitrary", "arbitrary"),
        name="shared_ln2",
    )(*args)


def _first_max(vals, iota, n):
    m = jnp.max(vals, axis=0, keepdims=True)
    idx = jnp.min(jnp.where(vals == m, iota, n), axis=0, keepdims=True)
    return m, idx


def _route_kernel(lt_ref, b_ref, eidx_ref, w_ref, rank_ref, cnt_ref, base_ref):
    @pl.when(pl.program_id(0) == 0)
    def _():
        base_ref[...] = jnp.zeros_like(base_ref)

    tm = lt_ref.shape[1]
    per = N_EXPERTS // N_GROUPS
    s = jax.nn.sigmoid(lt_ref[...])
    sel = s + b_ref[...]
    io_g = lax.broadcasted_iota(jnp.int32, (per, tm), 0)
    scores = []
    for g in range(N_GROUPS):
        blk = sel[g * per:(g + 1) * per]
        m1, i1 = _first_max(blk, io_g, per)
        m2 = jnp.max(jnp.where(io_g == i1, -jnp.inf, blk), axis=0, keepdims=True)
        scores.append(m1 + m2)
    cur = jnp.concatenate(scores, axis=0)
    io_8 = lax.broadcasted_iota(jnp.int32, (N_GROUPS, tm), 0)
    gmask = jnp.zeros((N_GROUPS, tm), F32)
    for _ in range(TOPK_GROUPS):
        _, gi = _first_max(cur, io_8, N_GROUPS)
        hit = io_8 == gi
        gmask = jnp.where(hit, 1.0, gmask)
        cur = jnp.where(hit, -jnp.inf, cur)
    masked = jnp.concatenate(
        [jnp.where(gmask[g:g + 1] > 0.0, sel[g * per:(g + 1) * per], -jnp.inf) for g in range(N_GROUPS)], axis=0)
    io_e = lax.broadcasted_iota(jnp.int32, (N_EXPERTS, tm), 0)
    chosen = jnp.zeros((N_EXPERTS, tm), F32)
    eidx, gates = [], []
    for _ in range(TOP_K):
        _, ei = _first_max(masked, io_e, N_EXPERTS)
        hit = io_e == ei
        eidx.append(ei)
        gates.append(jnp.sum(jnp.where(hit, s, 0.0), axis=0, keepdims=True))
        masked = jnp.where(hit, -jnp.inf, masked)
        chosen = jnp.where(hit, 1.0, chosen)
    wk = jnp.concatenate(gates, axis=0)
    w_ref[...] = wk / jnp.sum(wk, axis=0, keepdims=True) * ROUTED_SCALE
    eidx_ref[...] = jnp.concatenate(eidx, axis=0)
    earlier = (lax.broadcasted_iota(jnp.int32, (tm, tm), 0) < lax.broadcasted_iota(jnp.int32, (tm, tm), 1))
    pos = jnp.dot(chosen.astype(BF16), jnp.where(earlier, 1.0, 0.0).astype(BF16), preferred_element_type=F32)
    pos = pos + base_ref[...]
    ranks = [jnp.sum(jnp.where(io_e == eidx[k], pos, 0.0), axis=0, keepdims=True) for k in range(TOP_K)]
    rank_ref[...] = jnp.concatenate(ranks, axis=0).astype(jnp.int32)
    base_ref[...] = base_ref[...] + jnp.sum(chosen, axis=1, keepdims=True)
    cnt_ref[...] = base_ref[...]


def _route(logits_t, b_corr):
    E, T = logits_t.shape
    tm = ROUTE_TILE
    tokk = pl.BlockSpec((TOP_K, tm), lambda i: (0, i))
    return pl.pallas_call(
        _route_kernel,
        grid=(T // tm,),
        in_specs=[pl.BlockSpec((E, tm), lambda i: (0, i)), pl.BlockSpec((E, 1), lambda i: (0, 0))],
        out_specs=[tokk, tokk, tokk, pl.BlockSpec((E, 1), lambda i: (0, 0))],
        out_shape=[jax.ShapeDtypeStruct((TOP_K, T), jnp.int32), jax.ShapeDtypeStruct((TOP_K, T), F32),
                   jax.ShapeDtypeStruct((TOP_K, T), jnp.int32), jax.ShapeDtypeStruct((E, 1), F32)],
        scratch_shapes=[pltpu.VMEM((E, 1), F32)],
        compiler_params=_cparams("arbitrary"),
        name="route",
    )(logits_t, b_corr.reshape(E, 1))


def _slot_kernel(eidx_ref, rank_ref, pstart_ref, dest_ref):
    tm = eidx_ref.shape[1]
    io_e = lax.broadcasted_iota(jnp.int32, (N_EXPERTS, tm), 0)
    ei = eidx_ref[...]
    starts = [jnp.sum(jnp.where(io_e == ei[k:k + 1], pstart_ref[...], 0.0), axis=0, keepdims=True)
              for k in range(TOP_K)]
    dest_ref[...] = jnp.concatenate(starts, axis=0).astype(jnp.int32) + rank_ref[...]


def _slots(eidx, rank, pstart):
    K, T = eidx.shape
    tm = ROUTE_TILE
    tokk = pl.BlockSpec((K, tm), lambda i: (0, i))
    return pl.pallas_call(
        _slot_kernel,
        grid=(T // tm,),
        in_specs=[tokk, tokk, pl.BlockSpec((N_EXPERTS, 1), lambda i: (0, 0))],
        out_specs=tokk,
        out_shape=jax.ShapeDtypeStruct((K, T), jnp.int32),
        compiler_params=_cparams("arbitrary"),
        name="route_slots",
    )(eidx, rank, pstart)


def _slot_tokens(dest, tok, n_slots, n_tok):
    n_asg = dest.shape[0]
    n_sub = SC_CORES * SC_SUBCORES
    per = n_slots // n_sub
    chunk = SC_SCAN_CHUNK
    assert n_slots % n_sub == 0 and per % SC_LANES == 0 and n_asg % chunk == 0 and chunk % SC_LANES == 0
    mesh = plsc.VectorSubcoreMesh(core_axis_name="c", subcore_axis_name="s", num_cores=SC_CORES, num_subcores=SC_SUBCORES)

    def body(dest_hbm, tok_hbm, out_hbm, loc, dbuf, tbuf):
        s0 = (lax.axis_index("c") * SC_SUBCORES + lax.axis_index("s")) * per
        lanes = lax.iota(jnp.int32, SC_LANES)

        @pl.loop(0, per, step=SC_LANES)
        def _(i):
            loc[pl.ds(i, SC_LANES)] = lax.rem(s0 + i, n_tok - SC_LANES) + lanes

        @pl.loop(0, n_asg, step=chunk)
        def _(c):
            pltpu.sync_copy(dest_hbm.at[pl.ds(c, chunk)], dbuf)
            pltpu.sync_copy(tok_hbm.at[pl.ds(c, chunk)], tbuf)

            @pl.loop(0, chunk, step=SC_LANES)
            def _(i):
                d = dbuf[pl.ds(i, SC_LANES)] - s0
                mine = jnp.logical_and(d >= 0, d < per)
                plsc.store_scatter(loc, [d], tbuf[pl.ds(i, SC_LANES)], mask=mine)

        pltpu.sync_copy(loc, out_hbm.at[pl.ds(s0, per)])

    return pl.kernel(
        body, out_type=jax.ShapeDtypeStruct((n_slots,), jnp.int32), mesh=mesh,
        scratch_types=[pltpu.VMEM((per,), jnp.int32), pltpu.VMEM((chunk,), jnp.int32), pltpu.VMEM((chunk,), jnp.int32)],
        compiler_params=pltpu.CompilerParams(needs_layout_passes=False),
        name="slot_tokens",
    )(dest, tok)


def _moe_routed(h_flat, logits_t, b_corr, we_g, we_u, we_d, layer):
    T = h_flat.shape[0]
    n_tiles = T * TOP_K // MOE_TILE + N_EXPERTS
    eidx, gates, rank, counts = _route(logits_t, b_corr)
    counts = counts[:, 0].astype(jnp.int32)
    padded = (counts + MOE_TILE - 1) // MOE_TILE * MOE_TILE
    pends = jnp.cumsum(padded)
    tile_start = jnp.arange(n_tiles, dtype=jnp.int32) * MOE_TILE
    tile_expert = jnp.minimum(jnp.sum((pends[None, :] <= tile_start[:, None]).astype(jnp.int32), axis=1), N_EXPERTS - 1)
    n_used = (pends[-1] // MOE_TILE).astype(jnp.int32).reshape(1)
    dest = _slots(eidx, rank, (pends - padded).astype(F32).reshape(N_EXPERTS, 1))
    tok = jnp.broadcast_to(jnp.arange(T, dtype=jnp.int32)[None], (TOP_K, T))
    src_tok = _slot_tokens(dest.reshape(-1), tok.reshape(-1), n_tiles * MOE_TILE, T)
    assert n_tiles % 2 == 0
    half = n_tiles // 2
    take = lambda idx: h_flat.at[idx].get(mode='promise_in_bounds')
    ys = _moe_grouped(tile_expert, n_used, take(src_tok[:half * MOE_TILE]), we_g, we_u, we_d, layer, 0, n_tiles)
    ys = _moe_grouped(tile_expert, n_used, take(src_tok[half * MOE_TILE:]), we_g, we_u, we_d, layer, half, n_tiles, ys)
    return ys, dest, gates.T


_GLA_QK_BLK, _GLA_V_BLK, _GLA_R_BLK = 0, 1, 2
_GLA_LR_BLK = (2 * GLA_KEY_WIDTH + 2 * GLA_WIDTH) // LANE


def _rope_tables(L):
    t = np.arange(L)
    lane = np.arange(GLA_KEY_WIDTH)
    d = lane % GLA_DK
    pos = np.where(d[None, :] < GLA_DK // 2, (t // GRID_W)[:, None], (t % GRID_W)[:, None]).astype(np.float32)
    quarter = GLA_DK // 4
    inv = ROPE_BASE ** (-jnp.arange(quarter, dtype=F32) / quarter)
    ang = jnp.asarray(pos) * inv[jnp.asarray(d % quarter)][None, :]
    sign = np.where(d % (2 * quarter) < quarter, -1.0, 1.0).astype(np.float32)
    return jnp.cos(ang), jnp.sin(ang) * sign[None, :]


def _rope_partner(x):
    lane = lax.broadcasted_iota(jnp.int32, x.shape, 1)
    quarter = GLA_DK // 4
    return jnp.where(lane % (2 * quarter) < quarter, pltpu.roll(x, GLA_KEY_WIDTH - quarter, 1), pltpu.roll(x, quarter, 1))


_NN = (((1,), (0,)), ((), ()))


def _dot_split(a, b, rhs_exact=False, dims=_NN):
    dot = lambda x, y: lax.dot_general(x, y, dims, preferred_element_type=F32)
    a_hi = a.astype(BF16)
    a_lo = (a - a_hi.astype(F32)).astype(BF16)
    b_hi = b.astype(BF16)
    out = dot(a_hi, b_hi) + dot(a_lo, b_hi)
    if not rhs_exact:
        b_lo = (b - b_hi.astype(F32)).astype(BF16)
        out = out + dot(a_hi, b_lo)
    return out


def _log_sigmoid(x):
    return jnp.minimum(x, 0.0) - jnp.log(1.0 + jnp.exp(-jnp.abs(x)))


def _gla_kernel(reverse, finalize, *refs):
    if finalize:
        (qk_ref, v_ref, lr_ref, cos_ref, sin_ref, wa_ref, ba_ref, s0_ref, of_ref, r_ref, g_ref,
         o_ref, sfin_ref, s_scr) = refs
    else:
        qk_ref, v_ref, lr_ref, cos_ref, sin_ref, wa_ref, ba_ref, s0_ref, o_ref, sfin_ref, s_scr = refs

    @pl.when(pl.program_id(1) == 0)
    def _():
        s_scr[...] = s0_ref[...]

    C = GLA_CHUNK
    KW, VW = GLA_KEY_WIDTH, GLA_WIDTH
    cos, sin = cos_ref[...], sin_ref[...]

    ri = lax.broadcasted_iota(jnp.int32, (C, C), 0)
    ci = lax.broadcasted_iota(jnp.int32, (C, C), 1)
    tri = jnp.where((ci >= ri) if reverse else (ci <= ri), 1.0, 0.0)
    tri_h = jnp.concatenate([tri] * GLA_HEADS, axis=0)
    head_k = lax.broadcasted_iota(jnp.int32, (C, KW), 1) // GLA_DK
    head_v = lax.broadcasted_iota(jnp.int32, (C, VW), 1) // GLA_DV
    own_block = (lax.broadcasted_iota(jnp.int32, (KW, VW), 0) // GLA_DK
                 == lax.broadcasted_iota(jnp.int32, (KW, VW), 1) // GLA_DV)
    eye = lax.broadcasted_iota(jnp.int32, (KW, KW), 0) == lax.broadcasted_iota(jnp.int32, (KW, KW), 1)
    for bb in range(qk_ref.shape[0]):
        _gla_sample(reverse, finalize, bb, refs, cos, sin, tri_h, head_k, head_v, own_block, eye)


def _gla_sample(reverse, finalize, bb, refs, cos, sin, tri_h, head_k, head_v, own_block, eye):
    if finalize:
        (qk_ref, v_ref, lr_ref, _, _, wa_ref, ba_ref, _, of_ref, r_ref, g_ref, o_ref, sfin_ref, s_scr) = refs
    else:
        qk_ref, v_ref, lr_ref, _, _, wa_ref, ba_ref, _, o_ref, sfin_ref, s_scr = refs
    tg = qk_ref.shape[1]
    C = GLA_CHUNK
    KW, VW = GLA_KEY_WIDTH, GLA_WIDTH
    qk = qk_ref[bb]
    q = qk[:, :KW] * GLA_DK ** -0.5
    k = qk[:, KW:]
    q = q * cos + _rope_partner(q) * sin
    k = k * cos + _rope_partner(k) * sin
    v = v_ref[bb]
    logit = _dot_split(lr_ref[bb], wa_ref[...]) + ba_ref[...]
    la = _log_sigmoid(logit) / GLA_GATE_NORM
    pos = lax.broadcasted_iota(jnp.int32, (tg, KW), 0) % C
    cum = la
    step = 1
    while step < C:
        if reverse:
            cum = cum + jnp.where(pos < C - step, pltpu.roll(cum, tg - step, 0), 0.0)
        else:
            cum = cum + jnp.where(pos >= step, pltpu.roll(cum, step, 0), 0.0)
        step *= 2

    S = s_scr[bb]
    nc = tg // C
    outs = [None] * nc
    for c in (range(nc - 1, -1, -1) if reverse else range(nc)):
        sl = slice(c * C, (c + 1) * C)
        b = cum[sl]
        qt = q[sl] * jnp.exp(b)
        kt = k[sl] * jnp.exp(-b)
        qs = jnp.concatenate([jnp.where(head_k == h, qt, 0.0) for h in range(GLA_HEADS)], axis=0).astype(BF16)
        att = lax.dot_general(qs, kt.astype(BF16), _NT, preferred_element_type=F32)
        att = jnp.where(tri_h > 0.0, att, 0.0)
        vb = v[sl].astype(BF16)
        oi = jnp.dot(att.astype(BF16), vb, preferred_element_type=F32)
        o_intra = jnp.zeros((C, VW), F32)
        for h in range(GLA_HEADS):
            o_intra = jnp.where(head_v == h, oi[h * C:(h + 1) * C], o_intra)
        o_inter = jnp.dot(qt.astype(BF16), S.astype(BF16), preferred_element_type=F32)
        outs[c] = o_intra + o_inter
        b_last = b[0:1] if reverse else b[C - 1:C]
        kdec = (k[sl] * jnp.exp(b_last - b)).astype(BF16)
        kv = lax.dot_general(kdec, vb, (((0,), (0,)), ((), ())), preferred_element_type=F32)
        decay_col = jnp.sum(jnp.where(eye, jnp.exp(b_last), 0.0), axis=1, keepdims=True)
        S = decay_col * S + jnp.where(own_block, kv, 0.0)
    s_scr[bb] = S
    sfin_ref[bb] = S
    o = jnp.concatenate(outs, axis=0)
    if finalize:
        o = of_ref[bb] + o
        same_head = (lax.broadcasted_iota(jnp.int32, (VW, VW), 0) // GLA_DV
                     == lax.broadcasted_iota(jnp.int32, (VW, VW), 1) // GLA_DV)
        ms = _dot_split(o * o, jnp.where(same_head, 1.0 / GLA_DV, 0.0), rhs_exact=True)
        r = r_ref[bb]
        o = o * lax.rsqrt(ms + LN_EPS) * g_ref[...] * (r * jax.nn.sigmoid(r))
    o_ref[bb] = o


def _gla_pass(u_rest, cos, sin, wa, ba, s0, reverse, fin=None):
    B, T, _ = u_rest.shape
    tg = min(GLA_TILE, T)
    n = T // tg
    ti = (lambda i: n - 1 - i) if reverse else (lambda i: i)
    KW, VW = GLA_KEY_WIDTH, GLA_WIDTH
    gb = GLA_BATCH if B % GLA_BATCH == 0 else 1
    ublk = lambda w, j: pl.BlockSpec((gb, tg, w), lambda b, i: (b, ti(i), j))
    full = lambda s: pl.BlockSpec(s, lambda b, i: (0,) * len(s))
    state = pl.BlockSpec((gb, KW, VW), lambda b, i: (b, 0, 0))
    tab = pl.BlockSpec((tg, KW), lambda b, i: (ti(i), 0))
    in_specs = [ublk(2 * KW, _GLA_QK_BLK), ublk(VW, _GLA_V_BLK), ublk(LANE, _GLA_LR_BLK), tab, tab,
                full((LANE, KW)), full((1, KW)), state]
    args = [u_rest, u_rest, u_rest, cos, sin, wa, ba, s0]
    if fin is not None:
        in_specs += [ublk(VW, 0), ublk(VW, _GLA_R_BLK), full((1, VW))]
        args += [fin[0], u_rest, fin[1]]
    return pl.pallas_call(
        functools.partial(_gla_kernel, reverse, fin is not None),
        grid=(B // gb, n),
        in_specs=in_specs,
        out_specs=[ublk(VW, 0), state],
        out_shape=[jax.ShapeDtypeStruct((B, T, VW), F32), jax.ShapeDtypeStruct((B, KW, VW), F32)],
        scratch_shapes=[pltpu.VMEM((gb, KW, VW), F32)],
        compiler_params=_cparams("arbitrary", "arbitrary"),
        name="gla_bwd" if reverse else "gla_fwd",
    )(*args)


def _gla_bidir(u_rest, cos, sin, w_a2, b_a2, norm_g, s0_f, s0_b):
    def decay_w(d):
        return jnp.zeros((LANE, GLA_KEY_WIDTH), F32).at[d * GLA_RANK:(d + 1) * GLA_RANK].set(w_a2[d])

    o_f, s_f = _gla_pass(u_rest, cos, sin, decay_w(0), b_a2[0:1], s0_f, False)
    g = jnp.tile(norm_g, GLA_HEADS).reshape(1, GLA_WIDTH)
    o, s_b = _gla_pass(u_rest, cos, sin, decay_w(1), b_a2[1:2], s0_b, True, (o_f, g))
    return o, s_f, s_b


HY_LANES = LANE
HY_CH_TILE = 16
HY_CTX_N1 = 16


def _hy_consts(n1, N1):
    W = HY_LANES
    n1p = max(n1, 16)
    n1o = max(n1, 8)
    a = np.arange(N1)
    th1 = 2.0 * np.pi * ((a[:, None] * a[None, :]) % N1) / N1
    f1 = np.zeros((2 * N1, n1p)); f1[:N1, :n1] = np.cos(th1[:, :n1]); f1[N1:, :n1] = -np.sin(th1[:, :n1])
    f1_full = np.concatenate([np.cos(th1), -np.sin(th1)], axis=0)
    ginv = np.zeros((n1o, 2 * N1)); ginv[:n1, :N1] = np.cos(th1.T[:n1]); ginv[:n1, N1:] = -np.sin(th1.T[:n1])
    r = np.arange(W)
    tht = 2.0 * np.pi * ((a[:, None] * r[None, :]) % (N1 * W)) / (N1 * W)
    tr = np.tile(np.cos(tht), (1, HY_CH_TILE)); ti = np.tile(-np.sin(tht), (1, HY_CH_TILE))
    th2 = 2.0 * np.pi * ((r[:, None] * r[None, :]) % W) / W
    c2, s2 = np.cos(th2), -np.sin(th2)
    m2f = np.block([[c2, s2], [-s2, c2]])
    m2i = np.block([[c2, -s2], [s2, c2]])
    f = lambda m: jnp.asarray(m, F32)
    return dict(n1p=n1p, n1o=n1o, f1=f(f1), f1_full=f(f1_full), ginv=f(ginv), tr=f(tr), ti=f(ti), m2f=f(m2f), m2i=f(m2i))


def _short_conv_rows(u, w_ref, b_ref, n_rows):
    R, Wd = u.shape
    lane = lax.broadcasted_iota(jnp.int32, (R, Wd), 1) % HY_LANES
    row = lax.broadcasted_iota(jnp.int32, (R, Wd), 0)
    up = jnp.where(row == 0, 0.0, pltpu.roll(u, 1, 0))
    dn = jnp.where(row == n_rows - 1, 0.0, pltpu.roll(u, R - 1, 0))
    prev = jnp.where(lane == 0, pltpu.roll(up, Wd - (HY_LANES - 1), 1), pltpu.roll(u, 1, 1))
    nxt = jnp.where(lane == HY_LANES - 1, pltpu.roll(dn, HY_LANES - 1, 1), pltpu.roll(u, Wd - 1, 1))
    return b_ref[...] + prev * w_ref[0:1] + u * w_ref[1:2] + nxt * w_ref[2:3]


def _pad_rows(u, rows):
    return u if u.shape[0] == rows else jnp.concatenate([u, jnp.zeros((rows - u.shape[0], u.shape[1]), u.dtype)], axis=0)


def _hy_chunk_dft(z, f1, tr, ti, N1, prec):
    if prec is None:
        a = jnp.dot(f1.astype(BF16), z.astype(BF16), preferred_element_type=F32)
    else:
        a = jnp.dot(f1, z, preferred_element_type=F32, precision=prec)
    ar, ai = a[:N1], a[N1:]
    a_re, a_im = ar * tr - ai * ti, ar * ti + ai * tr
    W = HY_LANES
    return jnp.concatenate(
        [jnp.concatenate([a_re[:, c * W:(c + 1) * W], a_im[:, c * W:(c + 1) * W]], axis=1) for c in range(z.shape[1] // W)],
        axis=0)


def _hy_long_conv(z, kf, f1, ginv, tr, ti, m2f, m2i, N1):
    W = HY_LANES
    m = z.shape[1] // W
    a = _hy_chunk_dft(z, f1, tr, ti, N1, None).astype(BF16)
    x = jnp.dot(a, m2f.astype(BF16), preferred_element_type=F32)
    xr, xi = x[:, :W], x[:, W:]
    kr, ki = kf[:, :W], kf[:, W:]
    y = jnp.concatenate([xr * kr - xi * ki, xr * ki + xi * kr], axis=1).astype(BF16)
    p = jnp.dot(y, m2i.astype(BF16), preferred_element_type=F32)
    pr = jnp.concatenate([p[c * N1:(c + 1) * N1, :W] for c in range(m)], axis=1)
    pi = jnp.concatenate([p[c * N1:(c + 1) * N1, W:] for c in range(m)], axis=1)
    q = jnp.concatenate([pr * tr + pi * ti, pi * tr - pr * ti], axis=0).astype(BF16)
    return jnp.dot(ginv.astype(BF16), q, preferred_element_type=F32)


def _hy_mixer_kernel(n1, N1, n1p, v_ref, x1_ref, x2_ref, swv_ref, sbv_ref, swx1_ref, sbx1_ref, swx2_ref, sbx2_ref,
                     skip_ref, kf_ref, f1_ref, ginv_ref, tr_ref, ti_ref, m2f_ref, m2i_ref, o_ref):
    rows = max(n1, 8)
    cst = (f1_ref[...], ginv_ref[...], tr_ref[...], ti_ref[...], m2f_ref[...], m2i_ref[...], N1)
    z = _short_conv_rows(_pad_rows(v_ref[0], rows), swv_ref, sbv_ref, n1)
    gates = (_short_conv_rows(_pad_rows(x1_ref[0], rows), swx1_ref, sbx1_ref, n1),
             _short_conv_rows(_pad_rows(x2_ref[0], rows), swx2_ref, sbx2_ref, n1))
    for o in range(HY_ORDER):
        y = _hy_long_conv(_pad_rows(z, n1p), kf_ref[o, 0], *cst)
        z = gates[o] * (y + skip_ref[o:o + 1] * z)
    o_ref[0] = z[:n1]


HY_EMB_PAD = -(-HY_EMB // 8) * 8


def _hy_positions(Ls, N):
    m = np.arange(N)
    fwd = m < Ls
    bwd = m > N - Ls
    lag = np.where(fwd, m, np.where(bwd, N - m, 0))
    t = np.linspace(0.0, 1.0, Ls, dtype=np.float32)[lag]
    w = (2.0 * math.pi * np.arange(Ls, dtype=np.float32) / Ls).astype(np.float32)[lag]
    bands = np.linspace(1e-4, HY_BANDS - 1, HY_BANDS, dtype=np.float32)
    ang = jnp.asarray(w[None, :] * bands[:, None])
    z = jnp.concatenate([jnp.asarray(t)[None, :], jnp.cos(ang), -jnp.sin(ang),
                         jnp.zeros((HY_EMB_PAD - HY_EMB, N), F32)], axis=0)
    f = lambda a: jnp.asarray(a[None, :].astype(np.float32))
    return z, f(t), f(fwd), f(bwd)


def _hy_mlp_kernel(z_ref, w1_ref, b1_ref, f1_ref, w2_ref, b2_ref, f2_ref, h_ref):
    hi = lax.Precision.HIGHEST
    h = jnp.sin(f1_ref[...] * (jnp.dot(w1_ref[...], z_ref[...], preferred_element_type=F32, precision=hi) + b1_ref[...]))
    h_ref[...] = jnp.sin(f2_ref[...] * (jnp.dot(w2_ref[...], h, preferred_element_type=F32, precision=hi) + b2_ref[...]))


def _hy_kspec_kernel(N1, scale, h_ref, w3f_ref, w3b_ref, b3f_ref, b3b_ref, df_ref, db_ref, t_ref, vf_ref, vb_ref,
                     f1_ref, tr_ref, ti_ref, m2f_ref, kf_ref):
    hi = lax.Precision.HIGHEST
    W = HY_LANES
    h, t = h_ref[...], t_ref[...]
    kf_t = ((jnp.dot(w3f_ref[...], h, preferred_element_type=F32, precision=hi) + b3f_ref[...])
            * jnp.exp(-t * jnp.abs(df_ref[...])) * vf_ref[...]
            + (jnp.dot(w3b_ref[...], h, preferred_element_type=F32, precision=hi) + b3b_ref[...])
            * jnp.exp(-t * jnp.abs(db_ref[...])) * vb_ref[...])
    kern = jnp.concatenate(
        [jnp.concatenate([kf_t[c:c + 1, a * W:(a + 1) * W] for c in range(kf_t.shape[0])], axis=1) for a in range(N1)],
        axis=0)
    a = _hy_chunk_dft(kern, f1_ref[...], tr_ref[...], ti_ref[...], N1, hi)
    kf_ref[0, 0] = jnp.dot(a, m2f_ref[...], preferred_element_type=F32, precision=hi) * scale


def _hy_filter_spectrum(Ls, N1, cst, w1, b1, f1, w2, b2, f2, w3, b3, decay):
    W = HY_LANES
    N = N1 * W
    C = HY_CH
    hid = w2.shape[0]
    z, t_row, v_f, v_b = _hy_positions(Ls, N)
    col = lambda v: v.reshape(-1, 1)
    w1_t = jnp.pad(w1, ((0, HY_EMB_PAD - HY_EMB), (0, 0))).T
    full1 = lambda a: pl.BlockSpec(a.shape, lambda i: (0,) * a.ndim)
    mlp_args = (z, w1_t, col(b1), col(f1), w2.T, col(b2), col(f2))
    h = pl.pallas_call(
        _hy_mlp_kernel, grid=(1,), in_specs=[full1(a) for a in mlp_args],
        out_specs=pl.BlockSpec((hid, N), lambda i: (0, 0)), out_shape=jax.ShapeDtypeStruct((hid, N), F32),
        compiler_params=_cparams("arbitrary"), name="hyena_filter_mlp",
    )(*mlp_args)
    ct = HY_CH_TILE
    nj = C // ct
    nb = C // ct
    full = lambda a: pl.BlockSpec(a.shape, lambda o, j: (0,) * a.ndim)
    rows = lambda width, d: pl.BlockSpec((ct, width), lambda o, j: ((o * 2 + d) * nb + j, 0))
    w3_t, b3_c, dec_c = w3.T, col(b3), col(decay)
    consts = (t_row, v_f, v_b, cst["f1_full"], cst["tr"], cst["ti"], cst["m2f"])
    return pl.pallas_call(
        functools.partial(_hy_kspec_kernel, N1, 1.0 / N),
        grid=(HY_ORDER, nj),
        in_specs=[full(h), rows(hid, 0), rows(hid, 1), rows(1, 0), rows(1, 1), rows(1, 0), rows(1, 1)]
                 + [full(a) for a in consts],
        out_specs=pl.BlockSpec((1, 1, ct * N1, 2 * W), lambda o, j: (o, j, 0, 0)),
        out_shape=jax.ShapeDtypeStruct((HY_ORDER, nj, ct * N1, 2 * W), F32),
        compiler_params=_cparams("arbitrary", "arbitrary"),
        name="hyena_filter_spectrum",
    )(h, w3_t, w3_t, b3_c, b3_c, dec_c, dec_c, *consts)


def _hyena_mixer(hy_t, short_w, short_b, filt_params, skip, N1):
    B, n1, C3, W = hy_t.shape
    C = C3 // (HY_ORDER + 1)
    assert N1 >= 2 * n1 and C % HY_CH_TILE == 0, "circular length must cover the two-sided linear convolution"
    cst = _hy_consts(n1, N1)
    kf = _hy_filter_spectrum(n1 * W, N1, cst, *filt_params)
    Wd = HY_CH_TILE * W
    nj = C // HY_CH_TILE
    hy2 = hy_t.reshape(B, n1, C3 * W)
    rep = lambda v: jnp.repeat(v, W, axis=-1)
    sw, sb = rep(short_w), rep(short_b).reshape(1, C3 * W)
    sk = rep(skip)
    chan = lambda part: pl.BlockSpec((1, n1, Wd), lambda j, b: (b, 0, part * nj + j))
    wsp = lambda part: pl.BlockSpec((HY_SHORT, Wd), lambda j, b: (0, part * nj + j))
    bsp = lambda part: pl.BlockSpec((1, Wd), lambda j, b: (0, part * nj + j))
    full = lambda a: pl.BlockSpec(a.shape, lambda j, b: (0,) * a.ndim)
    consts = [cst[k] for k in ("f1", "ginv", "tr", "ti", "m2f", "m2i")]
    out = pl.pallas_call(
        functools.partial(_hy_mixer_kernel, n1, N1, cst["n1p"]),
        grid=(nj, B),
        in_specs=[chan(0), chan(1), chan(2), wsp(0), bsp(0), wsp(1), bsp(1), wsp(2), bsp(2),
                  pl.BlockSpec((HY_ORDER, Wd), lambda j, b: (0, j)),
                  pl.BlockSpec((HY_ORDER, 1, HY_CH_TILE * N1, 2 * W), lambda j, b: (0, j, 0, 0))]
                 + [full(a) for a in consts],
        out_specs=pl.BlockSpec((1, n1, Wd), lambda j, b: (b, 0, j)),
        out_shape=jax.ShapeDtypeStruct((B, n1, C * W), F32),
        compiler_params=_cparams("arbitrary", "arbitrary"),
        name="hyena_mixer",
    )(hy2, hy2, hy2, sw, sb, sw, sb, sw, sb, sk, kf, *consts)
    return out.reshape(B, n1, C, W)


def kernel(x, c, ctx, c_ctx, w_mod, b_mod, w_in, na_rpb, gla_w_a2, gla_b_a2, gla_norm_g, hy_short_w, hy_short_b, hy_w1, hy_b1, hy_f1, hy_w2, hy_b2, hy_f2, hy_w3, hy_b3, hy_decay, hy_skip, w_out, ln1_g, ln1_b, router_w, router_b, we_gate, we_up, we_down, ws_gate, ws_up, ws_down, ln2_g, ln2_b):
    B, L, D = x.shape
    C = ctx.shape[1]
    assert D == D_MODEL and L % TOK_TILE == 0 and C % HY_LANES == 0 and C <= TOK_TILE and w_mod.shape[0] == DEPTH
    rope_cos, rope_sin = _rope_tables(L)
    ctx_cos, ctx_sin = jnp.ones((C, GLA_KEY_WIDTH), F32), jnp.zeros((C, GLA_KEY_WIDTH), F32)
    zero_state = jnp.zeros((B, GLA_KEY_WIDTH, GLA_WIDTH), F32)
    na_bias = _na_bias_table(na_rpb)

    n_mod = -(-(B + 1) // 8) * 8
    cs = jnp.zeros((n_mod, D), F32).at[:B].set(c).at[B].set(c_ctx)
    mod_all = _modulation(cs, w_mod, b_mod)

    xc = ctx
    for l in range(DEPTH):
        last = l == DEPTH - 1
        mods = mod_all[l].reshape(n_mod, 6, 1, D)
        lat = lambda j: mods[:B, j]
        cm = lambda j: mods[B:B + 1, j]
        n_tok = NA_COLS + GLA_COLS
        w_pad = jnp.pad(w_in[l][:, :n_tok], ((0, 0), (0, D_IN_PAD - n_tok))).astype(BF16)
        w_hy_t = w_in[l][:, n_tok:].T.astype(BF16)
        w_out_b = w_out[l].astype(BF16)
        lg1, lb1 = ln1_g[l].reshape(1, D), ln1_b[l].reshape(1, D)
        lg2, lb2 = ln2_g[l].reshape(1, D), ln2_b[l].reshape(1, D)

        u_na, u_rest, hy = _inproj(x, lat(1), lat(0), w_pad, w_hy_t, True)
        uc_na, uc_rest, hyc = _inproj(xc, cm(1), cm(0), w_pad, w_hy_t, False)

        na_lat = _na_attention(u_na, uc_na, na_bias, l)

        gla_c, s_ctx_f, s_ctx_b = _gla_bidir(uc_rest, ctx_cos, ctx_sin, gla_w_a2[l], gla_b_a2[l], gla_norm_g[l],
                                             zero_state, zero_state)
        gla_lat, _, _ = _gla_bidir(u_rest, rope_cos, rope_sin, gla_w_a2[l], gla_b_a2[l], gla_norm_g[l],
                                   s_ctx_f, s_ctx_b)

        filt_args = (hy_w1[l], hy_b1[l], hy_f1[l], hy_w2[l], hy_b2[l], hy_f2[l], hy_w3[l], hy_b3[l], hy_decay[l])
        hy_lat = _hyena_mixer(hy, hy_short_w[l], hy_short_b[l], filt_args, hy_skip[l], 2 * L // HY_LANES)

        wr_t = router_w[l].T
        x, h_lat, logit_lat = _outproj(na_lat, gla_lat, hy_lat, x, lat(2), w_out_b, lg1, lb1, lat(4), lat(3), wr_t, True)
        if not last:
            na_c = _ctx_attention(uc_na)
            hy_c = _hyena_mixer(hyc, hy_short_w[l], hy_short_b[l], filt_args, hy_skip[l], HY_CTX_N1)
            xc, h_c, logit_c = _outproj(na_c, gla_c, hy_c, xc, cm(2), w_out_b, lg1, lb1, cm(4), cm(3), wr_t, False)

            h_flat = jnp.concatenate([h_lat.reshape(B * L, D), h_c.reshape(B * C, D)], axis=0)
            logit_t = jnp.concatenate([logit_lat, logit_c], axis=1)
        else:
            h_flat = h_lat.reshape(B * L, D)
            logit_t = logit_lat

        ys, dest, gates = _moe_routed(h_flat, logit_t, router_b[l], we_gate, we_up, we_down, l)
        wsg, wsu, wsd = ws_gate[l].astype(BF16), ws_up[l].astype(BF16), ws_down[l].astype(BF16)
        assert B % 2 == 0
        t_half = (B // 2) * L
        take = lambda d: ys.at[d].get(mode='promise_in_bounds')
        pk_a, pk_b = take(dest[:, :t_half]), take(dest[:, t_half:])
        sh = (wsg, wsu, wsd, lg2, lb2)
        x_a = _shared_ln2(x, h_lat, pk_a, gates, 0, 0, lat(5), *sh, True, 0, B // 2)
        x = _shared_ln2(x, h_lat, pk_b, gates, 0, t_half, lat(5), *sh, True, B // 2, B // 2, x_a)
        if not last:
            xc = _shared_ln2(xc, h_c, pk_b, gates, B * L - t_half, B * L, cm(5), *sh, False)
    return x
```

```python
import functools
import math

import numpy as np
import jax
import jax.numpy as jnp
from jax import lax
from jax.experimental import pallas as pl
from jax.experimental.pallas import tpu as pltpu
from jax.experimental.pallas import tpu_sc as plsc

F32 = jnp.float32
BF16 = jnp.bfloat16

D_MODEL = 1024
DEPTH = 4
GRID_W = 64
CTX_LEN = 256

NA_HEADS = 8
NA_HEAD_DIM = 64
NA_WIDTH = NA_HEADS * NA_HEAD_DIM
NA_WIN_ROWS = 8
NA_WIN_COLS = 16

GLA_HEADS = 4
GLA_DK = 32
GLA_DV = 64
GLA_KEY_WIDTH = GLA_HEADS * GLA_DK
GLA_WIDTH = GLA_HEADS * GLA_DV
GLA_RANK = 16
GLA_GATE_NORM = 16.0
GLA_CHUNK = 64

HY_CH = 256
HY_ORDER = 2
HY_SHORT = 3
HY_BANDS = 16
HY_EMB = 1 + 2 * HY_BANDS

MIX_WIDTH = NA_WIDTH + GLA_WIDTH + HY_CH
IN_SPLITS = (NA_WIDTH, NA_WIDTH, NA_WIDTH, GLA_KEY_WIDTH, GLA_KEY_WIDTH, GLA_WIDTH, GLA_WIDTH,
             2 * GLA_RANK, (HY_ORDER + 1) * HY_CH)
ROPE_BASE = 10000.0

N_EXPERTS = 128
TOP_K = 8
N_GROUPS = 8
TOPK_GROUPS = 4
EXPERT_HIDDEN = 256
ROUTED_SCALE = 2.5

DEEPNORM_ALPHA = (2 * DEPTH) ** 0.25
LN_EPS = 1e-6

LANE = 128
MXU_DIM = 256
VMEM_LIMIT = 48 * 1024 * 1024
SC_CORES = 2
SC_SUBCORES = 16
SC_LANES = 16
SC_SCAN_CHUNK = 16384

NA_COLS = 3 * NA_WIDTH
GLA_COLS = sum(IN_SPLITS[3:8])
HY_COLS = IN_SPLITS[8]
REST_COLS = -(-GLA_COLS // LANE) * LANE
D_IN_PAD = NA_COLS + REST_COLS
NA_QUAD = MXU_DIM // NA_HEAD_DIM
NEG_BIG = -1e30

TOK_TILE = 512
MOE_TILE = 512
NA_ROW_TILE = 8
NA_ROW_UNROLL = 8
ROUTE_TILE = 256
GLA_TILE = 512
GLA_BATCH = 2


def _cparams(*sem):
    return pltpu.CompilerParams(dimension_semantics=sem, vmem_limit_bytes=VMEM_LIMIT)


def _mod_kernel(c_ref, w_ref, b_ref, o_ref):
    c = c_ref[...]
    s = c * jax.nn.sigmoid(c)
    o_ref[0] = jnp.dot(s, w_ref[0], preferred_element_type=F32, precision=lax.Precision.HIGHEST) + b_ref[0]


def _modulation(cs, w_mod, b_mod):
    R = cs.shape[0]
    tn = 1536
    return pl.pallas_call(
        _mod_kernel,
        grid=(DEPTH, 6 * D_MODEL // tn),
        in_specs=[pl.BlockSpec((R, D_MODEL), lambda l, j: (0, 0)),
                  pl.BlockSpec((1, D_MODEL, tn), lambda l, j: (l, 0, j)),
                  pl.BlockSpec((1, 1, tn), lambda l, j: (l, 0, j))],
        out_specs=pl.BlockSpec((1, R, tn), lambda l, j: (l, 0, j)),
        out_shape=jax.ShapeDtypeStruct((DEPTH, R, 6 * D_MODEL), F32),
        compiler_params=_cparams("arbitrary", "arbitrary"),
        name="modulation",
    )(cs, w_mod, b_mod.reshape(DEPTH, 1, 6 * D_MODEL))


def _inproj_kernel(x_ref, sc_ref, sh_ref, w_ref, wh_ref, ona_ref, orest_ref, ohy_ref):
    xm = (x_ref[0] * (1.0 + sc_ref[0]) + sh_ref[0]).astype(BF16)
    step = 512
    for c0 in range(0, NA_COLS, step):
        ona_ref[0, :, c0:c0 + step] = jnp.dot(xm, w_ref[:, c0:c0 + step], preferred_element_type=F32).astype(BF16)
    for c0 in range(0, REST_COLS, step):
        c1 = min(c0 + step, REST_COLS)
        orest_ref[0, :, c0:c1] = jnp.dot(xm, w_ref[:, NA_COLS + c0:NA_COLS + c1], preferred_element_type=F32)
    hy = lax.dot_general(wh_ref[...], xm, _NT, preferred_element_type=F32)
    for j in range(ohy_ref.shape[1]):
        ohy_ref[0, j] = hy[:, j * LANE:(j + 1) * LANE]


def _inproj(x, sc, sh, w_pad, w_hy_t, per_batch_mod):
    B, T, D = x.shape
    tm = min(TOK_TILE, T)
    mod_idx = (lambda b, i: (b, 0, 0)) if per_batch_mod else (lambda b, i: (0, 0, 0))
    return pl.pallas_call(
        _inproj_kernel,
        grid=(B, T // tm),
        in_specs=[pl.BlockSpec((1, tm, D), lambda b, i: (b, i, 0)),
                  pl.BlockSpec((1, 1, D), mod_idx),
                  pl.BlockSpec((1, 1, D), mod_idx),
                  pl.BlockSpec((D, D_IN_PAD), lambda b, i: (0, 0)),
                  pl.BlockSpec((HY_COLS, D), lambda b, i: (0, 0))],
        out_specs=[pl.BlockSpec((1, tm, NA_COLS), lambda b, i: (b, i, 0)),
                   pl.BlockSpec((1, tm, REST_COLS), lambda b, i: (b, i, 0)),
                   pl.BlockSpec((1, tm // LANE, HY_COLS, LANE), lambda b, i: (b, i, 0, 0))],
        out_shape=[jax.ShapeDtypeStruct((B, T, NA_COLS), BF16),
                   jax.ShapeDtypeStruct((B, T, REST_COLS), F32),
                   jax.ShapeDtypeStruct((B, T // LANE, HY_COLS, LANE), F32)],
        compiler_params=_cparams("arbitrary", "arbitrary"),
        name="inproj",
    )(x, sc, sh, w_pad, w_hy_t)


def _stack_heads(q, n_rows):
    head = lax.broadcasted_iota(jnp.int32, (n_rows, MXU_DIM), 1) // NA_HEAD_DIM
    return jnp.concatenate([jnp.where(head == h, q, jnp.zeros_like(q)) for h in range(NA_QUAD)], axis=0)


def _unstack_heads(o, n_rows):
    head = lax.broadcasted_iota(jnp.int32, (n_rows, MXU_DIM), 1) // NA_HEAD_DIM
    out = jnp.zeros((n_rows, MXU_DIM), F32)
    for h in range(NA_QUAD):
        out = jnp.where(head == h, o[h * n_rows:(h + 1) * n_rows], out)
    return out


_NT = (((1,), (1,)), ((), ()))


def _na_kernel(q_ref, k_ref, v_ref, kc_ref, vc_ref, bias_ref, o_ref):
    rt = pl.program_id(2)
    scale = NA_HEAD_DIM ** -0.5
    kc = kc_ref[0]
    vc = vc_ref[0]
    n_loc = NA_WIN_ROWS * GRID_W

    def row(rl, carry):
        r = rt * NA_ROW_TILE + rl
        kr0 = jnp.clip(r - NA_WIN_ROWS // 2, 0, GRID_W - NA_WIN_ROWS)
        dr0 = kr0 - r + NA_WIN_ROWS - 1
        q = q_ref[0, pl.ds(pl.multiple_of(rl * GRID_W, GRID_W), GRID_W), :]
        qs = _stack_heads(q, GRID_W)
        k0 = pl.multiple_of(kr0 * GRID_W, GRID_W)
        ks = k_ref[0, pl.ds(k0, n_loc), :]
        vs = v_ref[0, pl.ds(k0, n_loc), :]
        s_loc = lax.dot_general(qs, ks, _NT, preferred_element_type=F32) * scale + bias_ref[0, dr0]
        s_ctx = lax.dot_general(qs, kc, _NT, preferred_element_type=F32) * scale
        m = jnp.maximum(jnp.max(s_loc, axis=-1, keepdims=True), jnp.max(s_ctx, axis=-1, keepdims=True))
        p_loc = jnp.exp(s_loc - m)
        p_ctx = jnp.exp(s_ctx - m)
        den = jnp.sum(p_loc, axis=-1, keepdims=True) + jnp.sum(p_ctx, axis=-1, keepdims=True)
        o = (jnp.dot(p_loc.astype(BF16), vs, preferred_element_type=F32)
             + jnp.dot(p_ctx.astype(BF16), vc, preferred_element_type=F32)) / den
        o_ref[0, pl.ds(pl.multiple_of(rl * GRID_W, GRID_W), GRID_W), :] = _unstack_heads(o, GRID_W).astype(o_ref.dtype)
        return carry

    lax.fori_loop(0, NA_ROW_TILE, row, 0, unroll=NA_ROW_UNROLL)


def _na_bias_table(rpb):
    n_lyr = rpb.shape[0]
    c = np.arange(GRID_W)
    kc0 = np.clip(c - NA_WIN_COLS // 2, 0, GRID_W - NA_WIN_COLS)
    kc = np.arange(GRID_W)
    valid = (kc[None, :] >= kc0[:, None]) & (kc[None, :] < kc0[:, None] + NA_WIN_COLS)
    dc = kc[None, :] - c[:, None] + NA_WIN_COLS - 1
    onehot = (np.arange(2 * NA_WIN_COLS - 1)[:, None, None] == dc[None]) & valid[None]
    toep = jnp.einsum('lhrd,dck->lhrck', rpb, jnp.asarray(onehot, F32), precision=lax.Precision.HIGHEST)
    toep = jnp.where(jnp.asarray(valid)[None, None, None], toep, NEG_BIG)
    tab = jnp.stack([toep[:, :, d:d + NA_WIN_ROWS] for d in range(NA_WIN_ROWS)], axis=2)
    tab = tab.transpose(0, 1, 2, 4, 3, 5).reshape(n_lyr, NA_HEADS // NA_QUAD, NA_QUAD, NA_WIN_ROWS, GRID_W,
                                                  NA_WIN_ROWS * GRID_W)
    return tab.transpose(0, 1, 3, 2, 4, 5).reshape(n_lyr, NA_HEADS // NA_QUAD, NA_WIN_ROWS, NA_QUAD * GRID_W,
                                                   NA_WIN_ROWS * GRID_W)


def _na_attention(u_na, uc_na, bias_tab, layer):
    B, L, _ = u_na.shape
    C = uc_na.shape[1]
    assert L == GRID_W * GRID_W and GRID_W % NA_ROW_TILE == 0
    nq = NA_WIDTH // MXU_DIM
    tq = NA_ROW_TILE * GRID_W
    return pl.pallas_call(
        _na_kernel,
        grid=(B, nq, L // tq),
        in_specs=[pl.BlockSpec((1, tq, MXU_DIM), lambda b, j, i: (b, i, j)),
                  pl.BlockSpec((1, L, MXU_DIM), lambda b, j, i: (b, 0, nq + j)),
                  pl.BlockSpec((1, L, MXU_DIM), lambda b, j, i: (b, 0, 2 * nq + j)),
                  pl.BlockSpec((1, C, MXU_DIM), lambda b, j, i: (b, 0, nq + j)),
                  pl.BlockSpec((1, C, MXU_DIM), lambda b, j, i: (b, 0, 2 * nq + j)),
                  pl.BlockSpec((None, 1, NA_WIN_ROWS, NA_QUAD * GRID_W, NA_WIN_ROWS * GRID_W),
                               lambda b, j, i: (layer, j, 0, 0, 0))],
        out_specs=pl.BlockSpec((1, tq, MXU_DIM), lambda b, j, i: (b, i, j)),
        out_shape=jax.ShapeDtypeStruct((B, L, NA_WIDTH), BF16),
        compiler_params=_cparams("arbitrary", "arbitrary", "arbitrary"),
        name="na_attention",
    )(u_na, u_na, u_na, uc_na, uc_na, bias_tab)


def _ctx_attn_kernel(q_ref, k_ref, v_ref, o_ref):
    C = q_ref.shape[1]
    qs = _stack_heads(q_ref[0], C)
    s = lax.dot_general(qs, k_ref[0], _NT, preferred_element_type=F32) * NA_HEAD_DIM ** -0.5
    p = jnp.exp(s - jnp.max(s, axis=-1, keepdims=True))
    den = jnp.sum(p, axis=-1, keepdims=True)
    o = jnp.dot(p.astype(BF16), v_ref[0], preferred_element_type=F32) / den
    o_ref[0] = _unstack_heads(o, C)


def _ctx_attention(uc_na):
    B, C, _ = uc_na.shape
    nq = NA_WIDTH // MXU_DIM
    return pl.pallas_call(
        _ctx_attn_kernel,
        grid=(B, nq),
        in_specs=[pl.BlockSpec((1, C, MXU_DIM), lambda b, j: (b, 0, j)),
                  pl.BlockSpec((1, C, MXU_DIM), lambda b, j: (b, 0, nq + j)),
                  pl.BlockSpec((1, C, MXU_DIM), lambda b, j: (b, 0, 2 * nq + j))],
        out_specs=pl.BlockSpec((1, C, MXU_DIM), lambda b, j: (b, 0, j)),
        out_shape=jax.ShapeDtypeStruct((B, C, NA_WIDTH), F32),
        compiler_params=_cparams("arbitrary", "arbitrary"),
        name="ctx_attention",
    )(uc_na, uc_na, uc_na)


def _layer_norm_rows(y, g, b):
    mu = jnp.mean(y, axis=-1, keepdims=True)
    d = y - mu
    var = jnp.mean(d * d, axis=-1, keepdims=True)
    return d * lax.rsqrt(var + LN_EPS) * g + b


def _outproj_kernel(na_ref, gla_ref, hy_ref, x_ref, g1_ref, w_ref, lg_ref, lb_ref, sc_ref, sh_ref, wr_ref,
                    xo_ref, h_ref, logit_ref):
    w_hy = w_ref[NA_WIDTH + GLA_WIDTH:, :]
    hy_mix = jnp.concatenate(
        [lax.dot_general(hy_ref[0, j].astype(BF16), w_hy, (((0,), (0,)), ((), ())), preferred_element_type=F32)
         for j in range(hy_ref.shape[1])], axis=0)
    mix = (jnp.dot(na_ref[0].astype(BF16), w_ref[0:NA_WIDTH, :], preferred_element_type=F32)
           + jnp.dot(gla_ref[0].astype(BF16), w_ref[NA_WIDTH:NA_WIDTH + GLA_WIDTH, :], preferred_element_type=F32)
           + hy_mix)
    xn = _layer_norm_rows(DEEPNORM_ALPHA * x_ref[0] + g1_ref[0] * mix, lg_ref[...], lb_ref[...])
    xo_ref[0] = xn
    h = xn * (1.0 + sc_ref[0]) + sh_ref[0]
    h_ref[0] = h.astype(BF16)
    logit_ref[...] = _dot_split(wr_ref[...], h, dims=_NT)


def _outproj(na, gla, hy, x, g1, w_out, ln_g, ln_b, sc2, sh2, w_router, per_batch_mod):
    B, T, D = x.shape
    tm = min(TOK_TILE, T)
    mod_idx = (lambda b, i: (b, 0, 0)) if per_batch_mod else (lambda b, i: (0, 0, 0))
    tok = lambda w: pl.BlockSpec((1, tm, w), lambda b, i: (b, i, 0))
    full = lambda s: pl.BlockSpec(s, lambda b, i: (0,) * len(s))
    mod = pl.BlockSpec((1, 1, D), mod_idx)
    return pl.pallas_call(
        _outproj_kernel,
        grid=(B, T // tm),
        in_specs=[tok(NA_WIDTH), tok(GLA_WIDTH),
                  pl.BlockSpec((1, tm // LANE, HY_CH, LANE), lambda b, i: (b, i, 0, 0)), tok(D), mod, full((MIX_WIDTH, D)),
                  full((1, D)), full((1, D)), mod, mod, full((N_EXPERTS, D))],
        out_specs=[tok(D), tok(D), pl.BlockSpec((N_EXPERTS, tm), lambda b, i: (0, b * (T // tm) + i))],
        out_shape=[jax.ShapeDtypeStruct((B, T, D), F32), jax.ShapeDtypeStruct((B, T, D), BF16),
                   jax.ShapeDtypeStruct((N_EXPERTS, B * T), F32)],
        compiler_params=_cparams("arbitrary", "arbitrary"),
        name="outproj_ln1",
    )(na, gla, hy, x, g1, w_out, ln_g, ln_b, sc2, sh2, w_router)


def _moe_kernel(tile_off, te_ref, nt_ref, xs_ref, wg_ref, wu_ref, wd_ref, *rest):
    ys_ref, wg_s, wu_s, wd_s = rest[-4:]
    step = pl.program_id(0)
    i = step + tile_off

    @pl.when(jnp.logical_or(step == 0, te_ref[i] != te_ref[jnp.maximum(i - 1, 0)]))
    def _():
        wg_s[...] = wg_ref[0].astype(BF16)
        wu_s[...] = wu_ref[0].astype(BF16)
        wd_s[...] = wd_ref[0].astype(BF16)

    @pl.when(i < nt_ref[0])
    def _():
        xs = xs_ref[...]
        g = jnp.dot(xs, wg_s[...], preferred_element_type=F32)
        u = jnp.dot(xs, wu_s[...], preferred_element_type=F32)
        a = (g * jax.nn.sigmoid(g) * u).astype(BF16)
        ys_ref[...] = jnp.dot(a, wd_s[...], preferred_element_type=F32).astype(ys_ref.dtype)

    @pl.when(i >= nt_ref[0])
    def _():
        ys_ref[...] = jnp.zeros_like(ys_ref)


def _moe_grouped(tile_expert, n_used, xs, we_g, we_u, we_d, layer, tile_off, n_tiles_all, ys_prev=None):
    rows, D = xs.shape
    H = EXPERT_HIDDEN
    off = tile_off
    in_specs = [pl.BlockSpec((MOE_TILE, D), lambda i, te, nt: (i, 0)),
                pl.BlockSpec((None, 1, D, H), lambda i, te, nt: (layer, te[i + off], 0, 0)),
                pl.BlockSpec((None, 1, D, H), lambda i, te, nt: (layer, te[i + off], 0, 0)),
                pl.BlockSpec((None, 1, H, D), lambda i, te, nt: (layer, te[i + off], 0, 0))]
    args = [tile_expert, n_used, xs, we_g, we_u, we_d]
    aliases = {}
    if ys_prev is not None:
        in_specs.append(pl.BlockSpec(memory_space=pl.ANY))
        args.append(ys_prev)
        aliases = {len(args) - 1: 0}
    grid_spec = pltpu.PrefetchScalarGridSpec(
        num_scalar_prefetch=2,
        grid=(rows // MOE_TILE,),
        in_specs=in_specs,
        out_specs=pl.BlockSpec((MOE_TILE, D), lambda i, te, nt: (i + off, 0)),
        scratch_shapes=[pltpu.VMEM((D, H), BF16), pltpu.VMEM((D, H), BF16), pltpu.VMEM((H, D), BF16)],
    )
    return pl.pallas_call(
        functools.partial(_moe_kernel, tile_off),
        grid_spec=grid_spec,
        out_shape=jax.ShapeDtypeStruct((n_tiles_all * MOE_TILE, D), BF16),
        input_output_aliases=aliases,
        compiler_params=_cparams("arbitrary"),
        name="moe_experts",
    )(*args)


def _shared_kernel(x_ref, h_ref, pk_ref, gt_ref, g2_ref, wg_ref, wu_ref, wd_ref, lg_ref, lb_ref, o_ref):
    h = h_ref[0]
    g = jnp.dot(h, wg_ref[...], preferred_element_type=F32)
    u = jnp.dot(h, wu_ref[...], preferred_element_type=F32)
    a = (g * jax.nn.sigmoid(g) * u).astype(BF16)
    ff = jnp.dot(a, wd_ref[...], preferred_element_type=F32)
    gates = gt_ref[...]
    for k in range(TOP_K):
        ff = ff + pk_ref[k].astype(F32) * gates[:, k:k + 1]
    o_ref[0] = _layer_norm_rows(DEEPNORM_ALPHA * x_ref[0] + g2_ref[0] * ff, lg_ref[...], lb_ref[...])


def _shared_ln2(x, h, picked, gates, tok_off, g2, ws_g, ws_u, ws_d, ln_g, ln_b, per_batch_mod):
    B, T, D = x.shape
    tm = min(TOK_TILE, T)
    mod_idx = (lambda b, i: (b, 0, 0)) if per_batch_mod else (lambda b, i: (0, 0, 0))
    tok = pl.BlockSpec((1, tm, D), lambda b, i: (b, i, 0))
    full = lambda s: pl.BlockSpec(s, lambda b, i: (0,) * len(s))
    flat = lambda b, i: tok_off // tm + b * (T // tm) + i
    return pl.pallas_call(
        _shared_kernel,
        grid=(B, T // tm),
        in_specs=[tok, tok, pl.BlockSpec((TOP_K, tm, D), lambda b, i: (0, flat(b, i), 0)),
                  pl.BlockSpec((tm, TOP_K), lambda b, i: (flat(b, i), 0)),
                  pl.BlockSpec((1, 1, D), mod_idx), full((D, EXPERT_HIDDEN)), full((D, EXPERT_HIDDEN)),
                  full((EXPERT_HIDDEN, D)), full((1, D)), full((1, D))],
        out_specs=tok,
        out_shape=jax.ShapeDtypeStruct((B, T, D), F32),
        compiler_params=_cparams("arbitrary", "arbitrary"),
        name="shared_ln2",
    )(x, h, picked, gates, g2, ws_g, ws_u, ws_d, ln_g, ln_b)


def _first_max(vals, iota, n):
    m = jnp.max(vals, axis=0, keepdims=True)
    idx = jnp.min(jnp.where(vals == m, iota, n), axis=0, keepdims=True)
    return m, idx


def _route_kernel(lt_ref, b_ref, eidx_ref, w_ref, rank_ref, cnt_ref, base_ref):
    @pl.when(pl.program_id(0) == 0)
    def _():
        base_ref[...] = jnp.zeros_like(base_ref)

    tm = lt_ref.shape[1]
    per = N_EXPERTS // N_GROUPS
    s = jax.nn.sigmoid(lt_ref[...])
    sel = s + b_ref[...]
    io_g = lax.broadcasted_iota(jnp.int32, (per, tm), 0)
    scores = []
    for g in range(N_GROUPS):
        blk = sel[g * per:(g + 1) * per]
        m1, i1 = _first_max(blk, io_g, per)
        m2 = jnp.max(jnp.where(io_g == i1, -jnp.inf, blk), axis=0, keepdims=True)
        scores.append(m1 + m2)
    cur = jnp.concatenate(scores, axis=0)
    io_8 = lax.broadcasted_iota(jnp.int32, (N_GROUPS, tm), 0)
    gmask = jnp.zeros((N_GROUPS, tm), F32)
    for _ in range(TOPK_GROUPS):
        _, gi = _first_max(cur, io_8, N_GROUPS)
        hit = io_8 == gi
        gmask = jnp.where(hit, 1.0, gmask)
        cur = jnp.where(hit, -jnp.inf, cur)
    masked = jnp.concatenate(
        [jnp.where(gmask[g:g + 1] > 0.0, sel[g * per:(g + 1) * per], -jnp.inf) for g in range(N_GROUPS)], axis=0)
    io_e = lax.broadcasted_iota(jnp.int32, (N_EXPERTS, tm), 0)
    chosen = jnp.zeros((N_EXPERTS, tm), F32)
    eidx, gates = [], []
    for _ in range(TOP_K):
        _, ei = _first_max(masked, io_e, N_EXPERTS)
        hit = io_e == ei
        eidx.append(ei)
        gates.append(jnp.sum(jnp.where(hit, s, 0.0), axis=0, keepdims=True))
        masked = jnp.where(hit, -jnp.inf, masked)
        chosen = jnp.where(hit, 1.0, chosen)
    wk = jnp.concatenate(gates, axis=0)
    w_ref[...] = wk / jnp.sum(wk, axis=0, keepdims=True) * ROUTED_SCALE
    eidx_ref[...] = jnp.concatenate(eidx, axis=0)
    earlier = (lax.broadcasted_iota(jnp.int32, (tm, tm), 0) < lax.broadcasted_iota(jnp.int32, (tm, tm), 1))
    pos = jnp.dot(chosen.astype(BF16), jnp.where(earlier, 1.0, 0.0).astype(BF16), preferred_element_type=F32)
    pos = pos + base_ref[...]
    ranks = [jnp.sum(jnp.where(io_e == eidx[k], pos, 0.0), axis=0, keepdims=True) for k in range(TOP_K)]
    rank_ref[...] = jnp.concatenate(ranks, axis=0).astype(jnp.int32)
    base_ref[...] = base_ref[...] + jnp.sum(chosen, axis=1, keepdims=True)
    cnt_ref[...] = base_ref[...]


def _route(logits_t, b_corr):
    E, T = logits_t.shape
    tm = ROUTE_TILE
    tokk = pl.BlockSpec((TOP_K, tm), lambda i: (0, i))
    return pl.pallas_call(
        _route_kernel,
        grid=(T // tm,),
        in_specs=[pl.BlockSpec((E, tm), lambda i: (0, i)), pl.BlockSpec((E, 1), lambda i: (0, 0))],
        out_specs=[tokk, tokk, tokk, pl.BlockSpec((E, 1), lambda i: (0, 0))],
        out_shape=[jax.ShapeDtypeStruct((TOP_K, T), jnp.int32), jax.ShapeDtypeStruct((TOP_K, T), F32),
                   jax.ShapeDtypeStruct((TOP_K, T), jnp.int32), jax.ShapeDtypeStruct((E, 1), F32)],
        scratch_shapes=[pltpu.VMEM((E, 1), F32)],
        compiler_params=_cparams("arbitrary"),
        name="route",
    )(logits_t, b_corr.reshape(E, 1))


def _slot_kernel(eidx_ref, rank_ref, pstart_ref, dest_ref):
    tm = eidx_ref.shape[1]
    io_e = lax.broadcasted_iota(jnp.int32, (N_EXPERTS, tm), 0)
    ei = eidx_ref[...]
    starts = [jnp.sum(jnp.where(io_e == ei[k:k + 1], pstart_ref[...], 0.0), axis=0, keepdims=True)
              for k in range(TOP_K)]
    dest_ref[...] = jnp.concatenate(starts, axis=0).astype(jnp.int32) + rank_ref[...]


def _slots(eidx, rank, pstart):
    K, T = eidx.shape
    tm = ROUTE_TILE
    tokk = pl.BlockSpec((K, tm), lambda i: (0, i))
    return pl.pallas_call(
        _slot_kernel,
        grid=(T // tm,),
        in_specs=[tokk, tokk, pl.BlockSpec((N_EXPERTS, 1), lambda i: (0, 0))],
        out_specs=tokk,
        out_shape=jax.ShapeDtypeStruct((K, T), jnp.int32),
        compiler_params=_cparams("arbitrary"),
        name="route_slots",
    )(eidx, rank, pstart)


def _slot_tokens(dest, tok, n_slots, n_tok):
    n_asg = dest.shape[0]
    n_sub = SC_CORES * SC_SUBCORES
    per = n_slots // n_sub
    chunk = SC_SCAN_CHUNK
    assert n_slots % n_sub == 0 and per % SC_LANES == 0 and n_asg % chunk == 0 and chunk % SC_LANES == 0
    mesh = plsc.VectorSubcoreMesh(core_axis_name="c", subcore_axis_name="s", num_cores=SC_CORES, num_subcores=SC_SUBCORES)

    def body(dest_hbm, tok_hbm, out_hbm, loc, dbuf, tbuf):
        s0 = (lax.axis_index("c") * SC_SUBCORES + lax.axis_index("s")) * per
        lanes = lax.iota(jnp.int32, SC_LANES)

        @pl.loop(0, per, step=SC_LANES)
        def _(i):
            loc[pl.ds(i, SC_LANES)] = lax.rem(s0 + i, n_tok - SC_LANES) + lanes

        @pl.loop(0, n_asg, step=chunk)
        def _(c):
            pltpu.sync_copy(dest_hbm.at[pl.ds(c, chunk)], dbuf)
            pltpu.sync_copy(tok_hbm.at[pl.ds(c, chunk)], tbuf)

            @pl.loop(0, chunk, step=SC_LANES)
            def _(i):
                d = dbuf[pl.ds(i, SC_LANES)] - s0
                mine = jnp.logical_and(d >= 0, d < per)
                plsc.store_scatter(loc, [d], tbuf[pl.ds(i, SC_LANES)], mask=mine)

        pltpu.sync_copy(loc, out_hbm.at[pl.ds(s0, per)])

    return pl.kernel(
        body, out_type=jax.ShapeDtypeStruct((n_slots,), jnp.int32), mesh=mesh,
        scratch_types=[pltpu.VMEM((per,), jnp.int32), pltpu.VMEM((chunk,), jnp.int32), pltpu.VMEM((chunk,), jnp.int32)],
        compiler_params=pltpu.CompilerParams(needs_layout_passes=False),
        name="slot_tokens",
    )(dest, tok)


def _moe_routed(h_flat, logits_t, b_corr, we_g, we_u, we_d, layer):
    T = h_flat.shape[0]
    n_tiles = T * TOP_K // MOE_TILE + N_EXPERTS
    eidx, gates, rank, counts = _route(logits_t, b_corr)
    counts = counts[:, 0].astype(jnp.int32)
    padded = (counts + MOE_TILE - 1) // MOE_TILE * MOE_TILE
    pends = jnp.cumsum(padded)
    tile_start = jnp.arange(n_tiles, dtype=jnp.int32) * MOE_TILE
    tile_expert = jnp.minimum(jnp.sum((pends[None, :] <= tile_start[:, None]).astype(jnp.int32), axis=1), N_EXPERTS - 1)
    n_used = (pends[-1] // MOE_TILE).astype(jnp.int32).reshape(1)
    dest = _slots(eidx, rank, (pends - padded).astype(F32).reshape(N_EXPERTS, 1))
    tok = jnp.broadcast_to(jnp.arange(T, dtype=jnp.int32)[None], (TOP_K, T))
    src_tok = _slot_tokens(dest.reshape(-1), tok.reshape(-1), n_tiles * MOE_TILE, T)
    assert n_tiles % 2 == 0
    half = n_tiles // 2
    take = lambda idx: h_flat.at[idx].get(mode='promise_in_bounds')
    ys = _moe_grouped(tile_expert, n_used, take(src_tok[:half * MOE_TILE]), we_g, we_u, we_d, layer, 0, n_tiles)
    ys = _moe_grouped(tile_expert, n_used, take(src_tok[half * MOE_TILE:]), we_g, we_u, we_d, layer, half, n_tiles, ys)
    return ys.at[dest].get(mode='promise_in_bounds'), gates.T


_GLA_QK_BLK, _GLA_V_BLK, _GLA_R_BLK = 0, 1, 2
_GLA_LR_BLK = (2 * GLA_KEY_WIDTH + 2 * GLA_WIDTH) // LANE


def _rope_tables(L):
    t = np.arange(L)
    lane = np.arange(GLA_KEY_WIDTH)
    d = lane % GLA_DK
    pos = np.where(d[None, :] < GLA_DK // 2, (t // GRID_W)[:, None], (t % GRID_W)[:, None]).astype(np.float32)
    quarter = GLA_DK // 4
    inv = ROPE_BASE ** (-jnp.arange(quarter, dtype=F32) / quarter)
    ang = jnp.asarray(pos) * inv[jnp.asarray(d % quarter)][None, :]
    sign = np.where(d % (2 * quarter) < quarter, -1.0, 1.0).astype(np.float32)
    return jnp.cos(ang), jnp.sin(ang) * sign[None, :]


def _rope_partner(x):
    lane = lax.broadcasted_iota(jnp.int32, x.shape, 1)
    quarter = GLA_DK // 4
    return jnp.where(lane % (2 * quarter) < quarter, pltpu.roll(x, GLA_KEY_WIDTH - quarter, 1), pltpu.roll(x, quarter, 1))


_NN = (((1,), (0,)), ((), ()))


def _dot_split(a, b, rhs_exact=False, dims=_NN):
    dot = lambda x, y: lax.dot_general(x, y, dims, preferred_element_type=F32)
    a_hi = a.astype(BF16)
    a_lo = (a - a_hi.astype(F32)).astype(BF16)
    b_hi = b.astype(BF16)
    out = dot(a_hi, b_hi) + dot(a_lo, b_hi)
    if not rhs_exact:
        b_lo = (b - b_hi.astype(F32)).astype(BF16)
        out = out + dot(a_hi, b_lo)
    return out


def _log_sigmoid(x):
    return jnp.minimum(x, 0.0) - jnp.log(1.0 + jnp.exp(-jnp.abs(x)))


def _gla_kernel(reverse, finalize, *refs):
    if finalize:
        (qk_ref, v_ref, lr_ref, cos_ref, sin_ref, wa_ref, ba_ref, s0_ref, of_ref, r_ref, g_ref,
         o_ref, sfin_ref, s_scr) = refs
    else:
        qk_ref, v_ref, lr_ref, cos_ref, sin_ref, wa_ref, ba_ref, s0_ref, o_ref, sfin_ref, s_scr = refs

    @pl.when(pl.program_id(1) == 0)
    def _():
        s_scr[...] = s0_ref[...]

    C = GLA_CHUNK
    KW, VW = GLA_KEY_WIDTH, GLA_WIDTH
    cos, sin = cos_ref[...], sin_ref[...]

    ri = lax.broadcasted_iota(jnp.int32, (C, C), 0)
    ci = lax.broadcasted_iota(jnp.int32, (C, C), 1)
    tri = jnp.where((ci >= ri) if reverse else (ci <= ri), 1.0, 0.0)
    tri_h = jnp.concatenate([tri] * GLA_HEADS, axis=0)
    head_k = lax.broadcasted_iota(jnp.int32, (C, KW), 1) // GLA_DK
    head_v = lax.broadcasted_iota(jnp.int32, (C, VW), 1) // GLA_DV
    own_block = (lax.broadcasted_iota(jnp.int32, (KW, VW), 0) // GLA_DK
                 == lax.broadcasted_iota(jnp.int32, (KW, VW), 1) // GLA_DV)
    eye = lax.broadcasted_iota(jnp.int32, (KW, KW), 0) == lax.broadcasted_iota(jnp.int32, (KW, KW), 1)
    for bb in range(qk_ref.shape[0]):
        _gla_sample(reverse, finalize, bb, refs, cos, sin, tri_h, head_k, head_v, own_block, eye)


def _gla_sample(reverse, finalize, bb, refs, cos, sin, tri_h, head_k, head_v, own_block, eye):
    if finalize:
        (qk_ref, v_ref, lr_ref, _, _, wa_ref, ba_ref, _, of_ref, r_ref, g_ref, o_ref, sfin_ref, s_scr) = refs
    else:
        qk_ref, v_ref, lr_ref, _, _, wa_ref, ba_ref, _, o_ref, sfin_ref, s_scr = refs
    tg = qk_ref.shape[1]
    C = GLA_CHUNK
    KW, VW = GLA_KEY_WIDTH, GLA_WIDTH
    qk = qk_ref[bb]
    q = qk[:, :KW] * GLA_DK ** -0.5
    k = qk[:, KW:]
    q = q * cos + _rope_partner(q) * sin
    k = k * cos + _rope_partner(k) * sin
    v = v_ref[bb]
    logit = _dot_split(lr_ref[bb], wa_ref[...]) + ba_ref[...]
    la = _log_sigmoid(logit) / GLA_GATE_NORM
    pos = lax.broadcasted_iota(jnp.int32, (tg, KW), 0) % C
    cum = la
    step = 1
    while step < C:
        if reverse:
            cum = cum + jnp.where(pos < C - step, pltpu.roll(cum, tg - step, 0), 0.0)
        else:
            cum = cum + jnp.where(pos >= step, pltpu.roll(cum, step, 0), 0.0)
        step *= 2

    S = s_scr[bb]
    nc = tg // C
    outs = [None] * nc
    for c in (range(nc - 1, -1, -1) if reverse else range(nc)):
        sl = slice(c * C, (c + 1) * C)
        b = cum[sl]
        qt = q[sl] * jnp.exp(b)
        kt = k[sl] * jnp.exp(-b)
        qs = jnp.concatenate([jnp.where(head_k == h, qt, 0.0) for h in range(GLA_HEADS)], axis=0).astype(BF16)
        att = lax.dot_general(qs, kt.astype(BF16), _NT, preferred_element_type=F32)
        att = jnp.where(tri_h > 0.0, att, 0.0)
        vb = v[sl].astype(BF16)
        oi = jnp.dot(att.astype(BF16), vb, preferred_element_type=F32)
        o_intra = jnp.zeros((C, VW), F32)
        for h in range(GLA_HEADS):
            o_intra = jnp.where(head_v == h, oi[h * C:(h + 1) * C], o_intra)
        o_inter = jnp.dot(qt.astype(BF16), S.astype(BF16), preferred_element_type=F32)
        outs[c] = o_intra + o_inter
        b_last = b[0:1] if reverse else b[C - 1:C]
        kdec = (k[sl] * jnp.exp(b_last - b)).astype(BF16)
        kv = lax.dot_general(kdec, vb, (((0,), (0,)), ((), ())), preferred_element_type=F32)
        decay_col = jnp.sum(jnp.where(eye, jnp.exp(b_last), 0.0), axis=1, keepdims=True)
        S = decay_col * S + jnp.where(own_block, kv, 0.0)
    s_scr[bb] = S
    sfin_ref[bb] = S
    o = jnp.concatenate(outs, axis=0)
    if finalize:
        o = of_ref[bb] + o
        same_head = (lax.broadcasted_iota(jnp.int32, (VW, VW), 0) // GLA_DV
                     == lax.broadcasted_iota(jnp.int32, (VW, VW), 1) // GLA_DV)
        ms = _dot_split(o * o, jnp.where(same_head, 1.0 / GLA_DV, 0.0), rhs_exact=True)
        r = r_ref[bb]
        o = o * lax.rsqrt(ms + LN_EPS) * g_ref[...] * (r * jax.nn.sigmoid(r))
    o_ref[bb] = o.astype(o_ref.dtype)


def _gla_pass(u_rest, cos, sin, wa, ba, s0, reverse, fin=None):
    B, T, _ = u_rest.shape
    tg = min(GLA_TILE, T)
    n = T // tg
    ti = (lambda i: n - 1 - i) if reverse else (lambda i: i)
    KW, VW = GLA_KEY_WIDTH, GLA_WIDTH
    gb = GLA_BATCH if B % GLA_BATCH == 0 else 1
    ublk = lambda w, j: pl.BlockSpec((gb, tg, w), lambda b, i: (b, ti(i), j))
    full = lambda s: pl.BlockSpec(s, lambda b, i: (0,) * len(s))
    state = pl.BlockSpec((gb, KW, VW), lambda b, i: (b, 0, 0))
    tab = pl.BlockSpec((tg, KW), lambda b, i: (ti(i), 0))
    in_specs = [ublk(2 * KW, _GLA_QK_BLK), ublk(VW, _GLA_V_BLK), ublk(LANE, _GLA_LR_BLK), tab, tab,
                full((LANE, KW)), full((1, KW)), state]
    args = [u_rest, u_rest, u_rest, cos, sin, wa, ba, s0]
    if fin is not None:
        in_specs += [ublk(VW, 0), ublk(VW, _GLA_R_BLK), full((1, VW))]
        args += [fin[0], u_rest, fin[1]]
    return pl.pallas_call(
        functools.partial(_gla_kernel, reverse, fin is not None),
        grid=(B // gb, n),
        in_specs=in_specs,
        out_specs=[ublk(VW, 0), state],
        out_shape=[jax.ShapeDtypeStruct((B, T, VW), F32 if fin is None else BF16), jax.ShapeDtypeStruct((B, KW, VW), F32)],
        scratch_shapes=[pltpu.VMEM((gb, KW, VW), F32)],
        compiler_params=_cparams("arbitrary", "arbitrary"),
        name="gla_bwd" if reverse else "gla_fwd",
    )(*args)


def _gla_bidir(u_rest, cos, sin, w_a2, b_a2, norm_g, s0_f, s0_b):
    def decay_w(d):
        return jnp.zeros((LANE, GLA_KEY_WIDTH), F32).at[d * GLA_RANK:(d + 1) * GLA_RANK].set(w_a2[d])

    o_f, s_f = _gla_pass(u_rest, cos, sin, decay_w(0), b_a2[0:1], s0_f, False)
    g = jnp.tile(norm_g, GLA_HEADS).reshape(1, GLA_WIDTH)
    o, s_b = _gla_pass(u_rest, cos, sin, decay_w(1), b_a2[1:2], s0_b, True, (o_f, g))
    return o, s_f, s_b


HY_LANES = LANE
HY_CH_TILE = 16
HY_CTX_N1 = 16


def _hy_consts(n1, N1):
    W = HY_LANES
    n1p = max(n1, 16)
    n1o = max(n1, 8)
    a = np.arange(N1)
    th1 = 2.0 * np.pi * ((a[:, None] * a[None, :]) % N1) / N1
    f1 = np.zeros((2 * N1, n1p)); f1[:N1, :n1] = np.cos(th1[:, :n1]); f1[N1:, :n1] = -np.sin(th1[:, :n1])
    f1_full = np.concatenate([np.cos(th1), -np.sin(th1)], axis=0)
    ginv = np.zeros((n1o, 2 * N1)); ginv[:n1, :N1] = np.cos(th1.T[:n1]); ginv[:n1, N1:] = -np.sin(th1.T[:n1])
    r = np.arange(W)
    tht = 2.0 * np.pi * ((a[:, None] * r[None, :]) % (N1 * W)) / (N1 * W)
    tr = np.tile(np.cos(tht), (1, HY_CH_TILE)); ti = np.tile(-np.sin(tht), (1, HY_CH_TILE))
    th2 = 2.0 * np.pi * ((r[:, None] * r[None, :]) % W) / W
    c2, s2 = np.cos(th2), -np.sin(th2)
    m2f = np.block([[c2, s2], [-s2, c2]])
    m2i = np.block([[c2, -s2], [s2, c2]])
    f = lambda m: jnp.asarray(m, F32)
    return dict(n1p=n1p, n1o=n1o, f1=f(f1), f1_full=f(f1_full), ginv=f(ginv), tr=f(tr), ti=f(ti), m2f=f(m2f), m2i=f(m2i))


def _short_conv_rows(u, w_ref, b_ref, n_rows):
    R, Wd = u.shape
    lane = lax.broadcasted_iota(jnp.int32, (R, Wd), 1) % HY_LANES
    row = lax.broadcasted_iota(jnp.int32, (R, Wd), 0)
    up = jnp.where(row == 0, 0.0, pltpu.roll(u, 1, 0))
    dn = jnp.where(row == n_rows - 1, 0.0, pltpu.roll(u, R - 1, 0))
    prev = jnp.where(lane == 0, pltpu.roll(up, Wd - (HY_LANES - 1), 1), pltpu.roll(u, 1, 1))
    nxt = jnp.where(lane == HY_LANES - 1, pltpu.roll(dn, HY_LANES - 1, 1), pltpu.roll(u, Wd - 1, 1))
    return b_ref[...] + prev * w_ref[0:1] + u * w_ref[1:2] + nxt * w_ref[2:3]


def _pad_rows(u, rows):
    return u if u.shape[0] == rows else jnp.concatenate([u, jnp.zeros((rows - u.shape[0], u.shape[1]), u.dtype)], axis=0)


def _hy_chunk_dft(z, f1, tr, ti, N1, prec):
    if prec is None:
        a = jnp.dot(f1.astype(BF16), z.astype(BF16), preferred_element_type=F32)
    else:
        a = jnp.dot(f1, z, preferred_element_type=F32, precision=prec)
    ar, ai = a[:N1], a[N1:]
    a_re, a_im = ar * tr - ai * ti, ar * ti + ai * tr
    W = HY_LANES
    return jnp.concatenate(
        [jnp.concatenate([a_re[:, c * W:(c + 1) * W], a_im[:, c * W:(c + 1) * W]], axis=1) for c in range(z.shape[1] // W)],
        axis=0)


def _hy_long_conv(z, kf, f1, ginv, tr, ti, m2f, m2i, N1):
    W = HY_LANES
    m = z.shape[1] // W
    a = _hy_chunk_dft(z, f1, tr, ti, N1, None).astype(BF16)
    x = jnp.dot(a, m2f.astype(BF16), preferred_element_type=F32)
    xr, xi = x[:, :W], x[:, W:]
    kr, ki = kf[:, :W], kf[:, W:]
    y = jnp.concatenate([xr * kr - xi * ki, xr * ki + xi * kr], axis=1).astype(BF16)
    p = jnp.dot(y, m2i.astype(BF16), preferred_element_type=F32)
    pr = jnp.concatenate([p[c * N1:(c + 1) * N1, :W] for c in range(m)], axis=1)
    pi = jnp.concatenate([p[c * N1:(c + 1) * N1, W:] for c in range(m)], axis=1)
    q = jnp.concatenate([pr * tr + pi * ti, pi * tr - pr * ti], axis=0).astype(BF16)
    return jnp.dot(ginv.astype(BF16), q, preferred_element_type=F32)


def _hy_mixer_kernel(n1, N1, n1p, v_ref, x1_ref, x2_ref, swv_ref, sbv_ref, swx1_ref, sbx1_ref, swx2_ref, sbx2_ref,
                     skip_ref, kf_ref, f1_ref, ginv_ref, tr_ref, ti_ref, m2f_ref, m2i_ref, o_ref):
    rows = max(n1, 8)
    cst = (f1_ref[...], ginv_ref[...], tr_ref[...], ti_ref[...], m2f_ref[...], m2i_ref[...], N1)
    z = _short_conv_rows(_pad_rows(v_ref[0], rows), swv_ref, sbv_ref, n1)
    gates = (_short_conv_rows(_pad_rows(x1_ref[0], rows), swx1_ref, sbx1_ref, n1),
             _short_conv_rows(_pad_rows(x2_ref[0], rows), swx2_ref, sbx2_ref, n1))
    for o in range(HY_ORDER):
        y = _hy_long_conv(_pad_rows(z, n1p), kf_ref[o, 0], *cst)
        z = gates[o] * (y + skip_ref[o:o + 1] * z)
    o_ref[0] = z[:n1]


HY_EMB_PAD = -(-HY_EMB // 8) * 8


def _hy_positions(Ls, N):
    m = np.arange(N)
    fwd = m < Ls
    bwd = m > N - Ls
    lag = np.where(fwd, m, np.where(bwd, N - m, 0))
    t = np.linspace(0.0, 1.0, Ls, dtype=np.float32)[lag]
    w = (2.0 * math.pi * np.arange(Ls, dtype=np.float32) / Ls).astype(np.float32)[lag]
    bands = np.linspace(1e-4, HY_BANDS - 1, HY_BANDS, dtype=np.float32)
    ang = jnp.asarray(w[None, :] * bands[:, None])
    z = jnp.concatenate([jnp.asarray(t)[None, :], jnp.cos(ang), -jnp.sin(ang),
                         jnp.zeros((HY_EMB_PAD - HY_EMB, N), F32)], axis=0)
    f = lambda a: jnp.asarray(a[None, :].astype(np.float32))
    return z, f(t), f(fwd), f(bwd)


def _hy_mlp_kernel(z_ref, w1_ref, b1_ref, f1_ref, w2_ref, b2_ref, f2_ref, h_ref):
    hi = lax.Precision.HIGHEST
    h = jnp.sin(f1_ref[...] * (jnp.dot(w1_ref[...], z_ref[...], preferred_element_type=F32, precision=hi) + b1_ref[...]))
    h_ref[...] = jnp.sin(f2_ref[...] * (jnp.dot(w2_ref[...], h, preferred_element_type=F32, precision=hi) + b2_ref[...]))


def _hy_kspec_kernel(N1, scale, h_ref, w3f_ref, w3b_ref, b3f_ref, b3b_ref, df_ref, db_ref, t_ref, vf_ref, vb_ref,
                     f1_ref, tr_ref, ti_ref, m2f_ref, kf_ref):
    hi = lax.Precision.HIGHEST
    W = HY_LANES
    h, t = h_ref[...], t_ref[...]
    kf_t = ((jnp.dot(w3f_ref[...], h, preferred_element_type=F32, precision=hi) + b3f_ref[...])
            * jnp.exp(-t * jnp.abs(df_ref[...])) * vf_ref[...]
            + (jnp.dot(w3b_ref[...], h, preferred_element_type=F32, precision=hi) + b3b_ref[...])
            * jnp.exp(-t * jnp.abs(db_ref[...])) * vb_ref[...])
    kern = jnp.concatenate(
        [jnp.concatenate([kf_t[c:c + 1, a * W:(a + 1) * W] for c in range(kf_t.shape[0])], axis=1) for a in range(N1)],
        axis=0)
    a = _hy_chunk_dft(kern, f1_ref[...], tr_ref[...], ti_ref[...], N1, hi)
    kf_ref[0, 0] = jnp.dot(a, m2f_ref[...], preferred_element_type=F32, precision=hi) * scale


def _hy_filter_spectrum(Ls, N1, cst, w1, b1, f1, w2, b2, f2, w3, b3, decay):
    W = HY_LANES
    N = N1 * W
    C = HY_CH
    hid = w2.shape[0]
    z, t_row, v_f, v_b = _hy_positions(Ls, N)
    col = lambda v: v.reshape(-1, 1)
    w1_t = jnp.pad(w1, ((0, HY_EMB_PAD - HY_EMB), (0, 0))).T
    full1 = lambda a: pl.BlockSpec(a.shape, lambda i: (0,) * a.ndim)
    mlp_args = (z, w1_t, col(b1), col(f1), w2.T, col(b2), col(f2))
    h = pl.pallas_call(
        _hy_mlp_kernel, grid=(1,), in_specs=[full1(a) for a in mlp_args],
        out_specs=pl.BlockSpec((hid, N), lambda i: (0, 0)), out_shape=jax.ShapeDtypeStruct((hid, N), F32),
        compiler_params=_cparams("arbitrary"), name="hyena_filter_mlp",
    )(*mlp_args)
    ct = HY_CH_TILE
    nj = C // ct
    nb = C // ct
    full = lambda a: pl.BlockSpec(a.shape, lambda o, j: (0,) * a.ndim)
    rows = lambda width, d: pl.BlockSpec((ct, width), lambda o, j: ((o * 2 + d) * nb + j, 0))
    w3_t, b3_c, dec_c = w3.T, col(b3), col(decay)
    consts = (t_row, v_f, v_b, cst["f1_full"], cst["tr"], cst["ti"], cst["m2f"])
    return pl.pallas_call(
        functools.partial(_hy_kspec_kernel, N1, 1.0 / N),
        grid=(HY_ORDER, nj),
        in_specs=[full(h), rows(hid, 0), rows(hid, 1), rows(1, 0), rows(1, 1), rows(1, 0), rows(1, 1)]
                 + [full(a) for a in consts],
        out_specs=pl.BlockSpec((1, 1, ct * N1, 2 * W), lambda o, j: (o, j, 0, 0)),
        out_shape=jax.ShapeDtypeStruct((HY_ORDER, nj, ct * N1, 2 * W), F32),
        compiler_params=_cparams("arbitrary", "arbitrary"),
        name="hyena_filter_spectrum",
    )(h, w3_t, w3_t, b3_c, b3_c, dec_c, dec_c, *consts)


def _hyena_mixer(hy_t, short_w, short_b, filt_params, skip, N1):
    B, n1, C3, W = hy_t.shape
    C = C3 // (HY_ORDER + 1)
    assert N1 >= 2 * n1 and C % HY_CH_TILE == 0, "circular length must cover the two-sided linear convolution"
    cst = _hy_consts(n1, N1)
    kf = _hy_filter_spectrum(n1 * W, N1, cst, *filt_params)
    Wd = HY_CH_TILE * W
    nj = C // HY_CH_TILE
    hy2 = hy_t.reshape(B, n1, C3 * W)
    rep = lambda v: jnp.repeat(v, W, axis=-1)
    sw, sb = rep(short_w), rep(short_b).reshape(1, C3 * W)
    sk = rep(skip)
    chan = lambda part: pl.BlockSpec((1, n1, Wd), lambda j, b: (b, 0, part * nj + j))
    wsp = lambda part: pl.BlockSpec((HY_SHORT, Wd), lambda j, b: (0, part * nj + j))
    bsp = lambda part: pl.BlockSpec((1, Wd), lambda j, b: (0, part * nj + j))
    full = lambda a: pl.BlockSpec(a.shape, lambda j, b: (0,) * a.ndim)
    consts = [cst[k] for k in ("f1", "ginv", "tr", "ti", "m2f", "m2i")]
    out = pl.pallas_call(
        functools.partial(_hy_mixer_kernel, n1, N1, cst["n1p"]),
        grid=(nj, B),
        in_specs=[chan(0), chan(1), chan(2), wsp(0), bsp(0), wsp(1), bsp(1), wsp(2), bsp(2),
                  pl.BlockSpec((HY_ORDER, Wd), lambda j, b: (0, j)),
                  pl.BlockSpec((HY_ORDER, 1, HY_CH_TILE * N1, 2 * W), lambda j, b: (0, j, 0, 0))]
                 + [full(a) for a in consts],
        out_specs=pl.BlockSpec((1, n1, Wd), lambda j, b: (b, 0, j)),
        out_shape=jax.ShapeDtypeStruct((B, n1, C * W), F32),
        compiler_params=_cparams("arbitrary", "arbitrary"),
        name="hyena_mixer",
    )(hy2, hy2, hy2, sw, sb, sw, sb, sw, sb, sk, kf, *consts)
    return out.reshape(B, n1, C, W)


def kernel(x, c, ctx, c_ctx, w_mod, b_mod, w_in, na_rpb, gla_w_a2, gla_b_a2, gla_norm_g, hy_short_w, hy_short_b, hy_w1, hy_b1, hy_f1, hy_w2, hy_b2, hy_f2, hy_w3, hy_b3, hy_decay, hy_skip, w_out, ln1_g, ln1_b, router_w, router_b, we_gate, we_up, we_down, ws_gate, ws_up, ws_down, ln2_g, ln2_b):
    B, L, D = x.shape
    C = ctx.shape[1]
    assert D == D_MODEL and L % TOK_TILE == 0 and C % HY_LANES == 0 and C <= TOK_TILE and w_mod.shape[0] == DEPTH
    rope_cos, rope_sin = _rope_tables(L)
    ctx_cos, ctx_sin = jnp.ones((C, GLA_KEY_WIDTH), F32), jnp.zeros((C, GLA_KEY_WIDTH), F32)
    zero_state = jnp.zeros((B, GLA_KEY_WIDTH, GLA_WIDTH), F32)
    na_bias = _na_bias_table(na_rpb)

    n_mod = -(-(B + 1) // 8) * 8
    cs = jnp.zeros((n_mod, D), F32).at[:B].set(c).at[B].set(c_ctx)
    mod_all = _modulation(cs, w_mod, b_mod)

    xc = ctx
    for l in range(DEPTH):
        last = l == DEPTH - 1
        mods = mod_all[l].reshape(n_mod, 6, 1, D)
        lat = lambda j: mods[:B, j]
        cm = lambda j: mods[B:B + 1, j]
        n_tok = NA_COLS + GLA_COLS
        w_pad = jnp.pad(w_in[l][:, :n_tok], ((0, 0), (0, D_IN_PAD - n_tok))).astype(BF16)
        w_hy_t = w_in[l][:, n_tok:].T.astype(BF16)
        w_out_b = w_out[l].astype(BF16)
        lg1, lb1 = ln1_g[l].reshape(1, D), ln1_b[l].reshape(1, D)
        lg2, lb2 = ln2_g[l].reshape(1, D), ln2_b[l].reshape(1, D)

        u_na, u_rest, hy = _inproj(x, lat(1), lat(0), w_pad, w_hy_t, True)
        uc_na, uc_rest, hyc = _inproj(xc, cm(1), cm(0), w_pad, w_hy_t, False)

        na_lat = _na_attention(u_na, uc_na, na_bias, l)

        gla_c, s_ctx_f, s_ctx_b = _gla_bidir(uc_rest, ctx_cos, ctx_sin, gla_w_a2[l], gla_b_a2[l], gla_norm_g[l],
                                             zero_state, zero_state)
        gla_lat, _, _ = _gla_bidir(u_rest, rope_cos, rope_sin, gla_w_a2[l], gla_b_a2[l], gla_norm_g[l],
                                   s_ctx_f, s_ctx_b)

        filt_args = (hy_w1[l], hy_b1[l], hy_f1[l], hy_w2[l], hy_b2[l], hy_f2[l], hy_w3[l], hy_b3[l], hy_decay[l])
        hy_lat = _hyena_mixer(hy, hy_short_w[l], hy_short_b[l], filt_args, hy_skip[l], 2 * L // HY_LANES)

        wr_t = router_w[l].T
        x, h_lat, logit_lat = _outproj(na_lat, gla_lat, hy_lat, x, lat(2), w_out_b, lg1, lb1, lat(4), lat(3), wr_t, True)
        if not last:
            na_c = _ctx_attention(uc_na)
            hy_c = _hyena_mixer(hyc, hy_short_w[l], hy_short_b[l], filt_args, hy_skip[l], HY_CTX_N1)
            xc, h_c, logit_c = _outproj(na_c, gla_c, hy_c, xc, cm(2), w_out_b, lg1, lb1, cm(4), cm(3), wr_t, False)

            h_flat = jnp.concatenate([h_lat.reshape(B * L, D), h_c.reshape(B * C, D)], axis=0)
            logit_t = jnp.concatenate([logit_lat, logit_c], axis=1)
        else:
            h_flat = h_lat.reshape(B * L, D)
            logit_t = logit_lat

        picked, gates = _moe_routed(h_flat, logit_t, router_b[l], we_gate, we_up, we_down, l)
        wsg, wsu, wsd = ws_gate[l].astype(BF16), ws_up[l].astype(BF16), ws_down[l].astype(BF16)
        x = _shared_ln2(x, h_lat, picked, gates, 0, lat(5), wsg, wsu, wsd, lg2, lb2, True)
        if not last:
            xc = _shared_ln2(xc, h_c, picked, gates, B * L, cm(5), wsg, wsu, wsd, lg2, lb2, False)
    return x
```
